```python
import math
import jax, jax.numpy as jnp
from jax import lax
import numpy as np

D_MODEL = 1024
BATCH = 8
SEQ = 2048
DEPTH = 2

CHUNK = 64
Q_BLOCK = 128
DIFF_HEADS = D_MODEL // 256
DIFF_HEAD_DIM = 64
DIFF_WIDTH = DIFF_HEADS * 2 * DIFF_HEAD_DIM
MLA_HEADS = D_MODEL // 256
MLA_NOPE_DIM = 128
MLA_ROPE_DIM = 64
MLA_V_DIM = 128
MLA_Q_RANK = D_MODEL // 4
MLA_KV_RANK = D_MODEL // 8
MLA_WIDTH = MLA_HEADS * MLA_V_DIM
MIX_WIDTH = DIFF_WIDTH + MLA_WIDTH
IN_WIDTH = 3 * DIFF_WIDTH + MLA_Q_RANK + MLA_KV_RANK + MLA_ROPE_DIM
D_FF = 4 * D_MODEL
N_BUCKETS = 32
MAX_DISTANCE = 128
ROPE_THETA = 10000.0
ALPHA = (2 * DEPTH) ** 0.25
BETA = (8 * DEPTH) ** -0.25
LN_EPS = 1e-5
RMS_EPS = 1e-6

kernel_name = "hybrid_diffattn_mla_deepnorm_encoder"


def _layer_norm(x, g, b):
    xf = x.astype(jnp.float32)
    mu = jnp.mean(xf, axis=-1, keepdims=True)
    var = jnp.mean(jnp.square(xf - mu), axis=-1, keepdims=True)
    y = (xf - mu) * lax.rsqrt(var + LN_EPS)
    return (y * g.astype(jnp.float32) + b.astype(jnp.float32)).astype(x.dtype)


def _rms_norm(x, g):
    xf = x.astype(jnp.float32)
    y = xf * lax.rsqrt(jnp.mean(jnp.square(xf), axis=-1, keepdims=True) + RMS_EPS)
    return (y * g.astype(jnp.float32)).astype(x.dtype)


def _t5_bucket(rel):
    nb = N_BUCKETS // 2
    ret = (rel > 0).astype(jnp.int32) * nb
    n = jnp.abs(rel)
    max_exact = nb // 2
    nf = jnp.maximum(n, 1).astype(jnp.float32)
    large = max_exact + (jnp.log(nf / max_exact) / math.log(MAX_DISTANCE / max_exact)
                         * (nb - max_exact)).astype(jnp.int32)
    large = jnp.minimum(large, nb - 1)
    return ret + jnp.where(n < max_exact, n, large)


def _chunk_mask(q_pos, k_pos):
    return (k_pos // CHUNK)[None, :] <= (q_pos // CHUNK)[:, None]


def _rope_tables(seq):
    pos = jnp.arange(seq, dtype=jnp.float32)
    inv = ROPE_THETA ** (-jnp.arange(0, MLA_ROPE_DIM, 2, dtype=jnp.float32) / MLA_ROPE_DIM)
    ang = pos[:, None] * inv[None, :]
    return jnp.cos(ang), jnp.sin(ang)


def _apply_rope(x, cos, sin):
    xf = x.astype(jnp.float32)
    x1, x2 = jnp.split(xf, 2, axis=-1)
    out = jnp.concatenate([x1 * cos - x2 * sin, x1 * sin + x2 * cos], axis=-1)
    return out.astype(x.dtype)


def _diff_attention(q, k, v, lam, rel_bias):
    S = q.shape[1]
    scale = DIFF_HEAD_DIM ** -0.5
    outs = []
    for i in range(S // Q_BLOCK):
        L = (i + 1) * Q_BLOCK
        q_pos = jnp.arange(i * Q_BLOCK, L)
        k_pos = jnp.arange(L)
        q_b = q[:, i * Q_BLOCK:L]
        logits = jnp.einsum('bqhmd,bkhmd->bhmqk', q_b, k[:, :L]).astype(jnp.float32) * scale
        bias = rel_bias[_t5_bucket(k_pos[None, :] - q_pos[:, None])]
        bias = jnp.transpose(bias, (2, 0, 1)).astype(jnp.float32)
        logits = logits + bias[None, :, None]
        mask = _chunk_mask(q_pos, k_pos)
        logits = jnp.where(mask[None, None, None], logits, -jnp.inf)
        p = jax.nn.softmax(logits, axis=-1)
        a = p[:, :, 0] - lam * p[:, :, 1]
        outs.append(jnp.einsum('bhqk,bkhe->bqhe', a.astype(v.dtype), v[:, :L]))
    return jnp.concatenate(outs, axis=1)


def _mla_attention(q_nope, q_rope, k_nope, k_rope, v):
    S = q_nope.shape[1]
    scale = (MLA_NOPE_DIM + MLA_ROPE_DIM) ** -0.5
    outs = []
    for i in range(S // Q_BLOCK):
        L = (i + 1) * Q_BLOCK
        q_pos = jnp.arange(i * Q_BLOCK, L)
        k_pos = jnp.arange(L)
        qs = slice(i * Q_BLOCK, L)
        logits = (jnp.einsum('bqhd,bkhd->bhqk', q_nope[:, qs], k_nope[:, :L]).astype(jnp.float32)
                  + jnp.einsum('bqhr,bkr->bhqk', q_rope[:, qs], k_rope[:, :L]).astype(jnp.float32)) * scale
        mask = _chunk_mask(q_pos, k_pos)
        logits = jnp.where(mask[None, None], logits, -jnp.inf)
        p = jax.nn.softmax(logits, axis=-1)
        outs.append(jnp.einsum('bhqk,bkhe->bqhe', p.astype(v.dtype), v[:, :L]))
    return jnp.concatenate(outs, axis=1)


def setup_inputs(seed: int = 0) -> dict:
    key = jax.random.key(seed)
    ks = jax.random.split(key, 20)
    f32 = jnp.float32

    def nrm(k, shape, std):
        return jax.random.normal(k, shape, f32) * std

    x = jax.random.normal(ks[0], (BATCH, SEQ, D_MODEL), f32)
    w_in = nrm(ks[1], (DEPTH, D_MODEL, IN_WIDTH), D_MODEL ** -0.5)
    col_scale = jnp.ones((IN_WIDTH,), f32).at[2 * DIFF_WIDTH:3 * DIFF_WIDTH].set(BETA)
    w_in = w_in * col_scale
    lambda_q1 = nrm(ks[2], (DEPTH, DIFF_HEAD_DIM), 0.1)
    lambda_k1 = nrm(ks[3], (DEPTH, DIFF_HEAD_DIM), 0.1)
    lambda_q2 = nrm(ks[4], (DEPTH, DIFF_HEAD_DIM), 0.1)
    lambda_k2 = nrm(ks[5], (DEPTH, DIFF_HEAD_DIM), 0.1)
    subln_g = 1.0 + nrm(ks[6], (DEPTH, 2 * DIFF_HEAD_DIM), 0.02)
    q_norm_g = 1.0 + nrm(ks[7], (DEPTH, MLA_Q_RANK), 0.02)
    w_uq = nrm(ks[8], (DEPTH, MLA_Q_RANK, MLA_HEADS * (MLA_NOPE_DIM + MLA_ROPE_DIM)), MLA_Q_RANK ** -0.5)
    kv_norm_g = 1.0 + nrm(ks[9], (DEPTH, MLA_KV_RANK), 0.02)
    w_ukv = nrm(ks[10], (DEPTH, MLA_KV_RANK, MLA_HEADS, MLA_NOPE_DIM + MLA_V_DIM), MLA_KV_RANK ** -0.5)
    v_scale = jnp.concatenate([jnp.ones((MLA_NOPE_DIM,), f32), jnp.full((MLA_V_DIM,), BETA, f32)])
    w_ukv = (w_ukv * v_scale).reshape(DEPTH, MLA_KV_RANK, MLA_HEADS * (MLA_NOPE_DIM + MLA_V_DIM))
    rel_bias = nrm(ks[11], (N_BUCKETS, DIFF_HEADS), 0.5)
    w_o = nrm(ks[12], (DEPTH, MIX_WIDTH, D_MODEL), MIX_WIDTH ** -0.5 * BETA)
    ln1_g = 1.0 + nrm(ks[13], (DEPTH, D_MODEL), 0.02)
    ln1_b = nrm(ks[14], (DEPTH, D_MODEL), 0.02)
    w_mlp_in = nrm(ks[15], (DEPTH, D_MODEL, D_FF), D_MODEL ** -0.5 * BETA)
    w_mlp_out = nrm(ks[16], (DEPTH, D_FF, D_MODEL), D_FF ** -0.5 * BETA)
    ln2_g = 1.0 + nrm(ks[17], (DEPTH, D_MODEL), 0.02)
    ln2_b = nrm(ks[18], (DEPTH, D_MODEL), 0.02)
    return {"x": x, "w_in": w_in, "lambda_q1": lambda_q1, "lambda_k1": lambda_k1,
            "lambda_q2": lambda_q2, "lambda_k2": lambda_k2, "subln_g": subln_g,
            "q_norm_g": q_norm_g, "w_uq": w_uq, "kv_norm_g": kv_norm_g, "w_ukv": w_ukv,
            "rel_bias": rel_bias, "w_o": w_o, "ln1_g": ln1_g, "ln1_b": ln1_b,
            "w_mlp_in": w_mlp_in, "w_mlp_out": w_mlp_out, "ln2_g": ln2_g, "ln2_b": ln2_b}


def reference(x, w_in, lambda_q1, lambda_k1, lambda_q2, lambda_k2, subln_g,
              q_norm_g, w_uq, kv_norm_g, w_ukv, rel_bias, w_o, ln1_g, ln1_b,
              w_mlp_in, w_mlp_out, ln2_g, ln2_b):
    B, S, _ = x.shape
    cos, sin = _rope_tables(S)
    o_q = DIFF_WIDTH
    o_k = 2 * DIFF_WIDTH
    o_cq = 3 * DIFF_WIDTH
    o_ckv = o_cq + MLA_Q_RANK
    o_kr = o_ckv + MLA_KV_RANK
    for l in range(DEPTH):
        lambda_init = 0.8 - 0.6 * math.exp(-0.3 * l)
        h = jnp.einsum('bsd,de->bse', x, w_in[l])

        dq = h[..., :o_q].reshape(B, S, DIFF_HEADS, 2, DIFF_HEAD_DIM)
        dk = h[..., o_q:o_k].reshape(B, S, DIFF_HEADS, 2, DIFF_HEAD_DIM)
        dv = h[..., o_k:o_cq].reshape(B, S, DIFF_HEADS, 2 * DIFF_HEAD_DIM)
        lam = (jnp.exp(jnp.sum(lambda_q1[l].astype(jnp.float32) * lambda_k1[l].astype(jnp.float32)))
               - jnp.exp(jnp.sum(lambda_q2[l].astype(jnp.float32) * lambda_k2[l].astype(jnp.float32)))
               + lambda_init)
        a_out = _diff_attention(dq, dk, dv, lam, rel_bias)
        a_out = _rms_norm(a_out, subln_g[l]) * (1.0 - lambda_init)

        c_q = _rms_norm(h[..., o_cq:o_ckv], q_norm_g[l])
        c_kv = _rms_norm(h[..., o_ckv:o_kr], kv_norm_g[l])
        k_rope = _apply_rope(h[..., o_kr:], cos, sin)
        qf = jnp.einsum('bsr,re->bse', c_q, w_uq[l]).reshape(B, S, MLA_HEADS, MLA_NOPE_DIM + MLA_ROPE_DIM)
        q_nope = qf[..., :MLA_NOPE_DIM]
        q_rope = _apply_rope(qf[..., MLA_NOPE_DIM:], cos[:, None, :], sin[:, None, :])
        kvf = jnp.einsum('bsr,re->bse', c_kv, w_ukv[l]).reshape(B, S, MLA_HEADS, MLA_NOPE_DIM + MLA_V_DIM)
        k_nope = kvf[..., :MLA_NOPE_DIM]
        mv = kvf[..., MLA_NOPE_DIM:]
        b_out = _mla_attention(q_nope, q_rope, k_nope, k_rope, mv)

        mix = jnp.concatenate([a_out.reshape(B, S, DIFF_WIDTH), b_out.reshape(B, S, MLA_WIDTH)], axis=-1)
        y = jnp.einsum('bse,ed->bsd', mix, w_o[l])
        x = _layer_norm(ALPHA * x + y, ln1_g[l], ln1_b[l])

        u = jnp.square(jax.nn.relu(jnp.einsum('bsd,df->bsf', x, w_mlp_in[l])))
        y = jnp.einsum('bsf,fd->bsd', u, w_mlp_out[l])
        x = _layer_norm(ALPHA * x + y, ln2_g[l], ln2_b[l])
    return x
```

```python
import functools
import math

import jax
import jax.numpy as jnp
from jax import lax
from jax.experimental import pallas as pl
from jax.experimental.pallas import tpu as pltpu

F32 = jnp.float32
BF16 = jnp.bfloat16

D_MODEL = 1024
DEPTH = 2
CHUNK = 64
HEADS = 4
DIFF_HEAD_DIM = 64
DIFF_V_DIM = 2 * DIFF_HEAD_DIM
DIFF_WIDTH = HEADS * DIFF_V_DIM
MLA_NOPE_DIM = 128
MLA_ROPE_DIM = 64
MLA_V_DIM = 128
MLA_Q_RANK = 256
MLA_KV_RANK = 128
MLA_WIDTH = HEADS * MLA_V_DIM
MLA_QK_PAD = 256
D_FF = 4 * D_MODEL
N_BUCKETS = 32
MAX_DISTANCE = 128
ROPE_THETA = 10000.0
ALPHA = (2 * DEPTH) ** 0.25
LN_EPS = 1e-5
RMS_EPS = 1e-6
NEG_BIG = -1e30

LANES = 128
ATT_TILE = 256
PROJ_ROWS = 512
POST_ROWS = 512
FF_CHUNK = 1024
VMEM_LIMIT = 56 * 1024 * 1024

C_DQ = 0
C_DK = C_DQ + DIFF_WIDTH
C_CQ = C_DK + DIFF_WIDTH
C_CKV = C_CQ + MLA_Q_RANK
C_KR = C_CKV + MLA_KV_RANK
C_KRS = C_KR + LANES
W1_COLS = C_KRS + LANES

_NT = (((1,), (1,)), ((), ()))


def _rms(x, g):
    return x * lax.rsqrt(jnp.mean(x * x, axis=-1, keepdims=True) + RMS_EPS) * g


def _layer_norm(x, g, b):
    mu = jnp.mean(x, axis=-1, keepdims=True)
    xc = x - mu
    var = jnp.mean(xc * xc, axis=-1, keepdims=True)
    return xc * lax.rsqrt(var + LN_EPS) * g + b


def _proj_kernel(x_ref, w1_ref, wvt_ref, wuq_ref, wuqs_ref, wukvk_ref, wukvvt_ref,
                 gq_ref, gkv_ref, cos_ref, sin_ref,
                 dq_ref, dk_ref, dvt_ref, qm_ref, km_ref, vmt_ref):
    rows = x_ref.shape[0]
    xb = x_ref[...].astype(BF16)
    h = jnp.dot(xb, w1_ref[...], preferred_element_type=F32)
    dq_ref[...] = h[:, C_DQ:C_DQ + DIFF_WIDTH].astype(BF16)
    dk_ref[...] = h[:, C_DK:C_DK + DIFF_WIDTH].astype(BF16)
    dvt = lax.dot_general(wvt_ref[...], xb, _NT, preferred_element_type=F32).astype(BF16)

    cos = cos_ref[...]
    sin = sin_ref[...]
    c_q = _rms(h[:, C_CQ:C_CQ + MLA_Q_RANK], gq_ref[...]).astype(BF16)
    qf = jnp.dot(c_q, wuq_ref[...], preferred_element_type=F32)
    qsw = jnp.dot(c_q, wuqs_ref[...], preferred_element_type=F32)
    c_kv = _rms(h[:, C_CKV:C_CKV + MLA_KV_RANK], gkv_ref[...]).astype(BF16)
    kn = jnp.dot(c_kv, wukvk_ref[...], preferred_element_type=F32)
    vmt = lax.dot_general(wukvvt_ref[...], c_kv, _NT, preferred_element_type=F32).astype(BF16)
    k_rope = (h[:, C_KR:C_KR + LANES] * cos + h[:, C_KRS:C_KRS + LANES] * sin).astype(BF16)

    for hh in range(HEADS):
        b0 = hh * MLA_QK_PAD
        qm_ref[:, b0:b0 + LANES] = qf[:, b0:b0 + LANES].astype(BF16)
        qm_ref[:, b0 + LANES:b0 + 2 * LANES] = (
            qf[:, b0 + LANES:b0 + 2 * LANES] * cos + qsw[:, hh * LANES:(hh + 1) * LANES] * sin
        ).astype(BF16)
        km_ref[:, b0:b0 + LANES] = kn[:, hh * LANES:(hh + 1) * LANES].astype(BF16)
        km_ref[:, b0 + LANES:b0 + 2 * LANES] = k_rope
        for c in range(rows // ATT_TILE):
            cs = slice(c * ATT_TILE, (c + 1) * ATT_TILE)
            dvt_ref[0, hh, c] = dvt[hh * LANES:(hh + 1) * LANES, cs]
            vmt_ref[0, hh, c] = vmt[hh * LANES:(hh + 1) * LANES, cs]


def _proj_call(x2, wp, seq):
    ntok = x2.shape[0]
    batch = ntok // seq
    rows = PROJ_ROWS
    steps_per_seq = seq // rows
    nk = seq // ATT_TILE
    tiles_per_step = rows // ATT_TILE

    def const(shape):
        return pl.BlockSpec(shape, lambda i: (0,) * len(shape))

    def tok(width):
        return pl.BlockSpec((rows, width), lambda i: (i, 0))

    vt_spec = pl.BlockSpec((1, HEADS, tiles_per_step, LANES, ATT_TILE),
                           lambda i: (i // steps_per_seq, 0, i % steps_per_seq, 0, 0))
    pos_spec = pl.BlockSpec((rows, LANES), lambda i: (i % steps_per_seq, 0))
    vt_shape = jax.ShapeDtypeStruct((batch, HEADS, nk, LANES, ATT_TILE), BF16)
    return pl.pallas_call(
        _proj_kernel,
        grid=(ntok // rows,),
        in_specs=[tok(D_MODEL), const(wp["w1"].shape), const(wp["wvt"].shape),
                  const(wp["wuq"].shape), const(wp["wuqs"].shape), const(wp["wukvk"].shape),
                  const(wp["wukvvt"].shape), const(wp["gq"].shape), const(wp["gkv"].shape),
                  pos_spec, pos_spec],
        out_specs=[tok(DIFF_WIDTH), tok(DIFF_WIDTH), vt_spec,
                   tok(HEADS * MLA_QK_PAD), tok(HEADS * MLA_QK_PAD), vt_spec],
        out_shape=[jax.ShapeDtypeStruct((ntok, DIFF_WIDTH), BF16),
                   jax.ShapeDtypeStruct((ntok, DIFF_WIDTH), BF16),
                   vt_shape,
                   jax.ShapeDtypeStruct((ntok, HEADS * MLA_QK_PAD), BF16),
                   jax.ShapeDtypeStruct((ntok, HEADS * MLA_QK_PAD), BF16),
                   vt_shape],
        compiler_params=pltpu.CompilerParams(dimension_semantics=("arbitrary",),
                                             vmem_limit_bytes=VMEM_LIMIT),
        name="proj",
    )(x2, wp["w1"], wp["wvt"], wp["wuq"], wp["wuqs"], wp["wukvk"], wp["wukvvt"],
      wp["gq"], wp["gkv"], wp["cos"], wp["sin"])


def _flash_step(st, vt, m_ref, l_ref, acc_ref):
    m_old = m_ref[...]
    m_new = jnp.maximum(m_old, jnp.max(st, axis=0, keepdims=True))
    alpha = jnp.exp(m_old - m_new)
    e = jnp.exp(st - m_new)
    l_ref[...] = alpha * l_ref[...] + jnp.sum(e, axis=0, keepdims=True)
    acc_ref[...] = alpha * acc_ref[...] + jnp.dot(vt, e.astype(BF16), preferred_element_type=F32)
    m_ref[...] = m_new


def _init_state(m_ref, l_ref, acc_ref):
    m_ref[...] = jnp.full(m_ref.shape, NEG_BIG, F32)
    l_ref[...] = jnp.zeros(l_ref.shape, F32)
    acc_ref[...] = jnp.zeros(acc_ref.shape, F32)


def _diff_attn_kernel(q_ref, k_ref, vt_ref, bias_ref, lam_ref, g_ref, o_ref,
                      m_ref, l_ref, acc_ref, *, lambda_init):
    qi = pl.program_id(2)
    tq = q_ref.shape[0]
    q = q_ref[...]
    lane = lax.broadcasted_iota(jnp.int32, q.shape, 1)
    zero = jnp.zeros_like(q)
    qs = jnp.concatenate([jnp.where(lane < DIFF_HEAD_DIM, q, zero),
                          jnp.where(lane >= DIFF_HEAD_DIM, q, zero)], axis=0)
    _init_state(m_ref, l_ref, acc_ref)

    def body(j, carry):
        k = k_ref[pl.ds(pl.multiple_of(j * ATT_TILE, ATT_TILE), ATT_TILE), :]
        st = lax.dot_general(k, qs, _NT, preferred_element_type=F32)
        st = st + bias_ref[0, qi - j]
        _flash_step(st, vt_ref[0, 0, j], m_ref, l_ref, acc_ref)
        return carry

    lax.fori_loop(0, qi + 1, body, 0)

    lp = lam_ref[...]
    lam = (jnp.exp(jnp.sum(lp[0:1] * lp[1:2], axis=-1, keepdims=True))
           - jnp.exp(jnp.sum(lp[2:3] * lp[3:4], axis=-1, keepdims=True)) + lambda_init)
    o = acc_ref[...] / l_ref[...]
    ot = o[:, :tq] - lam * o[:, tq:]
    y = _rms(ot.T, g_ref[...]) * (1.0 - lambda_init)
    o_ref[...] = y.astype(BF16)


def _mla_attn_kernel(q_ref, k_ref, vt_ref, o_ref, m_ref, l_ref, acc_ref):
    qi = pl.program_id(2)
    q = q_ref[...]
    _init_state(m_ref, l_ref, acc_ref)

    def scores(j):
        k = k_ref[pl.ds(pl.multiple_of(j * ATT_TILE, ATT_TILE), ATT_TILE), :]
        return lax.dot_general(k, q, _NT, preferred_element_type=F32)

    def body(j, carry):
        _flash_step(scores(j), vt_ref[0, 0, j], m_ref, l_ref, acc_ref)
        return carry

    lax.fori_loop(0, qi, body, 0)

    st = scores(qi)
    kc = lax.broadcasted_iota(jnp.int32, st.shape, 0) // CHUNK
    qc = lax.broadcasted_iota(jnp.int32, st.shape, 1) // CHUNK
    st = jnp.where(kc <= qc, st, NEG_BIG)
    _flash_step(st, vt_ref[0, 0, qi], m_ref, l_ref, acc_ref)

    o = acc_ref[...] / l_ref[...]
    o_ref[...] = o.T.astype(BF16)


def _attn_specs(seq, q_width):
    nq = seq // ATT_TILE
    q_spec = pl.BlockSpec((ATT_TILE, q_width), lambda b, h, i: (b * nq + i, h))
    k_spec = pl.BlockSpec((seq, q_width), lambda b, h, i: (b, h))
    vt_spec = pl.BlockSpec((1, 1, nq, LANES, ATT_TILE), lambda b, h, i: (b, h, 0, 0, 0))
    o_spec = pl.BlockSpec((ATT_TILE, LANES), lambda b, h, i: (b * nq + i, h))
    return nq, q_spec, k_spec, vt_spec, o_spec


def _diff_attn_call(dq, dk, dvt, bias, lam_params, g, seq, lambda_init):
    ntok = dq.shape[0]
    batch = ntok // seq
    nq, q_spec, k_spec, vt_spec, o_spec = _attn_specs(seq, LANES)
    return pl.pallas_call(
        functools.partial(_diff_attn_kernel, lambda_init=lambda_init),
        grid=(batch, HEADS, nq),
        in_specs=[q_spec, k_spec, vt_spec,
                  pl.BlockSpec((1, nq, ATT_TILE, 2 * ATT_TILE), lambda b, h, i: (h, 0, 0, 0)),
                  pl.BlockSpec(lam_params.shape, lambda b, h, i: (0, 0)),
                  pl.BlockSpec(g.shape, lambda b, h, i: (0, 0))],
        out_specs=o_spec,
        out_shape=jax.ShapeDtypeStruct((ntok, DIFF_WIDTH), BF16),
        scratch_shapes=[pltpu.VMEM((1, 2 * ATT_TILE), F32), pltpu.VMEM((1, 2 * ATT_TILE), F32),
                        pltpu.VMEM((DIFF_V_DIM, 2 * ATT_TILE), F32)],
        compiler_params=pltpu.CompilerParams(
            dimension_semantics=("arbitrary", "arbitrary", "arbitrary"),
            vmem_limit_bytes=VMEM_LIMIT),
        name="diff_attn",
    )(dq, dk, dvt, bias, lam_params, g)


def _mla_attn_call(qm, km, vmt, seq):
    ntok = qm.shape[0]
    batch = ntok // seq
    nq, q_spec, k_spec, vt_spec, o_spec = _attn_specs(seq, MLA_QK_PAD)
    return pl.pallas_call(
        _mla_attn_kernel,
        grid=(batch, HEADS, nq),
        in_specs=[q_spec, k_spec, vt_spec],
        out_specs=o_spec,
        out_shape=jax.ShapeDtypeStruct((ntok, MLA_WIDTH), BF16),
        scratch_shapes=[pltpu.VMEM((1, ATT_TILE), F32), pltpu.VMEM((1, ATT_TILE), F32),
                        pltpu.VMEM((MLA_V_DIM, ATT_TILE), F32)],
        compiler_params=pltpu.CompilerParams(
            dimension_semantics=("arbitrary", "arbitrary", "arbitrary"),
            vmem_limit_bytes=VMEM_LIMIT),
        name="mla_attn",
    )(qm, km, vmt)


def _post_kernel(x_ref, ma_ref, mb_ref, wo_ref, g1_ref, b1_ref, wi_ref, wout_ref, g2_ref, b2_ref,
                 o_ref):
    y = (jnp.dot(ma_ref[...], wo_ref[:DIFF_WIDTH, :], preferred_element_type=F32)
         + jnp.dot(mb_ref[...], wo_ref[DIFF_WIDTH:, :], preferred_element_type=F32))
    x1 = _layer_norm(ALPHA * x_ref[...] + y, g1_ref[...], b1_ref[...])
    x1b = x1.astype(BF16)
    y2 = jnp.zeros_like(x1)
    for c in range(D_FF // FF_CHUNK):
        cs = slice(c * FF_CHUNK, (c + 1) * FF_CHUNK)
        u = jnp.dot(x1b, wi_ref[:, cs], preferred_element_type=F32)
        u = jnp.square(jnp.maximum(u, 0.0)).astype(BF16)
        y2 = y2 + jnp.dot(u, wout_ref[cs, :], preferred_element_type=F32)
    o_ref[...] = _layer_norm(ALPHA * x1 + y2, g2_ref[...], b2_ref[...])


def _post_call(x2, mix_a, mix_b, wp):
    ntok = x2.shape[0]
    rows = POST_ROWS

    def const(shape):
        return pl.BlockSpec(shape, lambda i: (0,) * len(shape), pipeline_mode=pl.Buffered(1))

    def tok(width):
        return pl.BlockSpec((rows, width), lambda i: (i, 0))

    vec = const((1, D_MODEL))
    return pl.pallas_call(
        _post_kernel,
        grid=(ntok // rows,),
        in_specs=[tok(D_MODEL), tok(DIFF_WIDTH), tok(MLA_WIDTH), const(wp["wo"].shape), vec, vec,
                  const(wp["wi"].shape), const(wp["wout"].shape), vec, vec],
        out_specs=tok(D_MODEL),
        out_shape=jax.ShapeDtypeStruct((ntok, D_MODEL), F32),
        compiler_params=pltpu.CompilerParams(dimension_semantics=("arbitrary",),
                                             vmem_limit_bytes=VMEM_LIMIT),
        name="post",
    )(x2, mix_a, mix_b, wp["wo"], wp["g1"], wp["b1"], wp["wi"], wp["wout"], wp["g2"], wp["b2"])


def _t5_bucket(rel):
    nb = N_BUCKETS // 2
    ret = (rel > 0).astype(jnp.int32) * nb
    n = jnp.abs(rel)
    max_exact = nb // 2
    nf = jnp.maximum(n, 1).astype(F32)
    large = max_exact + (jnp.log(nf / max_exact) / math.log(MAX_DISTANCE / max_exact)
                         * (nb - max_exact)).astype(jnp.int32)
    large = jnp.minimum(large, nb - 1)
    return ret + jnp.where(n < max_exact, n, large)


def _bias_tiles(rel_bias, seq):
    nq = seq // ATT_TILE
    kk = jnp.arange(ATT_TILE, dtype=jnp.int32)[:, None]
    qq = jnp.arange(ATT_TILE, dtype=jnp.int32)[None, :]
    d = jnp.arange(nq, dtype=jnp.int32)[:, None, None]
    rel = kk[None] - (d * ATT_TILE + qq[None])
    bias = jnp.transpose(rel_bias[_t5_bucket(rel)], (3, 0, 1, 2)).astype(F32)
    allowed = (kk // CHUNK <= qq // CHUNK)[None, None] | (d > 0)[None]
    bias = jnp.where(allowed, bias, NEG_BIG)
    return jnp.concatenate([bias, bias], axis=-1)


def _rope_tables(seq):
    pos = jnp.arange(seq, dtype=F32)
    inv = ROPE_THETA ** (-jnp.arange(0, MLA_ROPE_DIM, 2, dtype=F32) / MLA_ROPE_DIM)
    ang = pos[:, None] * inv[None, :]
    cos, sin = jnp.cos(ang), jnp.sin(ang)
    z = jnp.zeros((seq, LANES - MLA_ROPE_DIM), F32)
    return (jnp.concatenate([cos, cos, z], axis=-1), jnp.concatenate([-sin, sin, z], axis=-1))


def _swap_halves(w):
    half = w.shape[-1] // 2
    return jnp.concatenate([w[..., half:], w[..., :half]], axis=-1)


def _pack_layer(l, w_in, subln_g, q_norm_g, w_uq, kv_norm_g, w_ukv, w_o, ln1_g, ln1_b,
                w_mlp_in, w_mlp_out, ln2_g, ln2_b, cos_t, sin_t):
    wl = w_in[l]
    o_k, o_v, o_cq = DIFF_WIDTH, 2 * DIFF_WIDTH, 3 * DIFF_WIDTH
    o_ckv = o_cq + MLA_Q_RANK
    o_kr = o_ckv + MLA_KV_RANK
    kr = wl[:, o_kr:]
    zk = jnp.zeros((D_MODEL, LANES - MLA_ROPE_DIM), F32)
    w1 = jnp.concatenate([wl[:, :o_k] * (DIFF_HEAD_DIM ** -0.5), wl[:, o_k:o_v], wl[:, o_cq:o_ckv],
                          wl[:, o_ckv:o_kr], kr, zk, _swap_halves(kr), zk], axis=1)
    mla_scale = (MLA_NOPE_DIM + MLA_ROPE_DIM) ** -0.5
    uq = (w_uq[l] * mla_scale).reshape(MLA_Q_RANK, HEADS, MLA_NOPE_DIM + MLA_ROPE_DIM)
    zq = jnp.zeros((MLA_Q_RANK, HEADS, MLA_QK_PAD - MLA_NOPE_DIM - MLA_ROPE_DIM), F32)
    wuq = jnp.concatenate([uq, zq], axis=-1).reshape(MLA_Q_RANK, HEADS * MLA_QK_PAD)
    wuqs = jnp.concatenate([_swap_halves(uq[..., MLA_NOPE_DIM:]), zq], axis=-1)
    wuqs = wuqs.reshape(MLA_Q_RANK, HEADS * LANES)
    ukv = w_ukv[l].reshape(MLA_KV_RANK, HEADS, MLA_NOPE_DIM + MLA_V_DIM)
    wukvk = ukv[..., :MLA_NOPE_DIM].reshape(MLA_KV_RANK, HEADS * MLA_NOPE_DIM)
    wukvvt = ukv[..., MLA_NOPE_DIM:].reshape(MLA_KV_RANK, HEADS * MLA_V_DIM).T
    row = lambda v: v[l].reshape(1, -1).astype(F32)
    return dict(
        w1=w1.astype(BF16), wvt=wl[:, o_v:o_cq].T.astype(BF16), wuq=wuq.astype(BF16),
        wuqs=wuqs.astype(BF16), wukvk=wukvk.astype(BF16), wukvvt=wukvvt.astype(BF16),
        gq=row(q_norm_g), gkv=row(kv_norm_g), cos=cos_t, sin=sin_t, subln=row(subln_g),
        wo=w_o[l].astype(BF16), g1=row(ln1_g), b1=row(ln1_b), wi=w_mlp_in[l].astype(BF16),
        wout=w_mlp_out[l].astype(BF16), g2=row(ln2_g), b2=row(ln2_b))


def kernel(x, w_in, lambda_q1, lambda_k1, lambda_q2, lambda_k2, subln_g, q_norm_g, w_uq, kv_norm_g,
           w_ukv, rel_bias, w_o, ln1_g, ln1_b, w_mlp_in, w_mlp_out, ln2_g, ln2_b):
    batch, seq, _ = x.shape
    assert seq % PROJ_ROWS == 0 and seq % ATT_TILE == 0 and (batch * seq) % POST_ROWS == 0
    cos_t, sin_t = _rope_tables(seq)
    bias = _bias_tiles(rel_bias, seq)
    x2 = x.reshape(batch * seq, D_MODEL)
    for l in range(DEPTH):
        lambda_init = 0.8 - 0.6 * math.exp(-0.3 * l)
        wp = _pack_layer(l, w_in, subln_g, q_norm_g, w_uq, kv_norm_g, w_ukv, w_o, ln1_g, ln1_b,
                         w_mlp_in, w_mlp_out, ln2_g, ln2_b, cos_t, sin_t)
        lam_params = jnp.stack([lambda_q1[l], lambda_k1[l], lambda_q2[l], lambda_k2[l]]).astype(F32)
        dq, dk, dvt, qm, km, vmt = _proj_call(x2, wp, seq)
        mix_a = _diff_attn_call(dq, dk, dvt, bias, lam_params, wp["subln"], seq, lambda_init)
        mix_b = _mla_attn_call(qm, km, vmt, seq)
        x2 = _post_call(x2, mix_a, mix_b, wp)
    return x2.reshape(batch, seq, D_MODEL)
```

```python
import functools
import math

import jax
import jax.numpy as jnp
from jax import lax
from jax.experimental import pallas as pl
from jax.experimental.pallas import tpu as pltpu

F32 = jnp.float32
BF16 = jnp.bfloat16

D_MODEL = 1024
DEPTH = 2
CHUNK = 64
HEADS = 4
DIFF_HEAD_DIM = 64
DIFF_V_DIM = 2 * DIFF_HEAD_DIM
DIFF_WIDTH = HEADS * DIFF_V_DIM
MLA_NOPE_DIM = 128
MLA_ROPE_DIM = 64
MLA_V_DIM = 128
MLA_Q_RANK = 256
MLA_KV_RANK = 128
MLA_WIDTH = HEADS * MLA_V_DIM
MLA_QK_PAD = 256
D_FF = 4 * D_MODEL
N_BUCKETS = 32
MAX_DISTANCE = 128
ROPE_THETA = 10000.0
ALPHA = (2 * DEPTH) ** 0.25
LN_EPS = 1e-5
RMS_EPS = 1e-6
NEG_BIG = -1e30

LANES = 128
ATT_TILE = 256
PROJ_ROWS = 512
POST_ROWS = 512
FF_CHUNK = 1024
VMEM_LIMIT = 56 * 1024 * 1024
assert ATT_TILE >= MAX_DISTANCE and ATT_TILE % CHUNK == 0

C_DQ = 0
C_DK = C_DQ + DIFF_WIDTH
C_CQ = C_DK + DIFF_WIDTH
C_CKV = C_CQ + MLA_Q_RANK
C_KR = C_CKV + MLA_KV_RANK
C_KRS = C_KR + LANES
W1_COLS = C_KRS + LANES

_NT = (((1,), (1,)), ((), ()))


def _rms(x, g):
    return x * lax.rsqrt(jnp.mean(x * x, axis=-1, keepdims=True) + RMS_EPS) * g


def _layer_norm(x, g, b):
    mu = jnp.mean(x, axis=-1, keepdims=True)
    xc = x - mu
    var = jnp.mean(xc * xc, axis=-1, keepdims=True)
    return xc * lax.rsqrt(var + LN_EPS) * g + b


def _proj_kernel(x_ref, w1_ref, wvt_ref, wuq_ref, wuqs_ref, wukvk_ref, wukvvt_ref,
                 gq_ref, gkv_ref, cos_ref, sin_ref,
                 dq_ref, dk_ref, dvt_ref, qm_ref, km_ref, vmt_ref):
    xb = x_ref[...].astype(BF16)
    h = jnp.dot(xb, w1_ref[...], preferred_element_type=F32)
    dq_ref[...] = h[:, C_DQ:C_DQ + DIFF_WIDTH].astype(BF16)
    dk_ref[...] = h[:, C_DK:C_DK + DIFF_WIDTH].astype(BF16)
    dvt_ref[0] = lax.dot_general(wvt_ref[...], xb, _NT, preferred_element_type=F32).astype(BF16)

    cos = cos_ref[...]
    sin = sin_ref[...]
    c_q = _rms(h[:, C_CQ:C_CQ + MLA_Q_RANK], gq_ref[...]).astype(BF16)
    qf = jnp.dot(c_q, wuq_ref[...], preferred_element_type=F32)
    qsw = jnp.dot(c_q, wuqs_ref[...], preferred_element_type=F32)
    c_kv = _rms(h[:, C_CKV:C_CKV + MLA_KV_RANK], gkv_ref[...]).astype(BF16)
    kn = jnp.dot(c_kv, wukvk_ref[...], preferred_element_type=F32)
    vmt_ref[0] = lax.dot_general(wukvvt_ref[...], c_kv, _NT,
                                 preferred_element_type=F32).astype(BF16)
    k_rope = (h[:, C_KR:C_KR + LANES] * cos + h[:, C_KRS:C_KRS + LANES] * sin).astype(BF16)

    for hh in range(HEADS):
        b0 = hh * MLA_QK_PAD
        qm_ref[:, b0:b0 + LANES] = qf[:, b0:b0 + LANES].astype(BF16)
        qm_ref[:, b0 + LANES:b0 + 2 * LANES] = (
            qf[:, b0 + LANES:b0 + 2 * LANES] * cos + qsw[:, hh * LANES:(hh + 1) * LANES] * sin
        ).astype(BF16)
        km_ref[:, b0:b0 + LANES] = kn[:, hh * LANES:(hh + 1) * LANES].astype(BF16)
        km_ref[:, b0 + LANES:b0 + 2 * LANES] = k_rope


def _proj_call(x2, wp, seq):
    ntok = x2.shape[0]
    batch = ntok // seq
    rows = PROJ_ROWS
    steps_per_seq = seq // rows

    def const(shape):
        return pl.BlockSpec(shape, lambda i: (0,) * len(shape))

    def tok(width):
        return pl.BlockSpec((rows, width), lambda i: (i, 0))

    vt_spec = pl.BlockSpec((1, HEADS * LANES, rows),
                           lambda i: (i // steps_per_seq, 0, i % steps_per_seq))
    pos_spec = pl.BlockSpec((rows, LANES), lambda i: (i % steps_per_seq, 0))
    vt_shape = jax.ShapeDtypeStruct((batch, HEADS * LANES, seq), BF16)
    return pl.pallas_call(
        _proj_kernel,
        grid=(ntok // rows,),
        in_specs=[tok(D_MODEL), const(wp["w1"].shape), const(wp["wvt"].shape),
                  const(wp["wuq"].shape), const(wp["wuqs"].shape), const(wp["wukvk"].shape),
                  const(wp["wukvvt"].shape), const(wp["gq"].shape), const(wp["gkv"].shape),
                  pos_spec, pos_spec],
        out_specs=[tok(DIFF_WIDTH), tok(DIFF_WIDTH), vt_spec,
                   tok(HEADS * MLA_QK_PAD), tok(HEADS * MLA_QK_PAD), vt_spec],
        out_shape=[jax.ShapeDtypeStruct((ntok, DIFF_WIDTH), BF16),
                   jax.ShapeDtypeStruct((ntok, DIFF_WIDTH), BF16),
                   vt_shape,
                   jax.ShapeDtypeStruct((ntok, HEADS * MLA_QK_PAD), BF16),
                   jax.ShapeDtypeStruct((ntok, HEADS * MLA_QK_PAD), BF16),
                   vt_shape],
        compiler_params=pltpu.CompilerParams(dimension_semantics=("arbitrary",),
                                             vmem_limit_bytes=VMEM_LIMIT),
        name="proj",
    )(x2, wp["w1"], wp["wvt"], wp["wuq"], wp["wuqs"], wp["wukvk"], wp["wukvvt"],
      wp["gq"], wp["gkv"], wp["cos"], wp["sin"])


def _col_max(a, b):
    m = jnp.max(b, axis=0, keepdims=True)
    return m if a is None else jnp.maximum(a, m)


def _softmax_pv(s_buf, e_buf, vt_ref, m, n_keys):
    e = jnp.exp(s_buf[0:n_keys, :] - m)
    l = jnp.sum(e, axis=0, keepdims=True)
    e_buf[0:n_keys, :] = e.astype(BF16)
    acc = jnp.dot(vt_ref[0, :, 0:n_keys], e_buf[0:n_keys, :], preferred_element_type=F32)
    return acc / l


def _diff_attn_kernel(q_ref, k_ref, vt_ref, bias_ref, lam_ref, g_ref, o_ref, s_scr, e_scr,
                      *, lambda_init):
    seq = q_ref.shape[0]
    t = ATT_TILE
    lane = lax.broadcasted_iota(jnp.int32, (t, LANES), 1)
    lp = lam_ref[...]
    lam = (jnp.exp(jnp.sum(lp[0:1] * lp[1:2], axis=-1, keepdims=True))
           - jnp.exp(jnp.sum(lp[2:3] * lp[3:4], axis=-1, keepdims=True)) + lambda_init)
    g = g_ref[...]
    for qi in range(seq // t):
        s_buf = s_scr.at[qi % 2]
        e_buf = e_scr.at[qi % 2]
        q = q_ref[qi * t:(qi + 1) * t, :]
        zero = jnp.zeros_like(q)
        qs = jnp.concatenate([jnp.where(lane < DIFF_HEAD_DIM, q, zero),
                              jnp.where(lane >= DIFF_HEAD_DIM, q, zero)], axis=0)
        m = None
        n_far = max(qi - 1, 0) * t
        if n_far > 0:
            s = lax.dot_general(k_ref[0:n_far, :], qs, _NT, preferred_element_type=F32)
            s_buf[0:n_far, :] = s
            m = _col_max(m, s)
        for d in ((1, 0) if qi >= 1 else (0,)):
            r0 = (qi - d) * t
            s = lax.dot_general(k_ref[r0:r0 + t, :], qs, _NT, preferred_element_type=F32)
            s = s + bias_ref[0, d]
            s_buf[r0:r0 + t, :] = s
            m = _col_max(m, s)
        o = _softmax_pv(s_buf, e_buf, vt_ref, m, (qi + 1) * t)
        ot = o[:, :t] - lam * o[:, t:]
        y = _rms(ot.T, g) * (1.0 - lambda_init)
        o_ref[qi * t:(qi + 1) * t, :] = y.astype(BF16)


def _mla_attn_kernel(q_ref, k_ref, vt_ref, o_ref, s_scr, e_scr):
    seq = q_ref.shape[0]
    t = ATT_TILE
    kc = lax.broadcasted_iota(jnp.int32, (t, t), 0) // CHUNK
    qc = lax.broadcasted_iota(jnp.int32, (t, t), 1) // CHUNK
    allowed = kc <= qc
    for qi in range(seq // t):
        s_buf = s_scr.at[qi % 2]
        e_buf = e_scr.at[qi % 2]
        q = q_ref[qi * t:(qi + 1) * t, :]
        m = None
        n_far = qi * t
        if n_far > 0:
            s = lax.dot_general(k_ref[0:n_far, :], q, _NT, preferred_element_type=F32)
            s_buf[0:n_far, :] = s
            m = _col_max(m, s)
        s = lax.dot_general(k_ref[n_far:n_far + t, :], q, _NT, preferred_element_type=F32)
        s = jnp.where(allowed, s, NEG_BIG)
        s_buf[n_far:n_far + t, :] = s
        m = _col_max(m, s)
        o = _softmax_pv(s_buf, e_buf, vt_ref, m, n_far + t)
        o_ref[qi * t:(qi + 1) * t, :] = o.T.astype(BF16)


def _attn_specs(seq, q_width):
    q_spec = pl.BlockSpec((seq, q_width), lambda b, h: (b, h))
    vt_spec = pl.BlockSpec((1, LANES, seq), lambda b, h: (b, h, 0))
    o_spec = pl.BlockSpec((seq, LANES), lambda b, h: (b, h))
    return q_spec, vt_spec, o_spec


def _attn_scratch(seq, n_queries):
    return [pltpu.VMEM((2, seq, n_queries), F32), pltpu.VMEM((2, seq, n_queries), BF16)]


_ATTN_PARAMS = pltpu.CompilerParams(dimension_semantics=("arbitrary", "arbitrary"),
                                    vmem_limit_bytes=VMEM_LIMIT)


def _diff_attn_call(dq, dk, dvt, bias, lam_params, g, seq, lambda_init):
    ntok = dq.shape[0]
    q_spec, vt_spec, o_spec = _attn_specs(seq, LANES)
    return pl.pallas_call(
        functools.partial(_diff_attn_kernel, lambda_init=lambda_init),
        grid=(ntok // seq, HEADS),
        in_specs=[q_spec, q_spec, vt_spec,
                  pl.BlockSpec((1,) + bias.shape[1:], lambda b, h: (h, 0, 0, 0)),
                  pl.BlockSpec(lam_params.shape, lambda b, h: (0, 0)),
                  pl.BlockSpec(g.shape, lambda b, h: (0, 0))],
        out_specs=o_spec,
        out_shape=jax.ShapeDtypeStruct((ntok, DIFF_WIDTH), BF16),
        scratch_shapes=_attn_scratch(seq, 2 * ATT_TILE),
        compiler_params=_ATTN_PARAMS,
        name="diff_attn",
    )(dq, dk, dvt, bias, lam_params, g)


def _mla_attn_call(qm, km, vmt, seq):
    ntok = qm.shape[0]
    q_spec, vt_spec, o_spec = _attn_specs(seq, MLA_QK_PAD)
    return pl.pallas_call(
        _mla_attn_kernel,
        grid=(ntok // seq, HEADS),
        in_specs=[q_spec, q_spec, vt_spec],
        out_specs=o_spec,
        out_shape=jax.ShapeDtypeStruct((ntok, MLA_WIDTH), BF16),
        scratch_shapes=_attn_scratch(seq, ATT_TILE),
        compiler_params=_ATTN_PARAMS,
        name="mla_attn",
    )(qm, km, vmt)


def _post_kernel(x_ref, ma_ref, mb_ref, wo_ref, g1_ref, b1_ref, wi_ref, wout_ref, g2_ref, b2_ref,
                 o_ref):
    y = (jnp.dot(ma_ref[...], wo_ref[:DIFF_WIDTH, :], preferred_element_type=F32)
         + jnp.dot(mb_ref[...], wo_ref[DIFF_WIDTH:, :], preferred_element_type=F32))
    x1 = _layer_norm(ALPHA * x_ref[...] + y, g1_ref[...], b1_ref[...])
    x1b = x1.astype(BF16)
    y2 = jnp.zeros_like(x1)
    for c in range(D_FF // FF_CHUNK):
        cs = slice(c * FF_CHUNK, (c + 1) * FF_CHUNK)
        u = jnp.dot(x1b, wi_ref[:, cs], preferred_element_type=F32)
        u = jnp.square(jnp.maximum(u, 0.0)).astype(BF16)
        y2 = y2 + jnp.dot(u, wout_ref[cs, :], preferred_element_type=F32)
    o_ref[...] = _layer_norm(ALPHA * x1 + y2, g2_ref[...], b2_ref[...])


def _post_call(x2, mix_a, mix_b, wp):
    ntok = x2.shape[0]
    rows = POST_ROWS

    def const(shape):
        return pl.BlockSpec(shape, lambda i: (0,) * len(shape), pipeline_mode=pl.Buffered(1))

    def tok(width):
        return pl.BlockSpec((rows, width), lambda i: (i, 0))

    vec = const((1, D_MODEL))
    return pl.pallas_call(
        _post_kernel,
        grid=(ntok // rows,),
        in_specs=[tok(D_MODEL), tok(DIFF_WIDTH), tok(MLA_WIDTH), const(wp["wo"].shape), vec, vec,
                  const(wp["wi"].shape), const(wp["wout"].shape), vec, vec],
        out_specs=tok(D_MODEL),
        out_shape=jax.ShapeDtypeStruct((ntok, D_MODEL), F32),
        compiler_params=pltpu.CompilerParams(dimension_semantics=("arbitrary",),
                                             vmem_limit_bytes=VMEM_LIMIT),
        name="post",
    )(x2, mix_a, mix_b, wp["wo"], wp["g1"], wp["b1"], wp["wi"], wp["wout"], wp["g2"], wp["b2"])


def _t5_bucket(rel):
    nb = N_BUCKETS // 2
    ret = (rel > 0).astype(jnp.int32) * nb
    n = jnp.abs(rel)
    max_exact = nb // 2
    nf = jnp.maximum(n, 1).astype(F32)
    large = max_exact + (jnp.log(nf / max_exact) / math.log(MAX_DISTANCE / max_exact)
                         * (nb - max_exact)).astype(jnp.int32)
    large = jnp.minimum(large, nb - 1)
    return ret + jnp.where(n < max_exact, n, large)


def _bias_tiles(rel_bias):
    kk = jnp.arange(ATT_TILE, dtype=jnp.int32)[:, None]
    qq = jnp.arange(ATT_TILE, dtype=jnp.int32)[None, :]
    d = jnp.arange(2, dtype=jnp.int32)[:, None, None]
    bucket = _t5_bucket(kk[None] - (d * ATT_TILE + qq[None]))
    onehot = bucket[..., None] == jnp.arange(N_BUCKETS, dtype=jnp.int32)
    table = rel_bias.astype(F32).T
    bias = jnp.sum(jnp.where(onehot[None], table[:, None, None, None, :], 0.0), axis=-1)
    far = table[:, N_BUCKETS // 2 - 1]
    bias = bias - far[:, None, None, None]
    allowed = (kk // CHUNK <= qq // CHUNK)[None, None] | (d > 0)[None]
    bias = jnp.where(allowed, bias, NEG_BIG)
    return jnp.concatenate([bias, bias], axis=-1)


def _rope_tables(seq):
    pos = jnp.arange(seq, dtype=F32)
    inv = ROPE_THETA ** (-jnp.arange(0, MLA_ROPE_DIM, 2, dtype=F32) / MLA_ROPE_DIM)
    ang = pos[:, None] * inv[None, :]
    cos, sin = jnp.cos(ang), jnp.sin(ang)
    z = jnp.zeros((seq, LANES - MLA_ROPE_DIM), F32)
    return (jnp.concatenate([cos, cos, z], axis=-1), jnp.concatenate([-sin, sin, z], axis=-1))


def _swap_halves(w):
    half = w.shape[-1] // 2
    return jnp.concatenate([w[..., half:], w[..., :half]], axis=-1)


def _pack_layer(l, w_in, subln_g, q_norm_g, w_uq, kv_norm_g, w_ukv, w_o, ln1_g, ln1_b,
                w_mlp_in, w_mlp_out, ln2_g, ln2_b, cos_t, sin_t):
    wl = w_in[l]
    o_k, o_v, o_cq = DIFF_WIDTH, 2 * DIFF_WIDTH, 3 * DIFF_WIDTH
    o_ckv = o_cq + MLA_Q_RANK
    o_kr = o_ckv + MLA_KV_RANK
    kr = wl[:, o_kr:]
    zk = jnp.zeros((D_MODEL, LANES - MLA_ROPE_DIM), F32)
    w1 = jnp.concatenate([wl[:, :o_k] * (DIFF_HEAD_DIM ** -0.5), wl[:, o_k:o_v], wl[:, o_cq:o_ckv],
                          wl[:, o_ckv:o_kr], kr, zk, _swap_halves(kr), zk], axis=1)
    mla_scale = (MLA_NOPE_DIM + MLA_ROPE_DIM) ** -0.5
    uq = (w_uq[l] * mla_scale).reshape(MLA_Q_RANK, HEADS, MLA_NOPE_DIM + MLA_ROPE_DIM)
    zq = jnp.zeros((MLA_Q_RANK, HEADS, MLA_QK_PAD - MLA_NOPE_DIM - MLA_ROPE_DIM), F32)
    wuq = jnp.concatenate([uq, zq], axis=-1).reshape(MLA_Q_RANK, HEADS * MLA_QK_PAD)
    wuqs = jnp.concatenate([_swap_halves(uq[..., MLA_NOPE_DIM:]), zq], axis=-1)
    wuqs = wuqs.reshape(MLA_Q_RANK, HEADS * LANES)
    ukv = w_ukv[l].reshape(MLA_KV_RANK, HEADS, MLA_NOPE_DIM + MLA_V_DIM)
    wukvk = ukv[..., :MLA_NOPE_DIM].reshape(MLA_KV_RANK, HEADS * MLA_NOPE_DIM)
    wukvvt = ukv[..., MLA_NOPE_DIM:].reshape(MLA_KV_RANK, HEADS * MLA_V_DIM).T
    row = lambda v: v[l].reshape(1, -1).astype(F32)
    return dict(
        w1=w1.astype(BF16), wvt=wl[:, o_v:o_cq].T.astype(BF16), wuq=wuq.astype(BF16),
        wuqs=wuqs.astype(BF16), wukvk=wukvk.astype(BF16), wukvvt=wukvvt.astype(BF16),
        gq=row(q_norm_g), gkv=row(kv_norm_g), cos=cos_t, sin=sin_t, subln=row(subln_g),
        wo=w_o[l].astype(BF16), g1=row(ln1_g), b1=row(ln1_b), wi=w_mlp_in[l].astype(BF16),
        wout=w_mlp_out[l].astype(BF16), g2=row(ln2_g), b2=row(ln2_b))


def kernel(x, w_in, lambda_q1, lambda_k1, lambda_q2, lambda_k2, subln_g, q_norm_g, w_uq, kv_norm_g,
           w_ukv, rel_bias, w_o, ln1_g, ln1_b, w_mlp_in, w_mlp_out, ln2_g, ln2_b):
    batch, seq, _ = x.shape
    assert seq % PROJ_ROWS == 0 and seq % ATT_TILE == 0 and (batch * seq) % POST_ROWS == 0
    cos_t, sin_t = _rope_tables(seq)
    bias = _bias_tiles(rel_bias)
    x2 = x.reshape(batch * seq, D_MODEL)
    for l in range(DEPTH):
        lambda_init = 0.8 - 0.6 * math.exp(-0.3 * l)
        wp = _pack_layer(l, w_in, subln_g, q_norm_g, w_uq, kv_norm_g, w_ukv, w_o, ln1_g, ln1_b,
                         w_mlp_in, w_mlp_out, ln2_g, ln2_b, cos_t, sin_t)
        lam_params = jnp.stack([lambda_q1[l], lambda_k1[l], lambda_q2[l], lambda_k2[l]]).astype(F32)
        dq, dk, dvt, qm, km, vmt = _proj_call(x2, wp, seq)
        mix_a = _diff_attn_call(dq, dk, dvt, bias, lam_params, wp["subln"], seq, lambda_init)
        mix_b = _mla_attn_call(qm, km, vmt, seq)
        x2 = _post_call(x2, mix_a, mix_b, wp)
    return x2.reshape(batch, seq, D_MODEL)
```

```python
import functools
import math

import jax
import jax.numpy as jnp
from jax import lax
from jax.experimental import pallas as pl
from jax.experimental.pallas import tpu as pltpu

F32 = jnp.float32
BF16 = jnp.bfloat16

D_MODEL = 1024
DEPTH = 2
CHUNK = 64
HEADS = 4
DIFF_HEAD_DIM = 64
DIFF_V_DIM = 2 * DIFF_HEAD_DIM
DIFF_WIDTH = HEADS * DIFF_V_DIM
MLA_NOPE_DIM = 128
MLA_ROPE_DIM = 64
MLA_V_DIM = 128
MLA_Q_RANK = 256
MLA_KV_RANK = 128
MLA_WIDTH = HEADS * MLA_V_DIM
MLA_QK_PAD = 256
D_FF = 4 * D_MODEL
N_BUCKETS = 32
MAX_DISTANCE = 128
ROPE_THETA = 10000.0
ALPHA = (2 * DEPTH) ** 0.25
LN_EPS = 1e-5
RMS_EPS = 1e-6
NEG_BIG = -1e30
LOG2E = math.log2(math.e)

LANES = 128
BF16_SUBLANES = 16
ATT_TILE = 256
PROJ_ROWS = 512
POST_ROWS = 512
FF_CHUNK = 1024
VMEM_LIMIT = 56 * 1024 * 1024
assert ATT_TILE >= MAX_DISTANCE and ATT_TILE % CHUNK == 0

C_DQ = 0
C_DK = C_DQ + DIFF_WIDTH
C_CQ = C_DK + DIFF_WIDTH
C_CKV = C_CQ + MLA_Q_RANK
C_KR = C_CKV + MLA_KV_RANK
C_KRS = C_KR + LANES
W1_COLS = C_KRS + LANES

_NT = (((1,), (1,)), ((), ()))


def _rms(x, g):
    return x * lax.rsqrt(jnp.mean(x * x, axis=-1, keepdims=True) + RMS_EPS) * g


def _layer_norm(x, g, b):
    mu = jnp.mean(x, axis=-1, keepdims=True)
    xc = x - mu
    var = jnp.mean(xc * xc, axis=-1, keepdims=True)
    return xc * lax.rsqrt(var + LN_EPS) * g + b


def _proj_kernel(x_ref, w1_ref, wvt_ref, wuq_ref, wuqs_ref, wukvk_ref, wukvvt_ref,
                 gq_ref, gkv_ref, cos_ref, sin_ref,
                 dq_ref, dk_ref, dvt_ref, qm_ref, km_ref, vmt_ref):
    xb = x_ref[...].astype(BF16)
    h = jnp.dot(xb, w1_ref[...], preferred_element_type=F32)
    dq_ref[...] = h[:, C_DQ:C_DQ + DIFF_WIDTH].astype(BF16)
    dk_ref[...] = h[:, C_DK:C_DK + DIFF_WIDTH].astype(BF16)
    dvt_ref[0] = lax.dot_general(wvt_ref[...], xb, _NT, preferred_element_type=F32).astype(BF16)

    cos = cos_ref[...]
    sin = sin_ref[...]
    c_q = _rms(h[:, C_CQ:C_CQ + MLA_Q_RANK], gq_ref[...]).astype(BF16)
    qf = jnp.dot(c_q, wuq_ref[...], preferred_element_type=F32)
    qsw = jnp.dot(c_q, wuqs_ref[...], preferred_element_type=F32)
    c_kv = _rms(h[:, C_CKV:C_CKV + MLA_KV_RANK], gkv_ref[...]).astype(BF16)
    kn = jnp.dot(c_kv, wukvk_ref[...], preferred_element_type=F32)
    vmt_ref[0] = lax.dot_general(wukvvt_ref[...], c_kv, _NT,
                                 preferred_element_type=F32).astype(BF16)
    k_rope = (h[:, C_KR:C_KR + LANES] * cos + h[:, C_KRS:C_KRS + LANES] * sin).astype(BF16)

    for hh in range(HEADS):
        b0 = hh * MLA_QK_PAD
        qm_ref[:, b0:b0 + LANES] = qf[:, b0:b0 + LANES].astype(BF16)
        qm_ref[:, b0 + LANES:b0 + 2 * LANES] = (
            qf[:, b0 + LANES:b0 + 2 * LANES] * cos + qsw[:, hh * LANES:(hh + 1) * LANES] * sin
        ).astype(BF16)
        km_ref[:, b0:b0 + LANES] = kn[:, hh * LANES:(hh + 1) * LANES].astype(BF16)
        km_ref[:, b0 + LANES:b0 + 2 * LANES] = k_rope


def _proj_call(x2, wp, seq):
    ntok = x2.shape[0]
    batch = ntok // seq
    rows = PROJ_ROWS
    steps_per_seq = seq // rows

    def const(shape):
        return pl.BlockSpec(shape, lambda i: (0,) * len(shape))

    def tok(width):
        return pl.BlockSpec((rows, width), lambda i: (i, 0))

    vt_spec = pl.BlockSpec((1, HEADS * LANES, rows),
                           lambda i: (i // steps_per_seq, 0, i % steps_per_seq))
    pos_spec = pl.BlockSpec((rows, LANES), lambda i: (i % steps_per_seq, 0))
    vt_shape = jax.ShapeDtypeStruct((batch, HEADS * LANES, seq), BF16)
    return pl.pallas_call(
        _proj_kernel,
        grid=(ntok // rows,),
        in_specs=[tok(D_MODEL), const(wp["w1"].shape), const(wp["wvt"].shape),
                  const(wp["wuq"].shape), const(wp["wuqs"].shape), const(wp["wukvk"].shape),
                  const(wp["wukvvt"].shape), const(wp["gq"].shape), const(wp["gkv"].shape),
                  pos_spec, pos_spec],
        out_specs=[tok(DIFF_WIDTH), tok(DIFF_WIDTH), vt_spec,
                   tok(HEADS * MLA_QK_PAD), tok(HEADS * MLA_QK_PAD), vt_spec],
        out_shape=[jax.ShapeDtypeStruct((ntok, DIFF_WIDTH), BF16),
                   jax.ShapeDtypeStruct((ntok, DIFF_WIDTH), BF16),
                   vt_shape,
                   jax.ShapeDtypeStruct((ntok, HEADS * MLA_QK_PAD), BF16),
                   jax.ShapeDtypeStruct((ntok, HEADS * MLA_QK_PAD), BF16),
                   vt_shape],
        compiler_params=pltpu.CompilerParams(dimension_semantics=("arbitrary",),
                                             vmem_limit_bytes=VMEM_LIMIT),
        name="proj",
    )(x2, wp["w1"], wp["wvt"], wp["wuq"], wp["wuqs"], wp["wukvk"], wp["wukvvt"],
      wp["gq"], wp["gkv"], wp["cos"], wp["sin"])


def _col_max(a, b):
    m = jnp.max(b, axis=0, keepdims=True)
    return m if a is None else jnp.maximum(a, m)


def _softmax_pv(s_buf, e_buf, vt_ref, m, n_keys):
    e_buf[0:n_keys, :] = jnp.exp2(s_buf[0:n_keys, :] - m).astype(BF16)
    vt = jnp.concatenate([vt_ref[0, :, 0:n_keys], jnp.ones((BF16_SUBLANES, n_keys), BF16)], axis=0)
    acc = jnp.dot(vt, e_buf[0:n_keys, :], preferred_element_type=F32)
    return acc[:LANES] / acc[LANES:LANES + 1]


def _attend(n_tiles, scores_fn, finish_fn, s_bufs, e_bufs, vt_ref):
    m_next = scores_fn(0, s_bufs[0])
    for qi in range(n_tiles):
        m = m_next
        if qi + 1 < n_tiles:
            m_next = scores_fn(qi + 1, s_bufs[(qi + 1) % 2])
        finish_fn(qi, _softmax_pv(s_bufs[qi % 2], e_bufs[qi % 2], vt_ref, m, (qi + 1) * ATT_TILE))


def _diff_attn_kernel(q_ref, k_ref, vt_ref, bias_ref, lam_ref, g_ref, o_ref, s0, s1, e0, e1,
                      *, lambda_init):
    t = ATT_TILE
    lane = lax.broadcasted_iota(jnp.int32, (t, LANES), 1)
    lp = lam_ref[...]
    lam = (jnp.exp(jnp.sum(lp[0:1] * lp[1:2], axis=-1, keepdims=True))
           - jnp.exp(jnp.sum(lp[2:3] * lp[3:4], axis=-1, keepdims=True)) + lambda_init)
    g = g_ref[...]

    def scores(qi, s_buf):
        q = q_ref[qi * t:(qi + 1) * t, :]
        zero = jnp.zeros_like(q)
        qs = jnp.concatenate([jnp.where(lane < DIFF_HEAD_DIM, q, zero),
                              jnp.where(lane >= DIFF_HEAD_DIM, q, zero)], axis=0)
        m = None
        n_far = max(qi - 1, 0) * t
        if n_far > 0:
            s = lax.dot_general(k_ref[0:n_far, :], qs, _NT, preferred_element_type=F32)
            s_buf[0:n_far, :] = s
            m = _col_max(m, s)
        for d in ((1, 0) if qi >= 1 else (0,)):
            r0 = (qi - d) * t
            s = lax.dot_general(k_ref[r0:r0 + t, :], qs, _NT, preferred_element_type=F32)
            s = s + bias_ref[0, d]
            s_buf[r0:r0 + t, :] = s
            m = _col_max(m, s)
        return m

    def finish(qi, o):
        ot = o[:, :t] - lam * o[:, t:]
        y = _rms(ot.T, g) * (1.0 - lambda_init)
        o_ref[qi * t:(qi + 1) * t, :] = y.astype(BF16)

    _attend(q_ref.shape[0] // t, scores, finish, (s0, s1), (e0, e1), vt_ref)


def _mla_attn_kernel(q_ref, k_ref, vt_ref, o_ref, s0, s1, e0, e1):
    t = ATT_TILE
    kc = lax.broadcasted_iota(jnp.int32, (t, t), 0) // CHUNK
    qc = lax.broadcasted_iota(jnp.int32, (t, t), 1) // CHUNK
    allowed = kc <= qc

    def scores(qi, s_buf):
        q = q_ref[qi * t:(qi + 1) * t, :]
        m = None
        n_far = qi * t
        if n_far > 0:
            s = lax.dot_general(k_ref[0:n_far, :], q, _NT, preferred_element_type=F32)
            s_buf[0:n_far, :] = s
            m = _col_max(m, s)
        s = lax.dot_general(k_ref[n_far:n_far + t, :], q, _NT, preferred_element_type=F32)
        s = jnp.where(allowed, s, NEG_BIG)
        s_buf[n_far:n_far + t, :] = s
        return _col_max(m, s)

    def finish(qi, o):
        o_ref[qi * t:(qi + 1) * t, :] = o.T.astype(BF16)

    _attend(q_ref.shape[0] // t, scores, finish, (s0, s1), (e0, e1), vt_ref)


def _attn_specs(seq, q_width):
    q_spec = pl.BlockSpec((seq, q_width), lambda b, h: (b, h))
    vt_spec = pl.BlockSpec((1, LANES, seq), lambda b, h: (b, h, 0))
    o_spec = pl.BlockSpec((seq, LANES), lambda b, h: (b, h))
    return q_spec, vt_spec, o_spec


def _attn_scratch(seq, n_queries):
    return [pltpu.VMEM((seq, n_queries), F32), pltpu.VMEM((seq, n_queries), F32),
            pltpu.VMEM((seq, n_queries), BF16), pltpu.VMEM((seq, n_queries), BF16)]


_ATTN_PARAMS = pltpu.CompilerParams(dimension_semantics=("arbitrary", "arbitrary"),
                                    vmem_limit_bytes=VMEM_LIMIT)


def _diff_attn_call(dq, dk, dvt, bias, lam_params, g, seq, lambda_init):
    ntok = dq.shape[0]
    q_spec, vt_spec, o_spec = _attn_specs(seq, LANES)
    return pl.pallas_call(
        functools.partial(_diff_attn_kernel, lambda_init=lambda_init),
        grid=(ntok // seq, HEADS),
        in_specs=[q_spec, q_spec, vt_spec,
                  pl.BlockSpec((1,) + bias.shape[1:], lambda b, h: (h, 0, 0, 0)),
                  pl.BlockSpec(lam_params.shape, lambda b, h: (0, 0)),
                  pl.BlockSpec(g.shape, lambda b, h: (0, 0))],
        out_specs=o_spec,
        out_shape=jax.ShapeDtypeStruct((ntok, DIFF_WIDTH), BF16),
        scratch_shapes=_attn_scratch(seq, 2 * ATT_TILE),
        compiler_params=_ATTN_PARAMS,
        name="diff_attn",
    )(dq, dk, dvt, bias, lam_params, g)


def _mla_attn_call(qm, km, vmt, seq):
    ntok = qm.shape[0]
    q_spec, vt_spec, o_spec = _attn_specs(seq, MLA_QK_PAD)
    return pl.pallas_call(
        _mla_attn_kernel,
        grid=(ntok // seq, HEADS),
        in_specs=[q_spec, q_spec, vt_spec],
        out_specs=o_spec,
        out_shape=jax.ShapeDtypeStruct((ntok, MLA_WIDTH), BF16),
        scratch_shapes=_attn_scratch(seq, ATT_TILE),
        compiler_params=_ATTN_PARAMS,
        name="mla_attn",
    )(qm, km, vmt)


def _post_kernel(x_ref, ma_ref, mb_ref, wo_ref, g1_ref, b1_ref, wi_ref, wout_ref, g2_ref, b2_ref,
                 o_ref):
    y = (jnp.dot(ma_ref[...], wo_ref[:DIFF_WIDTH, :], preferred_element_type=F32)
         + jnp.dot(mb_ref[...], wo_ref[DIFF_WIDTH:, :], preferred_element_type=F32))
    x1 = _layer_norm(ALPHA * x_ref[...] + y, g1_ref[...], b1_ref[...])
    x1b = x1.astype(BF16)
    y2 = jnp.zeros_like(x1)
    for c in range(D_FF // FF_CHUNK):
        cs = slice(c * FF_CHUNK, (c + 1) * FF_CHUNK)
        u = jnp.dot(x1b, wi_ref[:, cs], preferred_element_type=F32)
        u = jnp.square(jnp.maximum(u, 0.0)).astype(BF16)
        y2 = y2 + jnp.dot(u, wout_ref[cs, :], preferred_element_type=F32)
    o_ref[...] = _layer_norm(ALPHA * x1 + y2, g2_ref[...], b2_ref[...])


def _post_call(x2, mix_a, mix_b, wp):
    ntok = x2.shape[0]
    rows = POST_ROWS

    def const(shape):
        return pl.BlockSpec(shape, lambda i: (0,) * len(shape), pipeline_mode=pl.Buffered(1))

    def tok(width):
        return pl.BlockSpec((rows, width), lambda i: (i, 0))

    vec = const((1, D_MODEL))
    return pl.pallas_call(
        _post_kernel,
        grid=(ntok // rows,),
        in_specs=[tok(D_MODEL), tok(DIFF_WIDTH), tok(MLA_WIDTH), const(wp["wo"].shape), vec, vec,
                  const(wp["wi"].shape), const(wp["wout"].shape), vec, vec],
        out_specs=tok(D_MODEL),
        out_shape=jax.ShapeDtypeStruct((ntok, D_MODEL), F32),
        compiler_params=pltpu.CompilerParams(dimension_semantics=("arbitrary",),
                                             vmem_limit_bytes=VMEM_LIMIT),
        name="post",
    )(x2, mix_a, mix_b, wp["wo"], wp["g1"], wp["b1"], wp["wi"], wp["wout"], wp["g2"], wp["b2"])


def _t5_bucket(rel):
    nb = N_BUCKETS // 2
    ret = (rel > 0).astype(jnp.int32) * nb
    n = jnp.abs(rel)
    max_exact = nb // 2
    nf = jnp.maximum(n, 1).astype(F32)
    large = max_exact + (jnp.log(nf / max_exact) / math.log(MAX_DISTANCE / max_exact)
                         * (nb - max_exact)).astype(jnp.int32)
    large = jnp.minimum(large, nb - 1)
    return ret + jnp.where(n < max_exact, n, large)


def _bias_tiles(rel_bias):
    t = ATT_TILE
    p = 2 * t
    table = rel_bias.astype(F32).T
    far = table[:, N_BUCKETS // 2 - 1]
    w = jnp.arange(p, dtype=jnp.int32)[None, :]
    d = jnp.arange(2, dtype=jnp.int32)[:, None]
    rel = (t - 1) - w - d * t
    onehot = _t5_bucket(rel)[..., None] == jnp.arange(N_BUCKETS, dtype=jnp.int32)
    vec = jnp.sum(jnp.where(onehot[None], table[:, None, None, :], 0.0), axis=-1)
    vec = (vec - far[:, None, None]) * LOG2E
    rows = jnp.tile(vec, (1, 1, t))[..., :t * (p - 1)].reshape(HEADS, 2, t, p - 1)
    bias = rows[..., t - 1:2 * t - 1]
    kk = jnp.arange(t, dtype=jnp.int32)[:, None]
    qq = jnp.arange(t, dtype=jnp.int32)[None, :]
    allowed = (kk // CHUNK <= qq // CHUNK)[None, None] | (d > 0)[None, :, :, None]
    bias = jnp.where(allowed, bias, NEG_BIG)
    return jnp.concatenate([bias, bias], axis=-1)


def _rope_tables(seq):
    pos = jnp.arange(seq, dtype=F32)
    inv = ROPE_THETA ** (-jnp.arange(0, MLA_ROPE_DIM, 2, dtype=F32) / MLA_ROPE_DIM)
    ang = pos[:, None] * inv[None, :]
    cos, sin = jnp.cos(ang), jnp.sin(ang)
    z = jnp.zeros((seq, LANES - MLA_ROPE_DIM), F32)
    return (jnp.concatenate([cos, cos, z], axis=-1), jnp.concatenate([-sin, sin, z], axis=-1))


def _swap_halves(w):
    half = w.shape[-1] // 2
    return jnp.concatenate([w[..., half:], w[..., :half]], axis=-1)


def _pack_layer(l, w_in, subln_g, q_norm_g, w_uq, kv_norm_g, w_ukv, w_o, ln1_g, ln1_b,
                w_mlp_in, w_mlp_out, ln2_g, ln2_b, cos_t, sin_t):
    wl = w_in[l]
    o_k, o_v, o_cq = DIFF_WIDTH, 2 * DIFF_WIDTH, 3 * DIFF_WIDTH
    o_ckv = o_cq + MLA_Q_RANK
    o_kr = o_ckv + MLA_KV_RANK
    kr = wl[:, o_kr:]
    zk = jnp.zeros((D_MODEL, LANES - MLA_ROPE_DIM), F32)
    w1 = jnp.concatenate([wl[:, :o_k] * (DIFF_HEAD_DIM ** -0.5 * LOG2E), wl[:, o_k:o_v],
                          wl[:, o_cq:o_ckv], wl[:, o_ckv:o_kr], kr, zk, _swap_halves(kr), zk],
                         axis=1)
    mla_scale = (MLA_NOPE_DIM + MLA_ROPE_DIM) ** -0.5 * LOG2E
    uq = (w_uq[l] * mla_scale).reshape(MLA_Q_RANK, HEADS, MLA_NOPE_DIM + MLA_ROPE_DIM)
    zq = jnp.zeros((MLA_Q_RANK, HEADS, MLA_QK_PAD - MLA_NOPE_DIM - MLA_ROPE_DIM), F32)
    wuq = jnp.concatenate([uq, zq], axis=-1).reshape(MLA_Q_RANK, HEADS * MLA_QK_PAD)
    wuqs = jnp.concatenate([_swap_halves(uq[..., MLA_NOPE_DIM:]), zq], axis=-1)
    wuqs = wuqs.reshape(MLA_Q_RANK, HEADS * LANES)
    ukv = w_ukv[l].reshape(MLA_KV_RANK, HEADS, MLA_NOPE_DIM + MLA_V_DIM)
    wukvk = ukv[..., :MLA_NOPE_DIM].reshape(MLA_KV_RANK, HEADS * MLA_NOPE_DIM)
    wukvvt = ukv[..., MLA_NOPE_DIM:].reshape(MLA_KV_RANK, HEADS * MLA_V_DIM).T
    row = lambda v: v[l].reshape(1, -1).astype(F32)
    return dict(
        w1=w1.astype(BF16), wvt=wl[:, o_v:o_cq].T.astype(BF16), wuq=wuq.astype(BF16),
        wuqs=wuqs.astype(BF16), wukvk=wukvk.astype(BF16), wukvvt=wukvvt.astype(BF16),
        gq=row(q_norm_g), gkv=row(kv_norm_g), cos=cos_t, sin=sin_t, subln=row(subln_g),
        wo=w_o[l].astype(BF16), g1=row(ln1_g), b1=row(ln1_b), wi=w_mlp_in[l].astype(BF16),
        wout=w_mlp_out[l].astype(BF16), g2=row(ln2_g), b2=row(ln2_b))


def kernel(x, w_in, lambda_q1, lambda_k1, lambda_q2, lambda_k2, subln_g, q_norm_g, w_uq, kv_norm_g,
           w_ukv, rel_bias, w_o, ln1_g, ln1_b, w_mlp_in, w_mlp_out, ln2_g, ln2_b):
    batch, seq, _ = x.shape
    assert seq % PROJ_ROWS == 0 and seq % ATT_TILE == 0 and (batch * seq) % POST_ROWS == 0
    cos_t, sin_t = _rope_tables(seq)
    bias = _bias_tiles(rel_bias)
    x2 = x.reshape(batch * seq, D_MODEL)
    for l in range(DEPTH):
        lambda_init = 0.8 - 0.6 * math.exp(-0.3 * l)
        wp = _pack_layer(l, w_in, subln_g, q_norm_g, w_uq, kv_norm_g, w_ukv, w_o, ln1_g, ln1_b,
                         w_mlp_in, w_mlp_out, ln2_g, ln2_b, cos_t, sin_t)
        lam_params = jnp.stack([lambda_q1[l], lambda_k1[l], lambda_q2[l], lambda_k2[l]]).astype(F32)
        dq, dk, dvt, qm, km, vmt = _proj_call(x2, wp, seq)
        mix_a = _diff_attn_call(dq, dk, dvt, bias, lam_params, wp["subln"], seq, lambda_init)
        mix_b = _mla_attn_call(qm, km, vmt, seq)
        x2 = _post_call(x2, mix_a, mix_b, wp)
    return x2.reshape(batch, seq, D_MODEL)
```

```python
import functools
import math

import jax
import jax.numpy as jnp
from jax import lax
from jax.experimental import pallas as pl
from jax.experimental.pallas import tpu as pltpu

F32 = jnp.float32
BF16 = jnp.bfloat16

D_MODEL = 1024
DEPTH = 2
CHUNK = 64
HEADS = 4
DIFF_HEAD_DIM = 64
DIFF_V_DIM = 2 * DIFF_HEAD_DIM
DIFF_WIDTH = HEADS * DIFF_V_DIM
MLA_NOPE_DIM = 128
MLA_ROPE_DIM = 64
MLA_V_DIM = 128
MLA_Q_RANK = 256
MLA_KV_RANK = 128
MLA_WIDTH = HEADS * MLA_V_DIM
MLA_QK_PAD = 256
D_FF = 4 * D_MODEL
N_BUCKETS = 32
MAX_DISTANCE = 128
ROPE_THETA = 10000.0
ALPHA = (2 * DEPTH) ** 0.25
LN_EPS = 1e-5
RMS_EPS = 1e-6
NEG_BIG = -1e30
LOG2E = math.log2(math.e)

LANES = 128
BF16_SUBLANES = 16
ATT_TILE = 256
PROJ_ROWS = 1024
POST_ROWS = 1024
FF_CHUNK = 1024
VMEM_LIMIT = 56 * 1024 * 1024
assert ATT_TILE >= MAX_DISTANCE and ATT_TILE % CHUNK == 0

C_DQ = 0
C_DK = C_DQ + DIFF_WIDTH
C_CQ = C_DK + DIFF_WIDTH
C_CKV = C_CQ + MLA_Q_RANK
C_KR = C_CKV + MLA_KV_RANK
C_KRS = C_KR + LANES
W1_COLS = C_KRS + LANES

_NT = (((1,), (1,)), ((), ()))


def _rms(x, g):
    return x * lax.rsqrt(jnp.mean(x * x, axis=-1, keepdims=True) + RMS_EPS) * g


def _layer_norm(x, g, b):
    mu = jnp.mean(x, axis=-1, keepdims=True)
    xc = x - mu
    var = jnp.mean(xc * xc, axis=-1, keepdims=True)
    return xc * lax.rsqrt(var + LN_EPS) * g + b


def _layer_spec(arr, l, **kwargs):
    index = (l,) + (0,) * (arr.ndim - 1)
    return pl.BlockSpec((1,) + arr.shape[1:], lambda *_: index, **kwargs)


def _proj_kernel(x_ref, w1_ref, wvt_ref, wuq_ref, wuqs_ref, wukvk_ref, wukvvt_ref,
                 gq_ref, gkv_ref, cos_ref, sin_ref,
                 dq_ref, dk_ref, dvt_ref, qm_ref, km_ref, vmt_ref):
    xb = x_ref[...].astype(BF16)
    h = jnp.dot(xb, w1_ref[0], preferred_element_type=F32)
    dq_ref[...] = h[:, C_DQ:C_DQ + DIFF_WIDTH].astype(BF16)
    dk_ref[...] = h[:, C_DK:C_DK + DIFF_WIDTH].astype(BF16)
    dvt_ref[0] = lax.dot_general(wvt_ref[0], xb, _NT, preferred_element_type=F32).astype(BF16)

    cos = cos_ref[...]
    sin = sin_ref[...]
    c_q = _rms(h[:, C_CQ:C_CQ + MLA_Q_RANK], gq_ref[0]).astype(BF16)
    qf = jnp.dot(c_q, wuq_ref[0], preferred_element_type=F32)
    qsw = jnp.dot(c_q, wuqs_ref[0], preferred_element_type=F32)
    c_kv = _rms(h[:, C_CKV:C_CKV + MLA_KV_RANK], gkv_ref[0]).astype(BF16)
    kn = jnp.dot(c_kv, wukvk_ref[0], preferred_element_type=F32)
    vmt_ref[0] = lax.dot_general(wukvvt_ref[0], c_kv, _NT,
                                 preferred_element_type=F32).astype(BF16)
    k_rope = (h[:, C_KR:C_KR + LANES] * cos + h[:, C_KRS:C_KRS + LANES] * sin).astype(BF16)

    for hh in range(HEADS):
        b0 = hh * MLA_QK_PAD
        qm_ref[:, b0:b0 + LANES] = qf[:, b0:b0 + LANES].astype(BF16)
        qm_ref[:, b0 + LANES:b0 + 2 * LANES] = (
            qf[:, b0 + LANES:b0 + 2 * LANES] * cos + qsw[:, hh * LANES:(hh + 1) * LANES] * sin
        ).astype(BF16)
        km_ref[:, b0:b0 + LANES] = kn[:, hh * LANES:(hh + 1) * LANES].astype(BF16)
        km_ref[:, b0 + LANES:b0 + 2 * LANES] = k_rope


def _proj_call(x2, wp, l, seq):
    ntok = x2.shape[0]
    batch = ntok // seq
    rows = PROJ_ROWS
    steps_per_seq = seq // rows

    def const(name):
        return _layer_spec(wp[name], l)

    def tok(width):
        return pl.BlockSpec((rows, width), lambda i: (i, 0))

    vt_spec = pl.BlockSpec((1, HEADS * LANES, rows),
                           lambda i: (i // steps_per_seq, 0, i % steps_per_seq))
    pos_spec = pl.BlockSpec((rows, LANES), lambda i: (i % steps_per_seq, 0))
    vt_shape = jax.ShapeDtypeStruct((batch, HEADS * LANES, seq), BF16)
    return pl.pallas_call(
        _proj_kernel,
        grid=(ntok // rows,),
        in_specs=[tok(D_MODEL), const("w1"), const("wvt"), const("wuq"), const("wuqs"),
                  const("wukvk"), const("wukvvt"), const("gq"), const("gkv"), pos_spec, pos_spec],
        out_specs=[tok(DIFF_WIDTH), tok(DIFF_WIDTH), vt_spec,
                   tok(HEADS * MLA_QK_PAD), tok(HEADS * MLA_QK_PAD), vt_spec],
        out_shape=[jax.ShapeDtypeStruct((ntok, DIFF_WIDTH), BF16),
                   jax.ShapeDtypeStruct((ntok, DIFF_WIDTH), BF16),
                   vt_shape,
                   jax.ShapeDtypeStruct((ntok, HEADS * MLA_QK_PAD), BF16),
                   jax.ShapeDtypeStruct((ntok, HEADS * MLA_QK_PAD), BF16),
                   vt_shape],
        compiler_params=pltpu.CompilerParams(dimension_semantics=("arbitrary",),
                                             vmem_limit_bytes=VMEM_LIMIT),
        name="proj",
    )(x2, wp["w1"], wp["wvt"], wp["wuq"], wp["wuqs"], wp["wukvk"], wp["wukvvt"],
      wp["gq"], wp["gkv"], wp["cos"], wp["sin"])


def _col_max(a, b):
    m = jnp.max(b, axis=0, keepdims=True)
    return m if a is None else jnp.maximum(a, m)


def _softmax_pv(s_buf, e_buf, vt_ref, m, n_keys):
    e_buf[0:n_keys, :] = jnp.exp2(s_buf[0:n_keys, :] - m).astype(BF16)
    vt = jnp.concatenate([vt_ref[0, :, 0:n_keys], jnp.ones((BF16_SUBLANES, n_keys), BF16)], axis=0)
    acc = jnp.dot(vt, e_buf[0:n_keys, :], preferred_element_type=F32)
    return acc[:LANES] / acc[LANES:LANES + 1]


def _attend(n_tiles, scores_fn, finish_fn, s_bufs, e_bufs, vt_ref):
    m_next = scores_fn(0, s_bufs[0])
    for qi in range(n_tiles):
        m = m_next
        if qi + 1 < n_tiles:
            m_next = scores_fn(qi + 1, s_bufs[(qi + 1) % 2])
        finish_fn(qi, _softmax_pv(s_bufs[qi % 2], e_bufs[qi % 2], vt_ref, m, (qi + 1) * ATT_TILE))


def _diff_attn_kernel(q_ref, k_ref, vt_ref, bias_ref, lam_ref, g_ref, o_ref, s0, s1, e0, e1,
                      *, lambda_init):
    t = ATT_TILE
    lane = lax.broadcasted_iota(jnp.int32, (t, LANES), 1)
    lp = lam_ref[0]
    lam = (jnp.exp(jnp.sum(lp[0:1] * lp[1:2], axis=-1, keepdims=True))
           - jnp.exp(jnp.sum(lp[2:3] * lp[3:4], axis=-1, keepdims=True)) + lambda_init)
    g = g_ref[0]

    def scores(qi, s_buf):
        q = q_ref[qi * t:(qi + 1) * t, :]
        zero = jnp.zeros_like(q)
        qs = jnp.concatenate([jnp.where(lane < DIFF_HEAD_DIM, q, zero),
                              jnp.where(lane >= DIFF_HEAD_DIM, q, zero)], axis=0)
        m = None
        n_far = max(qi - 1, 0) * t
        if n_far > 0:
            s = lax.dot_general(k_ref[0:n_far, :], qs, _NT, preferred_element_type=F32)
            s_buf[0:n_far, :] = s
            m = _col_max(m, s)
        for d in ((1, 0) if qi >= 1 else (0,)):
            r0 = (qi - d) * t
            s = lax.dot_general(k_ref[r0:r0 + t, :], qs, _NT, preferred_element_type=F32)
            s = s + bias_ref[0, d]
            s_buf[r0:r0 + t, :] = s
            m = _col_max(m, s)
        return m

    def finish(qi, o):
        ot = o[:, :t] - lam * o[:, t:]
        y = _rms(ot.T, g) * (1.0 - lambda_init)
        o_ref[qi * t:(qi + 1) * t, :] = y.astype(BF16)

    _attend(q_ref.shape[0] // t, scores, finish, (s0, s1), (e0, e1), vt_ref)


def _mla_attn_kernel(q_ref, k_ref, vt_ref, o_ref, s0, s1, e0, e1):
    t = ATT_TILE
    kc = lax.broadcasted_iota(jnp.int32, (t, t), 0) // CHUNK
    qc = lax.broadcasted_iota(jnp.int32, (t, t), 1) // CHUNK
    allowed = kc <= qc

    def scores(qi, s_buf):
        q = q_ref[qi * t:(qi + 1) * t, :]
        m = None
        n_far = qi * t
        if n_far > 0:
            s = lax.dot_general(k_ref[0:n_far, :], q, _NT, preferred_element_type=F32)
            s_buf[0:n_far, :] = s
            m = _col_max(m, s)
        s = lax.dot_general(k_ref[n_far:n_far + t, :], q, _NT, preferred_element_type=F32)
        s = jnp.where(allowed, s, NEG_BIG)
        s_buf[n_far:n_far + t, :] = s
        return _col_max(m, s)

    def finish(qi, o):
        o_ref[qi * t:(qi + 1) * t, :] = o.T.astype(BF16)

    _attend(q_ref.shape[0] // t, scores, finish, (s0, s1), (e0, e1), vt_ref)


def _attn_specs(seq, q_width):
    q_spec = pl.BlockSpec((seq, q_width), lambda b, h: (b, h))
    vt_spec = pl.BlockSpec((1, LANES, seq), lambda b, h: (b, h, 0))
    o_spec = pl.BlockSpec((seq, LANES), lambda b, h: (b, h))
    return q_spec, vt_spec, o_spec


def _attn_scratch(seq, n_queries):
    return [pltpu.VMEM((seq, n_queries), F32), pltpu.VMEM((seq, n_queries), F32),
            pltpu.VMEM((seq, n_queries), BF16), pltpu.VMEM((seq, n_queries), BF16)]


_ATTN_PARAMS = pltpu.CompilerParams(dimension_semantics=("arbitrary", "arbitrary"),
                                    vmem_limit_bytes=VMEM_LIMIT)


def _diff_attn_call(dq, dk, dvt, bias, wp, l, seq, lambda_init):
    ntok = dq.shape[0]
    q_spec, vt_spec, o_spec = _attn_specs(seq, LANES)
    return pl.pallas_call(
        functools.partial(_diff_attn_kernel, lambda_init=lambda_init),
        grid=(ntok // seq, HEADS),
        in_specs=[q_spec, q_spec, vt_spec,
                  pl.BlockSpec((1,) + bias.shape[1:], lambda b, h: (h, 0, 0, 0)),
                  _layer_spec(wp["lam"], l), _layer_spec(wp["subln"], l)],
        out_specs=o_spec,
        out_shape=jax.ShapeDtypeStruct((ntok, DIFF_WIDTH), BF16),
        scratch_shapes=_attn_scratch(seq, 2 * ATT_TILE),
        compiler_params=_ATTN_PARAMS,
        name="diff_attn",
    )(dq, dk, dvt, bias, wp["lam"], wp["subln"])


def _mla_attn_call(qm, km, vmt, seq):
    ntok = qm.shape[0]
    q_spec, vt_spec, o_spec = _attn_specs(seq, MLA_QK_PAD)
    return pl.pallas_call(
        _mla_attn_kernel,
        grid=(ntok // seq, HEADS),
        in_specs=[q_spec, q_spec, vt_spec],
        out_specs=o_spec,
        out_shape=jax.ShapeDtypeStruct((ntok, MLA_WIDTH), BF16),
        scratch_shapes=_attn_scratch(seq, ATT_TILE),
        compiler_params=_ATTN_PARAMS,
        name="mla_attn",
    )(qm, km, vmt)


def _post_kernel(x_ref, ma_ref, mb_ref, wo_ref, g1_ref, b1_ref, wi_ref, wout_ref, g2_ref, b2_ref,
                 o_ref):
    y = (jnp.dot(ma_ref[...], wo_ref[0, :DIFF_WIDTH, :], preferred_element_type=F32)
         + jnp.dot(mb_ref[...], wo_ref[0, DIFF_WIDTH:, :], preferred_element_type=F32))
    x1 = _layer_norm(ALPHA * x_ref[...] + y, g1_ref[0], b1_ref[0])
    x1b = x1.astype(BF16)
    y2 = jnp.zeros_like(x1)
    for c in range(D_FF // FF_CHUNK):
        cs = slice(c * FF_CHUNK, (c + 1) * FF_CHUNK)
        u = jnp.dot(x1b, wi_ref[0, :, cs], preferred_element_type=F32)
        u = jnp.square(jnp.maximum(u, 0.0)).astype(BF16)
        y2 = y2 + jnp.dot(u, wout_ref[0, cs, :], preferred_element_type=F32)
    o_ref[...] = _layer_norm(ALPHA * x1 + y2, g2_ref[0], b2_ref[0])


def _post_call(x2, mix_a, mix_b, wp, l):
    ntok = x2.shape[0]
    rows = POST_ROWS

    def const(name):
        return _layer_spec(wp[name], l, pipeline_mode=pl.Buffered(1))

    def tok(width):
        return pl.BlockSpec((rows, width), lambda i: (i, 0))

    return pl.pallas_call(
        _post_kernel,
        grid=(ntok // rows,),
        in_specs=[tok(D_MODEL), tok(DIFF_WIDTH), tok(MLA_WIDTH), const("wo"), const("g1"),
                  const("b1"), const("wi"), const("wout"), const("g2"), const("b2")],
        out_specs=tok(D_MODEL),
        out_shape=jax.ShapeDtypeStruct((ntok, D_MODEL), F32),
        compiler_params=pltpu.CompilerParams(dimension_semantics=("arbitrary",),
                                             vmem_limit_bytes=VMEM_LIMIT),
        name="post",
    )(x2, mix_a, mix_b, wp["wo"], wp["g1"], wp["b1"], wp["wi"], wp["wout"], wp["g2"], wp["b2"])


def _t5_bucket(rel):
    nb = N_BUCKETS // 2
    ret = (rel > 0).astype(jnp.int32) * nb
    n = jnp.abs(rel)
    max_exact = nb // 2
    nf = jnp.maximum(n, 1).astype(F32)
    large = max_exact + (jnp.log(nf / max_exact) / math.log(MAX_DISTANCE / max_exact)
                         * (nb - max_exact)).astype(jnp.int32)
    large = jnp.minimum(large, nb - 1)
    return ret + jnp.where(n < max_exact, n, large)


def _bias_tiles(rel_bias):
    t = ATT_TILE
    p = 2 * t
    table = rel_bias.astype(F32).T
    far = table[:, N_BUCKETS // 2 - 1]
    w = jnp.arange(p, dtype=jnp.int32)[None, :]
    d = jnp.arange(2, dtype=jnp.int32)[:, None]
    rel = (t - 1) - w - d * t
    onehot = _t5_bucket(rel)[..., None] == jnp.arange(N_BUCKETS, dtype=jnp.int32)
    vec = jnp.sum(jnp.where(onehot[None], table[:, None, None, :], 0.0), axis=-1)
    vec = (vec - far[:, None, None]) * LOG2E
    rows = jnp.tile(vec, (1, 1, t))[..., :t * (p - 1)].reshape(HEADS, 2, t, p - 1)
    bias = rows[..., t - 1:2 * t - 1]
    kk = jnp.arange(t, dtype=jnp.int32)[:, None]
    qq = jnp.arange(t, dtype=jnp.int32)[None, :]
    allowed = (kk // CHUNK <= qq // CHUNK)[None, None] | (d > 0)[None, :, :, None]
    bias = jnp.where(allowed, bias, NEG_BIG)
    return jnp.concatenate([bias, bias], axis=-1)


def _rope_tables(seq):
    pos = jnp.arange(seq, dtype=F32)
    inv = ROPE_THETA ** (-jnp.arange(0, MLA_ROPE_DIM, 2, dtype=F32) / MLA_ROPE_DIM)
    ang = pos[:, None] * inv[None, :]
    cos, sin = jnp.cos(ang), jnp.sin(ang)
    z = jnp.zeros((seq, LANES - MLA_ROPE_DIM), F32)
    return (jnp.concatenate([cos, cos, z], axis=-1), jnp.concatenate([-sin, sin, z], axis=-1))


def _swap_halves(w):
    half = w.shape[-1] // 2
    return jnp.concatenate([w[..., half:], w[..., :half]], axis=-1)


def _pack_params(w_in, lambda_q1, lambda_k1, lambda_q2, lambda_k2, subln_g, q_norm_g, w_uq,
                 kv_norm_g, w_ukv, w_o, ln1_g, ln1_b, w_mlp_in, w_mlp_out, ln2_g, ln2_b):
    depth = w_in.shape[0]
    o_k, o_v, o_cq = DIFF_WIDTH, 2 * DIFF_WIDTH, 3 * DIFF_WIDTH
    o_ckv = o_cq + MLA_Q_RANK
    o_kr = o_ckv + MLA_KV_RANK
    kr = w_in[..., o_kr:]
    zk = jnp.zeros((depth, D_MODEL, LANES - MLA_ROPE_DIM), F32)
    w1 = jnp.concatenate([w_in[..., :o_k] * (DIFF_HEAD_DIM ** -0.5 * LOG2E), w_in[..., o_k:o_v],
                          w_in[..., o_cq:o_ckv], w_in[..., o_ckv:o_kr], kr, zk, _swap_halves(kr),
                          zk], axis=-1)
    mla_scale = (MLA_NOPE_DIM + MLA_ROPE_DIM) ** -0.5 * LOG2E
    uq = (w_uq * mla_scale).reshape(depth, MLA_Q_RANK, HEADS, MLA_NOPE_DIM + MLA_ROPE_DIM)
    zq = jnp.zeros((depth, MLA_Q_RANK, HEADS, MLA_QK_PAD - MLA_NOPE_DIM - MLA_ROPE_DIM), F32)
    wuq = jnp.concatenate([uq, zq], axis=-1).reshape(depth, MLA_Q_RANK, HEADS * MLA_QK_PAD)
    wuqs = jnp.concatenate([_swap_halves(uq[..., MLA_NOPE_DIM:]), zq], axis=-1)
    wuqs = wuqs.reshape(depth, MLA_Q_RANK, HEADS * LANES)
    ukv = w_ukv.reshape(depth, MLA_KV_RANK, HEADS, MLA_NOPE_DIM + MLA_V_DIM)
    wukvk = ukv[..., :MLA_NOPE_DIM].reshape(depth, MLA_KV_RANK, HEADS * MLA_NOPE_DIM)
    wukvv = ukv[..., MLA_NOPE_DIM:].reshape(depth, MLA_KV_RANK, HEADS * MLA_V_DIM)
    row = lambda v: v.reshape(depth, 1, -1).astype(F32)
    return dict(
        w1=w1.astype(BF16), wvt=jnp.swapaxes(w_in[..., o_v:o_cq], 1, 2).astype(BF16),
        wuq=wuq.astype(BF16), wuqs=wuqs.astype(BF16), wukvk=wukvk.astype(BF16),
        wukvvt=jnp.swapaxes(wukvv, 1, 2).astype(BF16), gq=row(q_norm_g), gkv=row(kv_norm_g),
        subln=row(subln_g),
        lam=jnp.stack([lambda_q1, lambda_k1, lambda_q2, lambda_k2], axis=1).astype(F32),
        wo=w_o.astype(BF16), g1=row(ln1_g), b1=row(ln1_b), wi=w_mlp_in.astype(BF16),
        wout=w_mlp_out.astype(BF16), g2=row(ln2_g), b2=row(ln2_b))


def kernel(x, w_in, lambda_q1, lambda_k1, lambda_q2, lambda_k2, subln_g, q_norm_g, w_uq, kv_norm_g,
           w_ukv, rel_bias, w_o, ln1_g, ln1_b, w_mlp_in, w_mlp_out, ln2_g, ln2_b):
    batch, seq, _ = x.shape
    assert seq % PROJ_ROWS == 0 and seq % ATT_TILE == 0 and (batch * seq) % POST_ROWS == 0
    wp = _pack_params(w_in, lambda_q1, lambda_k1, lambda_q2, lambda_k2, subln_g, q_norm_g, w_uq,
                      kv_norm_g, w_ukv, w_o, ln1_g, ln1_b, w_mlp_in, w_mlp_out, ln2_g, ln2_b)
    wp["cos"], wp["sin"] = _rope_tables(seq)
    bias = _bias_tiles(rel_bias)
    x2 = x.reshape(batch * seq, D_MODEL)
    for l in range(DEPTH):
        lambda_init = 0.8 - 0.6 * math.exp(-0.3 * l)
        dq, dk, dvt, qm, km, vmt = _proj_call(x2, wp, l, seq)
        mix_a = _diff_attn_call(dq, dk, dvt, bias, wp, l, seq, lambda_init)
        mix_b = _mla_attn_call(qm, km, vmt, seq)
        x2 = _post_call(x2, mix_a, mix_b, wp, l)
    return x2.reshape(batch, seq, D_MODEL)
```

```python
import functools
import math

import jax
import jax.numpy as jnp
from jax import lax
from jax.experimental import pallas as pl
from jax.experimental.pallas import tpu as pltpu

F32 = jnp.float32
BF16 = jnp.bfloat16

D_MODEL = 1024
DEPTH = 2
CHUNK = 64
HEADS = 4
DIFF_HEAD_DIM = 64
DIFF_V_DIM = 2 * DIFF_HEAD_DIM
DIFF_WIDTH = HEADS * DIFF_V_DIM
MLA_NOPE_DIM = 128
MLA_ROPE_DIM = 64
MLA_V_DIM = 128
MLA_Q_RANK = 256
MLA_KV_RANK = 128
MLA_WIDTH = HEADS * MLA_V_DIM
MLA_QK_PAD = 256
D_FF = 4 * D_MODEL
N_BUCKETS = 32
MAX_DISTANCE = 128
ROPE_THETA = 10000.0
ALPHA = (2 * DEPTH) ** 0.25
LN_EPS = 1e-5
RMS_EPS = 1e-6
NEG_BIG = -1e30
LOG2E = math.log2(math.e)

LANES = 128
BF16_SUBLANES = 16
ATT_TILE = 256
PROJ_ROWS = 1024
POST_ROWS = 1024
FF_CHUNK = 1024
DIFF_LOOKAHEAD = 4
MLA_LOOKAHEAD = 8
VMEM_LIMIT = 56 * 1024 * 1024
assert ATT_TILE >= MAX_DISTANCE and ATT_TILE % CHUNK == 0

C_DQ = 0
C_DK = C_DQ + DIFF_WIDTH
C_CQ = C_DK + DIFF_WIDTH
C_CKV = C_CQ + MLA_Q_RANK
C_KR = C_CKV + MLA_KV_RANK
C_KRS = C_KR + LANES
W1_COLS = C_KRS + LANES

_NT = (((1,), (1,)), ((), ()))


def _rms(x, g):
    return x * lax.rsqrt(jnp.mean(x * x, axis=-1, keepdims=True) + RMS_EPS) * g


def _layer_norm(x, g, b):
    mu = jnp.mean(x, axis=-1, keepdims=True)
    xc = x - mu
    var = jnp.mean(xc * xc, axis=-1, keepdims=True)
    return xc * lax.rsqrt(var + LN_EPS) * g + b


def _layer_spec(arr, l, **kwargs):
    index = (l,) + (0,) * (arr.ndim - 1)
    return pl.BlockSpec((1,) + arr.shape[1:], lambda *_: index, **kwargs)


def _proj_kernel(x_ref, w1_ref, wvt_ref, wuq_ref, wuqs_ref, wukvk_ref, wukvvt_ref,
                 gq_ref, gkv_ref, cos_ref, sin_ref,
                 dq_ref, dk_ref, dvt_ref, qm_ref, km_ref, vmt_ref):
    xb = x_ref[...].astype(BF16)
    h = jnp.dot(xb, w1_ref[0], preferred_element_type=F32)
    dq_ref[...] = h[:, C_DQ:C_DQ + DIFF_WIDTH].astype(BF16)
    dk_ref[...] = h[:, C_DK:C_DK + DIFF_WIDTH].astype(BF16)
    dvt_ref[0] = lax.dot_general(wvt_ref[0], xb, _NT, preferred_element_type=F32).astype(BF16)

    cos = cos_ref[...]
    sin = sin_ref[...]
    c_q = _rms(h[:, C_CQ:C_CQ + MLA_Q_RANK], gq_ref[0]).astype(BF16)
    qf = jnp.dot(c_q, wuq_ref[0], preferred_element_type=F32)
    qsw = jnp.dot(c_q, wuqs_ref[0], preferred_element_type=F32)
    c_kv = _rms(h[:, C_CKV:C_CKV + MLA_KV_RANK], gkv_ref[0]).astype(BF16)
    kn = jnp.dot(c_kv, wukvk_ref[0], preferred_element_type=F32)
    vmt_ref[0] = lax.dot_general(wukvvt_ref[0], c_kv, _NT,
                                 preferred_element_type=F32).astype(BF16)
    k_rope = (h[:, C_KR:C_KR + LANES] * cos + h[:, C_KRS:C_KRS + LANES] * sin).astype(BF16)

    for hh in range(HEADS):
        b0 = hh * MLA_QK_PAD
        qm_ref[:, b0:b0 + LANES] = qf[:, b0:b0 + LANES].astype(BF16)
        qm_ref[:, b0 + LANES:b0 + 2 * LANES] = (
            qf[:, b0 + LANES:b0 + 2 * LANES] * cos + qsw[:, hh * LANES:(hh + 1) * LANES] * sin
        ).astype(BF16)
        km_ref[:, b0:b0 + LANES] = kn[:, hh * LANES:(hh + 1) * LANES].astype(BF16)
        km_ref[:, b0 + LANES:b0 + 2 * LANES] = k_rope


def _proj_call(x2, wp, l, seq):
    ntok = x2.shape[0]
    batch = ntok // seq
    rows = PROJ_ROWS
    steps_per_seq = seq // rows

    def const(name):
        return _layer_spec(wp[name], l)

    def tok(width):
        return pl.BlockSpec((rows, width), lambda i: (i, 0))

    vt_spec = pl.BlockSpec((1, HEADS * LANES, rows),
                           lambda i: (i // steps_per_seq, 0, i % steps_per_seq))
    pos_spec = pl.BlockSpec((rows, LANES), lambda i: (i % steps_per_seq, 0))
    vt_shape = jax.ShapeDtypeStruct((batch, HEADS * LANES, seq), BF16)
    return pl.pallas_call(
        _proj_kernel,
        grid=(ntok // rows,),
        in_specs=[tok(D_MODEL), const("w1"), const("wvt"), const("wuq"), const("wuqs"),
                  const("wukvk"), const("wukvvt"), const("gq"), const("gkv"), pos_spec, pos_spec],
        out_specs=[tok(DIFF_WIDTH), tok(DIFF_WIDTH), vt_spec,
                   tok(HEADS * MLA_QK_PAD), tok(HEADS * MLA_QK_PAD), vt_spec],
        out_shape=[jax.ShapeDtypeStruct((ntok, DIFF_WIDTH), BF16),
                   jax.ShapeDtypeStruct((ntok, DIFF_WIDTH), BF16),
                   vt_shape,
                   jax.ShapeDtypeStruct((ntok, HEADS * MLA_QK_PAD), BF16),
                   jax.ShapeDtypeStruct((ntok, HEADS * MLA_QK_PAD), BF16),
                   vt_shape],
        compiler_params=pltpu.CompilerParams(dimension_semantics=("arbitrary",),
                                             vmem_limit_bytes=VMEM_LIMIT),
        name="proj",
    )(x2, wp["w1"], wp["wvt"], wp["wuq"], wp["wuqs"], wp["wukvk"], wp["wukvvt"],
      wp["gq"], wp["gkv"], wp["cos"], wp["sin"])


def _col_max(a, b):
    m = jnp.max(b, axis=0, keepdims=True)
    return m if a is None else jnp.maximum(a, m)


def _attend(n_tiles, q_fn, score_fn, finish_fn, vt_ref, lookahead):
    t = ATT_TILE
    ones = jnp.ones((BF16_SUBLANES, t), BF16)
    items = [(qi, c) for qi in range(n_tiles) for c in range(qi + 1)]
    qs, scores = {}, {}

    def issue(idx):
        qi, c = items[idx]
        if c == 0:
            qs[qi] = q_fn(qi)
        scores[idx] = score_fn(qi, qs[qi], c)

    for idx in range(min(lookahead, len(items))):
        issue(idx)
    m, acc = None, None
    for idx, (qi, c) in enumerate(items):
        if idx + lookahead < len(items):
            issue(idx + lookahead)
        s = scores.pop(idx)
        m_new = _col_max(m, s)
        e = jnp.exp2(s - m_new).astype(BF16)
        vt = jnp.concatenate([vt_ref[0, :, c * t:(c + 1) * t], ones], axis=0)
        pv = jnp.dot(vt, e, preferred_element_type=F32)
        acc = pv if acc is None else acc * jnp.exp2(m - m_new) + pv
        m = m_new
        if c == qi:
            finish_fn(qi, acc[:LANES] / acc[LANES:LANES + 1])
            m, acc = None, None


def _diff_attn_kernel(q_ref, k_ref, vt_ref, bias_ref, lam_ref, g_ref, o_ref, *, lambda_init):
    t = ATT_TILE
    lane = lax.broadcasted_iota(jnp.int32, (t, LANES), 1)
    lp = lam_ref[0]
    lam = (jnp.exp(jnp.sum(lp[0:1] * lp[1:2], axis=-1, keepdims=True))
           - jnp.exp(jnp.sum(lp[2:3] * lp[3:4], axis=-1, keepdims=True)) + lambda_init)
    g = g_ref[0]

    def q_rows(qi):
        q = q_ref[qi * t:(qi + 1) * t, :]
        zero = jnp.zeros_like(q)
        return jnp.concatenate([jnp.where(lane < DIFF_HEAD_DIM, q, zero),
                                jnp.where(lane >= DIFF_HEAD_DIM, q, zero)], axis=0)

    def score(qi, qs, c):
        s = lax.dot_general(k_ref[c * t:(c + 1) * t, :], qs, _NT, preferred_element_type=F32)
        return s + bias_ref[0, qi - c] if qi - c <= 1 else s

    def finish(qi, o):
        ot = o[:, :t] - lam * o[:, t:]
        y = _rms(ot.T, g) * (1.0 - lambda_init)
        o_ref[qi * t:(qi + 1) * t, :] = y.astype(BF16)

    _attend(q_ref.shape[0] // t, q_rows, score, finish, vt_ref, DIFF_LOOKAHEAD)


def _mla_attn_kernel(q_ref, k_ref, vt_ref, o_ref):
    t = ATT_TILE
    kc = lax.broadcasted_iota(jnp.int32, (t, t), 0) // CHUNK
    qc = lax.broadcasted_iota(jnp.int32, (t, t), 1) // CHUNK
    allowed = kc <= qc

    def q_rows(qi):
        return q_ref[qi * t:(qi + 1) * t, :]

    def score(qi, q, c):
        s = lax.dot_general(k_ref[c * t:(c + 1) * t, :], q, _NT, preferred_element_type=F32)
        return jnp.where(allowed, s, NEG_BIG) if c == qi else s

    def finish(qi, o):
        o_ref[qi * t:(qi + 1) * t, :] = o.T.astype(BF16)

    _attend(q_ref.shape[0] // t, q_rows, score, finish, vt_ref, MLA_LOOKAHEAD)


def _attn_specs(seq, q_width):
    q_spec = pl.BlockSpec((seq, q_width), lambda b, h: (b, h))
    vt_spec = pl.BlockSpec((1, LANES, seq), lambda b, h: (b, h, 0))
    o_spec = pl.BlockSpec((seq, LANES), lambda b, h: (b, h))
    return q_spec, vt_spec, o_spec


_ATTN_PARAMS = pltpu.CompilerParams(dimension_semantics=("arbitrary", "arbitrary"),
                                    vmem_limit_bytes=VMEM_LIMIT)


def _diff_attn_call(dq, dk, dvt, bias, wp, l, seq, lambda_init):
    ntok = dq.shape[0]
    q_spec, vt_spec, o_spec = _attn_specs(seq, LANES)
    return pl.pallas_call(
        functools.partial(_diff_attn_kernel, lambda_init=lambda_init),
        grid=(ntok // seq, HEADS),
        in_specs=[q_spec, q_spec, vt_spec,
                  pl.BlockSpec((1,) + bias.shape[1:], lambda b, h: (h, 0, 0, 0)),
                  _layer_spec(wp["lam"], l), _layer_spec(wp["subln"], l)],
        out_specs=o_spec,
        out_shape=jax.ShapeDtypeStruct((ntok, DIFF_WIDTH), BF16),
        compiler_params=_ATTN_PARAMS,
        name="diff_attn",
    )(dq, dk, dvt, bias, wp["lam"], wp["subln"])


def _mla_attn_call(qm, km, vmt, seq):
    ntok = qm.shape[0]
    q_spec, vt_spec, o_spec = _attn_specs(seq, MLA_QK_PAD)
    return pl.pallas_call(
        _mla_attn_kernel,
        grid=(ntok // seq, HEADS),
        in_specs=[q_spec, q_spec, vt_spec],
        out_specs=o_spec,
        out_shape=jax.ShapeDtypeStruct((ntok, MLA_WIDTH), BF16),
        compiler_params=_ATTN_PARAMS,
        name="mla_attn",
    )(qm, km, vmt)


def _post_kernel(x_ref, ma_ref, mb_ref, wo_ref, g1_ref, b1_ref, wi_ref, wout_ref, g2_ref, b2_ref,
                 o_ref):
    y = (jnp.dot(ma_ref[...], wo_ref[0, :DIFF_WIDTH, :], preferred_element_type=F32)
         + jnp.dot(mb_ref[...], wo_ref[0, DIFF_WIDTH:, :], preferred_element_type=F32))
    x1 = _layer_norm(ALPHA * x_ref[...] + y, g1_ref[0], b1_ref[0])
    x1b = x1.astype(BF16)
    y2 = jnp.zeros_like(x1)
    for c in range(D_FF // FF_CHUNK):
        cs = slice(c * FF_CHUNK, (c + 1) * FF_CHUNK)
        u = jnp.dot(x1b, wi_ref[0, :, cs], preferred_element_type=F32)
        u = jnp.square(jnp.maximum(u, 0.0)).astype(BF16)
        y2 = y2 + jnp.dot(u, wout_ref[0, cs, :], preferred_element_type=F32)
    o_ref[...] = _layer_norm(ALPHA * x1 + y2, g2_ref[0], b2_ref[0])


def _post_call(x2, mix_a, mix_b, wp, l):
    ntok = x2.shape[0]
    rows = POST_ROWS

    def const(name):
        return _layer_spec(wp[name], l, pipeline_mode=pl.Buffered(1))

    def tok(width):
        return pl.BlockSpec((rows, width), lambda i: (i, 0))

    return pl.pallas_call(
        _post_kernel,
        grid=(ntok // rows,),
        in_specs=[tok(D_MODEL), tok(DIFF_WIDTH), tok(MLA_WIDTH), const("wo"), const("g1"),
                  const("b1"), const("wi"), const("wout"), const("g2"), const("b2")],
        out_specs=tok(D_MODEL),
        out_shape=jax.ShapeDtypeStruct((ntok, D_MODEL), F32),
        compiler_params=pltpu.CompilerParams(dimension_semantics=("arbitrary",),
                                             vmem_limit_bytes=VMEM_LIMIT),
        name="post",
    )(x2, mix_a, mix_b, wp["wo"], wp["g1"], wp["b1"], wp["wi"], wp["wout"], wp["g2"], wp["b2"])


def _t5_bucket(rel):
    nb = N_BUCKETS // 2
    ret = (rel > 0).astype(jnp.int32) * nb
    n = jnp.abs(rel)
    max_exact = nb // 2
    nf = jnp.maximum(n, 1).astype(F32)
    large = max_exact + (jnp.log(nf / max_exact) / math.log(MAX_DISTANCE / max_exact)
                         * (nb - max_exact)).astype(jnp.int32)
    large = jnp.minimum(large, nb - 1)
    return ret + jnp.where(n < max_exact, n, large)


def _bias_tiles(rel_bias):
    t = ATT_TILE
    p = 2 * t
    table = rel_bias.astype(F32).T
    far = table[:, N_BUCKETS // 2 - 1]
    w = jnp.arange(p, dtype=jnp.int32)[None, :]
    d = jnp.arange(2, dtype=jnp.int32)[:, None]
    rel = (t - 1) - w - d * t
    onehot = _t5_bucket(rel)[..., None] == jnp.arange(N_BUCKETS, dtype=jnp.int32)
    vec = jnp.sum(jnp.where(onehot[None], table[:, None, None, :], 0.0), axis=-1)
    vec = (vec - far[:, None, None]) * LOG2E
    rows = jnp.tile(vec, (1, 1, t))[..., :t * (p - 1)].reshape(HEADS, 2, t, p - 1)
    bias = rows[..., t - 1:2 * t - 1]
    kk = jnp.arange(t, dtype=jnp.int32)[:, None]
    qq = jnp.arange(t, dtype=jnp.int32)[None, :]
    allowed = (kk // CHUNK <= qq // CHUNK)[None, None] | (d > 0)[None, :, :, None]
    bias = jnp.where(allowed, bias, NEG_BIG)
    return jnp.concatenate([bias, bias], axis=-1)


def _rope_tables(seq):
    pos = jnp.arange(seq, dtype=F32)
    inv = ROPE_THETA ** (-jnp.arange(0, MLA_ROPE_DIM, 2, dtype=F32) / MLA_ROPE_DIM)
    ang = pos[:, None] * inv[None, :]
    cos, sin = jnp.cos(ang), jnp.sin(ang)
    z = jnp.zeros((seq, LANES - MLA_ROPE_DIM), F32)
    return (jnp.concatenate([cos, cos, z], axis=-1), jnp.concatenate([-sin, sin, z], axis=-1))


def _swap_halves(w):
    half = w.shape[-1] // 2
    return jnp.concatenate([w[..., half:], w[..., :half]], axis=-1)


def _pack_params(w_in, lambda_q1, lambda_k1, lambda_q2, lambda_k2, subln_g, q_norm_g, w_uq,
                 kv_norm_g, w_ukv, w_o, ln1_g, ln1_b, w_mlp_in, w_mlp_out, ln2_g, ln2_b):
    depth = w_in.shape[0]
    o_k, o_v, o_cq = DIFF_WIDTH, 2 * DIFF_WIDTH, 3 * DIFF_WIDTH
    o_ckv = o_cq + MLA_Q_RANK
    o_kr = o_ckv + MLA_KV_RANK
    kr = w_in[..., o_kr:]
    zk = jnp.zeros((depth, D_MODEL, LANES - MLA_ROPE_DIM), F32)
    w1 = jnp.concatenate([w_in[..., :o_k] * (DIFF_HEAD_DIM ** -0.5 * LOG2E), w_in[..., o_k:o_v],
                          w_in[..., o_cq:o_ckv], w_in[..., o_ckv:o_kr], kr, zk, _swap_halves(kr),
                          zk], axis=-1)
    mla_scale = (MLA_NOPE_DIM + MLA_ROPE_DIM) ** -0.5 * LOG2E
    uq = (w_uq * mla_scale).reshape(depth, MLA_Q_RANK, HEADS, MLA_NOPE_DIM + MLA_ROPE_DIM)
    zq = jnp.zeros((depth, MLA_Q_RANK, HEADS, MLA_QK_PAD - MLA_NOPE_DIM - MLA_ROPE_DIM), F32)
    wuq = jnp.concatenate([uq, zq], axis=-1).reshape(depth, MLA_Q_RANK, HEADS * MLA_QK_PAD)
    wuqs = jnp.concatenate([_swap_halves(uq[..., MLA_NOPE_DIM:]), zq], axis=-1)
    wuqs = wuqs.reshape(depth, MLA_Q_RANK, HEADS * LANES)
    ukv = w_ukv.reshape(depth, MLA_KV_RANK, HEADS, MLA_NOPE_DIM + MLA_V_DIM)
    wukvk = ukv[..., :MLA_NOPE_DIM].reshape(depth, MLA_KV_RANK, HEADS * MLA_NOPE_DIM)
    wukvv = ukv[..., MLA_NOPE_DIM:].reshape(depth, MLA_KV_RANK, HEADS * MLA_V_DIM)
    row = lambda v: v.reshape(depth, 1, -1).astype(F32)
    return dict(
        w1=w1.astype(BF16), wvt=jnp.swapaxes(w_in[..., o_v:o_cq], 1, 2).astype(BF16),
        wuq=wuq.astype(BF16), wuqs=wuqs.astype(BF16), wukvk=wukvk.astype(BF16),
        wukvvt=jnp.swapaxes(wukvv, 1, 2).astype(BF16), gq=row(q_norm_g), gkv=row(kv_norm_g),
        subln=row(subln_g),
        lam=jnp.stack([lambda_q1, lambda_k1, lambda_q2, lambda_k2], axis=1).astype(F32),
        wo=w_o.astype(BF16), g1=row(ln1_g), b1=row(ln1_b), wi=w_mlp_in.astype(BF16),
        wout=w_mlp_out.astype(BF16), g2=row(ln2_g), b2=row(ln2_b))


def kernel(x, w_in, lambda_q1, lambda_k1, lambda_q2, lambda_k2, subln_g, q_norm_g, w_uq, kv_norm_g,
           w_ukv, rel_bias, w_o, ln1_g, ln1_b, w_mlp_in, w_mlp_out, ln2_g, ln2_b):
    batch, seq, _ = x.shape
    assert seq % PROJ_ROWS == 0 and seq % ATT_TILE == 0 and (batch * seq) % POST_ROWS == 0
    wp = _pack_params(w_in, lambda_q1, lambda_k1, lambda_q2, lambda_k2, subln_g, q_norm_g, w_uq,
                      kv_norm_g, w_ukv, w_o, ln1_g, ln1_b, w_mlp_in, w_mlp_out, ln2_g, ln2_b)
    wp["cos"], wp["sin"] = _rope_tables(seq)
    bias = _bias_tiles(rel_bias)
    x2 = x.reshape(batch * seq, D_MODEL)
    for l in range(DEPTH):
        lambda_init = 0.8 - 0.6 * math.exp(-0.3 * l)
        dq, dk, dvt, qm, km, vmt = _proj_call(x2, wp, l, seq)
        mix_a = _diff_attn_call(dq, dk, dvt, bias, wp, l, seq, lambda_init)
        mix_b = _mla_attn_call(qm, km, vmt, seq)
        x2 = _post_call(x2, mix_a, mix_b, wp, l)
    return x2.reshape(batch, seq, D_MODEL)
```

```python
import functools
import math

import jax
import jax.numpy as jnp
from jax import lax
from jax.experimental import pallas as pl
from jax.experimental.pallas import tpu as pltpu

F32 = jnp.float32
BF16 = jnp.bfloat16

D_MODEL = 1024
DEPTH = 2
CHUNK = 64
HEADS = 4
DIFF_HEAD_DIM = 64
DIFF_V_DIM = 2 * DIFF_HEAD_DIM
DIFF_WIDTH = HEADS * DIFF_V_DIM
MLA_NOPE_DIM = 128
MLA_ROPE_DIM = 64
MLA_V_DIM = 128
MLA_Q_RANK = 256
MLA_KV_RANK = 128
MLA_WIDTH = HEADS * MLA_V_DIM
MLA_QK_PAD = 256
D_FF = 4 * D_MODEL
N_BUCKETS = 32
MAX_DISTANCE = 128
ROPE_THETA = 10000.0
ALPHA = (2 * DEPTH) ** 0.25
LN_EPS = 1e-5
RMS_EPS = 1e-6
NEG_BIG = -1e30
LOG2E = math.log2(math.e)

LANES = 128
BF16_SUBLANES = 16
ATT_TILE = 256
PROJ_ROWS = 1024
POST_ROWS = 1024
POST_ROW_BLOCK = 256
FF_CHUNK = 1024
DIFF_LOOKAHEAD = 4
MLA_LOOKAHEAD = 8
VMEM_LIMIT = 56 * 1024 * 1024
assert ATT_TILE >= MAX_DISTANCE and ATT_TILE % CHUNK == 0

C_DQ = 0
C_DK = C_DQ + DIFF_WIDTH
C_CQ = C_DK + DIFF_WIDTH
C_CKV = C_CQ + MLA_Q_RANK
C_KR = C_CKV + MLA_KV_RANK
C_KRS = C_KR + LANES
W1_COLS = C_KRS + LANES

_NT = (((1,), (1,)), ((), ()))


def _rms(x, g):
    return x * lax.rsqrt(jnp.mean(x * x, axis=-1, keepdims=True) + RMS_EPS) * g


def _layer_norm(x, g, b):
    mu = jnp.mean(x, axis=-1, keepdims=True)
    xc = x - mu
    var = jnp.mean(xc * xc, axis=-1, keepdims=True)
    return xc * lax.rsqrt(var + LN_EPS) * g + b


def _layer_spec(arr, l, **kwargs):
    index = (l,) + (0,) * (arr.ndim - 1)
    return pl.BlockSpec((1,) + arr.shape[1:], lambda *_: index, **kwargs)


def _proj_kernel(x_ref, w1_ref, wvt_ref, wuq_ref, wuqs_ref, wukvk_ref, wukvvt_ref,
                 gq_ref, gkv_ref, cos_ref, sin_ref,
                 dq_ref, dk_ref, dvt_ref, qm_ref, km_ref, vmt_ref):
    xb = x_ref[...].astype(BF16)
    h = jnp.dot(xb, w1_ref[0], preferred_element_type=F32)
    dq_ref[...] = h[:, C_DQ:C_DQ + DIFF_WIDTH].astype(BF16)
    dk_ref[...] = h[:, C_DK:C_DK + DIFF_WIDTH].astype(BF16)
    dvt_ref[0] = lax.dot_general(wvt_ref[0], xb, _NT, preferred_element_type=F32).astype(BF16)

    cos = cos_ref[...]
    sin = sin_ref[...]
    c_q = _rms(h[:, C_CQ:C_CQ + MLA_Q_RANK], gq_ref[0]).astype(BF16)
    qf = jnp.dot(c_q, wuq_ref[0], preferred_element_type=F32)
    qsw = jnp.dot(c_q, wuqs_ref[0], preferred_element_type=F32)
    c_kv = _rms(h[:, C_CKV:C_CKV + MLA_KV_RANK], gkv_ref[0]).astype(BF16)
    kn = jnp.dot(c_kv, wukvk_ref[0], preferred_element_type=F32)
    vmt_ref[0] = lax.dot_general(wukvvt_ref[0], c_kv, _NT,
                                 preferred_element_type=F32).astype(BF16)
    k_rope = (h[:, C_KR:C_KR + LANES] * cos + h[:, C_KRS:C_KRS + LANES] * sin).astype(BF16)

    for hh in range(HEADS):
        b0 = hh * MLA_QK_PAD
        qm_ref[:, b0:b0 + LANES] = qf[:, b0:b0 + LANES].astype(BF16)
        qm_ref[:, b0 + LANES:b0 + 2 * LANES] = (
            qf[:, b0 + LANES:b0 + 2 * LANES] * cos + qsw[:, hh * LANES:(hh + 1) * LANES] * sin
        ).astype(BF16)
        km_ref[:, b0:b0 + LANES] = kn[:, hh * LANES:(hh + 1) * LANES].astype(BF16)
        km_ref[:, b0 + LANES:b0 + 2 * LANES] = k_rope


def _proj_call(x2, wp, l, seq):
    ntok = x2.shape[0]
    batch = ntok // seq
    rows = PROJ_ROWS
    steps_per_seq = seq // rows

    def const(name):
        return _layer_spec(wp[name], l)

    def tok(width):
        return pl.BlockSpec((rows, width), lambda i: (i, 0))

    vt_spec = pl.BlockSpec((1, HEADS * LANES, rows),
                           lambda i: (i // steps_per_seq, 0, i % steps_per_seq))
    pos_spec = pl.BlockSpec((rows, LANES), lambda i: (i % steps_per_seq, 0))
    vt_shape = jax.ShapeDtypeStruct((batch, HEADS * LANES, seq), BF16)
    return pl.pallas_call(
        _proj_kernel,
        grid=(ntok // rows,),
        in_specs=[tok(D_MODEL), const("w1"), const("wvt"), const("wuq"), const("wuqs"),
                  const("wukvk"), const("wukvvt"), const("gq"), const("gkv"), pos_spec, pos_spec],
        out_specs=[tok(DIFF_WIDTH), tok(DIFF_WIDTH), vt_spec,
                   tok(HEADS * MLA_QK_PAD), tok(HEADS * MLA_QK_PAD), vt_spec],
        out_shape=[jax.ShapeDtypeStruct((ntok, DIFF_WIDTH), BF16),
                   jax.ShapeDtypeStruct((ntok, DIFF_WIDTH), BF16),
                   vt_shape,
                   jax.ShapeDtypeStruct((ntok, HEADS * MLA_QK_PAD), BF16),
                   jax.ShapeDtypeStruct((ntok, HEADS * MLA_QK_PAD), BF16),
                   vt_shape],
        compiler_params=pltpu.CompilerParams(dimension_semantics=("arbitrary",),
                                             vmem_limit_bytes=VMEM_LIMIT),
        name="proj",
    )(x2, wp["w1"], wp["wvt"], wp["wuq"], wp["wuqs"], wp["wukvk"], wp["wukvvt"],
      wp["gq"], wp["gkv"], wp["cos"], wp["sin"])


def _col_max(a, b):
    m = jnp.max(b, axis=0, keepdims=True)
    return m if a is None else jnp.maximum(a, m)


def _attend(n_tiles, q_fn, score_fn, finish_fn, vt_ref, lookahead):
    t = ATT_TILE
    ones = jnp.ones((BF16_SUBLANES, t), BF16)
    items = [(qi, c) for qi in range(n_tiles) for c in range(qi + 1)]
    qs, scores = {}, {}

    def issue(idx):
        qi, c = items[idx]
        if c == 0:
            qs[qi] = q_fn(qi)
        scores[idx] = score_fn(qi, qs[qi], c)

    for idx in range(min(lookahead, len(items))):
        issue(idx)
    m, acc = None, None
    for idx, (qi, c) in enumerate(items):
        if idx + lookahead < len(items):
            issue(idx + lookahead)
        s = scores.pop(idx)
        m_new = _col_max(m, s)
        e = jnp.exp2(s - m_new).astype(BF16)
        vt = jnp.concatenate([vt_ref[0, :, c * t:(c + 1) * t], ones], axis=0)
        pv = jnp.dot(vt, e, preferred_element_type=F32)
        acc = pv if acc is None else acc * jnp.exp2(m - m_new) + pv
        m = m_new
        if c == qi:
            finish_fn(qi, acc[:LANES] / acc[LANES:LANES + 1])
            m, acc = None, None


def _diff_attn_kernel(q_ref, k_ref, vt_ref, bias_ref, lam_ref, g_ref, o_ref, *, lambda_init):
    t = ATT_TILE
    lane = lax.broadcasted_iota(jnp.int32, (t, LANES), 1)
    lp = lam_ref[0]
    lam = (jnp.exp(jnp.sum(lp[0:1] * lp[1:2], axis=-1, keepdims=True))
           - jnp.exp(jnp.sum(lp[2:3] * lp[3:4], axis=-1, keepdims=True)) + lambda_init)
    g = g_ref[0]

    def q_rows(qi):
        q = q_ref[qi * t:(qi + 1) * t, :]
        zero = jnp.zeros_like(q)
        return jnp.concatenate([jnp.where(lane < DIFF_HEAD_DIM, q, zero),
                                jnp.where(lane >= DIFF_HEAD_DIM, q, zero)], axis=0)

    def score(qi, qs, c):
        s = lax.dot_general(k_ref[c * t:(c + 1) * t, :], qs, _NT, preferred_element_type=F32)
        return s + bias_ref[0, qi - c] if qi - c <= 1 else s

    def finish(qi, o):
        ot = o[:, :t] - lam * o[:, t:]
        y = _rms(ot.T, g) * (1.0 - lambda_init)
        o_ref[qi * t:(qi + 1) * t, :] = y.astype(BF16)

    _attend(q_ref.shape[0] // t, q_rows, score, finish, vt_ref, DIFF_LOOKAHEAD)


def _mla_attn_kernel(q_ref, k_ref, vt_ref, o_ref):
    t = ATT_TILE
    kc = lax.broadcasted_iota(jnp.int32, (t, t), 0) // CHUNK
    qc = lax.broadcasted_iota(jnp.int32, (t, t), 1) // CHUNK
    allowed = kc <= qc

    def q_rows(qi):
        return q_ref[qi * t:(qi + 1) * t, :]

    def score(qi, q, c):
        s = lax.dot_general(k_ref[c * t:(c + 1) * t, :], q, _NT, preferred_element_type=F32)
        return jnp.where(allowed, s, NEG_BIG) if c == qi else s

    def finish(qi, o):
        o_ref[qi * t:(qi + 1) * t, :] = o.T.astype(BF16)

    _attend(q_ref.shape[0] // t, q_rows, score, finish, vt_ref, MLA_LOOKAHEAD)


def _attn_specs(seq, q_width):
    q_spec = pl.BlockSpec((seq, q_width), lambda b, h: (b, h))
    vt_spec = pl.BlockSpec((1, LANES, seq), lambda b, h: (b, h, 0))
    o_spec = pl.BlockSpec((seq, LANES), lambda b, h: (b, h))
    return q_spec, vt_spec, o_spec


_ATTN_PARAMS = pltpu.CompilerParams(dimension_semantics=("arbitrary", "arbitrary"),
                                    vmem_limit_bytes=VMEM_LIMIT)


def _diff_attn_call(dq, dk, dvt, bias, wp, l, seq, lambda_init):
    ntok = dq.shape[0]
    q_spec, vt_spec, o_spec = _attn_specs(seq, LANES)
    return pl.pallas_call(
        functools.partial(_diff_attn_kernel, lambda_init=lambda_init),
        grid=(ntok // seq, HEADS),
        in_specs=[q_spec, q_spec, vt_spec,
                  pl.BlockSpec((1,) + bias.shape[1:], lambda b, h: (h, 0, 0, 0)),
                  _layer_spec(wp["lam"], l), _layer_spec(wp["subln"], l)],
        out_specs=o_spec,
        out_shape=jax.ShapeDtypeStruct((ntok, DIFF_WIDTH), BF16),
        compiler_params=_ATTN_PARAMS,
        name="diff_attn",
    )(dq, dk, dvt, bias, wp["lam"], wp["subln"])


def _mla_attn_call(qm, km, vmt, seq):
    ntok = qm.shape[0]
    q_spec, vt_spec, o_spec = _attn_specs(seq, MLA_QK_PAD)
    return pl.pallas_call(
        _mla_attn_kernel,
        grid=(ntok // seq, HEADS),
        in_specs=[q_spec, q_spec, vt_spec],
        out_specs=o_spec,
        out_shape=jax.ShapeDtypeStruct((ntok, MLA_WIDTH), BF16),
        compiler_params=_ATTN_PARAMS,
        name="mla_attn",
    )(qm, km, vmt)


def _post_kernel(x_ref, ma_ref, mb_ref, wo_ref, g1_ref, b1_ref, wi_ref, wout_ref, g2_ref, b2_ref,
                 o_ref):
    n_blocks = x_ref.shape[0] // POST_ROW_BLOCK
    n_chunks = D_FF // FF_CHUNK

    def attn_out(r):
        rs = slice(r * POST_ROW_BLOCK, (r + 1) * POST_ROW_BLOCK)
        y = (jnp.dot(ma_ref[rs, :], wo_ref[0, :DIFF_WIDTH, :], preferred_element_type=F32)
             + jnp.dot(mb_ref[rs, :], wo_ref[0, DIFF_WIDTH:, :], preferred_element_type=F32))
        return _layer_norm(ALPHA * x_ref[rs, :] + y, g1_ref[0], b1_ref[0])

    def up(x1b, c):
        return jnp.dot(x1b, wi_ref[0, :, c * FF_CHUNK:(c + 1) * FF_CHUNK],
                       preferred_element_type=F32)

    x1 = attn_out(0)
    for r in range(n_blocks):
        x1_next = attn_out(r + 1) if r + 1 < n_blocks else None
        x1b = x1.astype(BF16)
        u = up(x1b, 0)
        y2 = None
        for c in range(n_chunks):
            u_next = up(x1b, c + 1) if c + 1 < n_chunks else None
            act = jnp.square(jnp.maximum(u, 0.0)).astype(BF16)
            d = jnp.dot(act, wout_ref[0, c * FF_CHUNK:(c + 1) * FF_CHUNK, :],
                        preferred_element_type=F32)
            y2 = d if y2 is None else y2 + d
            u = u_next
        o_ref[r * POST_ROW_BLOCK:(r + 1) * POST_ROW_BLOCK, :] = _layer_norm(
            ALPHA * x1 + y2, g2_ref[0], b2_ref[0])
        x1 = x1_next


def _post_call(x2, mix_a, mix_b, wp, l):
    ntok = x2.shape[0]
    rows = POST_ROWS

    def const(name):
        return _layer_spec(wp[name], l, pipeline_mode=pl.Buffered(1))

    def tok(width):
        return pl.BlockSpec((rows, width), lambda i: (i, 0))

    return pl.pallas_call(
        _post_kernel,
        grid=(ntok // rows,),
        in_specs=[tok(D_MODEL), tok(DIFF_WIDTH), tok(MLA_WIDTH), const("wo"), const("g1"),
                  const("b1"), const("wi"), const("wout"), const("g2"), const("b2")],
        out_specs=tok(D_MODEL),
        out_shape=jax.ShapeDtypeStruct((ntok, D_MODEL), F32),
        compiler_params=pltpu.CompilerParams(dimension_semantics=("arbitrary",),
                                             vmem_limit_bytes=VMEM_LIMIT),
        name="post",
    )(x2, mix_a, mix_b, wp["wo"], wp["g1"], wp["b1"], wp["wi"], wp["wout"], wp["g2"], wp["b2"])


def _t5_bucket(rel):
    nb = N_BUCKETS // 2
    ret = (rel > 0).astype(jnp.int32) * nb
    n = jnp.abs(rel)
    max_exact = nb // 2
    nf = jnp.maximum(n, 1).astype(F32)
    large = max_exact + (jnp.log(nf / max_exact) / math.log(MAX_DISTANCE / max_exact)
                         * (nb - max_exact)).astype(jnp.int32)
    large = jnp.minimum(large, nb - 1)
    return ret + jnp.where(n < max_exact, n, large)


def _bias_tiles(rel_bias):
    t = ATT_TILE
    p = 2 * t
    table = rel_bias.astype(F32).T
    far = table[:, N_BUCKETS // 2 - 1]
    w = jnp.arange(p, dtype=jnp.int32)[None, :]
    d = jnp.arange(2, dtype=jnp.int32)[:, None]
    rel = (t - 1) - w - d * t
    onehot = _t5_bucket(rel)[..., None] == jnp.arange(N_BUCKETS, dtype=jnp.int32)
    vec = jnp.sum(jnp.where(onehot[None], table[:, None, None, :], 0.0), axis=-1)
    vec = (vec - far[:, None, None]) * LOG2E
    rows = jnp.tile(vec, (1, 1, t))[..., :t * (p - 1)].reshape(HEADS, 2, t, p - 1)
    bias = rows[..., t - 1:2 * t - 1]
    kk = jnp.arange(t, dtype=jnp.int32)[:, None]
    qq = jnp.arange(t, dtype=jnp.int32)[None, :]
    allowed = (kk // CHUNK <= qq // CHUNK)[None, None] | (d > 0)[None, :, :, None]
    bias = jnp.where(allowed, bias, NEG_BIG)
    return jnp.concatenate([bias, bias], axis=-1)


def _rope_tables(seq):
    pos = jnp.arange(seq, dtype=F32)
    inv = ROPE_THETA ** (-jnp.arange(0, MLA_ROPE_DIM, 2, dtype=F32) / MLA_ROPE_DIM)
    ang = pos[:, None] * inv[None, :]
    cos, sin = jnp.cos(ang), jnp.sin(ang)
    z = jnp.zeros((seq, LANES - MLA_ROPE_DIM), F32)
    return (jnp.concatenate([cos, cos, z], axis=-1), jnp.concatenate([-sin, sin, z], axis=-1))


def _swap_halves(w):
    half = w.shape[-1] // 2
    return jnp.concatenate([w[..., half:], w[..., :half]], axis=-1)


def _pack_params(w_in, lambda_q1, lambda_k1, lambda_q2, lambda_k2, subln_g, q_norm_g, w_uq,
                 kv_norm_g, w_ukv, w_o, ln1_g, ln1_b, w_mlp_in, w_mlp_out, ln2_g, ln2_b):
    depth = w_in.shape[0]
    o_k, o_v, o_cq = DIFF_WIDTH, 2 * DIFF_WIDTH, 3 * DIFF_WIDTH
    o_ckv = o_cq + MLA_Q_RANK
    o_kr = o_ckv + MLA_KV_RANK
    kr = w_in[..., o_kr:]
    zk = jnp.zeros((depth, D_MODEL, LANES - MLA_ROPE_DIM), F32)
    w1 = jnp.concatenate([w_in[..., :o_k] * (DIFF_HEAD_DIM ** -0.5 * LOG2E), w_in[..., o_k:o_v],
                          w_in[..., o_cq:o_ckv], w_in[..., o_ckv:o_kr], kr, zk, _swap_halves(kr),
                          zk], axis=-1)
    mla_scale = (MLA_NOPE_DIM + MLA_ROPE_DIM) ** -0.5 * LOG2E
    uq = (w_uq * mla_scale).reshape(depth, MLA_Q_RANK, HEADS, MLA_NOPE_DIM + MLA_ROPE_DIM)
    zq = jnp.zeros((depth, MLA_Q_RANK, HEADS, MLA_QK_PAD - MLA_NOPE_DIM - MLA_ROPE_DIM), F32)
    wuq = jnp.concatenate([uq, zq], axis=-1).reshape(depth, MLA_Q_RANK, HEADS * MLA_QK_PAD)
    wuqs = jnp.concatenate([_swap_halves(uq[..., MLA_NOPE_DIM:]), zq], axis=-1)
    wuqs = wuqs.reshape(depth, MLA_Q_RANK, HEADS * LANES)
    ukv = w_ukv.reshape(depth, MLA_KV_RANK, HEADS, MLA_NOPE_DIM + MLA_V_DIM)
    wukvk = ukv[..., :MLA_NOPE_DIM].reshape(depth, MLA_KV_RANK, HEADS * MLA_NOPE_DIM)
    wukvv = ukv[..., MLA_NOPE_DIM:].reshape(depth, MLA_KV_RANK, HEADS * MLA_V_DIM)
    row = lambda v: v.reshape(depth, 1, -1).astype(F32)
    return dict(
        w1=w1.astype(BF16), wvt=jnp.swapaxes(w_in[..., o_v:o_cq], 1, 2).astype(BF16),
        wuq=wuq.astype(BF16), wuqs=wuqs.astype(BF16), wukvk=wukvk.astype(BF16),
        wukvvt=jnp.swapaxes(wukvv, 1, 2).astype(BF16), gq=row(q_norm_g), gkv=row(kv_norm_g),
        subln=row(subln_g),
        lam=jnp.stack([lambda_q1, lambda_k1, lambda_q2, lambda_k2], axis=1).astype(F32),
        wo=w_o.astype(BF16), g1=row(ln1_g), b1=row(ln1_b), wi=w_mlp_in.astype(BF16),
        wout=w_mlp_out.astype(BF16), g2=row(ln2_g), b2=row(ln2_b))


def kernel(x, w_in, lambda_q1, lambda_k1, lambda_q2, lambda_k2, subln_g, q_norm_g, w_uq, kv_norm_g,
           w_ukv, rel_bias, w_o, ln1_g, ln1_b, w_mlp_in, w_mlp_out, ln2_g, ln2_b):
    batch, seq, _ = x.shape
    assert seq % PROJ_ROWS == 0 and seq % ATT_TILE == 0 and (batch * seq) % POST_ROWS == 0
    wp = _pack_params(w_in, lambda_q1, lambda_k1, lambda_q2, lambda_k2, subln_g, q_norm_g, w_uq,
                      kv_norm_g, w_ukv, w_o, ln1_g, ln1_b, w_mlp_in, w_mlp_out, ln2_g, ln2_b)
    wp["cos"], wp["sin"] = _rope_tables(seq)
    bias = _bias_tiles(rel_bias)
    x2 = x.reshape(batch * seq, D_MODEL)
    for l in range(DEPTH):
        lambda_init = 0.8 - 0.6 * math.exp(-0.3 * l)
        dq, dk, dvt, qm, km, vmt = _proj_call(x2, wp, l, seq)
        mix_a = _diff_attn_call(dq, dk, dvt, bias, wp, l, seq, lambda_init)
        mix_b = _mla_attn_call(qm, km, vmt, seq)
        x2 = _post_call(x2, mix_a, mix_b, wp, l)
    return x2.reshape(batch, seq, D_MODEL)
```

```python
import functools
import math

import jax
import jax.numpy as jnp
from jax import lax
from jax.experimental import pallas as pl
from jax.experimental.pallas import tpu as pltpu

F32 = jnp.float32
BF16 = jnp.bfloat16

D_MODEL = 1024
DEPTH = 2
CHUNK = 64
HEADS = 4
DIFF_HEAD_DIM = 64
DIFF_V_DIM = 2 * DIFF_HEAD_DIM
DIFF_WIDTH = HEADS * DIFF_V_DIM
MLA_NOPE_DIM = 128
MLA_ROPE_DIM = 64
MLA_V_DIM = 128
MLA_Q_RANK = 256
MLA_KV_RANK = 128
MLA_WIDTH = HEADS * MLA_V_DIM
MLA_QK_PAD = 256
D_FF = 4 * D_MODEL
N_BUCKETS = 32
MAX_DISTANCE = 128
ROPE_THETA = 10000.0
ALPHA = (2 * DEPTH) ** 0.25
LN_EPS = 1e-5
RMS_EPS = 1e-6
NEG_BIG = -1e30
LOG2E = math.log2(math.e)

LANES = 128
BF16_SUBLANES = 16
ATT_TILE = 256
ATT_HEADS_PER_STEP = 2
PROJ_ROWS = 1024
POST_ROWS = 1024
POST_ROW_BLOCK = 256
FF_CHUNK = 1024
DIFF_LOOKAHEAD = 4
MLA_LOOKAHEAD = 8
VMEM_LIMIT = 56 * 1024 * 1024
assert ATT_TILE >= MAX_DISTANCE and ATT_TILE % CHUNK == 0

C_DQ = 0
C_DK = C_DQ + DIFF_WIDTH
C_CQ = C_DK + DIFF_WIDTH
C_CKV = C_CQ + MLA_Q_RANK
C_KR = C_CKV + MLA_KV_RANK
W1_COLS = C_KR + LANES

_NT = (((1,), (1,)), ((), ()))


def _rms(x, g):
    return x * lax.rsqrt(jnp.mean(x * x, axis=-1, keepdims=True) + RMS_EPS) * g


def _layer_norm(x, g, b):
    mu = jnp.mean(x, axis=-1, keepdims=True)
    xc = x - mu
    var = jnp.mean(xc * xc, axis=-1, keepdims=True)
    return xc * lax.rsqrt(var + LN_EPS) * g + b


def _layer_spec(arr, l, **kwargs):
    index = (l,) + (0,) * (arr.ndim - 1)
    return pl.BlockSpec((1,) + arr.shape[1:], lambda *_: index, **kwargs)


def _proj_kernel(x_ref, w1_ref, wvt_ref, wuq_ref, wuqs_ref, wukvk_ref, wukvvt_ref,
                 gq_ref, gkv_ref, cos_ref, sin_ref,
                 dq_ref, dk_ref, dvt_ref, qm_ref, km_ref, vmt_ref):
    xb = x_ref[...].astype(BF16)
    h = jnp.dot(xb, w1_ref[0], preferred_element_type=F32)
    dq_ref[...] = h[:, C_DQ:C_DQ + DIFF_WIDTH].astype(BF16)
    dk_ref[...] = h[:, C_DK:C_DK + DIFF_WIDTH].astype(BF16)
    dvt_ref[0] = lax.dot_general(wvt_ref[0], xb, _NT, preferred_element_type=F32).astype(BF16)

    cos = cos_ref[...]
    sin = sin_ref[...]
    c_q = _rms(h[:, C_CQ:C_CQ + MLA_Q_RANK], gq_ref[0]).astype(BF16)
    qf = jnp.dot(c_q, wuq_ref[0], preferred_element_type=F32)
    qsw = jnp.dot(c_q, wuqs_ref[0], preferred_element_type=F32)
    c_kv = _rms(h[:, C_CKV:C_CKV + MLA_KV_RANK], gkv_ref[0]).astype(BF16)
    kn = jnp.dot(c_kv, wukvk_ref[0], preferred_element_type=F32)
    vmt_ref[0] = lax.dot_general(wukvvt_ref[0], c_kv, _NT,
                                 preferred_element_type=F32).astype(BF16)
    kr = h[:, C_KR:C_KR + LANES]
    k_rope = (kr * cos + pltpu.roll(kr, LANES // 2, 1) * sin).astype(BF16)

    for hh in range(HEADS):
        b0 = hh * MLA_QK_PAD
        qm_ref[:, b0:b0 + LANES] = qf[:, b0:b0 + LANES].astype(BF16)
        qm_ref[:, b0 + LANES:b0 + 2 * LANES] = (
            qf[:, b0 + LANES:b0 + 2 * LANES] * cos + qsw[:, hh * LANES:(hh + 1) * LANES] * sin
        ).astype(BF16)
        km_ref[:, b0:b0 + LANES] = kn[:, hh * LANES:(hh + 1) * LANES].astype(BF16)
        km_ref[:, b0 + LANES:b0 + 2 * LANES] = k_rope


def _proj_call(x2, wp, l, seq):
    ntok = x2.shape[0]
    batch = ntok // seq
    rows = PROJ_ROWS
    steps_per_seq = seq // rows

    def const(name):
        return _layer_spec(wp[name], l)

    def tok(width):
        return pl.BlockSpec((rows, width), lambda i: (i, 0))

    vt_spec = pl.BlockSpec((1, HEADS * LANES, rows),
                           lambda i: (i // steps_per_seq, 0, i % steps_per_seq))
    pos_spec = pl.BlockSpec((rows, LANES), lambda i: (i % steps_per_seq, 0))
    vt_shape = jax.ShapeDtypeStruct((batch, HEADS * LANES, seq), BF16)
    return pl.pallas_call(
        _proj_kernel,
        grid=(ntok // rows,),
        in_specs=[tok(D_MODEL), const("w1"), const("wvt"), const("wuq"), const("wuqs"),
                  const("wukvk"), const("wukvvt"), const("gq"), const("gkv"), pos_spec, pos_spec],
        out_specs=[tok(DIFF_WIDTH), tok(DIFF_WIDTH), vt_spec,
                   tok(HEADS * MLA_QK_PAD), tok(HEADS * MLA_QK_PAD), vt_spec],
        out_shape=[jax.ShapeDtypeStruct((ntok, DIFF_WIDTH), BF16),
                   jax.ShapeDtypeStruct((ntok, DIFF_WIDTH), BF16),
                   vt_shape,
                   jax.ShapeDtypeStruct((ntok, HEADS * MLA_QK_PAD), BF16),
                   jax.ShapeDtypeStruct((ntok, HEADS * MLA_QK_PAD), BF16),
                   vt_shape],
        compiler_params=pltpu.CompilerParams(dimension_semantics=("arbitrary",),
                                             vmem_limit_bytes=VMEM_LIMIT),
        name="proj",
    )(x2, wp["w1"], wp["wvt"], wp["wuq"], wp["wuqs"], wp["wukvk"], wp["wukvvt"],
      wp["gq"], wp["gkv"], wp["cos"], wp["sin"])


def _col_max(a, b):
    m = jnp.max(b, axis=0, keepdims=True)
    return m if a is None else jnp.maximum(a, m)


def _attend(n_heads, n_tiles, q_fn, score_fn, finish_fn, vt_fn, lookahead):
    ones = jnp.ones((BF16_SUBLANES, ATT_TILE), BF16)
    items = [(h, qi, c) for h in range(n_heads) for qi in range(n_tiles) for c in range(qi + 1)]
    qs, scores = {}, {}

    def issue(idx):
        h, qi, c = items[idx]
        if c == 0:
            qs[h, qi] = q_fn(h, qi)
        scores[idx] = score_fn(h, qi, qs[h, qi], c)

    for idx in range(min(lookahead, len(items))):
        issue(idx)
    m, acc = None, None
    for idx, (h, qi, c) in enumerate(items):
        if idx + lookahead < len(items):
            issue(idx + lookahead)
        s = scores.pop(idx)
        m_new = _col_max(m, s)
        e = jnp.exp2(s - m_new).astype(BF16)
        vt = jnp.concatenate([vt_fn(h, c), ones], axis=0)
        pv = jnp.dot(vt, e, preferred_element_type=F32)
        acc = pv if acc is None else acc * jnp.exp2(m - m_new) + pv
        m = m_new
        if c == qi:
            finish_fn(h, qi, acc[:LANES] / acc[LANES:LANES + 1])
            m, acc = None, None


def _diff_attn_kernel(q_ref, k_ref, vt_ref, bias_ref, lam_ref, g_ref, o_ref, *, lambda_init):
    t = ATT_TILE
    lane = lax.broadcasted_iota(jnp.int32, (t, LANES), 1)
    lp = lam_ref[0]
    lam = (jnp.exp(jnp.sum(lp[0:1] * lp[1:2], axis=-1, keepdims=True))
           - jnp.exp(jnp.sum(lp[2:3] * lp[3:4], axis=-1, keepdims=True)) + lambda_init)
    g = g_ref[0]

    def q_rows(h, qi):
        q = q_ref[qi * t:(qi + 1) * t, h * LANES:(h + 1) * LANES]
        zero = jnp.zeros_like(q)
        return jnp.concatenate([jnp.where(lane < DIFF_HEAD_DIM, q, zero),
                                jnp.where(lane >= DIFF_HEAD_DIM, q, zero)], axis=0)

    def score(h, qi, qs, c):
        k = k_ref[c * t:(c + 1) * t, h * LANES:(h + 1) * LANES]
        s = lax.dot_general(k, qs, _NT, preferred_element_type=F32)
        return s + bias_ref[h, qi - c] if qi - c <= 1 else s

    def values(h, c):
        return vt_ref[0, h * LANES:(h + 1) * LANES, c * t:(c + 1) * t]

    def finish(h, qi, o):
        ot = o[:, :t] - lam * o[:, t:]
        y = _rms(ot.T, g) * (1.0 - lambda_init)
        o_ref[qi * t:(qi + 1) * t, h * LANES:(h + 1) * LANES] = y.astype(BF16)

    _attend(ATT_HEADS_PER_STEP, q_ref.shape[0] // t, q_rows, score, finish, values,
            DIFF_LOOKAHEAD)


def _mla_attn_kernel(q_ref, k_ref, vt_ref, o_ref):
    t = ATT_TILE
    w = MLA_QK_PAD
    kc = lax.broadcasted_iota(jnp.int32, (t, t), 0) // CHUNK
    qc = lax.broadcasted_iota(jnp.int32, (t, t), 1) // CHUNK
    allowed = kc <= qc

    def q_rows(h, qi):
        return q_ref[qi * t:(qi + 1) * t, h * w:(h + 1) * w]

    def score(h, qi, q, c):
        k = k_ref[c * t:(c + 1) * t, h * w:(h + 1) * w]
        s = lax.dot_general(k, q, _NT, preferred_element_type=F32)
        return jnp.where(allowed, s, NEG_BIG) if c == qi else s

    def values(h, c):
        return vt_ref[0, h * LANES:(h + 1) * LANES, c * t:(c + 1) * t]

    def finish(h, qi, o):
        o_ref[qi * t:(qi + 1) * t, h * LANES:(h + 1) * LANES] = o.T.astype(BF16)

    _attend(ATT_HEADS_PER_STEP, q_ref.shape[0] // t, q_rows, score, finish, values,
            MLA_LOOKAHEAD)


def _attn_specs(seq, q_width):
    n = ATT_HEADS_PER_STEP
    q_spec = pl.BlockSpec((seq, n * q_width), lambda b, h: (b, h))
    vt_spec = pl.BlockSpec((1, n * LANES, seq), lambda b, h: (b, h, 0))
    o_spec = pl.BlockSpec((seq, n * LANES), lambda b, h: (b, h))
    return q_spec, vt_spec, o_spec


_ATTN_PARAMS = pltpu.CompilerParams(dimension_semantics=("arbitrary", "arbitrary"),
                                    vmem_limit_bytes=VMEM_LIMIT)


def _diff_attn_call(dq, dk, dvt, bias, wp, l, seq, lambda_init):
    ntok = dq.shape[0]
    q_spec, vt_spec, o_spec = _attn_specs(seq, LANES)
    return pl.pallas_call(
        functools.partial(_diff_attn_kernel, lambda_init=lambda_init),
        grid=(ntok // seq, HEADS // ATT_HEADS_PER_STEP),
        in_specs=[q_spec, q_spec, vt_spec,
                  pl.BlockSpec((ATT_HEADS_PER_STEP,) + bias.shape[1:], lambda b, h: (h, 0, 0, 0)),
                  _layer_spec(wp["lam"], l), _layer_spec(wp["subln"], l)],
        out_specs=o_spec,
        out_shape=jax.ShapeDtypeStruct((ntok, DIFF_WIDTH), BF16),
        compiler_params=_ATTN_PARAMS,
        name="diff_attn",
    )(dq, dk, dvt, bias, wp["lam"], wp["subln"])


def _mla_attn_call(qm, km, vmt, seq):
    ntok = qm.shape[0]
    q_spec, vt_spec, o_spec = _attn_specs(seq, MLA_QK_PAD)
    return pl.pallas_call(
        _mla_attn_kernel,
        grid=(ntok // seq, HEADS // ATT_HEADS_PER_STEP),
        in_specs=[q_spec, q_spec, vt_spec],
        out_specs=o_spec,
        out_shape=jax.ShapeDtypeStruct((ntok, MLA_WIDTH), BF16),
        compiler_params=_ATTN_PARAMS,
        name="mla_attn",
    )(qm, km, vmt)


def _post_kernel(x_ref, ma_ref, mb_ref, wo_ref, g1_ref, b1_ref, wi_ref, wout_ref, g2_ref, b2_ref,
                 o_ref):
    n_blocks = x_ref.shape[0] // POST_ROW_BLOCK
    n_chunks = D_FF // FF_CHUNK

    def attn_out(r):
        rs = slice(r * POST_ROW_BLOCK, (r + 1) * POST_ROW_BLOCK)
        y = (jnp.dot(ma_ref[rs, :], wo_ref[0, :DIFF_WIDTH, :], preferred_element_type=F32)
             + jnp.dot(mb_ref[rs, :], wo_ref[0, DIFF_WIDTH:, :], preferred_element_type=F32))
        return _layer_norm(ALPHA * x_ref[rs, :] + y, g1_ref[0], b1_ref[0])

    def up(x1b, c):
        return jnp.dot(x1b, wi_ref[0, :, c * FF_CHUNK:(c + 1) * FF_CHUNK],
                       preferred_element_type=F32)

    x1 = attn_out(0)
    for r in range(n_blocks):
        x1_next = attn_out(r + 1) if r + 1 < n_blocks else None
        x1b = x1.astype(BF16)
        u = up(x1b, 0)
        y2 = None
        for c in range(n_chunks):
            u_next = up(x1b, c + 1) if c + 1 < n_chunks else None
            act = jnp.square(jnp.maximum(u, 0.0)).astype(BF16)
            d = jnp.dot(act, wout_ref[0, c * FF_CHUNK:(c + 1) * FF_CHUNK, :],
                        preferred_element_type=F32)
            y2 = d if y2 is None else y2 + d
            u = u_next
        o_ref[r * POST_ROW_BLOCK:(r + 1) * POST_ROW_BLOCK, :] = _layer_norm(
            ALPHA * x1 + y2, g2_ref[0], b2_ref[0])
        x1 = x1_next


def _post_call(x2, mix_a, mix_b, wp, l):
    ntok = x2.shape[0]
    rows = POST_ROWS

    def const(name):
        return _layer_spec(wp[name], l, pipeline_mode=pl.Buffered(1))

    def tok(width):
        return pl.BlockSpec((rows, width), lambda i: (i, 0))

    return pl.pallas_call(
        _post_kernel,
        grid=(ntok // rows,),
        in_specs=[tok(D_MODEL), tok(DIFF_WIDTH), tok(MLA_WIDTH), const("wo"), const("g1"),
                  const("b1"), const("wi"), const("wout"), const("g2"), const("b2")],
        out_specs=tok(D_MODEL),
        out_shape=jax.ShapeDtypeStruct((ntok, D_MODEL), F32),
        compiler_params=pltpu.CompilerParams(dimension_semantics=("arbitrary",),
                                             vmem_limit_bytes=VMEM_LIMIT),
        name="post",
    )(x2, mix_a, mix_b, wp["wo"], wp["g1"], wp["b1"], wp["wi"], wp["wout"], wp["g2"], wp["b2"])


def _t5_bucket(rel):
    nb = N_BUCKETS // 2
    ret = (rel > 0).astype(jnp.int32) * nb
    n = jnp.abs(rel)
    max_exact = nb // 2
    nf = jnp.maximum(n, 1).astype(F32)
    large = max_exact + (jnp.log(nf / max_exact) / math.log(MAX_DISTANCE / max_exact)
                         * (nb - max_exact)).astype(jnp.int32)
    large = jnp.minimum(large, nb - 1)
    return ret + jnp.where(n < max_exact, n, large)


def _bias_tiles(rel_bias):
    t = ATT_TILE
    p = 2 * t
    table = rel_bias.astype(F32).T
    far = table[:, N_BUCKETS // 2 - 1]
    w = jnp.arange(p, dtype=jnp.int32)[None, :]
    d = jnp.arange(2, dtype=jnp.int32)[:, None]
    rel = (t - 1) - w - d * t
    vec = (table[:, _t5_bucket(rel)] - far[:, None, None]) * LOG2E
    rows = jnp.tile(vec, (1, 1, t))[..., :t * (p - 1)].reshape(HEADS, 2, t, p - 1)
    bias = rows[..., t - 1:2 * t - 1]
    kk = jnp.arange(t, dtype=jnp.int32)[:, None]
    qq = jnp.arange(t, dtype=jnp.int32)[None, :]
    allowed = (kk // CHUNK <= qq // CHUNK)[None, None] | (d > 0)[None, :, :, None]
    bias = jnp.where(allowed, bias, NEG_BIG)
    return jnp.concatenate([bias, bias], axis=-1)


def _rope_tables(seq):
    pos = jnp.arange(seq, dtype=F32)
    inv = ROPE_THETA ** (-jnp.arange(0, MLA_ROPE_DIM, 2, dtype=F32) / MLA_ROPE_DIM)
    ang = pos[:, None] * inv[None, :]
    cos, sin = jnp.cos(ang), jnp.sin(ang)
    z = jnp.zeros((seq, LANES - MLA_ROPE_DIM), F32)
    return (jnp.concatenate([cos, cos, z], axis=-1), jnp.concatenate([-sin, sin, z], axis=-1))


def _swap_halves(w):
    half = w.shape[-1] // 2
    return jnp.concatenate([w[..., half:], w[..., :half]], axis=-1)


def _pack_params(w_in, lambda_q1, lambda_k1, lambda_q2, lambda_k2, subln_g, q_norm_g, w_uq,
                 kv_norm_g, w_ukv, w_o, ln1_g, ln1_b, w_mlp_in, w_mlp_out, ln2_g, ln2_b):
    depth = w_in.shape[0]
    o_k, o_v, o_cq = DIFF_WIDTH, 2 * DIFF_WIDTH, 3 * DIFF_WIDTH
    o_ckv = o_cq + MLA_Q_RANK
    o_kr = o_ckv + MLA_KV_RANK
    kr = w_in[..., o_kr:]
    w1 = jnp.concatenate([w_in[..., :o_k] * (DIFF_HEAD_DIM ** -0.5 * LOG2E), w_in[..., o_k:o_v],
                          w_in[..., o_cq:o_ckv], w_in[..., o_ckv:o_kr], kr, _swap_halves(kr)],
                         axis=-1)
    mla_scale = (MLA_NOPE_DIM + MLA_ROPE_DIM) ** -0.5 * LOG2E
    uq = (w_uq * mla_scale).reshape(depth, MLA_Q_RANK, HEADS, MLA_NOPE_DIM + MLA_ROPE_DIM)
    zq = jnp.zeros((depth, MLA_Q_RANK, HEADS, MLA_QK_PAD - MLA_NOPE_DIM - MLA_ROPE_DIM), F32)
    wuq = jnp.concatenate([uq, zq], axis=-1).reshape(depth, MLA_Q_RANK, HEADS * MLA_QK_PAD)
    wuqs = jnp.concatenate([_swap_halves(uq[..., MLA_NOPE_DIM:]), zq], axis=-1)
    wuqs = wuqs.reshape(depth, MLA_Q_RANK, HEADS * LANES)
    ukv = w_ukv.reshape(depth, MLA_KV_RANK, HEADS, MLA_NOPE_DIM + MLA_V_DIM)
    wukvk = ukv[..., :MLA_NOPE_DIM].reshape(depth, MLA_KV_RANK, HEADS * MLA_NOPE_DIM)
    wukvv = ukv[..., MLA_NOPE_DIM:].reshape(depth, MLA_KV_RANK, HEADS * MLA_V_DIM)
    row = lambda v: v.reshape(depth, 1, -1).astype(F32)
    return dict(
        w1=w1.astype(BF16), wvt=jnp.swapaxes(w_in[..., o_v:o_cq], 1, 2).astype(BF16),
        wuq=wuq.astype(BF16), wuqs=wuqs.astype(BF16), wukvk=wukvk.astype(BF16),
        wukvvt=jnp.swapaxes(wukvv, 1, 2).astype(BF16), gq=row(q_norm_g), gkv=row(kv_norm_g),
        subln=row(subln_g),
        lam=jnp.stack([lambda_q1, lambda_k1, lambda_q2, lambda_k2], axis=1).astype(F32),
        wo=w_o.astype(BF16), g1=row(ln1_g), b1=row(ln1_b), wi=w_mlp_in.astype(BF16),
        wout=w_mlp_out.astype(BF16), g2=row(ln2_g), b2=row(ln2_b))


def kernel(x, w_in, lambda_q1, lambda_k1, lambda_q2, lambda_k2, subln_g, q_norm_g, w_uq, kv_norm_g,
           w_ukv, rel_bias, w_o, ln1_g, ln1_b, w_mlp_in, w_mlp_out, ln2_g, ln2_b):
    batch, seq, _ = x.shape
    assert seq % PROJ_ROWS == 0 and seq % ATT_TILE == 0 and (batch * seq) % POST_ROWS == 0
    wp = _pack_params(w_in, lambda_q1, lambda_k1, lambda_q2, lambda_k2, subln_g, q_norm_g, w_uq,
                      kv_norm_g, w_ukv, w_o, ln1_g, ln1_b, w_mlp_in, w_mlp_out, ln2_g, ln2_b)
    wp["cos"], wp["sin"] = _rope_tables(seq)
    bias = _bias_tiles(rel_bias)
    x2 = x.reshape(batch * seq, D_MODEL)
    for l in range(DEPTH):
        lambda_init = 0.8 - 0.6 * math.exp(-0.3 * l)
        dq, dk, dvt, qm, km, vmt = _proj_call(x2, wp, l, seq)
        mix_a = _diff_attn_call(dq, dk, dvt, bias, wp, l, seq, lambda_init)
        mix_b = _mla_attn_call(qm, km, vmt, seq)
        x2 = _post_call(x2, mix_a, mix_b, wp, l)
    return x2.reshape(batch, seq, D_MODEL)
```

```python
import functools
import math

import jax
import jax.numpy as jnp
from jax import lax
from jax.experimental import pallas as pl
from jax.experimental.pallas import tpu as pltpu

F32 = jnp.float32
BF16 = jnp.bfloat16

D_MODEL = 1024
DEPTH = 2
CHUNK = 64
HEADS = 4
DIFF_HEAD_DIM = 64
DIFF_V_DIM = 2 * DIFF_HEAD_DIM
DIFF_WIDTH = HEADS * DIFF_V_DIM
MLA_NOPE_DIM = 128
MLA_ROPE_DIM = 64
MLA_V_DIM = 128
MLA_Q_RANK = 256
MLA_KV_RANK = 128
MLA_WIDTH = HEADS * MLA_V_DIM
MLA_QK_PAD = 256
D_FF = 4 * D_MODEL
N_BUCKETS = 32
MAX_DISTANCE = 128
ROPE_THETA = 10000.0
ALPHA = (2 * DEPTH) ** 0.25
LN_EPS = 1e-5
RMS_EPS = 1e-6
NEG_BIG = -1e30
LOG2E = math.log2(math.e)

LANES = 128
BF16_SUBLANES = 16
ATT_TILE = 256
ATT_HEADS_PER_STEP = 2
PROJ_ROWS = 1024
POST_ROWS = 1024
POST_ROW_BLOCK = 256
FF_CHUNK = 1024
DIFF_LOOKAHEAD = 2
MLA_LOOKAHEAD = 4
VMEM_LIMIT = 56 * 1024 * 1024
assert ATT_TILE >= MAX_DISTANCE and ATT_TILE % CHUNK == 0

C_DQ = 0
C_DK = C_DQ + DIFF_WIDTH
C_CQ = C_DK + DIFF_WIDTH
C_CKV = C_CQ + MLA_Q_RANK
C_KR = C_CKV + MLA_KV_RANK
W1_COLS = C_KR + LANES

_NT = (((1,), (1,)), ((), ()))


def _rms(x, g):
    return x * lax.rsqrt(jnp.mean(x * x, axis=-1, keepdims=True) + RMS_EPS) * g


def _layer_norm(x, g, b):
    mu = jnp.mean(x, axis=-1, keepdims=True)
    xc = x - mu
    var = jnp.mean(xc * xc, axis=-1, keepdims=True)
    return xc * lax.rsqrt(var + LN_EPS) * g + b


def _layer_spec(arr, l, **kwargs):
    index = (l,) + (0,) * (arr.ndim - 1)
    return pl.BlockSpec((1,) + arr.shape[1:], lambda *_: index, **kwargs)


def _proj_kernel(x_ref, w1_ref, wvt_ref, wuq_ref, wuqs_ref, wukvk_ref, wukvvt_ref,
                 gq_ref, gkv_ref, cos_ref, sin_ref,
                 dq_ref, dk_ref, dvt_ref, qm_ref, km_ref, vmt_ref):
    xb = x_ref[...].astype(BF16)
    h = jnp.dot(xb, w1_ref[0], preferred_element_type=F32)
    dq_ref[...] = h[:, C_DQ:C_DQ + DIFF_WIDTH].astype(BF16)
    dk_ref[...] = h[:, C_DK:C_DK + DIFF_WIDTH].astype(BF16)
    dvt_ref[0] = lax.dot_general(wvt_ref[0], xb, _NT, preferred_element_type=F32).astype(BF16)

    cos = cos_ref[...]
    sin = sin_ref[...]
    c_q = _rms(h[:, C_CQ:C_CQ + MLA_Q_RANK], gq_ref[0]).astype(BF16)
    qf = jnp.dot(c_q, wuq_ref[0], preferred_element_type=F32)
    qsw = jnp.dot(c_q, wuqs_ref[0], preferred_element_type=F32)
    c_kv = _rms(h[:, C_CKV:C_CKV + MLA_KV_RANK], gkv_ref[0]).astype(BF16)
    kn = jnp.dot(c_kv, wukvk_ref[0], preferred_element_type=F32)
    vmt_ref[0] = lax.dot_general(wukvvt_ref[0], c_kv, _NT,
                                 preferred_element_type=F32).astype(BF16)
    kr = h[:, C_KR:C_KR + LANES]
    k_rope = (kr * cos + pltpu.roll(kr, LANES // 2, 1) * sin).astype(BF16)

    for hh in range(HEADS):
        b0 = hh * MLA_QK_PAD
        qm_ref[:, b0:b0 + LANES] = qf[:, b0:b0 + LANES].astype(BF16)
        qm_ref[:, b0 + LANES:b0 + 2 * LANES] = (
            qf[:, b0 + LANES:b0 + 2 * LANES] * cos + qsw[:, hh * LANES:(hh + 1) * LANES] * sin
        ).astype(BF16)
        km_ref[:, b0:b0 + LANES] = kn[:, hh * LANES:(hh + 1) * LANES].astype(BF16)
        km_ref[:, b0 + LANES:b0 + 2 * LANES] = k_rope


def _proj_call(x2, wp, l, seq):
    ntok = x2.shape[0]
    batch = ntok // seq
    rows = PROJ_ROWS
    steps_per_seq = seq // rows

    def const(name):
        return _layer_spec(wp[name], l)

    def tok(width):
        return pl.BlockSpec((rows, width), lambda i: (i, 0))

    vt_spec = pl.BlockSpec((1, HEADS * LANES, rows),
                           lambda i: (i // steps_per_seq, 0, i % steps_per_seq))
    pos_spec = pl.BlockSpec((rows, LANES), lambda i: (i % steps_per_seq, 0))
    vt_shape = jax.ShapeDtypeStruct((batch, HEADS * LANES, seq), BF16)
    return pl.pallas_call(
        _proj_kernel,
        grid=(ntok // rows,),
        in_specs=[tok(D_MODEL), const("w1"), const("wvt"), const("wuq"), const("wuqs"),
                  const("wukvk"), const("wukvvt"), const("gq"), const("gkv"), pos_spec, pos_spec],
        out_specs=[tok(DIFF_WIDTH), tok(DIFF_WIDTH), vt_spec,
                   tok(HEADS * MLA_QK_PAD), tok(HEADS * MLA_QK_PAD), vt_spec],
        out_shape=[jax.ShapeDtypeStruct((ntok, DIFF_WIDTH), BF16),
                   jax.ShapeDtypeStruct((ntok, DIFF_WIDTH), BF16),
                   vt_shape,
                   jax.ShapeDtypeStruct((ntok, HEADS * MLA_QK_PAD), BF16),
                   jax.ShapeDtypeStruct((ntok, HEADS * MLA_QK_PAD), BF16),
                   vt_shape],
        compiler_params=pltpu.CompilerParams(dimension_semantics=("arbitrary",),
                                             vmem_limit_bytes=VMEM_LIMIT),
        name="proj",
    )(x2, wp["w1"], wp["wvt"], wp["wuq"], wp["wuqs"], wp["wukvk"], wp["wukvvt"],
      wp["gq"], wp["gkv"], wp["cos"], wp["sin"])


def _col_max(a, b):
    m = jnp.max(b, axis=0, keepdims=True)
    return m if a is None else jnp.maximum(a, m)


def _attend(n_heads, n_tiles, q_fn, score_fn, finish_fn, vt_fn, lookahead):
    ones = jnp.ones((BF16_SUBLANES, ATT_TILE), BF16)
    items = [(h, qi, c) for h in range(n_heads) for qi in range(n_tiles) for c in range(qi + 1)]
    qs, scores = {}, {}

    def issue(idx):
        h, qi, c = items[idx]
        if c == 0:
            qs[h, qi] = q_fn(h, qi)
        scores[idx] = score_fn(h, qi, qs[h, qi], c)

    for idx in range(min(lookahead, len(items))):
        issue(idx)
    m, acc = None, None
    for idx, (h, qi, c) in enumerate(items):
        if idx + lookahead < len(items):
            issue(idx + lookahead)
        s = scores.pop(idx)
        m_new = _col_max(m, s)
        e = jnp.exp2(s - m_new).astype(BF16)
        vt = jnp.concatenate([vt_fn(h, c), ones], axis=0)
        pv = jnp.dot(vt, e, preferred_element_type=F32)
        acc = pv if acc is None else acc * jnp.exp2(m - m_new) + pv
        m = m_new
        if c == qi:
            finish_fn(h, qi, acc[:LANES] / acc[LANES:LANES + 1])
            m, acc = None, None
        yield


def _diff_steps(q_ref, k_ref, vt_ref, bias_ref, lam_ref, g_ref, o_ref, lambda_init):
    t = ATT_TILE
    lane = lax.broadcasted_iota(jnp.int32, (t, LANES), 1)
    lp = lam_ref[0]
    lam = (jnp.exp(jnp.sum(lp[0:1] * lp[1:2], axis=-1, keepdims=True))
           - jnp.exp(jnp.sum(lp[2:3] * lp[3:4], axis=-1, keepdims=True)) + lambda_init)
    g = g_ref[0]

    def q_rows(h, qi):
        q = q_ref[qi * t:(qi + 1) * t, h * LANES:(h + 1) * LANES]
        zero = jnp.zeros_like(q)
        return jnp.concatenate([jnp.where(lane < DIFF_HEAD_DIM, q, zero),
                                jnp.where(lane >= DIFF_HEAD_DIM, q, zero)], axis=0)

    def score(h, qi, qs, c):
        k = k_ref[c * t:(c + 1) * t, h * LANES:(h + 1) * LANES]
        s = lax.dot_general(k, qs, _NT, preferred_element_type=F32)
        return s + bias_ref[h, qi - c] if qi - c <= 1 else s

    def values(h, c):
        return vt_ref[0, h * LANES:(h + 1) * LANES, c * t:(c + 1) * t]

    def finish(h, qi, o):
        ot = o[:, :t] - lam * o[:, t:]
        y = _rms(ot.T, g) * (1.0 - lambda_init)
        o_ref[qi * t:(qi + 1) * t, h * LANES:(h + 1) * LANES] = y.astype(BF16)

    return _attend(ATT_HEADS_PER_STEP, q_ref.shape[0] // t, q_rows, score, finish, values,
                   DIFF_LOOKAHEAD)


def _mla_steps(q_ref, k_ref, vt_ref, o_ref):
    t = ATT_TILE
    w = MLA_QK_PAD
    kc = lax.broadcasted_iota(jnp.int32, (t, t), 0) // CHUNK
    qc = lax.broadcasted_iota(jnp.int32, (t, t), 1) // CHUNK
    allowed = kc <= qc

    def q_rows(h, qi):
        return q_ref[qi * t:(qi + 1) * t, h * w:(h + 1) * w]

    def score(h, qi, q, c):
        k = k_ref[c * t:(c + 1) * t, h * w:(h + 1) * w]
        s = lax.dot_general(k, q, _NT, preferred_element_type=F32)
        return jnp.where(allowed, s, NEG_BIG) if c == qi else s

    def values(h, c):
        return vt_ref[0, h * LANES:(h + 1) * LANES, c * t:(c + 1) * t]

    def finish(h, qi, o):
        o_ref[qi * t:(qi + 1) * t, h * LANES:(h + 1) * LANES] = o.T.astype(BF16)

    return _attend(ATT_HEADS_PER_STEP, q_ref.shape[0] // t, q_rows, score, finish, values,
                   MLA_LOOKAHEAD)


def _attn_kernel(dq_ref, dk_ref, dvt_ref, bias_ref, lam_ref, g_ref, qm_ref, km_ref, vmt_ref,
                 oa_ref, ob_ref, *, lambda_init):
    streams = [_diff_steps(dq_ref, dk_ref, dvt_ref, bias_ref, lam_ref, g_ref, oa_ref, lambda_init),
               _mla_steps(qm_ref, km_ref, vmt_ref, ob_ref)]
    while streams:
        for s in list(streams):
            if next(s, "done") == "done":
                streams.remove(s)


def _attn_call(dq, dk, dvt, bias, wp, l, qm, km, vmt, seq, lambda_init):
    ntok = dq.shape[0]
    n = ATT_HEADS_PER_STEP

    def rows(width):
        return pl.BlockSpec((seq, n * width), lambda b, h: (b, h))

    vt_spec = pl.BlockSpec((1, n * LANES, seq), lambda b, h: (b, h, 0))
    return pl.pallas_call(
        functools.partial(_attn_kernel, lambda_init=lambda_init),
        grid=(ntok // seq, HEADS // n),
        in_specs=[rows(LANES), rows(LANES), vt_spec,
                  pl.BlockSpec((n,) + bias.shape[1:], lambda b, h: (h, 0, 0, 0)),
                  _layer_spec(wp["lam"], l), _layer_spec(wp["subln"], l),
                  rows(MLA_QK_PAD), rows(MLA_QK_PAD), vt_spec],
        out_specs=[rows(LANES), rows(LANES)],
        out_shape=[jax.ShapeDtypeStruct((ntok, DIFF_WIDTH), BF16),
                   jax.ShapeDtypeStruct((ntok, MLA_WIDTH), BF16)],
        compiler_params=pltpu.CompilerParams(dimension_semantics=("arbitrary", "arbitrary"),
                                             vmem_limit_bytes=VMEM_LIMIT),
        name="attn",
    )(dq, dk, dvt, bias, wp["lam"], wp["subln"], qm, km, vmt)


def _post_kernel(x_ref, ma_ref, mb_ref, wo_ref, g1_ref, b1_ref, wi_ref, wout_ref, g2_ref, b2_ref,
                 o_ref):
    n_blocks = x_ref.shape[0] // POST_ROW_BLOCK
    n_chunks = D_FF // FF_CHUNK

    def attn_out(r):
        rs = slice(r * POST_ROW_BLOCK, (r + 1) * POST_ROW_BLOCK)
        y = (jnp.dot(ma_ref[rs, :], wo_ref[0, :DIFF_WIDTH, :], preferred_element_type=F32)
             + jnp.dot(mb_ref[rs, :], wo_ref[0, DIFF_WIDTH:, :], preferred_element_type=F32))
        return _layer_norm(ALPHA * x_ref[rs, :] + y, g1_ref[0], b1_ref[0])

    def up(x1b, c):
        return jnp.dot(x1b, wi_ref[0, :, c * FF_CHUNK:(c + 1) * FF_CHUNK],
                       preferred_element_type=F32)

    x1 = attn_out(0)
    for r in range(n_blocks):
        x1_next = attn_out(r + 1) if r + 1 < n_blocks else None
        x1b = x1.astype(BF16)
        u = up(x1b, 0)
        y2 = None
        for c in range(n_chunks):
            u_next = up(x1b, c + 1) if c + 1 < n_chunks else None
            act = jnp.square(jnp.maximum(u, 0.0)).astype(BF16)
            d = jnp.dot(act, wout_ref[0, c * FF_CHUNK:(c + 1) * FF_CHUNK, :],
                        preferred_element_type=F32)
            y2 = d if y2 is None else y2 + d
            u = u_next
        o_ref[r * POST_ROW_BLOCK:(r + 1) * POST_ROW_BLOCK, :] = _layer_norm(
            ALPHA * x1 + y2, g2_ref[0], b2_ref[0])
        x1 = x1_next


def _post_call(x2, mix_a, mix_b, wp, l):
    ntok = x2.shape[0]
    rows = POST_ROWS

    def const(name):
        return _layer_spec(wp[name], l, pipeline_mode=pl.Buffered(1))

    def tok(width):
        return pl.BlockSpec((rows, width), lambda i: (i, 0))

    return pl.pallas_call(
        _post_kernel,
        grid=(ntok // rows,),
        in_specs=[tok(D_MODEL), tok(DIFF_WIDTH), tok(MLA_WIDTH), const("wo"), const("g1"),
                  const("b1"), const("wi"), const("wout"), const("g2"), const("b2")],
        out_specs=tok(D_MODEL),
        out_shape=jax.ShapeDtypeStruct((ntok, D_MODEL), F32),
        compiler_params=pltpu.CompilerParams(dimension_semantics=("arbitrary",),
                                             vmem_limit_bytes=VMEM_LIMIT),
        name="post",
    )(x2, mix_a, mix_b, wp["wo"], wp["g1"], wp["b1"], wp["wi"], wp["wout"], wp["g2"], wp["b2"])


def _t5_bucket(rel):
    nb = N_BUCKETS // 2
    ret = (rel > 0).astype(jnp.int32) * nb
    n = jnp.abs(rel)
    max_exact = nb // 2
    nf = jnp.maximum(n, 1).astype(F32)
    large = max_exact + (jnp.log(nf / max_exact) / math.log(MAX_DISTANCE / max_exact)
                         * (nb - max_exact)).astype(jnp.int32)
    large = jnp.minimum(large, nb - 1)
    return ret + jnp.where(n < max_exact, n, large)


def _bias_tiles(rel_bias):
    t = ATT_TILE
    p = 2 * t
    table = rel_bias.astype(F32).T
    far = table[:, N_BUCKETS // 2 - 1]
    w = jnp.arange(p, dtype=jnp.int32)[None, :]
    d = jnp.arange(2, dtype=jnp.int32)[:, None]
    rel = (t - 1) - w - d * t
    vec = (table[:, _t5_bucket(rel)] - far[:, None, None]) * LOG2E
    rows = jnp.tile(vec, (1, 1, t))[..., :t * (p - 1)].reshape(HEADS, 2, t, p - 1)
    bias = rows[..., t - 1:2 * t - 1]
    kk = jnp.arange(t, dtype=jnp.int32)[:, None]
    qq = jnp.arange(t, dtype=jnp.int32)[None, :]
    allowed = (kk // CHUNK <= qq // CHUNK)[None, None] | (d > 0)[None, :, :, None]
    bias = jnp.where(allowed, bias, NEG_BIG)
    return jnp.concatenate([bias, bias], axis=-1)


def _rope_tables(seq):
    pos = jnp.arange(seq, dtype=F32)
    inv = ROPE_THETA ** (-jnp.arange(0, MLA_ROPE_DIM, 2, dtype=F32) / MLA_ROPE_DIM)
    ang = pos[:, None] * inv[None, :]
    cos, sin = jnp.cos(ang), jnp.sin(ang)
    z = jnp.zeros((seq, LANES - MLA_ROPE_DIM), F32)
    return (jnp.concatenate([cos, cos, z], axis=-1), jnp.concatenate([-sin, sin, z], axis=-1))


def _swap_halves(w):
    half = w.shape[-1] // 2
    return jnp.concatenate([w[..., half:], w[..., :half]], axis=-1)


def _pack_params(w_in, lambda_q1, lambda_k1, lambda_q2, lambda_k2, subln_g, q_norm_g, w_uq,
                 kv_norm_g, w_ukv, w_o, ln1_g, ln1_b, w_mlp_in, w_mlp_out, ln2_g, ln2_b):
    depth = w_in.shape[0]
    o_k, o_v, o_cq = DIFF_WIDTH, 2 * DIFF_WIDTH, 3 * DIFF_WIDTH
    o_ckv = o_cq + MLA_Q_RANK
    o_kr = o_ckv + MLA_KV_RANK
    kr = w_in[..., o_kr:]
    w1 = jnp.concatenate([w_in[..., :o_k] * (DIFF_HEAD_DIM ** -0.5 * LOG2E), w_in[..., o_k:o_v],
                          w_in[..., o_cq:o_ckv], w_in[..., o_ckv:o_kr], kr, _swap_halves(kr)],
                         axis=-1)
    mla_scale = (MLA_NOPE_DIM + MLA_ROPE_DIM) ** -0.5 * LOG2E
    uq = (w_uq * mla_scale).reshape(depth, MLA_Q_RANK, HEADS, MLA_NOPE_DIM + MLA_ROPE_DIM)
    zq = jnp.zeros((depth, MLA_Q_RANK, HEADS, MLA_QK_PAD - MLA_NOPE_DIM - MLA_ROPE_DIM), F32)
    wuq = jnp.concatenate([uq, zq], axis=-1).reshape(depth, MLA_Q_RANK, HEADS * MLA_QK_PAD)
    wuqs = jnp.concatenate([_swap_halves(uq[..., MLA_NOPE_DIM:]), zq], axis=-1)
    wuqs = wuqs.reshape(depth, MLA_Q_RANK, HEADS * LANES)
    ukv = w_ukv.reshape(depth, MLA_KV_RANK, HEADS, MLA_NOPE_DIM + MLA_V_DIM)
    wukvk = ukv[..., :MLA_NOPE_DIM].reshape(depth, MLA_KV_RANK, HEADS * MLA_NOPE_DIM)
    wukvv = ukv[..., MLA_NOPE_DIM:].reshape(depth, MLA_KV_RANK, HEADS * MLA_V_DIM)
    row = lambda v: v.reshape(depth, 1, -1).astype(F32)
    return dict(
        w1=w1.astype(BF16), wvt=jnp.swapaxes(w_in[..., o_v:o_cq], 1, 2).astype(BF16),
        wuq=wuq.astype(BF16), wuqs=wuqs.astype(BF16), wukvk=wukvk.astype(BF16),
        wukvvt=jnp.swapaxes(wukvv, 1, 2).astype(BF16), gq=row(q_norm_g), gkv=row(kv_norm_g),
        subln=row(subln_g),
        lam=jnp.stack([lambda_q1, lambda_k1, lambda_q2, lambda_k2], axis=1).astype(F32),
        wo=w_o.astype(BF16), g1=row(ln1_g), b1=row(ln1_b), wi=w_mlp_in.astype(BF16),
        wout=w_mlp_out.astype(BF16), g2=row(ln2_g), b2=row(ln2_b))


def kernel(x, w_in, lambda_q1, lambda_k1, lambda_q2, lambda_k2, subln_g, q_norm_g, w_uq, kv_norm_g,
           w_ukv, rel_bias, w_o, ln1_g, ln1_b, w_mlp_in, w_mlp_out, ln2_g, ln2_b):
    batch, seq, _ = x.shape
    assert seq % PROJ_ROWS == 0 and seq % ATT_TILE == 0 and (batch * seq) % POST_ROWS == 0
    wp = _pack_params(w_in, lambda_q1, lambda_k1, lambda_q2, lambda_k2, subln_g, q_norm_g, w_uq,
                      kv_norm_g, w_ukv, w_o, ln1_g, ln1_b, w_mlp_in, w_mlp_out, ln2_g, ln2_b)
    wp["cos"], wp["sin"] = _rope_tables(seq)
    bias = _bias_tiles(rel_bias)
    x2 = x.reshape(batch * seq, D_MODEL)
    for l in range(DEPTH):
        lambda_init = 0.8 - 0.6 * math.exp(-0.3 * l)
        dq, dk, dvt, qm, km, vmt = _proj_call(x2, wp, l, seq)
        mix_a, mix_b = _attn_call(dq, dk, dvt, bias, wp, l, qm, km, vmt, seq, lambda_init)
        x2 = _post_call(x2, mix_a, mix_b, wp, l)
    return x2.reshape(batch, seq, D_MODEL)
```

```python
import functools
import math

import jax
import jax.numpy as jnp
from jax import lax
from jax.experimental import pallas as pl
from jax.experimental.pallas import tpu as pltpu

F32 = jnp.float32
BF16 = jnp.bfloat16

D_MODEL = 1024
DEPTH = 2
CHUNK = 64
HEADS = 4
DIFF_HEAD_DIM = 64
DIFF_V_DIM = 2 * DIFF_HEAD_DIM
DIFF_WIDTH = HEADS * DIFF_V_DIM
MLA_NOPE_DIM = 128
MLA_ROPE_DIM = 64
MLA_V_DIM = 128
MLA_Q_RANK = 256
MLA_KV_RANK = 128
MLA_WIDTH = HEADS * MLA_V_DIM
MLA_QK_PAD = 256
D_FF = 4 * D_MODEL
N_BUCKETS = 32
MAX_DISTANCE = 128
ROPE_THETA = 10000.0
ALPHA = (2 * DEPTH) ** 0.25
LN_EPS = 1e-5
RMS_EPS = 1e-6
NEG_BIG = -1e30
LOG2E = math.log2(math.e)

LANES = 128
BF16_SUBLANES = 16
ATT_TILE = 256
ATT_HEADS_PER_STEP = 2
PROJ_ROWS = 1024
POST_ROWS = 1024
POST_ROW_BLOCK = 256
FF_CHUNK = 1024
DIFF_LOOKAHEAD = 2
MLA_LOOKAHEAD = 4
VMEM_LIMIT = 56 * 1024 * 1024
assert ATT_TILE >= MAX_DISTANCE and ATT_TILE % CHUNK == 0

C_DQ = 0
C_DK = C_DQ + DIFF_WIDTH
C_CQ = C_DK + DIFF_WIDTH
C_CKV = C_CQ + MLA_Q_RANK
C_KR = C_CKV + MLA_KV_RANK
W1_COLS = C_KR + LANES

_NT = (((1,), (1,)), ((), ()))


def _rms(x, g):
    return x * lax.rsqrt(jnp.mean(x * x, axis=-1, keepdims=True) + RMS_EPS) * g


def _layer_norm(x, g, b):
    mu = jnp.mean(x, axis=-1, keepdims=True)
    xc = x - mu
    var = jnp.mean(xc * xc, axis=-1, keepdims=True)
    return xc * lax.rsqrt(var + LN_EPS) * g + b


def _layer_spec(arr, l, **kwargs):
    index = (l,) + (0,) * (arr.ndim - 1)
    return pl.BlockSpec((1,) + arr.shape[1:], lambda *_: index, **kwargs)


def _proj_kernel(x_ref, w1_ref, wvt_ref, wuq_ref, wukvk_ref, wukvvt_ref,
                 gq_ref, gkv_ref, cos_ref, sin_ref,
                 dq_ref, dk_ref, dvt_ref, qm_ref, km_ref, vmt_ref):
    xb = x_ref[...].astype(BF16)
    h = jnp.dot(xb, w1_ref[0], preferred_element_type=F32)
    dq_ref[...] = h[:, C_DQ:C_DQ + DIFF_WIDTH].astype(BF16)
    dk_ref[...] = h[:, C_DK:C_DK + DIFF_WIDTH].astype(BF16)
    dvt_ref[0] = lax.dot_general(wvt_ref[0], xb, _NT, preferred_element_type=F32).astype(BF16)

    cos = cos_ref[...]
    sin = sin_ref[...]
    c_q = _rms(h[:, C_CQ:C_CQ + MLA_Q_RANK], gq_ref[0]).astype(BF16)
    qf = jnp.dot(c_q, wuq_ref[0], preferred_element_type=F32)
    c_kv = _rms(h[:, C_CKV:C_CKV + MLA_KV_RANK], gkv_ref[0]).astype(BF16)
    kn = jnp.dot(c_kv, wukvk_ref[0], preferred_element_type=F32)
    vmt_ref[0] = lax.dot_general(wukvvt_ref[0], c_kv, _NT,
                                 preferred_element_type=F32).astype(BF16)
    kr = h[:, C_KR:C_KR + LANES]
    k_rope = (kr * cos + pltpu.roll(kr, LANES // 2, 1) * sin).astype(BF16)

    first_half = lax.broadcasted_iota(jnp.int32, kr.shape, 1) < MLA_ROPE_DIM // 2
    for hh in range(HEADS):
        b0 = hh * MLA_QK_PAD
        qm_ref[:, b0:b0 + LANES] = qf[:, b0:b0 + LANES].astype(BF16)
        qr = qf[:, b0 + LANES:b0 + 2 * LANES]
        qr_sw = jnp.where(first_half, pltpu.roll(qr, LANES - MLA_ROPE_DIM // 2, 1),
                          pltpu.roll(qr, MLA_ROPE_DIM // 2, 1))
        qm_ref[:, b0 + LANES:b0 + 2 * LANES] = (qr * cos + qr_sw * sin).astype(BF16)
        km_ref[:, b0:b0 + LANES] = kn[:, hh * LANES:(hh + 1) * LANES].astype(BF16)
        km_ref[:, b0 + LANES:b0 + 2 * LANES] = k_rope


def _proj_call(x2, wp, l, seq):
    ntok = x2.shape[0]
    batch = ntok // seq
    rows = PROJ_ROWS
    steps_per_seq = seq // rows

    def const(name):
        return _layer_spec(wp[name], l)

    def tok(width):
        return pl.BlockSpec((rows, width), lambda i: (i, 0))

    vt_spec = pl.BlockSpec((1, HEADS * LANES, rows),
                           lambda i: (i // steps_per_seq, 0, i % steps_per_seq))
    pos_spec = pl.BlockSpec((rows, LANES), lambda i: (i % steps_per_seq, 0))
    vt_shape = jax.ShapeDtypeStruct((batch, HEADS * LANES, seq), BF16)
    return pl.pallas_call(
        _proj_kernel,
        grid=(ntok // rows,),
        in_specs=[tok(D_MODEL), const("w1"), const("wvt"), const("wuq"), const("wukvk"),
                  const("wukvvt"), const("gq"), const("gkv"), pos_spec, pos_spec],
        out_specs=[tok(DIFF_WIDTH), tok(DIFF_WIDTH), vt_spec,
                   tok(HEADS * MLA_QK_PAD), tok(HEADS * MLA_QK_PAD), vt_spec],
        out_shape=[jax.ShapeDtypeStruct((ntok, DIFF_WIDTH), BF16),
                   jax.ShapeDtypeStruct((ntok, DIFF_WIDTH), BF16),
                   vt_shape,
                   jax.ShapeDtypeStruct((ntok, HEADS * MLA_QK_PAD), BF16),
                   jax.ShapeDtypeStruct((ntok, HEADS * MLA_QK_PAD), BF16),
                   vt_shape],
        compiler_params=pltpu.CompilerParams(dimension_semantics=("arbitrary",),
                                             vmem_limit_bytes=VMEM_LIMIT),
        name="proj",
    )(x2, wp["w1"], wp["wvt"], wp["wuq"], wp["wukvk"], wp["wukvvt"],
      wp["gq"], wp["gkv"], wp["cos"], wp["sin"])


def _col_max(a, b):
    m = jnp.max(b, axis=0, keepdims=True)
    return m if a is None else jnp.maximum(a, m)


def _attend(heads, n_tiles, q_fn, score_fn, finish_fn, vt_fn, lookahead):
    ones = jnp.ones((BF16_SUBLANES, ATT_TILE), BF16)
    items = [(h, qi, c) for h in heads for qi in range(n_tiles) for c in range(qi + 1)]
    qs, scores = {}, {}

    def issue(idx):
        h, qi, c = items[idx]
        if c == 0:
            qs[h, qi] = q_fn(h, qi)
        scores[idx] = score_fn(h, qi, qs[h, qi], c)

    for idx in range(min(lookahead, len(items))):
        issue(idx)
    m, acc = None, None
    for idx, (h, qi, c) in enumerate(items):
        if idx + lookahead < len(items):
            issue(idx + lookahead)
        s = scores.pop(idx)
        m_new = _col_max(m, s)
        e = jnp.exp2(s - m_new).astype(BF16)
        vt = jnp.concatenate([vt_fn(h, c), ones], axis=0)
        pv = jnp.dot(vt, e, preferred_element_type=F32)
        acc = pv if acc is None else acc * jnp.exp2(m - m_new) + pv
        m = m_new
        if c == qi:
            finish_fn(h, qi, acc[:LANES] / acc[LANES:LANES + 1])
            m, acc = None, None
        yield


def _diff_steps(heads, q_ref, k_ref, vt_ref, bias_ref, lam_ref, g_ref, o_ref, lambda_init):
    t = ATT_TILE
    lane = lax.broadcasted_iota(jnp.int32, (t, LANES), 1)
    lp = lam_ref[0]
    lam = (jnp.exp(jnp.sum(lp[0:1] * lp[1:2], axis=-1, keepdims=True))
           - jnp.exp(jnp.sum(lp[2:3] * lp[3:4], axis=-1, keepdims=True)) + lambda_init)
    g = g_ref[0]

    def q_rows(h, qi):
        q = q_ref[qi * t:(qi + 1) * t, h * LANES:(h + 1) * LANES]
        zero = jnp.zeros_like(q)
        return jnp.concatenate([jnp.where(lane < DIFF_HEAD_DIM, q, zero),
                                jnp.where(lane >= DIFF_HEAD_DIM, q, zero)], axis=0)

    def score(h, qi, qs, c):
        k = k_ref[c * t:(c + 1) * t, h * LANES:(h + 1) * LANES]
        s = lax.dot_general(k, qs, _NT, preferred_element_type=F32)
        return s + bias_ref[h, qi - c] if qi - c <= 1 else s

    def values(h, c):
        return vt_ref[0, h * LANES:(h + 1) * LANES, c * t:(c + 1) * t]

    def finish(h, qi, o):
        ot = o[:, :t] - lam * o[:, t:]
        y = _rms(ot.T, g) * (1.0 - lambda_init)
        o_ref[qi * t:(qi + 1) * t, h * LANES:(h + 1) * LANES] = y.astype(BF16)

    return _attend(heads, q_ref.shape[0] // t, q_rows, score, finish, values, DIFF_LOOKAHEAD)


def _mla_steps(heads, q_ref, k_ref, vt_ref, o_ref):
    t = ATT_TILE
    w = MLA_QK_PAD
    kc = lax.broadcasted_iota(jnp.int32, (t, t), 0) // CHUNK
    qc = lax.broadcasted_iota(jnp.int32, (t, t), 1) // CHUNK
    allowed = kc <= qc

    def q_rows(h, qi):
        return q_ref[qi * t:(qi + 1) * t, h * w:(h + 1) * w]

    def score(h, qi, q, c):
        k = k_ref[c * t:(c + 1) * t, h * w:(h + 1) * w]
        s = lax.dot_general(k, q, _NT, preferred_element_type=F32)
        return jnp.where(allowed, s, NEG_BIG) if c == qi else s

    def values(h, c):
        return vt_ref[0, h * LANES:(h + 1) * LANES, c * t:(c + 1) * t]

    def finish(h, qi, o):
        o_ref[qi * t:(qi + 1) * t, h * LANES:(h + 1) * LANES] = o.T.astype(BF16)

    return _attend(heads, q_ref.shape[0] // t, q_rows, score, finish, values, MLA_LOOKAHEAD)


def _attn_kernel(dq_ref, dk_ref, dvt_ref, bias_ref, lam_ref, g_ref, qm_ref, km_ref, vmt_ref,
                 oa_ref, ob_ref, *, lambda_init):
    heads = range(ATT_HEADS_PER_STEP)
    streams = [_diff_steps(heads, dq_ref, dk_ref, dvt_ref, bias_ref, lam_ref, g_ref, oa_ref,
                           lambda_init),
               _mla_steps(heads, qm_ref, km_ref, vmt_ref, ob_ref)]
    while streams:
        for s in list(streams):
            if next(s, "done") == "done":
                streams.remove(s)


def _attn_call(dq, dk, dvt, bias, wp, l, qm, km, vmt, seq, lambda_init):
    ntok = dq.shape[0]
    n = ATT_HEADS_PER_STEP

    def rows(width):
        return pl.BlockSpec((seq, n * width), lambda b, h: (b, h))

    vt_spec = pl.BlockSpec((1, n * LANES, seq), lambda b, h: (b, h, 0))
    return pl.pallas_call(
        functools.partial(_attn_kernel, lambda_init=lambda_init),
        grid=(ntok // seq, HEADS // n),
        in_specs=[rows(LANES), rows(LANES), vt_spec,
                  pl.BlockSpec((n,) + bias.shape[1:], lambda b, h: (h, 0, 0, 0)),
                  _layer_spec(wp["lam"], l), _layer_spec(wp["subln"], l),
                  rows(MLA_QK_PAD), rows(MLA_QK_PAD), vt_spec],
        out_specs=[rows(LANES), rows(LANES)],
        out_shape=[jax.ShapeDtypeStruct((ntok, DIFF_WIDTH), BF16),
                   jax.ShapeDtypeStruct((ntok, MLA_WIDTH), BF16)],
        compiler_params=pltpu.CompilerParams(dimension_semantics=("arbitrary", "arbitrary"),
                                             vmem_limit_bytes=VMEM_LIMIT),
        name="attn",
    )(dq, dk, dvt, bias, wp["lam"], wp["subln"], qm, km, vmt)


def _post_kernel(x_ref, ma_ref, mb_ref, wo_ref, g1_ref, b1_ref, wi_ref, wout_ref, g2_ref, b2_ref,
                 o_ref):
    n_blocks = x_ref.shape[0] // POST_ROW_BLOCK
    n_chunks = D_FF // FF_CHUNK

    def attn_out(r):
        rs = slice(r * POST_ROW_BLOCK, (r + 1) * POST_ROW_BLOCK)
        y = (jnp.dot(ma_ref[rs, :], wo_ref[0, :DIFF_WIDTH, :], preferred_element_type=F32)
             + jnp.dot(mb_ref[rs, :], wo_ref[0, DIFF_WIDTH:, :], preferred_element_type=F32))
        return _layer_norm(ALPHA * x_ref[rs, :] + y, g1_ref[0], b1_ref[0])

    def up(x1b, c):
        return jnp.dot(x1b, wi_ref[0, :, c * FF_CHUNK:(c + 1) * FF_CHUNK],
                       preferred_element_type=F32)

    x1 = attn_out(0)
    for r in range(n_blocks):
        x1_next = attn_out(r + 1) if r + 1 < n_blocks else None
        x1b = x1.astype(BF16)
        u = up(x1b, 0)
        y2 = None
        for c in range(n_chunks):
            u_next = up(x1b, c + 1) if c + 1 < n_chunks else None
            act = jnp.square(jnp.maximum(u, 0.0)).astype(BF16)
            d = jnp.dot(act, wout_ref[0, c * FF_CHUNK:(c + 1) * FF_CHUNK, :],
                        preferred_element_type=F32)
            y2 = d if y2 is None else y2 + d
            u = u_next
        o_ref[r * POST_ROW_BLOCK:(r + 1) * POST_ROW_BLOCK, :] = _layer_norm(
            ALPHA * x1 + y2, g2_ref[0], b2_ref[0])
        x1 = x1_next


def _post_call(x2, mix_a, mix_b, wp, l):
    ntok = x2.shape[0]
    rows = POST_ROWS

    def const(name):
        return _layer_spec(wp[name], l, pipeline_mode=pl.Buffered(1))

    def tok(width):
        return pl.BlockSpec((rows, width), lambda i: (i, 0))

    return pl.pallas_call(
        _post_kernel,
        grid=(ntok // rows,),
        in_specs=[tok(D_MODEL), tok(DIFF_WIDTH), tok(MLA_WIDTH), const("wo"), const("g1"),
                  const("b1"), const("wi"), const("wout"), const("g2"), const("b2")],
        out_specs=tok(D_MODEL),
        out_shape=jax.ShapeDtypeStruct((ntok, D_MODEL), F32),
        compiler_params=pltpu.CompilerParams(dimension_semantics=("arbitrary",),
                                             vmem_limit_bytes=VMEM_LIMIT),
        name="post",
    )(x2, mix_a, mix_b, wp["wo"], wp["g1"], wp["b1"], wp["wi"], wp["wout"], wp["g2"], wp["b2"])


def _t5_bucket(rel):
    nb = N_BUCKETS // 2
    ret = (rel > 0).astype(jnp.int32) * nb
    n = jnp.abs(rel)
    max_exact = nb // 2
    nf = jnp.maximum(n, 1).astype(F32)
    large = max_exact + (jnp.log(nf / max_exact) / math.log(MAX_DISTANCE / max_exact)
                         * (nb - max_exact)).astype(jnp.int32)
    large = jnp.minimum(large, nb - 1)
    return ret + jnp.where(n < max_exact, n, large)


def _bias_tiles(rel_bias):
    t = ATT_TILE
    p = 2 * t
    table = rel_bias.astype(F32).T
    far = table[:, N_BUCKETS // 2 - 1]
    w = jnp.arange(p, dtype=jnp.int32)[None, :]
    d = jnp.arange(2, dtype=jnp.int32)[:, None]
    rel = (t - 1) - w - d * t
    vec = (table[:, _t5_bucket(rel)] - far[:, None, None]) * LOG2E
    rows = jnp.tile(vec, (1, 1, t))[..., :t * (p - 1)].reshape(HEADS, 2, t, p - 1)
    bias = rows[..., t - 1:2 * t - 1]
    kk = jnp.arange(t, dtype=jnp.int32)[:, None]
    qq = jnp.arange(t, dtype=jnp.int32)[None, :]
    allowed = (kk // CHUNK <= qq // CHUNK)[None, None] | (d > 0)[None, :, :, None]
    bias = jnp.where(allowed, bias, NEG_BIG)
    return jnp.concatenate([bias, bias], axis=-1)


def _rope_tables(seq):
    pos = jnp.arange(seq, dtype=F32)
    inv = ROPE_THETA ** (-jnp.arange(0, MLA_ROPE_DIM, 2, dtype=F32) / MLA_ROPE_DIM)
    ang = pos[:, None] * inv[None, :]
    cos, sin = jnp.cos(ang), jnp.sin(ang)
    z = jnp.zeros((seq, LANES - MLA_ROPE_DIM), F32)
    return (jnp.concatenate([cos, cos, z], axis=-1), jnp.concatenate([-sin, sin, z], axis=-1))


def _swap_halves(w):
    half = w.shape[-1] // 2
    return jnp.concatenate([w[..., half:], w[..., :half]], axis=-1)


def _pack_params(w_in, lambda_q1, lambda_k1, lambda_q2, lambda_k2, subln_g, q_norm_g, w_uq,
                 kv_norm_g, w_ukv, w_o, ln1_g, ln1_b, w_mlp_in, w_mlp_out, ln2_g, ln2_b):
    depth = w_in.shape[0]
    o_k, o_v, o_cq = DIFF_WIDTH, 2 * DIFF_WIDTH, 3 * DIFF_WIDTH
    o_ckv = o_cq + MLA_Q_RANK
    o_kr = o_ckv + MLA_KV_RANK
    kr = w_in[..., o_kr:]
    w1 = jnp.concatenate([w_in[..., :o_k] * (DIFF_HEAD_DIM ** -0.5 * LOG2E), w_in[..., o_k:o_v],
                          w_in[..., o_cq:o_ckv], w_in[..., o_ckv:o_kr], kr, _swap_halves(kr)],
                         axis=-1)
    mla_scale = (MLA_NOPE_DIM + MLA_ROPE_DIM) ** -0.5 * LOG2E
    uq = (w_uq * mla_scale).reshape(depth, MLA_Q_RANK, HEADS, MLA_NOPE_DIM + MLA_ROPE_DIM)
    zq = jnp.zeros((depth, MLA_Q_RANK, HEADS, MLA_QK_PAD - MLA_NOPE_DIM - MLA_ROPE_DIM), F32)
    wuq = jnp.concatenate([uq, zq], axis=-1).reshape(depth, MLA_Q_RANK, HEADS * MLA_QK_PAD)
    ukv = w_ukv.reshape(depth, MLA_KV_RANK, HEADS, MLA_NOPE_DIM + MLA_V_DIM)
    wukvk = ukv[..., :MLA_NOPE_DIM].reshape(depth, MLA_KV_RANK, HEADS * MLA_NOPE_DIM)
    wukvv = ukv[..., MLA_NOPE_DIM:].reshape(depth, MLA_KV_RANK, HEADS * MLA_V_DIM)
    row = lambda v: v.reshape(depth, 1, -1).astype(F32)
    return dict(
        w1=w1.astype(BF16), wvt=jnp.swapaxes(w_in[..., o_v:o_cq], 1, 2).astype(BF16),
        wuq=wuq.astype(BF16), wukvk=wukvk.astype(BF16),
        wukvvt=jnp.swapaxes(wukvv, 1, 2).astype(BF16), gq=row(q_norm_g), gkv=row(kv_norm_g),
        subln=row(subln_g),
        lam=jnp.stack([lambda_q1, lambda_k1, lambda_q2, lambda_k2], axis=1).astype(F32),
        wo=w_o.astype(BF16), g1=row(ln1_g), b1=row(ln1_b), wi=w_mlp_in.astype(BF16),
        wout=w_mlp_out.astype(BF16), g2=row(ln2_g), b2=row(ln2_b))


def kernel(x, w_in, lambda_q1, lambda_k1, lambda_q2, lambda_k2, subln_g, q_norm_g, w_uq, kv_norm_g,
           w_ukv, rel_bias, w_o, ln1_g, ln1_b, w_mlp_in, w_mlp_out, ln2_g, ln2_b):
    batch, seq, _ = x.shape
    assert seq % PROJ_ROWS == 0 and seq % ATT_TILE == 0 and (batch * seq) % POST_ROWS == 0
    wp = _pack_params(w_in, lambda_q1, lambda_k1, lambda_q2, lambda_k2, subln_g, q_norm_g, w_uq,
                      kv_norm_g, w_ukv, w_o, ln1_g, ln1_b, w_mlp_in, w_mlp_out, ln2_g, ln2_b)
    wp["cos"], wp["sin"] = _rope_tables(seq)
    bias = _bias_tiles(rel_bias)
    x2 = x.reshape(batch * seq, D_MODEL)
    for l in range(DEPTH):
        lambda_init = 0.8 - 0.6 * math.exp(-0.3 * l)
        dq, dk, dvt, qm, km, vmt = _proj_call(x2, wp, l, seq)
        mix_a, mix_b = _attn_call(dq, dk, dvt, bias, wp, l, qm, km, vmt, seq, lambda_init)
        x2 = _post_call(x2, mix_a, mix_b, wp, l)
    return x2.reshape(batch, seq, D_MODEL)
```

```python
import functools
import math

import jax
import jax.numpy as jnp
from jax import lax
from jax.experimental import pallas as pl
from jax.experimental.pallas import tpu as pltpu

F32 = jnp.float32
BF16 = jnp.bfloat16

D_MODEL = 1024
DEPTH = 2
CHUNK = 64
HEADS = 4
DIFF_HEAD_DIM = 64
DIFF_V_DIM = 2 * DIFF_HEAD_DIM
DIFF_WIDTH = HEADS * DIFF_V_DIM
MLA_NOPE_DIM = 128
MLA_ROPE_DIM = 64
MLA_V_DIM = 128
MLA_Q_RANK = 256
MLA_KV_RANK = 128
MLA_WIDTH = HEADS * MLA_V_DIM
MLA_QK_PAD = 256
D_FF = 4 * D_MODEL
N_BUCKETS = 32
MAX_DISTANCE = 128
ROPE_THETA = 10000.0
ALPHA = (2 * DEPTH) ** 0.25
LN_EPS = 1e-5
RMS_EPS = 1e-6
NEG_BIG = -1e30
LOG2E = math.log2(math.e)

LANES = 128
BF16_SUBLANES = 16
ATT_TILE = 256
ATT_HEADS_PER_STEP = 2
PROJ_ROWS = 1024
POST_ROWS = 1024
POST_ROW_BLOCK = 256
FF_CHUNK = 1024
DIFF_LOOKAHEAD = 3
MLA_LOOKAHEAD = 6
VMEM_LIMIT = 56 * 1024 * 1024
assert ATT_TILE >= MAX_DISTANCE and ATT_TILE % CHUNK == 0

C_DQ = 0
C_DK = C_DQ + DIFF_WIDTH
C_CQ = C_DK + DIFF_WIDTH
C_CKV = C_CQ + MLA_Q_RANK
C_KR = C_CKV + MLA_KV_RANK
W1_COLS = C_KR + LANES

_NT = (((1,), (1,)), ((), ()))


def _rms(x, g):
    return x * lax.rsqrt(jnp.mean(x * x, axis=-1, keepdims=True) + RMS_EPS) * g


def _layer_norm(x, g, b):
    mu = jnp.mean(x, axis=-1, keepdims=True)
    xc = x - mu
    var = jnp.mean(xc * xc, axis=-1, keepdims=True)
    return xc * lax.rsqrt(var + LN_EPS) * g + b


def _layer_spec(arr, l, **kwargs):
    index = (l,) + (0,) * (arr.ndim - 1)
    return pl.BlockSpec((1,) + arr.shape[1:], lambda *_: index, **kwargs)


def _proj_kernel(x_ref, w1_ref, wvt_ref, wuq_ref, wuqs_ref, wukvk_ref, wukvvt_ref,
                 gq_ref, gkv_ref, cos_ref, sin_ref,
                 dq_ref, dk_ref, dvt_ref, qm_ref, km_ref, vmt_ref):
    xb = x_ref[...].astype(BF16)
    h = jnp.dot(xb, w1_ref[0], preferred_element_type=F32)
    dq_ref[...] = h[:, C_DQ:C_DQ + DIFF_WIDTH].astype(BF16)
    dk_ref[...] = h[:, C_DK:C_DK + DIFF_WIDTH].astype(BF16)
    dvt_ref[0] = lax.dot_general(wvt_ref[0], xb, _NT, preferred_element_type=F32).astype(BF16)

    cos = cos_ref[...]
    sin = sin_ref[...]
    c_q = _rms(h[:, C_CQ:C_CQ + MLA_Q_RANK], gq_ref[0]).astype(BF16)
    qf = jnp.dot(c_q, wuq_ref[0], preferred_element_type=F32)
    qsw = jnp.dot(c_q, wuqs_ref[0], preferred_element_type=F32)
    c_kv = _rms(h[:, C_CKV:C_CKV + MLA_KV_RANK], gkv_ref[0]).astype(BF16)
    kn = jnp.dot(c_kv, wukvk_ref[0], preferred_element_type=F32)
    vmt_ref[0] = lax.dot_general(wukvvt_ref[0], c_kv, _NT,
                                 preferred_element_type=F32).astype(BF16)
    kr = h[:, C_KR:C_KR + LANES]
    k_rope = (kr * cos + pltpu.roll(kr, LANES // 2, 1) * sin).astype(BF16)

    for hh in range(HEADS):
        b0 = hh * MLA_QK_PAD
        qm_ref[:, b0:b0 + LANES] = qf[:, b0:b0 + LANES].astype(BF16)
        qm_ref[:, b0 + LANES:b0 + 2 * LANES] = (
            qf[:, b0 + LANES:b0 + 2 * LANES] * cos + qsw[:, hh * LANES:(hh + 1) * LANES] * sin
        ).astype(BF16)
        km_ref[:, b0:b0 + LANES] = kn[:, hh * LANES:(hh + 1) * LANES].astype(BF16)
        km_ref[:, b0 + LANES:b0 + 2 * LANES] = k_rope


def _proj_call(x2, wp, l, seq):
    ntok = x2.shape[0]
    batch = ntok // seq
    rows = PROJ_ROWS
    steps_per_seq = seq // rows

    def const(name):
        return _layer_spec(wp[name], l)

    def tok(width):
        return pl.BlockSpec((rows, width), lambda i: (i, 0))

    vt_spec = pl.BlockSpec((1, HEADS * LANES, rows),
                           lambda i: (i // steps_per_seq, 0, i % steps_per_seq))
    pos_spec = pl.BlockSpec((rows, LANES), lambda i: (i % steps_per_seq, 0))
    vt_shape = jax.ShapeDtypeStruct((batch, HEADS * LANES, seq), BF16)
    return pl.pallas_call(
        _proj_kernel,
        grid=(ntok // rows,),
        in_specs=[tok(D_MODEL), const("w1"), const("wvt"), const("wuq"), const("wuqs"),
                  const("wukvk"), const("wukvvt"), const("gq"), const("gkv"), pos_spec, pos_spec],
        out_specs=[tok(DIFF_WIDTH), tok(DIFF_WIDTH), vt_spec,
                   tok(HEADS * MLA_QK_PAD), tok(HEADS * MLA_QK_PAD), vt_spec],
        out_shape=[jax.ShapeDtypeStruct((ntok, DIFF_WIDTH), BF16),
                   jax.ShapeDtypeStruct((ntok, DIFF_WIDTH), BF16),
                   vt_shape,
                   jax.ShapeDtypeStruct((ntok, HEADS * MLA_QK_PAD), BF16),
                   jax.ShapeDtypeStruct((ntok, HEADS * MLA_QK_PAD), BF16),
                   vt_shape],
        compiler_params=pltpu.CompilerParams(dimension_semantics=("arbitrary",),
                                             vmem_limit_bytes=VMEM_LIMIT),
        name="proj",
    )(x2, wp["w1"], wp["wvt"], wp["wuq"], wp["wuqs"], wp["wukvk"], wp["wukvvt"],
      wp["gq"], wp["gkv"], wp["cos"], wp["sin"])


def _col_max(a, b):
    m = jnp.max(b, axis=0, keepdims=True)
    return m if a is None else jnp.maximum(a, m)


def _attend(heads, n_tiles, q_fn, score_fn, finish_fn, vt_fn, lookahead):
    ones = jnp.ones((BF16_SUBLANES, ATT_TILE), BF16)
    items = [(h, qi, c) for h in heads for qi in range(n_tiles) for c in range(qi + 1)]
    qs, scores = {}, {}

    def issue(idx):
        h, qi, c = items[idx]
        if c == 0:
            qs[h, qi] = q_fn(h, qi)
        scores[idx] = score_fn(h, qi, qs[h, qi], c)

    for idx in range(min(lookahead, len(items))):
        issue(idx)
    m, acc = None, None
    for idx, (h, qi, c) in enumerate(items):
        if idx + lookahead < len(items):
            issue(idx + lookahead)
        s = scores.pop(idx)
        m_new = _col_max(m, s)
        e = jnp.exp2(s - m_new).astype(BF16)
        vt = jnp.concatenate([vt_fn(h, c), ones], axis=0)
        pv = jnp.dot(vt, e, preferred_element_type=F32)
        acc = pv if acc is None else acc * jnp.exp2(m - m_new) + pv
        m = m_new
        if c == qi:
            finish_fn(h, qi, acc[:LANES] / acc[LANES:LANES + 1])
            m, acc = None, None
        yield


def _diff_steps(heads, q_ref, k_ref, vt_ref, bias_ref, lam_ref, g_ref, o_ref, lambda_init):
    t = ATT_TILE
    lane = lax.broadcasted_iota(jnp.int32, (t, LANES), 1)
    lp = lam_ref[0]
    lam = (jnp.exp(jnp.sum(lp[0:1] * lp[1:2], axis=-1, keepdims=True))
           - jnp.exp(jnp.sum(lp[2:3] * lp[3:4], axis=-1, keepdims=True)) + lambda_init)
    g = g_ref[0]

    def q_rows(h, qi):
        q = q_ref[qi * t:(qi + 1) * t, h * LANES:(h + 1) * LANES]
        zero = jnp.zeros_like(q)
        return jnp.concatenate([jnp.where(lane < DIFF_HEAD_DIM, q, zero),
                                jnp.where(lane >= DIFF_HEAD_DIM, q, zero)], axis=0)

    def score(h, qi, qs, c):
        k = k_ref[c * t:(c + 1) * t, h * LANES:(h + 1) * LANES]
        s = lax.dot_general(k, qs, _NT, preferred_element_type=F32)
        return s + bias_ref[h, qi - c] if qi - c <= 1 else s

    def values(h, c):
        return vt_ref[0, h * LANES:(h + 1) * LANES, c * t:(c + 1) * t]

    def finish(h, qi, o):
        ot = o[:, :t] - lam * o[:, t:]
        y = _rms(ot.T, g) * (1.0 - lambda_init)
        o_ref[qi * t:(qi + 1) * t, h * LANES:(h + 1) * LANES] = y.astype(BF16)

    return _attend(heads, q_ref.shape[0] // t, q_rows, score, finish, values, DIFF_LOOKAHEAD)


def _mla_steps(heads, q_ref, k_ref, vt_ref, o_ref):
    t = ATT_TILE
    w = MLA_QK_PAD
    kc = lax.broadcasted_iota(jnp.int32, (t, t), 0) // CHUNK
    qc = lax.broadcasted_iota(jnp.int32, (t, t), 1) // CHUNK
    allowed = kc <= qc

    def q_rows(h, qi):
        return q_ref[qi * t:(qi + 1) * t, h * w:(h + 1) * w]

    def score(h, qi, q, c):
        k = k_ref[c * t:(c + 1) * t, h * w:(h + 1) * w]
        s = lax.dot_general(k, q, _NT, preferred_element_type=F32)
        return jnp.where(allowed, s, NEG_BIG) if c == qi else s

    def values(h, c):
        return vt_ref[0, h * LANES:(h + 1) * LANES, c * t:(c + 1) * t]

    def finish(h, qi, o):
        o_ref[qi * t:(qi + 1) * t, h * LANES:(h + 1) * LANES] = o.T.astype(BF16)

    return _attend(heads, q_ref.shape[0] // t, q_rows, score, finish, values, MLA_LOOKAHEAD)


def _attn_kernel(dq_ref, dk_ref, dvt_ref, bias_ref, lam_ref, g_ref, qm_ref, km_ref, vmt_ref,
                 oa_ref, ob_ref, *, lambda_init):
    heads = range(ATT_HEADS_PER_STEP)
    streams = [_diff_steps(heads, dq_ref, dk_ref, dvt_ref, bias_ref, lam_ref, g_ref, oa_ref,
                           lambda_init),
               _mla_steps(heads, qm_ref, km_ref, vmt_ref, ob_ref)]
    while streams:
        for s in list(streams):
            if next(s, "done") == "done":
                streams.remove(s)


def _attn_call(dq, dk, dvt, bias, wp, l, qm, km, vmt, seq, lambda_init):
    ntok = dq.shape[0]
    n = ATT_HEADS_PER_STEP

    def rows(width):
        return pl.BlockSpec((seq, n * width), lambda b, h: (b, h))

    vt_spec = pl.BlockSpec((1, n * LANES, seq), lambda b, h: (b, h, 0))
    return pl.pallas_call(
        functools.partial(_attn_kernel, lambda_init=lambda_init),
        grid=(ntok // seq, HEADS // n),
        in_specs=[rows(LANES), rows(LANES), vt_spec,
                  pl.BlockSpec((n,) + bias.shape[1:], lambda b, h: (h, 0, 0, 0)),
                  _layer_spec(wp["lam"], l), _layer_spec(wp["subln"], l),
                  rows(MLA_QK_PAD), rows(MLA_QK_PAD), vt_spec],
        out_specs=[rows(LANES), rows(LANES)],
        out_shape=[jax.ShapeDtypeStruct((ntok, DIFF_WIDTH), BF16),
                   jax.ShapeDtypeStruct((ntok, MLA_WIDTH), BF16)],
        compiler_params=pltpu.CompilerParams(dimension_semantics=("arbitrary", "arbitrary"),
                                             vmem_limit_bytes=VMEM_LIMIT),
        name="attn",
    )(dq, dk, dvt, bias, wp["lam"], wp["subln"], qm, km, vmt)


def _post_kernel(x_ref, ma_ref, mb_ref, wo_ref, g1_ref, b1_ref, wi_ref, wout_ref, g2_ref, b2_ref,
                 o_ref):
    n_blocks = x_ref.shape[0] // POST_ROW_BLOCK
    n_chunks = D_FF // FF_CHUNK

    def attn_out(r):
        rs = slice(r * POST_ROW_BLOCK, (r + 1) * POST_ROW_BLOCK)
        y = (jnp.dot(ma_ref[rs, :], wo_ref[0, :DIFF_WIDTH, :], preferred_element_type=F32)
             + jnp.dot(mb_ref[rs, :], wo_ref[0, DIFF_WIDTH:, :], preferred_element_type=F32))
        return _layer_norm(ALPHA * x_ref[rs, :] + y, g1_ref[0], b1_ref[0])

    def up(x1b, c):
        return jnp.dot(x1b, wi_ref[0, :, c * FF_CHUNK:(c + 1) * FF_CHUNK],
                       preferred_element_type=F32)

    x1 = attn_out(0)
    for r in range(n_blocks):
        x1_next = attn_out(r + 1) if r + 1 < n_blocks else None
        x1b = x1.astype(BF16)
        u = up(x1b, 0)
        y2 = None
        for c in range(n_chunks):
            u_next = up(x1b, c + 1) if c + 1 < n_chunks else None
            act = jnp.square(jnp.maximum(u, 0.0)).astype(BF16)
            d = jnp.dot(act, wout_ref[0, c * FF_CHUNK:(c + 1) * FF_CHUNK, :],
                        preferred_element_type=F32)
            y2 = d if y2 is None else y2 + d
            u = u_next
        o_ref[r * POST_ROW_BLOCK:(r + 1) * POST_ROW_BLOCK, :] = _layer_norm(
            ALPHA * x1 + y2, g2_ref[0], b2_ref[0])
        x1 = x1_next


def _post_call(x2, mix_a, mix_b, wp, l):
    ntok = x2.shape[0]
    rows = POST_ROWS

    def const(name):
        return _layer_spec(wp[name], l, pipeline_mode=pl.Buffered(1))

    def tok(width):
        return pl.BlockSpec((rows, width), lambda i: (i, 0))

    return pl.pallas_call(
        _post_kernel,
        grid=(ntok // rows,),
        in_specs=[tok(D_MODEL), tok(DIFF_WIDTH), tok(MLA_WIDTH), const("wo"), const("g1"),
                  const("b1"), const("wi"), const("wout"), const("g2"), const("b2")],
        out_specs=tok(D_MODEL),
        out_shape=jax.ShapeDtypeStruct((ntok, D_MODEL), F32),
        compiler_params=pltpu.CompilerParams(dimension_semantics=("arbitrary",),
                                             vmem_limit_bytes=VMEM_LIMIT),
        name="post",
    )(x2, mix_a, mix_b, wp["wo"], wp["g1"], wp["b1"], wp["wi"], wp["wout"], wp["g2"], wp["b2"])


def _t5_bucket(rel):
    nb = N_BUCKETS // 2
    ret = (rel > 0).astype(jnp.int32) * nb
    n = jnp.abs(rel)
    max_exact = nb // 2
    nf = jnp.maximum(n, 1).astype(F32)
    large = max_exact + (jnp.log(nf / max_exact) / math.log(MAX_DISTANCE / max_exact)
                         * (nb - max_exact)).astype(jnp.int32)
    large = jnp.minimum(large, nb - 1)
    return ret + jnp.where(n < max_exact, n, large)


def _bias_tiles(rel_bias):
    t = ATT_TILE
    p = 2 * t
    table = rel_bias.astype(F32).T
    far = table[:, N_BUCKETS // 2 - 1]
    w = jnp.arange(p, dtype=jnp.int32)[None, :]
    d = jnp.arange(2, dtype=jnp.int32)[:, None]
    rel = (t - 1) - w - d * t
    vec = (table[:, _t5_bucket(rel)] - far[:, None, None]) * LOG2E
    rows = jnp.tile(vec, (1, 1, t))[..., :t * (p - 1)].reshape(HEADS, 2, t, p - 1)
    bias = rows[..., t - 1:2 * t - 1]
    kk = jnp.arange(t, dtype=jnp.int32)[:, None]
    qq = jnp.arange(t, dtype=jnp.int32)[None, :]
    allowed = (kk // CHUNK <= qq // CHUNK)[None, None] | (d > 0)[None, :, :, None]
    bias = jnp.where(allowed, bias, NEG_BIG)
    return jnp.concatenate([bias, bias], axis=-1)


def _rope_tables(seq):
    pos = jnp.arange(seq, dtype=F32)
    inv = ROPE_THETA ** (-jnp.arange(0, MLA_ROPE_DIM, 2, dtype=F32) / MLA_ROPE_DIM)
    ang = pos[:, None] * inv[None, :]
    cos, sin = jnp.cos(ang), jnp.sin(ang)
    z = jnp.zeros((seq, LANES - MLA_ROPE_DIM), F32)
    return (jnp.concatenate([cos, cos, z], axis=-1), jnp.concatenate([-sin, sin, z], axis=-1))


def _swap_halves(w):
    half = w.shape[-1] // 2
    return jnp.concatenate([w[..., half:], w[..., :half]], axis=-1)


def _pack_params(w_in, lambda_q1, lambda_k1, lambda_q2, lambda_k2, subln_g, q_norm_g, w_uq,
                 kv_norm_g, w_ukv, w_o, ln1_g, ln1_b, w_mlp_in, w_mlp_out, ln2_g, ln2_b):
    depth = w_in.shape[0]
    o_k, o_v, o_cq = DIFF_WIDTH, 2 * DIFF_WIDTH, 3 * DIFF_WIDTH
    o_ckv = o_cq + MLA_Q_RANK
    o_kr = o_ckv + MLA_KV_RANK
    kr = w_in[..., o_kr:]
    w1 = jnp.concatenate([w_in[..., :o_k] * (DIFF_HEAD_DIM ** -0.5 * LOG2E), w_in[..., o_k:o_v],
                          w_in[..., o_cq:o_ckv], w_in[..., o_ckv:o_kr], kr, _swap_halves(kr)],
                         axis=-1)
    mla_scale = (MLA_NOPE_DIM + MLA_ROPE_DIM) ** -0.5 * LOG2E
    uq = (w_uq * mla_scale).reshape(depth, MLA_Q_RANK, HEADS, MLA_NOPE_DIM + MLA_ROPE_DIM)
    zq = jnp.zeros((depth, MLA_Q_RANK, HEADS, MLA_QK_PAD - MLA_NOPE_DIM - MLA_ROPE_DIM), F32)
    wuq = jnp.concatenate([uq, zq], axis=-1).reshape(depth, MLA_Q_RANK, HEADS * MLA_QK_PAD)
    wuqs = jnp.concatenate([_swap_halves(uq[..., MLA_NOPE_DIM:]), zq], axis=-1)
    wuqs = wuqs.reshape(depth, MLA_Q_RANK, HEADS * LANES)
    ukv = w_ukv.reshape(depth, MLA_KV_RANK, HEADS, MLA_NOPE_DIM + MLA_V_DIM)
    wukvk = ukv[..., :MLA_NOPE_DIM].reshape(depth, MLA_KV_RANK, HEADS * MLA_NOPE_DIM)
    wukvv = ukv[..., MLA_NOPE_DIM:].reshape(depth, MLA_KV_RANK, HEADS * MLA_V_DIM)
    row = lambda v: v.reshape(depth, 1, -1).astype(F32)
    return dict(
        w1=w1.astype(BF16), wvt=jnp.swapaxes(w_in[..., o_v:o_cq], 1, 2).astype(BF16),
        wuq=wuq.astype(BF16), wuqs=wuqs.astype(BF16), wukvk=wukvk.astype(BF16),
        wukvvt=jnp.swapaxes(wukvv, 1, 2).astype(BF16), gq=row(q_norm_g), gkv=row(kv_norm_g),
        subln=row(subln_g),
        lam=jnp.stack([lambda_q1, lambda_k1, lambda_q2, lambda_k2], axis=1).astype(F32),
        wo=w_o.astype(BF16), g1=row(ln1_g), b1=row(ln1_b), wi=w_mlp_in.astype(BF16),
        wout=w_mlp_out.astype(BF16), g2=row(ln2_g), b2=row(ln2_b))


def kernel(x, w_in, lambda_q1, lambda_k1, lambda_q2, lambda_k2, subln_g, q_norm_g, w_uq, kv_norm_g,
           w_ukv, rel_bias, w_o, ln1_g, ln1_b, w_mlp_in, w_mlp_out, ln2_g, ln2_b):
    batch, seq, _ = x.shape
    assert seq % PROJ_ROWS == 0 and seq % ATT_TILE == 0 and (batch * seq) % POST_ROWS == 0
    wp = _pack_params(w_in, lambda_q1, lambda_k1, lambda_q2, lambda_k2, subln_g, q_norm_g, w_uq,
                      kv_norm_g, w_ukv, w_o, ln1_g, ln1_b, w_mlp_in, w_mlp_out, ln2_g, ln2_b)
    wp["cos"], wp["sin"] = _rope_tables(seq)
    bias = _bias_tiles(rel_bias)
    x2 = x.reshape(batch * seq, D_MODEL)
    for l in range(DEPTH):
        lambda_init = 0.8 - 0.6 * math.exp(-0.3 * l)
        dq, dk, dvt, qm, km, vmt = _proj_call(x2, wp, l, seq)
        mix_a, mix_b = _attn_call(dq, dk, dvt, bias, wp, l, qm, km, vmt, seq, lambda_init)
        x2 = _post_call(x2, mix_a, mix_b, wp, l)
    return x2.reshape(batch, seq, D_MODEL)
```

```python
import functools
import math

import jax
import jax.numpy as jnp
from jax import lax
from jax.experimental import pallas as pl
from jax.experimental.pallas import tpu as pltpu

F32 = jnp.float32
BF16 = jnp.bfloat16

D_MODEL = 1024
DEPTH = 2
CHUNK = 64
HEADS = 4
DIFF_HEAD_DIM = 64
DIFF_V_DIM = 2 * DIFF_HEAD_DIM
DIFF_WIDTH = HEADS * DIFF_V_DIM
MLA_NOPE_DIM = 128
MLA_ROPE_DIM = 64
MLA_V_DIM = 128
MLA_Q_RANK = 256
MLA_KV_RANK = 128
MLA_WIDTH = HEADS * MLA_V_DIM
MLA_QK_PAD = 256
D_FF = 4 * D_MODEL
N_BUCKETS = 32
MAX_DISTANCE = 128
ROPE_THETA = 10000.0
ALPHA = (2 * DEPTH) ** 0.25
LN_EPS = 1e-5
RMS_EPS = 1e-6
NEG_BIG = -1e30
LOG2E = math.log2(math.e)

LANES = 128
BF16_SUBLANES = 16
ATT_TILE = 256
ATT_HEADS_PER_STEP = 2
PROJ_ROWS = 1024
POST_ROWS = 1024
POST_ROW_BLOCK = 256
FF_CHUNK = 1024
DIFF_LOOKAHEAD = 1
MLA_LOOKAHEAD = 2
VMEM_LIMIT = 56 * 1024 * 1024
assert ATT_TILE >= MAX_DISTANCE and ATT_TILE % CHUNK == 0

C_DQ = 0
C_DK = C_DQ + DIFF_WIDTH
C_CQ = C_DK + DIFF_WIDTH
C_CKV = C_CQ + MLA_Q_RANK
C_KR = C_CKV + MLA_KV_RANK
W1_COLS = C_KR + LANES

_NT = (((1,), (1,)), ((), ()))


def _rms(x, g):
    return x * lax.rsqrt(jnp.mean(x * x, axis=-1, keepdims=True) + RMS_EPS) * g


def _layer_norm(x, g, b):
    mu = jnp.mean(x, axis=-1, keepdims=True)
    xc = x - mu
    var = jnp.mean(xc * xc, axis=-1, keepdims=True)
    return xc * lax.rsqrt(var + LN_EPS) * g + b


def _layer_spec(arr, l, **kwargs):
    index = (l,) + (0,) * (arr.ndim - 1)
    return pl.BlockSpec((1,) + arr.shape[1:], lambda *_: index, **kwargs)


def _proj_kernel(x_ref, w1_ref, wvt_ref, wuq_ref, wuqs_ref, wukvk_ref, wukvvt_ref,
                 gq_ref, gkv_ref, cos_ref, sin_ref,
                 dq_ref, dk_ref, dvt_ref, qm_ref, km_ref, vmt_ref):
    xb = x_ref[...].astype(BF16)
    h = jnp.dot(xb, w1_ref[0], preferred_element_type=F32)
    dq_ref[...] = h[:, C_DQ:C_DQ + DIFF_WIDTH].astype(BF16)
    dk_ref[...] = h[:, C_DK:C_DK + DIFF_WIDTH].astype(BF16)
    dvt_ref[0] = lax.dot_general(wvt_ref[0], xb, _NT, preferred_element_type=F32).astype(BF16)

    cos = cos_ref[...]
    sin = sin_ref[...]
    c_q = _rms(h[:, C_CQ:C_CQ + MLA_Q_RANK], gq_ref[0]).astype(BF16)
    qf = jnp.dot(c_q, wuq_ref[0], preferred_element_type=F32)
    qsw = jnp.dot(c_q, wuqs_ref[0], preferred_element_type=F32)
    c_kv = _rms(h[:, C_CKV:C_CKV + MLA_KV_RANK], gkv_ref[0]).astype(BF16)
    kn = jnp.dot(c_kv, wukvk_ref[0], preferred_element_type=F32)
    vmt_ref[0] = lax.dot_general(wukvvt_ref[0], c_kv, _NT,
                                 preferred_element_type=F32).astype(BF16)
    kr = h[:, C_KR:C_KR + LANES]
    k_rope = (kr * cos + pltpu.roll(kr, LANES // 2, 1) * sin).astype(BF16)

    for hh in range(HEADS):
        b0 = hh * MLA_QK_PAD
        qm_ref[:, b0:b0 + LANES] = qf[:, b0:b0 + LANES].astype(BF16)
        qm_ref[:, b0 + LANES:b0 + 2 * LANES] = (
            qf[:, b0 + LANES:b0 + 2 * LANES] * cos + qsw[:, hh * LANES:(hh + 1) * LANES] * sin
        ).astype(BF16)
        km_ref[:, b0:b0 + LANES] = kn[:, hh * LANES:(hh + 1) * LANES].astype(BF16)
        km_ref[:, b0 + LANES:b0 + 2 * LANES] = k_rope


def _proj_call(x2, wp, l, seq):
    ntok = x2.shape[0]
    batch = ntok // seq
    rows = PROJ_ROWS
    steps_per_seq = seq // rows

    def const(name):
        return _layer_spec(wp[name], l)

    def tok(width):
        return pl.BlockSpec((rows, width), lambda i: (i, 0))

    vt_spec = pl.BlockSpec((1, HEADS * LANES, rows),
                           lambda i: (i // steps_per_seq, 0, i % steps_per_seq))
    pos_spec = pl.BlockSpec((rows, LANES), lambda i: (i % steps_per_seq, 0))
    vt_shape = jax.ShapeDtypeStruct((batch, HEADS * LANES, seq), BF16)
    return pl.pallas_call(
        _proj_kernel,
        grid=(ntok // rows,),
        in_specs=[tok(D_MODEL), const("w1"), const("wvt"), const("wuq"), const("wuqs"),
                  const("wukvk"), const("wukvvt"), const("gq"), const("gkv"), pos_spec, pos_spec],
        out_specs=[tok(DIFF_WIDTH), tok(DIFF_WIDTH), vt_spec,
                   tok(HEADS * MLA_QK_PAD), tok(HEADS * MLA_QK_PAD), vt_spec],
        out_shape=[jax.ShapeDtypeStruct((ntok, DIFF_WIDTH), BF16),
                   jax.ShapeDtypeStruct((ntok, DIFF_WIDTH), BF16),
                   vt_shape,
                   jax.ShapeDtypeStruct((ntok, HEADS * MLA_QK_PAD), BF16),
                   jax.ShapeDtypeStruct((ntok, HEADS * MLA_QK_PAD), BF16),
                   vt_shape],
        compiler_params=pltpu.CompilerParams(dimension_semantics=("arbitrary",),
                                             vmem_limit_bytes=VMEM_LIMIT),
        name="proj",
    )(x2, wp["w1"], wp["wvt"], wp["wuq"], wp["wuqs"], wp["wukvk"], wp["wukvvt"],
      wp["gq"], wp["gkv"], wp["cos"], wp["sin"])


def _col_max(a, b):
    m = jnp.max(b, axis=0, keepdims=True)
    return m if a is None else jnp.maximum(a, m)


def _attend(heads, n_tiles, q_fn, score_fn, finish_fn, vt_fn, lookahead):
    ones = jnp.ones((BF16_SUBLANES, ATT_TILE), BF16)
    items = [(h, qi, c) for h in heads for qi in range(n_tiles) for c in range(qi + 1)]
    qs, scores = {}, {}

    def issue(idx):
        h, qi, c = items[idx]
        if c == 0:
            qs[h, qi] = q_fn(h, qi)
        scores[idx] = score_fn(h, qi, qs[h, qi], c)

    for idx in range(min(lookahead, len(items))):
        issue(idx)
    m, acc = None, None
    for idx, (h, qi, c) in enumerate(items):
        if idx + lookahead < len(items):
            issue(idx + lookahead)
        s = scores.pop(idx)
        m_new = _col_max(m, s)
        e = jnp.exp2(s - m_new).astype(BF16)
        vt = jnp.concatenate([vt_fn(h, c), ones], axis=0)
        pv = jnp.dot(vt, e, preferred_element_type=F32)
        acc = pv if acc is None else acc * jnp.exp2(m - m_new) + pv
        m = m_new
        if c == qi:
            finish_fn(h, qi, acc[:LANES] / acc[LANES:LANES + 1])
            m, acc = None, None
        yield


def _diff_steps(heads, q_ref, k_ref, vt_ref, bias_ref, lam_ref, g_ref, o_ref, lambda_init):
    t = ATT_TILE
    lane = lax.broadcasted_iota(jnp.int32, (t, LANES), 1)
    lp = lam_ref[0]
    lam = (jnp.exp(jnp.sum(lp[0:1] * lp[1:2], axis=-1, keepdims=True))
           - jnp.exp(jnp.sum(lp[2:3] * lp[3:4], axis=-1, keepdims=True)) + lambda_init)
    g = g_ref[0]

    def q_rows(h, qi):
        q = q_ref[qi * t:(qi + 1) * t, h * LANES:(h + 1) * LANES]
        zero = jnp.zeros_like(q)
        return jnp.concatenate([jnp.where(lane < DIFF_HEAD_DIM, q, zero),
                                jnp.where(lane >= DIFF_HEAD_DIM, q, zero)], axis=0)

    def score(h, qi, qs, c):
        k = k_ref[c * t:(c + 1) * t, h * LANES:(h + 1) * LANES]
        s = lax.dot_general(k, qs, _NT, preferred_element_type=F32)
        return s + bias_ref[h, qi - c] if qi - c <= 1 else s

    def values(h, c):
        return vt_ref[0, h * LANES:(h + 1) * LANES, c * t:(c + 1) * t]

    def finish(h, qi, o):
        ot = o[:, :t] - lam * o[:, t:]
        y = _rms(ot.T, g) * (1.0 - lambda_init)
        o_ref[qi * t:(qi + 1) * t, h * LANES:(h + 1) * LANES] = y.astype(BF16)

    return _attend(heads, q_ref.shape[0] // t, q_rows, score, finish, values, DIFF_LOOKAHEAD)


def _mla_steps(heads, q_ref, k_ref, vt_ref, o_ref):
    t = ATT_TILE
    w = MLA_QK_PAD
    kc = lax.broadcasted_iota(jnp.int32, (t, t), 0) // CHUNK
    qc = lax.broadcasted_iota(jnp.int32, (t, t), 1) // CHUNK
    allowed = kc <= qc

    def q_rows(h, qi):
        return q_ref[qi * t:(qi + 1) * t, h * w:(h + 1) * w]

    def score(h, qi, q, c):
        k = k_ref[c * t:(c + 1) * t, h * w:(h + 1) * w]
        s = lax.dot_general(k, q, _NT, preferred_element_type=F32)
        return jnp.where(allowed, s, NEG_BIG) if c == qi else s

    def values(h, c):
        return vt_ref[0, h * LANES:(h + 1) * LANES, c * t:(c + 1) * t]

    def finish(h, qi, o):
        o_ref[qi * t:(qi + 1) * t, h * LANES:(h + 1) * LANES] = o.T.astype(BF16)

    return _attend(heads, q_ref.shape[0] // t, q_rows, score, finish, values, MLA_LOOKAHEAD)


def _attn_kernel(dq_ref, dk_ref, dvt_ref, bias_ref, lam_ref, g_ref, qm_ref, km_ref, vmt_ref,
                 oa_ref, ob_ref, *, lambda_init):
    heads = range(ATT_HEADS_PER_STEP)
    streams = [_diff_steps(heads, dq_ref, dk_ref, dvt_ref, bias_ref, lam_ref, g_ref, oa_ref,
                           lambda_init),
               _mla_steps(heads, qm_ref, km_ref, vmt_ref, ob_ref)]
    while streams:
        for s in list(streams):
            if next(s, "done") == "done":
                streams.remove(s)


def _attn_call(dq, dk, dvt, bias, wp, l, qm, km, vmt, seq, lambda_init):
    ntok = dq.shape[0]
    n = ATT_HEADS_PER_STEP

    def rows(width):
        return pl.BlockSpec((seq, n * width), lambda b, h: (b, h))

    vt_spec = pl.BlockSpec((1, n * LANES, seq), lambda b, h: (b, h, 0))
    return pl.pallas_call(
        functools.partial(_attn_kernel, lambda_init=lambda_init),
        grid=(ntok // seq, HEADS // n),
        in_specs=[rows(LANES), rows(LANES), vt_spec,
                  pl.BlockSpec((n,) + bias.shape[1:], lambda b, h: (h, 0, 0, 0)),
                  _layer_spec(wp["lam"], l), _layer_spec(wp["subln"], l),
                  rows(MLA_QK_PAD), rows(MLA_QK_PAD), vt_spec],
        out_specs=[rows(LANES), rows(LANES)],
        out_shape=[jax.ShapeDtypeStruct((ntok, DIFF_WIDTH), BF16),
                   jax.ShapeDtypeStruct((ntok, MLA_WIDTH), BF16)],
        compiler_params=pltpu.CompilerParams(dimension_semantics=("arbitrary", "arbitrary"),
                                             vmem_limit_bytes=VMEM_LIMIT),
        name="attn",
    )(dq, dk, dvt, bias, wp["lam"], wp["subln"], qm, km, vmt)


def _post_kernel(x_ref, ma_ref, mb_ref, wo_ref, g1_ref, b1_ref, wi_ref, wout_ref, g2_ref, b2_ref,
                 o_ref):
    n_blocks = x_ref.shape[0] // POST_ROW_BLOCK
    n_chunks = D_FF // FF_CHUNK

    def attn_out(r):
        rs = slice(r * POST_ROW_BLOCK, (r + 1) * POST_ROW_BLOCK)
        y = (jnp.dot(ma_ref[rs, :], wo_ref[0, :DIFF_WIDTH, :], preferred_element_type=F32)
             + jnp.dot(mb_ref[rs, :], wo_ref[0, DIFF_WIDTH:, :], preferred_element_type=F32))
        return _layer_norm(ALPHA * x_ref[rs, :] + y, g1_ref[0], b1_ref[0])

    def up(x1b, c):
        return jnp.dot(x1b, wi_ref[0, :, c * FF_CHUNK:(c + 1) * FF_CHUNK],
                       preferred_element_type=F32)

    x1 = attn_out(0)
    for r in range(n_blocks):
        x1_next = attn_out(r + 1) if r + 1 < n_blocks else None
        x1b = x1.astype(BF16)
        u = up(x1b, 0)
        y2 = None
        for c in range(n_chunks):
            u_next = up(x1b, c + 1) if c + 1 < n_chunks else None
            act = jnp.square(jnp.maximum(u, 0.0)).astype(BF16)
            d = jnp.dot(act, wout_ref[0, c * FF_CHUNK:(c + 1) * FF_CHUNK, :],
                        preferred_element_type=F32)
            y2 = d if y2 is None else y2 + d
            u = u_next
        o_ref[r * POST_ROW_BLOCK:(r + 1) * POST_ROW_BLOCK, :] = _layer_norm(
            ALPHA * x1 + y2, g2_ref[0], b2_ref[0])
        x1 = x1_next


def _post_call(x2, mix_a, mix_b, wp, l):
    ntok = x2.shape[0]
    rows = POST_ROWS

    def const(name):
        return _layer_spec(wp[name], l, pipeline_mode=pl.Buffered(1))

    def tok(width):
        return pl.BlockSpec((rows, width), lambda i: (i, 0))

    return pl.pallas_call(
        _post_kernel,
        grid=(ntok // rows,),
        in_specs=[tok(D_MODEL), tok(DIFF_WIDTH), tok(MLA_WIDTH), const("wo"), const("g1"),
                  const("b1"), const("wi"), const("wout"), const("g2"), const("b2")],
        out_specs=tok(D_MODEL),
        out_shape=jax.ShapeDtypeStruct((ntok, D_MODEL), F32),
        compiler_params=pltpu.CompilerParams(dimension_semantics=("arbitrary",),
                                             vmem_limit_bytes=VMEM_LIMIT),
        name="post",
    )(x2, mix_a, mix_b, wp["wo"], wp["g1"], wp["b1"], wp["wi"], wp["wout"], wp["g2"], wp["b2"])


def _t5_bucket(rel):
    nb = N_BUCKETS // 2
    ret = (rel > 0).astype(jnp.int32) * nb
    n = jnp.abs(rel)
    max_exact = nb // 2
    nf = jnp.maximum(n, 1).astype(F32)
    large = max_exact + (jnp.log(nf / max_exact) / math.log(MAX_DISTANCE / max_exact)
                         * (nb - max_exact)).astype(jnp.int32)
    large = jnp.minimum(large, nb - 1)
    return ret + jnp.where(n < max_exact, n, large)


def _bias_tiles(rel_bias):
    t = ATT_TILE
    p = 2 * t
    table = rel_bias.astype(F32).T
    far = table[:, N_BUCKETS // 2 - 1]
    w = jnp.arange(p, dtype=jnp.int32)[None, :]
    d = jnp.arange(2, dtype=jnp.int32)[:, None]
    rel = (t - 1) - w - d * t
    vec = (table[:, _t5_bucket(rel)] - far[:, None, None]) * LOG2E
    rows = jnp.tile(vec, (1, 1, t))[..., :t * (p - 1)].reshape(HEADS, 2, t, p - 1)
    bias = rows[..., t - 1:2 * t - 1]
    kk = jnp.arange(t, dtype=jnp.int32)[:, None]
    qq = jnp.arange(t, dtype=jnp.int32)[None, :]
    allowed = (kk // CHUNK <= qq // CHUNK)[None, None] | (d > 0)[None, :, :, None]
    bias = jnp.where(allowed, bias, NEG_BIG)
    return jnp.concatenate([bias, bias], axis=-1)


def _rope_tables(seq):
    pos = jnp.arange(seq, dtype=F32)
    inv = ROPE_THETA ** (-jnp.arange(0, MLA_ROPE_DIM, 2, dtype=F32) / MLA_ROPE_DIM)
    ang = pos[:, None] * inv[None, :]
    cos, sin = jnp.cos(ang), jnp.sin(ang)
    z = jnp.zeros((seq, LANES - MLA_ROPE_DIM), F32)
    return (jnp.concatenate([cos, cos, z], axis=-1), jnp.concatenate([-sin, sin, z], axis=-1))


def _swap_halves(w):
    half = w.shape[-1] // 2
    return jnp.concatenate([w[..., half:], w[..., :half]], axis=-1)


def _pack_params(w_in, lambda_q1, lambda_k1, lambda_q2, lambda_k2, subln_g, q_norm_g, w_uq,
                 kv_norm_g, w_ukv, w_o, ln1_g, ln1_b, w_mlp_in, w_mlp_out, ln2_g, ln2_b):
    depth = w_in.shape[0]
    o_k, o_v, o_cq = DIFF_WIDTH, 2 * DIFF_WIDTH, 3 * DIFF_WIDTH
    o_ckv = o_cq + MLA_Q_RANK
    o_kr = o_ckv + MLA_KV_RANK
    kr = w_in[..., o_kr:]
    w1 = jnp.concatenate([w_in[..., :o_k] * (DIFF_HEAD_DIM ** -0.5 * LOG2E), w_in[..., o_k:o_v],
                          w_in[..., o_cq:o_ckv], w_in[..., o_ckv:o_kr], kr, _swap_halves(kr)],
                         axis=-1)
    mla_scale = (MLA_NOPE_DIM + MLA_ROPE_DIM) ** -0.5 * LOG2E
    uq = (w_uq * mla_scale).reshape(depth, MLA_Q_RANK, HEADS, MLA_NOPE_DIM + MLA_ROPE_DIM)
    zq = jnp.zeros((depth, MLA_Q_RANK, HEADS, MLA_QK_PAD - MLA_NOPE_DIM - MLA_ROPE_DIM), F32)
    wuq = jnp.concatenate([uq, zq], axis=-1).reshape(depth, MLA_Q_RANK, HEADS * MLA_QK_PAD)
    wuqs = jnp.concatenate([_swap_halves(uq[..., MLA_NOPE_DIM:]), zq], axis=-1)
    wuqs = wuqs.reshape(depth, MLA_Q_RANK, HEADS * LANES)
    ukv = w_ukv.reshape(depth, MLA_KV_RANK, HEADS, MLA_NOPE_DIM + MLA_V_DIM)
    wukvk = ukv[..., :MLA_NOPE_DIM].reshape(depth, MLA_KV_RANK, HEADS * MLA_NOPE_DIM)
    wukvv = ukv[..., MLA_NOPE_DIM:].reshape(depth, MLA_KV_RANK, HEADS * MLA_V_DIM)
    row = lambda v: v.reshape(depth, 1, -1).astype(F32)
    return dict(
        w1=w1.astype(BF16), wvt=jnp.swapaxes(w_in[..., o_v:o_cq], 1, 2).astype(BF16),
        wuq=wuq.astype(BF16), wuqs=wuqs.astype(BF16), wukvk=wukvk.astype(BF16),
        wukvvt=jnp.swapaxes(wukvv, 1, 2).astype(BF16), gq=row(q_norm_g), gkv=row(kv_norm_g),
        subln=row(subln_g),
        lam=jnp.stack([lambda_q1, lambda_k1, lambda_q2, lambda_k2], axis=1).astype(F32),
        wo=w_o.astype(BF16), g1=row(ln1_g), b1=row(ln1_b), wi=w_mlp_in.astype(BF16),
        wout=w_mlp_out.astype(BF16), g2=row(ln2_g), b2=row(ln2_b))


def kernel(x, w_in, lambda_q1, lambda_k1, lambda_q2, lambda_k2, subln_g, q_norm_g, w_uq, kv_norm_g,
           w_ukv, rel_bias, w_o, ln1_g, ln1_b, w_mlp_in, w_mlp_out, ln2_g, ln2_b):
    batch, seq, _ = x.shape
    assert seq % PROJ_ROWS == 0 and seq % ATT_TILE == 0 and (batch * seq) % POST_ROWS == 0
    wp = _pack_params(w_in, lambda_q1, lambda_k1, lambda_q2, lambda_k2, subln_g, q_norm_g, w_uq,
                      kv_norm_g, w_ukv, w_o, ln1_g, ln1_b, w_mlp_in, w_mlp_out, ln2_g, ln2_b)
    wp["cos"], wp["sin"] = _rope_tables(seq)
    bias = _bias_tiles(rel_bias)
    x2 = x.reshape(batch * seq, D_MODEL)
    for l in range(DEPTH):
        lambda_init = 0.8 - 0.6 * math.exp(-0.3 * l)
        dq, dk, dvt, qm, km, vmt = _proj_call(x2, wp, l, seq)
        mix_a, mix_b = _attn_call(dq, dk, dvt, bias, wp, l, qm, km, vmt, seq, lambda_init)
        x2 = _post_call(x2, mix_a, mix_b, wp, l)
    return x2.reshape(batch, seq, D_MODEL)
```

```python
import functools
import math

import jax
import jax.numpy as jnp
from jax import lax
from jax.experimental import pallas as pl
from jax.experimental.pallas import tpu as pltpu

F32 = jnp.float32
BF16 = jnp.bfloat16

D_MODEL = 1024
DEPTH = 2
CHUNK = 64
HEADS = 4
DIFF_HEAD_DIM = 64
DIFF_V_DIM = 2 * DIFF_HEAD_DIM
DIFF_WIDTH = HEADS * DIFF_V_DIM
MLA_NOPE_DIM = 128
MLA_ROPE_DIM = 64
MLA_V_DIM = 128
MLA_Q_RANK = 256
MLA_KV_RANK = 128
MLA_WIDTH = HEADS * MLA_V_DIM
MLA_QK_PAD = 256
D_FF = 4 * D_MODEL
N_BUCKETS = 32
MAX_DISTANCE = 128
ROPE_THETA = 10000.0
ALPHA = (2 * DEPTH) ** 0.25
LN_EPS = 1e-5
RMS_EPS = 1e-6
NEG_BIG = -1e30
LOG2E = math.log2(math.e)

LANES = 128
BF16_SUBLANES = 16
ATT_TILE = 256
ATT_HEADS_PER_STEP = 2
PROJ_ROWS = 1024
POST_ROWS = 1024
POST_ROW_BLOCK = 256
FF_CHUNK = 1024
DIFF_LOOKAHEAD = 2
MLA_LOOKAHEAD = 3
VMEM_LIMIT = 56 * 1024 * 1024
assert ATT_TILE >= MAX_DISTANCE and ATT_TILE % CHUNK == 0

C_DQ = 0
C_DK = C_DQ + DIFF_WIDTH
C_CQ = C_DK + DIFF_WIDTH
C_CKV = C_CQ + MLA_Q_RANK
C_KR = C_CKV + MLA_KV_RANK
W1_COLS = C_KR + LANES

_NT = (((1,), (1,)), ((), ()))


def _rms(x, g):
    return x * lax.rsqrt(jnp.mean(x * x, axis=-1, keepdims=True) + RMS_EPS) * g


def _layer_norm(x, g, b):
    mu = jnp.mean(x, axis=-1, keepdims=True)
    xc = x - mu
    var = jnp.mean(xc * xc, axis=-1, keepdims=True)
    return xc * lax.rsqrt(var + LN_EPS) * g + b


def _layer_spec(arr, l, **kwargs):
    index = (l,) + (0,) * (arr.ndim - 1)
    return pl.BlockSpec((1,) + arr.shape[1:], lambda *_: index, **kwargs)


def _proj_kernel(x_ref, w1_ref, wvt_ref, wuq_ref, wuqs_ref, wukvk_ref, wukvvt_ref,
                 gq_ref, gkv_ref, cos_ref, sin_ref,
                 dq_ref, dk_ref, dvt_ref, qm_ref, km_ref, vmt_ref):
    xb = x_ref[...].astype(BF16)
    h = jnp.dot(xb, w1_ref[0], preferred_element_type=F32)
    dq_ref[...] = h[:, C_DQ:C_DQ + DIFF_WIDTH].astype(BF16)
    dk_ref[...] = h[:, C_DK:C_DK + DIFF_WIDTH].astype(BF16)
    dvt_ref[0] = lax.dot_general(wvt_ref[0], xb, _NT, preferred_element_type=F32).astype(BF16)

    cos = cos_ref[...]
    sin = sin_ref[...]
    c_q = _rms(h[:, C_CQ:C_CQ + MLA_Q_RANK], gq_ref[0]).astype(BF16)
    qf = jnp.dot(c_q, wuq_ref[0], preferred_element_type=F32)
    qsw = jnp.dot(c_q, wuqs_ref[0], preferred_element_type=F32)
    c_kv = _rms(h[:, C_CKV:C_CKV + MLA_KV_RANK], gkv_ref[0]).astype(BF16)
    kn = jnp.dot(c_kv, wukvk_ref[0], preferred_element_type=F32)
    vmt_ref[0] = lax.dot_general(wukvvt_ref[0], c_kv, _NT,
                                 preferred_element_type=F32).astype(BF16)
    kr = h[:, C_KR:C_KR + LANES]
    k_rope = (kr * cos + pltpu.roll(kr, LANES // 2, 1) * sin).astype(BF16)

    for hh in range(HEADS):
        b0 = hh * MLA_QK_PAD
        qm_ref[:, b0:b0 + LANES] = qf[:, b0:b0 + LANES].astype(BF16)
        qm_ref[:, b0 + LANES:b0 + 2 * LANES] = (
            qf[:, b0 + LANES:b0 + 2 * LANES] * cos + qsw[:, hh * LANES:(hh + 1) * LANES] * sin
        ).astype(BF16)
        km_ref[:, b0:b0 + LANES] = kn[:, hh * LANES:(hh + 1) * LANES].astype(BF16)
        km_ref[:, b0 + LANES:b0 + 2 * LANES] = k_rope


def _proj_call(x2, wp, l, seq):
    ntok = x2.shape[0]
    batch = ntok // seq
    rows = PROJ_ROWS
    steps_per_seq = seq // rows

    def const(name):
        return _layer_spec(wp[name], l)

    def tok(width):
        return pl.BlockSpec((rows, width), lambda i: (i, 0))

    vt_spec = pl.BlockSpec((1, HEADS * LANES, rows),
                           lambda i: (i // steps_per_seq, 0, i % steps_per_seq))
    pos_spec = pl.BlockSpec((rows, LANES), lambda i: (i % steps_per_seq, 0))
    vt_shape = jax.ShapeDtypeStruct((batch, HEADS * LANES, seq), BF16)
    return pl.pallas_call(
        _proj_kernel,
        grid=(ntok // rows,),
        in_specs=[tok(D_MODEL), const("w1"), const("wvt"), const("wuq"), const("wuqs"),
                  const("wukvk"), const("wukvvt"), const("gq"), const("gkv"), pos_spec, pos_spec],
        out_specs=[tok(DIFF_WIDTH), tok(DIFF_WIDTH), vt_spec,
                   tok(HEADS * MLA_QK_PAD), tok(HEADS * MLA_QK_PAD), vt_spec],
        out_shape=[jax.ShapeDtypeStruct((ntok, DIFF_WIDTH), BF16),
                   jax.ShapeDtypeStruct((ntok, DIFF_WIDTH), BF16),
                   vt_shape,
                   jax.ShapeDtypeStruct((ntok, HEADS * MLA_QK_PAD), BF16),
                   jax.ShapeDtypeStruct((ntok, HEADS * MLA_QK_PAD), BF16),
                   vt_shape],
        compiler_params=pltpu.CompilerParams(dimension_semantics=("arbitrary",),
                                             vmem_limit_bytes=VMEM_LIMIT),
        name="proj",
    )(x2, wp["w1"], wp["wvt"], wp["wuq"], wp["wuqs"], wp["wukvk"], wp["wukvvt"],
      wp["gq"], wp["gkv"], wp["cos"], wp["sin"])


def _col_max(a, b):
    m = jnp.max(b, axis=0, keepdims=True)
    return m if a is None else jnp.maximum(a, m)


def _attend(heads, n_tiles, q_fn, score_fn, finish_fn, vt_fn, lookahead):
    ones = jnp.ones((BF16_SUBLANES, ATT_TILE), BF16)
    items = [(h, qi, c) for h in heads for qi in range(n_tiles) for c in range(qi + 1)]
    qs, scores = {}, {}

    def issue(idx):
        h, qi, c = items[idx]
        if c == 0:
            qs[h, qi] = q_fn(h, qi)
        scores[idx] = score_fn(h, qi, qs[h, qi], c)

    for idx in range(min(lookahead, len(items))):
        issue(idx)
    m, acc = None, None
    for idx, (h, qi, c) in enumerate(items):
        if idx + lookahead < len(items):
            issue(idx + lookahead)
        s = scores.pop(idx)
        m_new = _col_max(m, s)
        e = jnp.exp2(s - m_new).astype(BF16)
        vt = jnp.concatenate([vt_fn(h, c), ones], axis=0)
        pv = jnp.dot(vt, e, preferred_element_type=F32)
        acc = pv if acc is None else acc * jnp.exp2(m - m_new) + pv
        m = m_new
        if c == qi:
            finish_fn(h, qi, acc[:LANES] / acc[LANES:LANES + 1])
            m, acc = None, None
        yield


def _diff_steps(heads, q_ref, k_ref, vt_ref, bias_ref, lam_ref, g_ref, o_ref, lambda_init):
    t = ATT_TILE
    lane = lax.broadcasted_iota(jnp.int32, (t, LANES), 1)
    lp = lam_ref[0]
    lam = (jnp.exp(jnp.sum(lp[0:1] * lp[1:2], axis=-1, keepdims=True))
           - jnp.exp(jnp.sum(lp[2:3] * lp[3:4], axis=-1, keepdims=True)) + lambda_init)
    g = g_ref[0]

    def q_rows(h, qi):
        q = q_ref[qi * t:(qi + 1) * t, h * LANES:(h + 1) * LANES]
        zero = jnp.zeros_like(q)
        return jnp.concatenate([jnp.where(lane < DIFF_HEAD_DIM, q, zero),
                                jnp.where(lane >= DIFF_HEAD_DIM, q, zero)], axis=0)

    def score(h, qi, qs, c):
        k = k_ref[c * t:(c + 1) * t, h * LANES:(h + 1) * LANES]
        s = lax.dot_general(k, qs, _NT, preferred_element_type=F32)
        return s + bias_ref[h, qi - c] if qi - c <= 1 else s

    def values(h, c):
        return vt_ref[0, h * LANES:(h + 1) * LANES, c * t:(c + 1) * t]

    def finish(h, qi, o):
        ot = o[:, :t] - lam * o[:, t:]
        y = _rms(ot.T, g) * (1.0 - lambda_init)
        o_ref[qi * t:(qi + 1) * t, h * LANES:(h + 1) * LANES] = y.astype(BF16)

    return _attend(heads, q_ref.shape[0] // t, q_rows, score, finish, values, DIFF_LOOKAHEAD)


def _mla_steps(heads, q_ref, k_ref, vt_ref, o_ref):
    t = ATT_TILE
    w = MLA_QK_PAD
    kc = lax.broadcasted_iota(jnp.int32, (t, t), 0) // CHUNK
    qc = lax.broadcasted_iota(jnp.int32, (t, t), 1) // CHUNK
    allowed = kc <= qc

    def q_rows(h, qi):
        return q_ref[qi * t:(qi + 1) * t, h * w:(h + 1) * w]

    def score(h, qi, q, c):
        k = k_ref[c * t:(c + 1) * t, h * w:(h + 1) * w]
        s = lax.dot_general(k, q, _NT, preferred_element_type=F32)
        return jnp.where(allowed, s, NEG_BIG) if c == qi else s

    def values(h, c):
        return vt_ref[0, h * LANES:(h + 1) * LANES, c * t:(c + 1) * t]

    def finish(h, qi, o):
        o_ref[qi * t:(qi + 1) * t, h * LANES:(h + 1) * LANES] = o.T.astype(BF16)

    return _attend(heads, q_ref.shape[0] // t, q_rows, score, finish, values, MLA_LOOKAHEAD)


def _attn_kernel(dq_ref, dk_ref, dvt_ref, bias_ref, lam_ref, g_ref, qm_ref, km_ref, vmt_ref,
                 oa_ref, ob_ref, *, lambda_init):
    heads = range(ATT_HEADS_PER_STEP)
    streams = [_diff_steps(heads, dq_ref, dk_ref, dvt_ref, bias_ref, lam_ref, g_ref, oa_ref,
                           lambda_init),
               _mla_steps(heads, qm_ref, km_ref, vmt_ref, ob_ref)]
    while streams:
        for s in list(streams):
            if next(s, "done") == "done":
                streams.remove(s)


def _attn_call(dq, dk, dvt, bias, wp, l, qm, km, vmt, seq, lambda_init):
    ntok = dq.shape[0]
    n = ATT_HEADS_PER_STEP

    def rows(width):
        return pl.BlockSpec((seq, n * width), lambda b, h: (b, h))

    vt_spec = pl.BlockSpec((1, n * LANES, seq), lambda b, h: (b, h, 0))
    return pl.pallas_call(
        functools.partial(_attn_kernel, lambda_init=lambda_init),
        grid=(ntok // seq, HEADS // n),
        in_specs=[rows(LANES), rows(LANES), vt_spec,
                  pl.BlockSpec((n,) + bias.shape[1:], lambda b, h: (h, 0, 0, 0)),
                  _layer_spec(wp["lam"], l), _layer_spec(wp["subln"], l),
                  rows(MLA_QK_PAD), rows(MLA_QK_PAD), vt_spec],
        out_specs=[rows(LANES), rows(LANES)],
        out_shape=[jax.ShapeDtypeStruct((ntok, DIFF_WIDTH), BF16),
                   jax.ShapeDtypeStruct((ntok, MLA_WIDTH), BF16)],
        compiler_params=pltpu.CompilerParams(dimension_semantics=("arbitrary", "arbitrary"),
                                             vmem_limit_bytes=VMEM_LIMIT),
        name="attn",
    )(dq, dk, dvt, bias, wp["lam"], wp["subln"], qm, km, vmt)


def _post_kernel(x_ref, ma_ref, mb_ref, wo_ref, g1_ref, b1_ref, wi_ref, wout_ref, g2_ref, b2_ref,
                 o_ref):
    n_blocks = x_ref.shape[0] // POST_ROW_BLOCK
    n_chunks = D_FF // FF_CHUNK

    def attn_out(r):
        rs = slice(r * POST_ROW_BLOCK, (r + 1) * POST_ROW_BLOCK)
        y = (jnp.dot(ma_ref[rs, :], wo_ref[0, :DIFF_WIDTH, :], preferred_element_type=F32)
             + jnp.dot(mb_ref[rs, :], wo_ref[0, DIFF_WIDTH:, :], preferred_element_type=F32))
        return _layer_norm(ALPHA * x_ref[rs, :] + y, g1_ref[0], b1_ref[0])

    def up(x1b, c):
        return jnp.dot(x1b, wi_ref[0, :, c * FF_CHUNK:(c + 1) * FF_CHUNK],
                       preferred_element_type=F32)

    x1 = attn_out(0)
    for r in range(n_blocks):
        x1_next = attn_out(r + 1) if r + 1 < n_blocks else None
        x1b = x1.astype(BF16)
        u = up(x1b, 0)
        y2 = None
        for c in range(n_chunks):
            u_next = up(x1b, c + 1) if c + 1 < n_chunks else None
            act = jnp.square(jnp.maximum(u, 0.0)).astype(BF16)
            d = jnp.dot(act, wout_ref[0, c * FF_CHUNK:(c + 1) * FF_CHUNK, :],
                        preferred_element_type=F32)
            y2 = d if y2 is None else y2 + d
            u = u_next
        o_ref[r * POST_ROW_BLOCK:(r + 1) * POST_ROW_BLOCK, :] = _layer_norm(
            ALPHA * x1 + y2, g2_ref[0], b2_ref[0])
        x1 = x1_next


def _post_call(x2, mix_a, mix_b, wp, l):
    ntok = x2.shape[0]
    rows = POST_ROWS

    def const(name):
        return _layer_spec(wp[name], l, pipeline_mode=pl.Buffered(1))

    def tok(width):
        return pl.BlockSpec((rows, width), lambda i: (i, 0))

    return pl.pallas_call(
        _post_kernel,
        grid=(ntok // rows,),
        in_specs=[tok(D_MODEL), tok(DIFF_WIDTH), tok(MLA_WIDTH), const("wo"), const("g1"),
                  const("b1"), const("wi"), const("wout"), const("g2"), const("b2")],
        out_specs=tok(D_MODEL),
        out_shape=jax.ShapeDtypeStruct((ntok, D_MODEL), F32),
        compiler_params=pltpu.CompilerParams(dimension_semantics=("arbitrary",),
                                             vmem_limit_bytes=VMEM_LIMIT),
        name="post",
    )(x2, mix_a, mix_b, wp["wo"], wp["g1"], wp["b1"], wp["wi"], wp["wout"], wp["g2"], wp["b2"])


def _t5_bucket(rel):
    nb = N_BUCKETS // 2
    ret = (rel > 0).astype(jnp.int32) * nb
    n = jnp.abs(rel)
    max_exact = nb // 2
    nf = jnp.maximum(n, 1).astype(F32)
    large = max_exact + (jnp.log(nf / max_exact) / math.log(MAX_DISTANCE / max_exact)
                         * (nb - max_exact)).astype(jnp.int32)
    large = jnp.minimum(large, nb - 1)
    return ret + jnp.where(n < max_exact, n, large)


def _bias_tiles(rel_bias):
    t = ATT_TILE
    p = 2 * t
    table = rel_bias.astype(F32).T
    far = table[:, N_BUCKETS // 2 - 1]
    w = jnp.arange(p, dtype=jnp.int32)[None, :]
    d = jnp.arange(2, dtype=jnp.int32)[:, None]
    rel = (t - 1) - w - d * t
    vec = (table[:, _t5_bucket(rel)] - far[:, None, None]) * LOG2E
    rows = jnp.tile(vec, (1, 1, t))[..., :t * (p - 1)].reshape(HEADS, 2, t, p - 1)
    bias = rows[..., t - 1:2 * t - 1]
    kk = jnp.arange(t, dtype=jnp.int32)[:, None]
    qq = jnp.arange(t, dtype=jnp.int32)[None, :]
    allowed = (kk // CHUNK <= qq // CHUNK)[None, None] | (d > 0)[None, :, :, None]
    bias = jnp.where(allowed, bias, NEG_BIG)
    return jnp.concatenate([bias, bias], axis=-1)


def _rope_tables(seq):
    pos = jnp.arange(seq, dtype=F32)
    inv = ROPE_THETA ** (-jnp.arange(0, MLA_ROPE_DIM, 2, dtype=F32) / MLA_ROPE_DIM)
    ang = pos[:, None] * inv[None, :]
    cos, sin = jnp.cos(ang), jnp.sin(ang)
    z = jnp.zeros((seq, LANES - MLA_ROPE_DIM), F32)
    return (jnp.concatenate([cos, cos, z], axis=-1), jnp.concatenate([-sin, sin, z], axis=-1))


def _swap_halves(w):
    half = w.shape[-1] // 2
    return jnp.concatenate([w[..., half:], w[..., :half]], axis=-1)


def _pack_params(w_in, lambda_q1, lambda_k1, lambda_q2, lambda_k2, subln_g, q_norm_g, w_uq,
                 kv_norm_g, w_ukv, w_o, ln1_g, ln1_b, w_mlp_in, w_mlp_out, ln2_g, ln2_b):
    depth = w_in.shape[0]
    o_k, o_v, o_cq = DIFF_WIDTH, 2 * DIFF_WIDTH, 3 * DIFF_WIDTH
    o_ckv = o_cq + MLA_Q_RANK
    o_kr = o_ckv + MLA_KV_RANK
    kr = w_in[..., o_kr:]
    w1 = jnp.concatenate([w_in[..., :o_k] * (DIFF_HEAD_DIM ** -0.5 * LOG2E), w_in[..., o_k:o_v],
                          w_in[..., o_cq:o_ckv], w_in[..., o_ckv:o_kr], kr, _swap_halves(kr)],
                         axis=-1)
    mla_scale = (MLA_NOPE_DIM + MLA_ROPE_DIM) ** -0.5 * LOG2E
    uq = (w_uq * mla_scale).reshape(depth, MLA_Q_RANK, HEADS, MLA_NOPE_DIM + MLA_ROPE_DIM)
    zq = jnp.zeros((depth, MLA_Q_RANK, HEADS, MLA_QK_PAD - MLA_NOPE_DIM - MLA_ROPE_DIM), F32)
    wuq = jnp.concatenate([uq, zq], axis=-1).reshape(depth, MLA_Q_RANK, HEADS * MLA_QK_PAD)
    wuqs = jnp.concatenate([_swap_halves(uq[..., MLA_NOPE_DIM:]), zq], axis=-1)
    wuqs = wuqs.reshape(depth, MLA_Q_RANK, HEADS * LANES)
    ukv = w_ukv.reshape(depth, MLA_KV_RANK, HEADS, MLA_NOPE_DIM + MLA_V_DIM)
    wukvk = ukv[..., :MLA_NOPE_DIM].reshape(depth, MLA_KV_RANK, HEADS * MLA_NOPE_DIM)
    wukvv = ukv[..., MLA_NOPE_DIM:].reshape(depth, MLA_KV_RANK, HEADS * MLA_V_DIM)
    row = lambda v: v.reshape(depth, 1, -1).astype(F32)
    return dict(
        w1=w1.astype(BF16), wvt=jnp.swapaxes(w_in[..., o_v:o_cq], 1, 2).astype(BF16),
        wuq=wuq.astype(BF16), wuqs=wuqs.astype(BF16), wukvk=wukvk.astype(BF16),
        wukvvt=jnp.swapaxes(wukvv, 1, 2).astype(BF16), gq=row(q_norm_g), gkv=row(kv_norm_g),
        subln=row(subln_g),
        lam=jnp.stack([lambda_q1, lambda_k1, lambda_q2, lambda_k2], axis=1).astype(F32),
        wo=w_o.astype(BF16), g1=row(ln1_g), b1=row(ln1_b), wi=w_mlp_in.astype(BF16),
        wout=w_mlp_out.astype(BF16), g2=row(ln2_g), b2=row(ln2_b))


def kernel(x, w_in, lambda_q1, lambda_k1, lambda_q2, lambda_k2, subln_g, q_norm_g, w_uq, kv_norm_g,
           w_ukv, rel_bias, w_o, ln1_g, ln1_b, w_mlp_in, w_mlp_out, ln2_g, ln2_b):
    batch, seq, _ = x.shape
    assert seq % PROJ_ROWS == 0 and seq % ATT_TILE == 0 and (batch * seq) % POST_ROWS == 0
    wp = _pack_params(w_in, lambda_q1, lambda_k1, lambda_q2, lambda_k2, subln_g, q_norm_g, w_uq,
                      kv_norm_g, w_ukv, w_o, ln1_g, ln1_b, w_mlp_in, w_mlp_out, ln2_g, ln2_b)
    wp["cos"], wp["sin"] = _rope_tables(seq)
    bias = _bias_tiles(rel_bias)
    x2 = x.reshape(batch * seq, D_MODEL)
    for l in range(DEPTH):
        lambda_init = 0.8 - 0.6 * math.exp(-0.3 * l)
        dq, dk, dvt, qm, km, vmt = _proj_call(x2, wp, l, seq)
        mix_a, mix_b = _attn_call(dq, dk, dvt, bias, wp, l, qm, km, vmt, seq, lambda_init)
        x2 = _post_call(x2, mix_a, mix_b, wp, l)
    return x2.reshape(batch, seq, D_MODEL)
```

```python
import functools
import math

import jax
import jax.numpy as jnp
from jax import lax
from jax.experimental import pallas as pl
from jax.experimental.pallas import tpu as pltpu

F32 = jnp.float32
BF16 = jnp.bfloat16

D_MODEL = 1024
DEPTH = 2
CHUNK = 64
HEADS = 4
DIFF_HEAD_DIM = 64
DIFF_V_DIM = 2 * DIFF_HEAD_DIM
DIFF_WIDTH = HEADS * DIFF_V_DIM
MLA_NOPE_DIM = 128
MLA_ROPE_DIM = 64
MLA_V_DIM = 128
MLA_Q_RANK = 256
MLA_KV_RANK = 128
MLA_WIDTH = HEADS * MLA_V_DIM
MLA_QK_PAD = 256
D_FF = 4 * D_MODEL
N_BUCKETS = 32
MAX_DISTANCE = 128
ROPE_THETA = 10000.0
ALPHA = (2 * DEPTH) ** 0.25
LN_EPS = 1e-5
RMS_EPS = 1e-6
NEG_BIG = -1e30
LOG2E = math.log2(math.e)

LANES = 128
BF16_SUBLANES = 16
ATT_TILE = 256
ATT_HEADS_PER_STEP = 2
PROJ_ROWS = 1024
POST_ROWS = 1024
POST_ROW_BLOCK = 256
FF_CHUNK = 1024
DIFF_LOOKAHEAD = 2
MLA_LOOKAHEAD = 5
VMEM_LIMIT = 56 * 1024 * 1024
assert ATT_TILE >= MAX_DISTANCE and ATT_TILE % CHUNK == 0

C_DQ = 0
C_DK = C_DQ + DIFF_WIDTH
C_CQ = C_DK + DIFF_WIDTH
C_CKV = C_CQ + MLA_Q_RANK
C_KR = C_CKV + MLA_KV_RANK
W1_COLS = C_KR + LANES

_NT = (((1,), (1,)), ((), ()))


def _rms(x, g):
    return x * lax.rsqrt(jnp.mean(x * x, axis=-1, keepdims=True) + RMS_EPS) * g


def _layer_norm(x, g, b):
    mu = jnp.mean(x, axis=-1, keepdims=True)
    xc = x - mu
    var = jnp.mean(xc * xc, axis=-1, keepdims=True)
    return xc * lax.rsqrt(var + LN_EPS) * g + b


def _layer_spec(arr, l, **kwargs):
    index = (l,) + (0,) * (arr.ndim - 1)
    return pl.BlockSpec((1,) + arr.shape[1:], lambda *_: index, **kwargs)


def _proj_kernel(x_ref, w1_ref, wvt_ref, wuq_ref, wuqs_ref, wukvk_ref, wukvvt_ref,
                 gq_ref, gkv_ref, cos_ref, sin_ref,
                 dq_ref, dk_ref, dvt_ref, qm_ref, km_ref, vmt_ref):
    xb = x_ref[...].astype(BF16)
    h = jnp.dot(xb, w1_ref[0], preferred_element_type=F32)
    dq_ref[...] = h[:, C_DQ:C_DQ + DIFF_WIDTH].astype(BF16)
    dk_ref[...] = h[:, C_DK:C_DK + DIFF_WIDTH].astype(BF16)
    dvt_ref[0] = lax.dot_general(wvt_ref[0], xb, _NT, preferred_element_type=F32).astype(BF16)

    cos = cos_ref[...]
    sin = sin_ref[...]
    c_q = _rms(h[:, C_CQ:C_CQ + MLA_Q_RANK], gq_ref[0]).astype(BF16)
    qf = jnp.dot(c_q, wuq_ref[0], preferred_element_type=F32)
    qsw = jnp.dot(c_q, wuqs_ref[0], preferred_element_type=F32)
    c_kv = _rms(h[:, C_CKV:C_CKV + MLA_KV_RANK], gkv_ref[0]).astype(BF16)
    kn = jnp.dot(c_kv, wukvk_ref[0], preferred_element_type=F32)
    vmt_ref[0] = lax.dot_general(wukvvt_ref[0], c_kv, _NT,
                                 preferred_element_type=F32).astype(BF16)
    kr = h[:, C_KR:C_KR + LANES]
    k_rope = (kr * cos + pltpu.roll(kr, LANES // 2, 1) * sin).astype(BF16)

    for hh in range(HEADS):
        b0 = hh * MLA_QK_PAD
        qm_ref[:, b0:b0 + LANES] = qf[:, b0:b0 + LANES].astype(BF16)
        qm_ref[:, b0 + LANES:b0 + 2 * LANES] = (
            qf[:, b0 + LANES:b0 + 2 * LANES] * cos + qsw[:, hh * LANES:(hh + 1) * LANES] * sin
        ).astype(BF16)
        km_ref[:, b0:b0 + LANES] = kn[:, hh * LANES:(hh + 1) * LANES].astype(BF16)
        km_ref[:, b0 + LANES:b0 + 2 * LANES] = k_rope


def _proj_call(x2, wp, l, seq):
    ntok = x2.shape[0]
    batch = ntok // seq
    rows = PROJ_ROWS
    steps_per_seq = seq // rows

    def const(name):
        return _layer_spec(wp[name], l)

    def tok(width):
        return pl.BlockSpec((rows, width), lambda i: (i, 0))

    vt_spec = pl.BlockSpec((1, HEADS * LANES, rows),
                           lambda i: (i // steps_per_seq, 0, i % steps_per_seq))
    pos_spec = pl.BlockSpec((rows, LANES), lambda i: (i % steps_per_seq, 0))
    vt_shape = jax.ShapeDtypeStruct((batch, HEADS * LANES, seq), BF16)
    return pl.pallas_call(
        _proj_kernel,
        grid=(ntok // rows,),
        in_specs=[tok(D_MODEL), const("w1"), const("wvt"), const("wuq"), const("wuqs"),
                  const("wukvk"), const("wukvvt"), const("gq"), const("gkv"), pos_spec, pos_spec],
        out_specs=[tok(DIFF_WIDTH), tok(DIFF_WIDTH), vt_spec,
                   tok(HEADS * MLA_QK_PAD), tok(HEADS * MLA_QK_PAD), vt_spec],
        out_shape=[jax.ShapeDtypeStruct((ntok, DIFF_WIDTH), BF16),
                   jax.ShapeDtypeStruct((ntok, DIFF_WIDTH), BF16),
                   vt_shape,
                   jax.ShapeDtypeStruct((ntok, HEADS * MLA_QK_PAD), BF16),
                   jax.ShapeDtypeStruct((ntok, HEADS * MLA_QK_PAD), BF16),
                   vt_shape],
        compiler_params=pltpu.CompilerParams(dimension_semantics=("arbitrary",),
                                             vmem_limit_bytes=VMEM_LIMIT),
        name="proj",
    )(x2, wp["w1"], wp["wvt"], wp["wuq"], wp["wuqs"], wp["wukvk"], wp["wukvvt"],
      wp["gq"], wp["gkv"], wp["cos"], wp["sin"])


def _col_max(a, b):
    m = jnp.max(b, axis=0, keepdims=True)
    return m if a is None else jnp.maximum(a, m)


def _attend(heads, n_tiles, q_fn, score_fn, finish_fn, vt_fn, lookahead):
    ones = jnp.ones((BF16_SUBLANES, ATT_TILE), BF16)
    items = [(h, qi, c) for h in heads for qi in range(n_tiles) for c in range(qi + 1)]
    qs, scores = {}, {}

    def issue(idx):
        h, qi, c = items[idx]
        if c == 0:
            qs[h, qi] = q_fn(h, qi)
        scores[idx] = score_fn(h, qi, qs[h, qi], c)

    for idx in range(min(lookahead, len(items))):
        issue(idx)
    m, acc = None, None
    for idx, (h, qi, c) in enumerate(items):
        if idx + lookahead < len(items):
            issue(idx + lookahead)
        s = scores.pop(idx)
        m_new = _col_max(m, s)
        e = jnp.exp2(s - m_new).astype(BF16)
        vt = jnp.concatenate([vt_fn(h, c), ones], axis=0)
        pv = jnp.dot(vt, e, preferred_element_type=F32)
        acc = pv if acc is None else acc * jnp.exp2(m - m_new) + pv
        m = m_new
        if c == qi:
            finish_fn(h, qi, acc[:LANES] / acc[LANES:LANES + 1])
            m, acc = None, None
        yield


def _diff_steps(heads, q_ref, k_ref, vt_ref, bias_ref, lam_ref, g_ref, o_ref, lambda_init):
    t = ATT_TILE
    lane = lax.broadcasted_iota(jnp.int32, (t, LANES), 1)
    lp = lam_ref[0]
    lam = (jnp.exp(jnp.sum(lp[0:1] * lp[1:2], axis=-1, keepdims=True))
           - jnp.exp(jnp.sum(lp[2:3] * lp[3:4], axis=-1, keepdims=True)) + lambda_init)
    g = g_ref[0]

    def q_rows(h, qi):
        q = q_ref[qi * t:(qi + 1) * t, h * LANES:(h + 1) * LANES]
        zero = jnp.zeros_like(q)
        return jnp.concatenate([jnp.where(lane < DIFF_HEAD_DIM, q, zero),
                                jnp.where(lane >= DIFF_HEAD_DIM, q, zero)], axis=0)

    def score(h, qi, qs, c):
        k = k_ref[c * t:(c + 1) * t, h * LANES:(h + 1) * LANES]
        s = lax.dot_general(k, qs, _NT, preferred_element_type=F32)
        return s + bias_ref[h, qi - c] if qi - c <= 1 else s

    def values(h, c):
        return vt_ref[0, h * LANES:(h + 1) * LANES, c * t:(c + 1) * t]

    def finish(h, qi, o):
        ot = o[:, :t] - lam * o[:, t:]
        y = _rms(ot.T, g) * (1.0 - lambda_init)
        o_ref[qi * t:(qi + 1) * t, h * LANES:(h + 1) * LANES] = y.astype(BF16)

    return _attend(heads, q_ref.shape[0] // t, q_rows, score, finish, values, DIFF_LOOKAHEAD)


def _mla_steps(heads, q_ref, k_ref, vt_ref, o_ref):
    t = ATT_TILE
    w = MLA_QK_PAD
    kc = lax.broadcasted_iota(jnp.int32, (t, t), 0) // CHUNK
    qc = lax.broadcasted_iota(jnp.int32, (t, t), 1) // CHUNK
    allowed = kc <= qc

    def q_rows(h, qi):
        return q_ref[qi * t:(qi + 1) * t, h * w:(h + 1) * w]

    def score(h, qi, q, c):
        k = k_ref[c * t:(c + 1) * t, h * w:(h + 1) * w]
        s = lax.dot_general(k, q, _NT, preferred_element_type=F32)
        return jnp.where(allowed, s, NEG_BIG) if c == qi else s

    def values(h, c):
        return vt_ref[0, h * LANES:(h + 1) * LANES, c * t:(c + 1) * t]

    def finish(h, qi, o):
        o_ref[qi * t:(qi + 1) * t, h * LANES:(h + 1) * LANES] = o.T.astype(BF16)

    return _attend(heads, q_ref.shape[0] // t, q_rows, score, finish, values, MLA_LOOKAHEAD)


def _attn_kernel(dq_ref, dk_ref, dvt_ref, bias_ref, lam_ref, g_ref, qm_ref, km_ref, vmt_ref,
                 oa_ref, ob_ref, *, lambda_init):
    heads = range(ATT_HEADS_PER_STEP)
    streams = [_diff_steps(heads, dq_ref, dk_ref, dvt_ref, bias_ref, lam_ref, g_ref, oa_ref,
                           lambda_init),
               _mla_steps(heads, qm_ref, km_ref, vmt_ref, ob_ref)]
    while streams:
        for s in list(streams):
            if next(s, "done") == "done":
                streams.remove(s)


def _attn_call(dq, dk, dvt, bias, wp, l, qm, km, vmt, seq, lambda_init):
    ntok = dq.shape[0]
    n = ATT_HEADS_PER_STEP

    def rows(width):
        return pl.BlockSpec((seq, n * width), lambda b, h: (b, h))

    vt_spec = pl.BlockSpec((1, n * LANES, seq), lambda b, h: (b, h, 0))
    return pl.pallas_call(
        functools.partial(_attn_kernel, lambda_init=lambda_init),
        grid=(ntok // seq, HEADS // n),
        in_specs=[rows(LANES), rows(LANES), vt_spec,
                  pl.BlockSpec((n,) + bias.shape[1:], lambda b, h: (h, 0, 0, 0)),
                  _layer_spec(wp["lam"], l), _layer_spec(wp["subln"], l),
                  rows(MLA_QK_PAD), rows(MLA_QK_PAD), vt_spec],
        out_specs=[rows(LANES), rows(LANES)],
        out_shape=[jax.ShapeDtypeStruct((ntok, DIFF_WIDTH), BF16),
                   jax.ShapeDtypeStruct((ntok, MLA_WIDTH), BF16)],
        compiler_params=pltpu.CompilerParams(dimension_semantics=("arbitrary", "arbitrary"),
                                             vmem_limit_bytes=VMEM_LIMIT),
        name="attn",
    )(dq, dk, dvt, bias, wp["lam"], wp["subln"], qm, km, vmt)


def _post_kernel(x_ref, ma_ref, mb_ref, wo_ref, g1_ref, b1_ref, wi_ref, wout_ref, g2_ref, b2_ref,
                 o_ref):
    n_blocks = x_ref.shape[0] // POST_ROW_BLOCK
    n_chunks = D_FF // FF_CHUNK

    def attn_out(r):
        rs = slice(r * POST_ROW_BLOCK, (r + 1) * POST_ROW_BLOCK)
        y = (jnp.dot(ma_ref[rs, :], wo_ref[0, :DIFF_WIDTH, :], preferred_element_type=F32)
             + jnp.dot(mb_ref[rs, :], wo_ref[0, DIFF_WIDTH:, :], preferred_element_type=F32))
        return _layer_norm(ALPHA * x_ref[rs, :] + y, g1_ref[0], b1_ref[0])

    def up(x1b, c):
        return jnp.dot(x1b, wi_ref[0, :, c * FF_CHUNK:(c + 1) * FF_CHUNK],
                       preferred_element_type=F32)

    x1 = attn_out(0)
    for r in range(n_blocks):
        x1_next = attn_out(r + 1) if r + 1 < n_blocks else None
        x1b = x1.astype(BF16)
        u = up(x1b, 0)
        y2 = None
        for c in range(n_chunks):
            u_next = up(x1b, c + 1) if c + 1 < n_chunks else None
            act = jnp.square(jnp.maximum(u, 0.0)).astype(BF16)
            d = jnp.dot(act, wout_ref[0, c * FF_CHUNK:(c + 1) * FF_CHUNK, :],
                        preferred_element_type=F32)
            y2 = d if y2 is None else y2 + d
            u = u_next
        o_ref[r * POST_ROW_BLOCK:(r + 1) * POST_ROW_BLOCK, :] = _layer_norm(
            ALPHA * x1 + y2, g2_ref[0], b2_ref[0])
        x1 = x1_next


def _post_call(x2, mix_a, mix_b, wp, l):
    ntok = x2.shape[0]
    rows = POST_ROWS

    def const(name):
        return _layer_spec(wp[name], l, pipeline_mode=pl.Buffered(1))

    def tok(width):
        return pl.BlockSpec((rows, width), lambda i: (i, 0))

    return pl.pallas_call(
        _post_kernel,
        grid=(ntok // rows,),
        in_specs=[tok(D_MODEL), tok(DIFF_WIDTH), tok(MLA_WIDTH), const("wo"), const("g1"),
                  const("b1"), const("wi"), const("wout"), const("g2"), const("b2")],
        out_specs=tok(D_MODEL),
        out_shape=jax.ShapeDtypeStruct((ntok, D_MODEL), F32),
        compiler_params=pltpu.CompilerParams(dimension_semantics=("arbitrary",),
                                             vmem_limit_bytes=VMEM_LIMIT),
        name="post",
    )(x2, mix_a, mix_b, wp["wo"], wp["g1"], wp["b1"], wp["wi"], wp["wout"], wp["g2"], wp["b2"])


def _t5_bucket(rel):
    nb = N_BUCKETS // 2
    ret = (rel > 0).astype(jnp.int32) * nb
    n = jnp.abs(rel)
    max_exact = nb // 2
    nf = jnp.maximum(n, 1).astype(F32)
    large = max_exact + (jnp.log(nf / max_exact) / math.log(MAX_DISTANCE / max_exact)
                         * (nb - max_exact)).astype(jnp.int32)
    large = jnp.minimum(large, nb - 1)
    return ret + jnp.where(n < max_exact, n, large)


def _bias_tiles(rel_bias):
    t = ATT_TILE
    p = 2 * t
    table = rel_bias.astype(F32).T
    far = table[:, N_BUCKETS // 2 - 1]
    w = jnp.arange(p, dtype=jnp.int32)[None, :]
    d = jnp.arange(2, dtype=jnp.int32)[:, None]
    rel = (t - 1) - w - d * t
    vec = (table[:, _t5_bucket(rel)] - far[:, None, None]) * LOG2E
    rows = jnp.tile(vec, (1, 1, t))[..., :t * (p - 1)].reshape(HEADS, 2, t, p - 1)
    bias = rows[..., t - 1:2 * t - 1]
    kk = jnp.arange(t, dtype=jnp.int32)[:, None]
    qq = jnp.arange(t, dtype=jnp.int32)[None, :]
    allowed = (kk // CHUNK <= qq // CHUNK)[None, None] | (d > 0)[None, :, :, None]
    bias = jnp.where(allowed, bias, NEG_BIG)
    return jnp.concatenate([bias, bias], axis=-1)


def _rope_tables(seq):
    pos = jnp.arange(seq, dtype=F32)
    inv = ROPE_THETA ** (-jnp.arange(0, MLA_ROPE_DIM, 2, dtype=F32) / MLA_ROPE_DIM)
    ang = pos[:, None] * inv[None, :]
    cos, sin = jnp.cos(ang), jnp.sin(ang)
    z = jnp.zeros((seq, LANES - MLA_ROPE_DIM), F32)
    return (jnp.concatenate([cos, cos, z], axis=-1), jnp.concatenate([-sin, sin, z], axis=-1))


def _swap_halves(w):
    half = w.shape[-1] // 2
    return jnp.concatenate([w[..., half:], w[..., :half]], axis=-1)


def _pack_params(w_in, lambda_q1, lambda_k1, lambda_q2, lambda_k2, subln_g, q_norm_g, w_uq,
                 kv_norm_g, w_ukv, w_o, ln1_g, ln1_b, w_mlp_in, w_mlp_out, ln2_g, ln2_b):
    depth = w_in.shape[0]
    o_k, o_v, o_cq = DIFF_WIDTH, 2 * DIFF_WIDTH, 3 * DIFF_WIDTH
    o_ckv = o_cq + MLA_Q_RANK
    o_kr = o_ckv + MLA_KV_RANK
    kr = w_in[..., o_kr:]
    w1 = jnp.concatenate([w_in[..., :o_k] * (DIFF_HEAD_DIM ** -0.5 * LOG2E), w_in[..., o_k:o_v],
                          w_in[..., o_cq:o_ckv], w_in[..., o_ckv:o_kr], kr, _swap_halves(kr)],
                         axis=-1)
    mla_scale = (MLA_NOPE_DIM + MLA_ROPE_DIM) ** -0.5 * LOG2E
    uq = (w_uq * mla_scale).reshape(depth, MLA_Q_RANK, HEADS, MLA_NOPE_DIM + MLA_ROPE_DIM)
    zq = jnp.zeros((depth, MLA_Q_RANK, HEADS, MLA_QK_PAD - MLA_NOPE_DIM - MLA_ROPE_DIM), F32)
    wuq = jnp.concatenate([uq, zq], axis=-1).reshape(depth, MLA_Q_RANK, HEADS * MLA_QK_PAD)
    wuqs = jnp.concatenate([_swap_halves(uq[..., MLA_NOPE_DIM:]), zq], axis=-1)
    wuqs = wuqs.reshape(depth, MLA_Q_RANK, HEADS * LANES)
    ukv = w_ukv.reshape(depth, MLA_KV_RANK, HEADS, MLA_NOPE_DIM + MLA_V_DIM)
    wukvk = ukv[..., :MLA_NOPE_DIM].reshape(depth, MLA_KV_RANK, HEADS * MLA_NOPE_DIM)
    wukvv = ukv[..., MLA_NOPE_DIM:].reshape(depth, MLA_KV_RANK, HEADS * MLA_V_DIM)
    row = lambda v: v.reshape(depth, 1, -1).astype(F32)
    return dict(
        w1=w1.astype(BF16), wvt=jnp.swapaxes(w_in[..., o_v:o_cq], 1, 2).astype(BF16),
        wuq=wuq.astype(BF16), wuqs=wuqs.astype(BF16), wukvk=wukvk.astype(BF16),
        wukvvt=jnp.swapaxes(wukvv, 1, 2).astype(BF16), gq=row(q_norm_g), gkv=row(kv_norm_g),
        subln=row(subln_g),
        lam=jnp.stack([lambda_q1, lambda_k1, lambda_q2, lambda_k2], axis=1).astype(F32),
        wo=w_o.astype(BF16), g1=row(ln1_g), b1=row(ln1_b), wi=w_mlp_in.astype(BF16),
        wout=w_mlp_out.astype(BF16), g2=row(ln2_g), b2=row(ln2_b))


def kernel(x, w_in, lambda_q1, lambda_k1, lambda_q2, lambda_k2, subln_g, q_norm_g, w_uq, kv_norm_g,
           w_ukv, rel_bias, w_o, ln1_g, ln1_b, w_mlp_in, w_mlp_out, ln2_g, ln2_b):
    batch, seq, _ = x.shape
    assert seq % PROJ_ROWS == 0 and seq % ATT_TILE == 0 and (batch * seq) % POST_ROWS == 0
    wp = _pack_params(w_in, lambda_q1, lambda_k1, lambda_q2, lambda_k2, subln_g, q_norm_g, w_uq,
                      kv_norm_g, w_ukv, w_o, ln1_g, ln1_b, w_mlp_in, w_mlp_out, ln2_g, ln2_b)
    wp["cos"], wp["sin"] = _rope_tables(seq)
    bias = _bias_tiles(rel_bias)
    x2 = x.reshape(batch * seq, D_MODEL)
    for l in range(DEPTH):
        lambda_init = 0.8 - 0.6 * math.exp(-0.3 * l)
        dq, dk, dvt, qm, km, vmt = _proj_call(x2, wp, l, seq)
        mix_a, mix_b = _attn_call(dq, dk, dvt, bias, wp, l, qm, km, vmt, seq, lambda_init)
        x2 = _post_call(x2, mix_a, mix_b, wp, l)
    return x2.reshape(batch, seq, D_MODEL)
```

```python
import functools
import math

import jax
import jax.numpy as jnp
from jax import lax
from jax.experimental import pallas as pl
from jax.experimental.pallas import tpu as pltpu

F32 = jnp.float32
BF16 = jnp.bfloat16

D_MODEL = 1024
DEPTH = 2
CHUNK = 64
HEADS = 4
DIFF_HEAD_DIM = 64
DIFF_V_DIM = 2 * DIFF_HEAD_DIM
DIFF_WIDTH = HEADS * DIFF_V_DIM
MLA_NOPE_DIM = 128
MLA_ROPE_DIM = 64
MLA_V_DIM = 128
MLA_Q_RANK = 256
MLA_KV_RANK = 128
MLA_WIDTH = HEADS * MLA_V_DIM
MLA_QK_PAD = 256
D_FF = 4 * D_MODEL
N_BUCKETS = 32
MAX_DISTANCE = 128
ROPE_THETA = 10000.0
ALPHA = (2 * DEPTH) ** 0.25
LN_EPS = 1e-5
RMS_EPS = 1e-6
NEG_BIG = -1e30
LOG2E = math.log2(math.e)

LANES = 128
BF16_SUBLANES = 16
ATT_TILE = 256
ATT_HEADS_PER_STEP = 2
PROJ_ROWS = 1024
POST_ROWS = 1024
POST_ROW_BLOCK = 512
FF_CHUNK = 1024
DIFF_LOOKAHEAD = 2
MLA_LOOKAHEAD = 4
VMEM_LIMIT = 56 * 1024 * 1024
assert ATT_TILE >= MAX_DISTANCE and ATT_TILE % CHUNK == 0

C_DQ = 0
C_DK = C_DQ + DIFF_WIDTH
C_CQ = C_DK + DIFF_WIDTH
C_CKV = C_CQ + MLA_Q_RANK
C_KR = C_CKV + MLA_KV_RANK
W1_COLS = C_KR + LANES

_NT = (((1,), (1,)), ((), ()))


def _rms(x, g):
    return x * lax.rsqrt(jnp.mean(x * x, axis=-1, keepdims=True) + RMS_EPS) * g


def _layer_norm(x, g, b):
    mu = jnp.mean(x, axis=-1, keepdims=True)
    xc = x - mu
    var = jnp.mean(xc * xc, axis=-1, keepdims=True)
    return xc * lax.rsqrt(var + LN_EPS) * g + b


def _layer_spec(arr, l, **kwargs):
    index = (l,) + (0,) * (arr.ndim - 1)
    return pl.BlockSpec((1,) + arr.shape[1:], lambda *_: index, **kwargs)


def _proj_kernel(x_ref, w1_ref, wvt_ref, wuq_ref, wuqs_ref, wukvk_ref, wukvvt_ref,
                 gq_ref, gkv_ref, cos_ref, sin_ref,
                 dq_ref, dk_ref, dvt_ref, qm_ref, km_ref, vmt_ref):
    xb = x_ref[...].astype(BF16)
    h = jnp.dot(xb, w1_ref[0], preferred_element_type=F32)
    dq_ref[...] = h[:, C_DQ:C_DQ + DIFF_WIDTH].astype(BF16)
    dk_ref[...] = h[:, C_DK:C_DK + DIFF_WIDTH].astype(BF16)
    dvt_ref[0] = lax.dot_general(wvt_ref[0], xb, _NT, preferred_element_type=F32).astype(BF16)

    cos = cos_ref[...]
    sin = sin_ref[...]
    c_q = _rms(h[:, C_CQ:C_CQ + MLA_Q_RANK], gq_ref[0]).astype(BF16)
    qf = jnp.dot(c_q, wuq_ref[0], preferred_element_type=F32)
    qsw = jnp.dot(c_q, wuqs_ref[0], preferred_element_type=F32)
    c_kv = _rms(h[:, C_CKV:C_CKV + MLA_KV_RANK], gkv_ref[0]).astype(BF16)
    kn = jnp.dot(c_kv, wukvk_ref[0], preferred_element_type=F32)
    vmt_ref[0] = lax.dot_general(wukvvt_ref[0], c_kv, _NT,
                                 preferred_element_type=F32).astype(BF16)
    kr = h[:, C_KR:C_KR + LANES]
    k_rope = (kr * cos + pltpu.roll(kr, LANES // 2, 1) * sin).astype(BF16)

    for hh in range(HEADS):
        b0 = hh * MLA_QK_PAD
        qm_ref[:, b0:b0 + LANES] = qf[:, b0:b0 + LANES].astype(BF16)
        qm_ref[:, b0 + LANES:b0 + 2 * LANES] = (
            qf[:, b0 + LANES:b0 + 2 * LANES] * cos + qsw[:, hh * LANES:(hh + 1) * LANES] * sin
        ).astype(BF16)
        km_ref[:, b0:b0 + LANES] = kn[:, hh * LANES:(hh + 1) * LANES].astype(BF16)
        km_ref[:, b0 + LANES:b0 + 2 * LANES] = k_rope


def _proj_call(x2, wp, l, seq):
    ntok = x2.shape[0]
    batch = ntok // seq
    rows = PROJ_ROWS
    steps_per_seq = seq // rows

    def const(name):
        return _layer_spec(wp[name], l)

    def tok(width):
        return pl.BlockSpec((rows, width), lambda i: (i, 0))

    vt_spec = pl.BlockSpec((1, HEADS * LANES, rows),
                           lambda i: (i // steps_per_seq, 0, i % steps_per_seq))
    pos_spec = pl.BlockSpec((rows, LANES), lambda i: (i % steps_per_seq, 0))
    vt_shape = jax.ShapeDtypeStruct((batch, HEADS * LANES, seq), BF16)
    return pl.pallas_call(
        _proj_kernel,
        grid=(ntok // rows,),
        in_specs=[tok(D_MODEL), const("w1"), const("wvt"), const("wuq"), const("wuqs"),
                  const("wukvk"), const("wukvvt"), const("gq"), const("gkv"), pos_spec, pos_spec],
        out_specs=[tok(DIFF_WIDTH), tok(DIFF_WIDTH), vt_spec,
                   tok(HEADS * MLA_QK_PAD), tok(HEADS * MLA_QK_PAD), vt_spec],
        out_shape=[jax.ShapeDtypeStruct((ntok, DIFF_WIDTH), BF16),
                   jax.ShapeDtypeStruct((ntok, DIFF_WIDTH), BF16),
                   vt_shape,
                   jax.ShapeDtypeStruct((ntok, HEADS * MLA_QK_PAD), BF16),
                   jax.ShapeDtypeStruct((ntok, HEADS * MLA_QK_PAD), BF16),
                   vt_shape],
        compiler_params=pltpu.CompilerParams(dimension_semantics=("arbitrary",),
                                             vmem_limit_bytes=VMEM_LIMIT),
        name="proj",
    )(x2, wp["w1"], wp["wvt"], wp["wuq"], wp["wuqs"], wp["wukvk"], wp["wukvvt"],
      wp["gq"], wp["gkv"], wp["cos"], wp["sin"])


def _col_max(a, b):
    m = jnp.max(b, axis=0, keepdims=True)
    return m if a is None else jnp.maximum(a, m)


def _attend(heads, n_tiles, q_fn, score_fn, finish_fn, vt_fn, lookahead):
    ones = jnp.ones((BF16_SUBLANES, ATT_TILE), BF16)
    items = [(h, qi, c) for h in heads for qi in range(n_tiles) for c in range(qi + 1)]
    qs, scores = {}, {}

    def issue(idx):
        h, qi, c = items[idx]
        if c == 0:
            qs[h, qi] = q_fn(h, qi)
        scores[idx] = score_fn(h, qi, qs[h, qi], c)

    for idx in range(min(lookahead, len(items))):
        issue(idx)
    m, acc = None, None
    for idx, (h, qi, c) in enumerate(items):
        if idx + lookahead < len(items):
            issue(idx + lookahead)
        s = scores.pop(idx)
        m_new = _col_max(m, s)
        e = jnp.exp2(s - m_new).astype(BF16)
        vt = jnp.concatenate([vt_fn(h, c), ones], axis=0)
        pv = jnp.dot(vt, e, preferred_element_type=F32)
        acc = pv if acc is None else acc * jnp.exp2(m - m_new) + pv
        m = m_new
        if c == qi:
            finish_fn(h, qi, acc[:LANES] / acc[LANES:LANES + 1])
            m, acc = None, None
        yield


def _diff_steps(heads, q_ref, k_ref, vt_ref, bias_ref, lam_ref, g_ref, o_ref, lambda_init):
    t = ATT_TILE
    lane = lax.broadcasted_iota(jnp.int32, (t, LANES), 1)
    lp = lam_ref[0]
    lam = (jnp.exp(jnp.sum(lp[0:1] * lp[1:2], axis=-1, keepdims=True))
           - jnp.exp(jnp.sum(lp[2:3] * lp[3:4], axis=-1, keepdims=True)) + lambda_init)
    g = g_ref[0]

    def q_rows(h, qi):
        q = q_ref[qi * t:(qi + 1) * t, h * LANES:(h + 1) * LANES]
        zero = jnp.zeros_like(q)
        return jnp.concatenate([jnp.where(lane < DIFF_HEAD_DIM, q, zero),
                                jnp.where(lane >= DIFF_HEAD_DIM, q, zero)], axis=0)

    def score(h, qi, qs, c):
        k = k_ref[c * t:(c + 1) * t, h * LANES:(h + 1) * LANES]
        s = lax.dot_general(k, qs, _NT, preferred_element_type=F32)
        return s + bias_ref[h, qi - c] if qi - c <= 1 else s

    def values(h, c):
        return vt_ref[0, h * LANES:(h + 1) * LANES, c * t:(c + 1) * t]

    def finish(h, qi, o):
        ot = o[:, :t] - lam * o[:, t:]
        y = _rms(ot.T, g) * (1.0 - lambda_init)
        o_ref[qi * t:(qi + 1) * t, h * LANES:(h + 1) * LANES] = y.astype(BF16)

    return _attend(heads, q_ref.shape[0] // t, q_rows, score, finish, values, DIFF_LOOKAHEAD)


def _mla_steps(heads, q_ref, k_ref, vt_ref, o_ref):
    t = ATT_TILE
    w = MLA_QK_PAD
    kc = lax.broadcasted_iota(jnp.int32, (t, t), 0) // CHUNK
    qc = lax.broadcasted_iota(jnp.int32, (t, t), 1) // CHUNK
    allowed = kc <= qc

    def q_rows(h, qi):
        return q_ref[qi * t:(qi + 1) * t, h * w:(h + 1) * w]

    def score(h, qi, q, c):
        k = k_ref[c * t:(c + 1) * t, h * w:(h + 1) * w]
        s = lax.dot_general(k, q, _NT, preferred_element_type=F32)
        return jnp.where(allowed, s, NEG_BIG) if c == qi else s

    def values(h, c):
        return vt_ref[0, h * LANES:(h + 1) * LANES, c * t:(c + 1) * t]

    def finish(h, qi, o):
        o_ref[qi * t:(qi + 1) * t, h * LANES:(h + 1) * LANES] = o.T.astype(BF16)

    return _attend(heads, q_ref.shape[0] // t, q_rows, score, finish, values, MLA_LOOKAHEAD)


def _attn_kernel(dq_ref, dk_ref, dvt_ref, bias_ref, lam_ref, g_ref, qm_ref, km_ref, vmt_ref,
                 oa_ref, ob_ref, *, lambda_init):
    heads = range(ATT_HEADS_PER_STEP)
    streams = [_diff_steps(heads, dq_ref, dk_ref, dvt_ref, bias_ref, lam_ref, g_ref, oa_ref,
                           lambda_init),
               _mla_steps(heads, qm_ref, km_ref, vmt_ref, ob_ref)]
    while streams:
        for s in list(streams):
            if next(s, "done") == "done":
                streams.remove(s)


def _attn_call(dq, dk, dvt, bias, wp, l, qm, km, vmt, seq, lambda_init):
    ntok = dq.shape[0]
    n = ATT_HEADS_PER_STEP

    def rows(width):
        return pl.BlockSpec((seq, n * width), lambda b, h: (b, h))

    vt_spec = pl.BlockSpec((1, n * LANES, seq), lambda b, h: (b, h, 0))
    return pl.pallas_call(
        functools.partial(_attn_kernel, lambda_init=lambda_init),
        grid=(ntok // seq, HEADS // n),
        in_specs=[rows(LANES), rows(LANES), vt_spec,
                  pl.BlockSpec((n,) + bias.shape[1:], lambda b, h: (h, 0, 0, 0)),
                  _layer_spec(wp["lam"], l), _layer_spec(wp["subln"], l),
                  rows(MLA_QK_PAD), rows(MLA_QK_PAD), vt_spec],
        out_specs=[rows(LANES), rows(LANES)],
        out_shape=[jax.ShapeDtypeStruct((ntok, DIFF_WIDTH), BF16),
                   jax.ShapeDtypeStruct((ntok, MLA_WIDTH), BF16)],
        compiler_params=pltpu.CompilerParams(dimension_semantics=("arbitrary", "arbitrary"),
                                             vmem_limit_bytes=VMEM_LIMIT),
        name="attn",
    )(dq, dk, dvt, bias, wp["lam"], wp["subln"], qm, km, vmt)


def _post_kernel(x_ref, ma_ref, mb_ref, wo_ref, g1_ref, b1_ref, wi_ref, wout_ref, g2_ref, b2_ref,
                 o_ref):
    n_blocks = x_ref.shape[0] // POST_ROW_BLOCK
    n_chunks = D_FF // FF_CHUNK

    def attn_out(r):
        rs = slice(r * POST_ROW_BLOCK, (r + 1) * POST_ROW_BLOCK)
        y = (jnp.dot(ma_ref[rs, :], wo_ref[0, :DIFF_WIDTH, :], preferred_element_type=F32)
             + jnp.dot(mb_ref[rs, :], wo_ref[0, DIFF_WIDTH:, :], preferred_element_type=F32))
        return _layer_norm(ALPHA * x_ref[rs, :] + y, g1_ref[0], b1_ref[0])

    def up(x1b, c):
        return jnp.dot(x1b, wi_ref[0, :, c * FF_CHUNK:(c + 1) * FF_CHUNK],
                       preferred_element_type=F32)

    x1 = attn_out(0)
    for r in range(n_blocks):
        x1_next = attn_out(r + 1) if r + 1 < n_blocks else None
        x1b = x1.astype(BF16)
        u = up(x1b, 0)
        y2 = None
        for c in range(n_chunks):
            u_next = up(x1b, c + 1) if c + 1 < n_chunks else None
            act = jnp.square(jnp.maximum(u, 0.0)).astype(BF16)
            d = jnp.dot(act, wout_ref[0, c * FF_CHUNK:(c + 1) * FF_CHUNK, :],
                        preferred_element_type=F32)
            y2 = d if y2 is None else y2 + d
            u = u_next
        o_ref[r * POST_ROW_BLOCK:(r + 1) * POST_ROW_BLOCK, :] = _layer_norm(
            ALPHA * x1 + y2, g2_ref[0], b2_ref[0])
        x1 = x1_next


def _post_call(x2, mix_a, mix_b, wp, l):
    ntok = x2.shape[0]
    rows = POST_ROWS

    def const(name):
        return _layer_spec(wp[name], l, pipeline_mode=pl.Buffered(1))

    def tok(width):
        return pl.BlockSpec((rows, width), lambda i: (i, 0))

    return pl.pallas_call(
        _post_kernel,
        grid=(ntok // rows,),
        in_specs=[tok(D_MODEL), tok(DIFF_WIDTH), tok(MLA_WIDTH), const("wo"), const("g1"),
                  const("b1"), const("wi"), const("wout"), const("g2"), const("b2")],
        out_specs=tok(D_MODEL),
        out_shape=jax.ShapeDtypeStruct((ntok, D_MODEL), F32),
        compiler_params=pltpu.CompilerParams(dimension_semantics=("arbitrary",),
                                             vmem_limit_bytes=VMEM_LIMIT),
        name="post",
    )(x2, mix_a, mix_b, wp["wo"], wp["g1"], wp["b1"], wp["wi"], wp["wout"], wp["g2"], wp["b2"])


def _t5_bucket(rel):
    nb = N_BUCKETS // 2
    ret = (rel > 0).astype(jnp.int32) * nb
    n = jnp.abs(rel)
    max_exact = nb // 2
    nf = jnp.maximum(n, 1).astype(F32)
    large = max_exact + (jnp.log(nf / max_exact) / math.log(MAX_DISTANCE / max_exact)
                         * (nb - max_exact)).astype(jnp.int32)
    large = jnp.minimum(large, nb - 1)
    return ret + jnp.where(n < max_exact, n, large)


def _bias_tiles(rel_bias):
    t = ATT_TILE
    p = 2 * t
    table = rel_bias.astype(F32).T
    far = table[:, N_BUCKETS // 2 - 1]
    w = jnp.arange(p, dtype=jnp.int32)[None, :]
    d = jnp.arange(2, dtype=jnp.int32)[:, None]
    rel = (t - 1) - w - d * t
    vec = (table[:, _t5_bucket(rel)] - far[:, None, None]) * LOG2E
    rows = jnp.tile(vec, (1, 1, t))[..., :t * (p - 1)].reshape(HEADS, 2, t, p - 1)
    bias = rows[..., t - 1:2 * t - 1]
    kk = jnp.arange(t, dtype=jnp.int32)[:, None]
    qq = jnp.arange(t, dtype=jnp.int32)[None, :]
    allowed = (kk // CHUNK <= qq // CHUNK)[None, None] | (d > 0)[None, :, :, None]
    bias = jnp.where(allowed, bias, NEG_BIG)
    return jnp.concatenate([bias, bias], axis=-1)


def _rope_tables(seq):
    pos = jnp.arange(seq, dtype=F32)
    inv = ROPE_THETA ** (-jnp.arange(0, MLA_ROPE_DIM, 2, dtype=F32) / MLA_ROPE_DIM)
    ang = pos[:, None] * inv[None, :]
    cos, sin = jnp.cos(ang), jnp.sin(ang)
    z = jnp.zeros((seq, LANES - MLA_ROPE_DIM), F32)
    return (jnp.concatenate([cos, cos, z], axis=-1), jnp.concatenate([-sin, sin, z], axis=-1))


def _swap_halves(w):
    half = w.shape[-1] // 2
    return jnp.concatenate([w[..., half:], w[..., :half]], axis=-1)


def _pack_params(w_in, lambda_q1, lambda_k1, lambda_q2, lambda_k2, subln_g, q_norm_g, w_uq,
                 kv_norm_g, w_ukv, w_o, ln1_g, ln1_b, w_mlp_in, w_mlp_out, ln2_g, ln2_b):
    depth = w_in.shape[0]
    o_k, o_v, o_cq = DIFF_WIDTH, 2 * DIFF_WIDTH, 3 * DIFF_WIDTH
    o_ckv = o_cq + MLA_Q_RANK
    o_kr = o_ckv + MLA_KV_RANK
    kr = w_in[..., o_kr:]
    w1 = jnp.concatenate([w_in[..., :o_k] * (DIFF_HEAD_DIM ** -0.5 * LOG2E), w_in[..., o_k:o_v],
                          w_in[..., o_cq:o_ckv], w_in[..., o_ckv:o_kr], kr, _swap_halves(kr)],
                         axis=-1)
    mla_scale = (MLA_NOPE_DIM + MLA_ROPE_DIM) ** -0.5 * LOG2E
    uq = (w_uq * mla_scale).reshape(depth, MLA_Q_RANK, HEADS, MLA_NOPE_DIM + MLA_ROPE_DIM)
    zq = jnp.zeros((depth, MLA_Q_RANK, HEADS, MLA_QK_PAD - MLA_NOPE_DIM - MLA_ROPE_DIM), F32)
    wuq = jnp.concatenate([uq, zq], axis=-1).reshape(depth, MLA_Q_RANK, HEADS * MLA_QK_PAD)
    wuqs = jnp.concatenate([_swap_halves(uq[..., MLA_NOPE_DIM:]), zq], axis=-1)
    wuqs = wuqs.reshape(depth, MLA_Q_RANK, HEADS * LANES)
    ukv = w_ukv.reshape(depth, MLA_KV_RANK, HEADS, MLA_NOPE_DIM + MLA_V_DIM)
    wukvk = ukv[..., :MLA_NOPE_DIM].reshape(depth, MLA_KV_RANK, HEADS * MLA_NOPE_DIM)
    wukvv = ukv[..., MLA_NOPE_DIM:].reshape(depth, MLA_KV_RANK, HEADS * MLA_V_DIM)
    row = lambda v: v.reshape(depth, 1, -1).astype(F32)
    return dict(
        w1=w1.astype(BF16), wvt=jnp.swapaxes(w_in[..., o_v:o_cq], 1, 2).astype(BF16),
        wuq=wuq.astype(BF16), wuqs=wuqs.astype(BF16), wukvk=wukvk.astype(BF16),
        wukvvt=jnp.swapaxes(wukvv, 1, 2).astype(BF16), gq=row(q_norm_g), gkv=row(kv_norm_g),
        subln=row(subln_g),
        lam=jnp.stack([lambda_q1, lambda_k1, lambda_q2, lambda_k2], axis=1).astype(F32),
        wo=w_o.astype(BF16), g1=row(ln1_g), b1=row(ln1_b), wi=w_mlp_in.astype(BF16),
        wout=w_mlp_out.astype(BF16), g2=row(ln2_g), b2=row(ln2_b))


def kernel(x, w_in, lambda_q1, lambda_k1, lambda_q2, lambda_k2, subln_g, q_norm_g, w_uq, kv_norm_g,
           w_ukv, rel_bias, w_o, ln1_g, ln1_b, w_mlp_in, w_mlp_out, ln2_g, ln2_b):
    batch, seq, _ = x.shape
    assert seq % PROJ_ROWS == 0 and seq % ATT_TILE == 0 and (batch * seq) % POST_ROWS == 0
    wp = _pack_params(w_in, lambda_q1, lambda_k1, lambda_q2, lambda_k2, subln_g, q_norm_g, w_uq,
                      kv_norm_g, w_ukv, w_o, ln1_g, ln1_b, w_mlp_in, w_mlp_out, ln2_g, ln2_b)
    wp["cos"], wp["sin"] = _rope_tables(seq)
    bias = _bias_tiles(rel_bias)
    x2 = x.reshape(batch * seq, D_MODEL)
    for l in range(DEPTH):
        lambda_init = 0.8 - 0.6 * math.exp(-0.3 * l)
        dq, dk, dvt, qm, km, vmt = _proj_call(x2, wp, l, seq)
        mix_a, mix_b = _attn_call(dq, dk, dvt, bias, wp, l, qm, km, vmt, seq, lambda_init)
        x2 = _post_call(x2, mix_a, mix_b, wp, l)
    return x2.reshape(batch, seq, D_MODEL)
```

```python
import functools
import math

import jax
import jax.numpy as jnp
from jax import lax
from jax.experimental import pallas as pl
from jax.experimental.pallas import tpu as pltpu

F32 = jnp.float32
BF16 = jnp.bfloat16

D_MODEL = 1024
DEPTH = 2
CHUNK = 64
HEADS = 4
DIFF_HEAD_DIM = 64
DIFF_V_DIM = 2 * DIFF_HEAD_DIM
DIFF_WIDTH = HEADS * DIFF_V_DIM
MLA_NOPE_DIM = 128
MLA_ROPE_DIM = 64
MLA_V_DIM = 128
MLA_Q_RANK = 256
MLA_KV_RANK = 128
MLA_WIDTH = HEADS * MLA_V_DIM
MLA_QK_PAD = 256
D_FF = 4 * D_MODEL
N_BUCKETS = 32
MAX_DISTANCE = 128
ROPE_THETA = 10000.0
ALPHA = (2 * DEPTH) ** 0.25
LN_EPS = 1e-5
RMS_EPS = 1e-6
NEG_BIG = -1e30
LOG2E = math.log2(math.e)

LANES = 128
BF16_SUBLANES = 16
ATT_TILE = 256
ATT_HEADS_PER_STEP = 2
PROJ_ROWS = 1024
POST_ROWS = 1024
POST_ROW_BLOCK = 256
FF_CHUNK = 1024
DIFF_LOOKAHEAD = 2
MLA_LOOKAHEAD = 4
VMEM_LIMIT = 56 * 1024 * 1024
assert ATT_TILE >= MAX_DISTANCE and ATT_TILE % CHUNK == 0

C_DQ = 0
C_DK = C_DQ + DIFF_WIDTH
C_CQ = C_DK + DIFF_WIDTH
C_CKV = C_CQ + MLA_Q_RANK
C_KR = C_CKV + MLA_KV_RANK
W1_COLS = C_KR + LANES

_NT = (((1,), (1,)), ((), ()))


def _rms(x, g):
    return x * lax.rsqrt(jnp.mean(x * x, axis=-1, keepdims=True) + RMS_EPS) * g


def _layer_norm(x, g, b):
    mu = jnp.mean(x, axis=-1, keepdims=True)
    xc = x - mu
    var = jnp.mean(xc * xc, axis=-1, keepdims=True)
    return xc * lax.rsqrt(var + LN_EPS) * g + b


def _layer_spec(arr, l, **kwargs):
    index = (l,) + (0,) * (arr.ndim - 1)
    return pl.BlockSpec((1,) + arr.shape[1:], lambda *_: index, **kwargs)


def _proj_kernel(x_ref, w1_ref, wvt_ref, wuq_ref, wuqs_ref, wukvk_ref, wukvvt_ref,
                 gq_ref, gkv_ref, cos_ref, sin_ref,
                 dq_ref, dk_ref, dvt_ref, qm_ref, km_ref, vmt_ref):
    xb = x_ref[...].astype(BF16)
    h = jnp.dot(xb, w1_ref[0], preferred_element_type=F32)
    dq_ref[...] = h[:, C_DQ:C_DQ + DIFF_WIDTH].astype(BF16)
    dk_ref[...] = h[:, C_DK:C_DK + DIFF_WIDTH].astype(BF16)
    dvt_ref[0] = lax.dot_general(wvt_ref[0], xb, _NT, preferred_element_type=F32).astype(BF16)

    cos = cos_ref[...]
    sin = sin_ref[...]
    c_q = _rms(h[:, C_CQ:C_CQ + MLA_Q_RANK], gq_ref[0]).astype(BF16)
    qf = jnp.dot(c_q, wuq_ref[0], preferred_element_type=F32)
    qsw = jnp.dot(c_q, wuqs_ref[0], preferred_element_type=F32)
    c_kv = _rms(h[:, C_CKV:C_CKV + MLA_KV_RANK], gkv_ref[0]).astype(BF16)
    kn = jnp.dot(c_kv, wukvk_ref[0], preferred_element_type=F32)
    vmt_ref[0] = lax.dot_general(wukvvt_ref[0], c_kv, _NT,
                                 preferred_element_type=F32).astype(BF16)
    kr = h[:, C_KR:C_KR + LANES]
    k_rope = (kr * cos + pltpu.roll(kr, LANES // 2, 1) * sin).astype(BF16)

    for hh in range(HEADS):
        b0 = hh * MLA_QK_PAD
        qm_ref[:, b0:b0 + LANES] = qf[:, b0:b0 + LANES].astype(BF16)
        qm_ref[:, b0 + LANES:b0 + 2 * LANES] = (
            qf[:, b0 + LANES:b0 + 2 * LANES] * cos + qsw[:, hh * LANES:(hh + 1) * LANES] * sin
        ).astype(BF16)
        km_ref[:, b0:b0 + LANES] = kn[:, hh * LANES:(hh + 1) * LANES].astype(BF16)
        km_ref[:, b0 + LANES:b0 + 2 * LANES] = k_rope


def _proj_call(x2, wp, l, seq):
    ntok = x2.shape[0]
    batch = ntok // seq
    rows = PROJ_ROWS
    steps_per_seq = seq // rows

    def const(name):
        return _layer_spec(wp[name], l)

    def tok(width):
        return pl.BlockSpec((rows, width), lambda i: (i, 0))

    vt_spec = pl.BlockSpec((1, HEADS * LANES, rows),
                           lambda i: (i // steps_per_seq, 0, i % steps_per_seq))
    pos_spec = pl.BlockSpec((rows, LANES), lambda i: (i % steps_per_seq, 0))
    vt_shape = jax.ShapeDtypeStruct((batch, HEADS * LANES, seq), BF16)
    return pl.pallas_call(
        _proj_kernel,
        grid=(ntok // rows,),
        in_specs=[tok(D_MODEL), const("w1"), const("wvt"), const("wuq"), const("wuqs"),
                  const("wukvk"), const("wukvvt"), const("gq"), const("gkv"), pos_spec, pos_spec],
        out_specs=[tok(DIFF_WIDTH), tok(DIFF_WIDTH), vt_spec,
                   tok(HEADS * MLA_QK_PAD), tok(HEADS * MLA_QK_PAD), vt_spec],
        out_shape=[jax.ShapeDtypeStruct((ntok, DIFF_WIDTH), BF16),
                   jax.ShapeDtypeStruct((ntok, DIFF_WIDTH), BF16),
                   vt_shape,
                   jax.ShapeDtypeStruct((ntok, HEADS * MLA_QK_PAD), BF16),
                   jax.ShapeDtypeStruct((ntok, HEADS * MLA_QK_PAD), BF16),
                   vt_shape],
        compiler_params=pltpu.CompilerParams(dimension_semantics=("arbitrary",),
                                             vmem_limit_bytes=VMEM_LIMIT),
        name="proj",
    )(x2, wp["w1"], wp["wvt"], wp["wuq"], wp["wuqs"], wp["wukvk"], wp["wukvvt"],
      wp["gq"], wp["gkv"], wp["cos"], wp["sin"])


def _col_max(a, b):
    m = jnp.max(b, axis=0, keepdims=True)
    return m if a is None else jnp.maximum(a, m)


def _attend(heads, n_tiles, q_fn, score_fn, finish_fn, vt_fn, lookahead):
    ones = jnp.ones((BF16_SUBLANES, ATT_TILE), BF16)
    items = [(h, qi, c) for h in heads for qi in range(n_tiles) for c in range(qi + 1)]
    qs, scores = {}, {}

    def issue(idx):
        h, qi, c = items[idx]
        if c == 0:
            qs[h, qi] = q_fn(h, qi)
        scores[idx] = score_fn(h, qi, qs[h, qi], c)

    for idx in range(min(lookahead, len(items))):
        issue(idx)
    m, acc = None, None
    for idx, (h, qi, c) in enumerate(items):
        if idx + lookahead < len(items):
            issue(idx + lookahead)
        s = scores.pop(idx)
        m_new = _col_max(m, s)
        e = jnp.exp2(s - m_new).astype(BF16)
        vt = jnp.concatenate([vt_fn(h, c), ones], axis=0)
        pv = jnp.dot(vt, e, preferred_element_type=F32)
        acc = pv if acc is None else acc * jnp.exp2(m - m_new) + pv
        m = m_new
        if c == qi:
            finish_fn(h, qi, acc[:LANES] / acc[LANES:LANES + 1])
            m, acc = None, None
        yield


def _diff_steps(heads, q_ref, k_ref, vt_ref, bias_ref, lam_ref, g_ref, o_ref, lambda_init):
    t = ATT_TILE
    lane = lax.broadcasted_iota(jnp.int32, (t, LANES), 1)
    lp = lam_ref[0]
    lam = (jnp.exp(jnp.sum(lp[0:1] * lp[1:2], axis=-1, keepdims=True))
           - jnp.exp(jnp.sum(lp[2:3] * lp[3:4], axis=-1, keepdims=True)) + lambda_init)
    g = g_ref[0]

    def q_rows(h, qi):
        q = q_ref[qi * t:(qi + 1) * t, h * LANES:(h + 1) * LANES]
        zero = jnp.zeros_like(q)
        return jnp.concatenate([jnp.where(lane < DIFF_HEAD_DIM, q, zero),
                                jnp.where(lane >= DIFF_HEAD_DIM, q, zero)], axis=0)

    def score(h, qi, qs, c):
        k = k_ref[c * t:(c + 1) * t, h * LANES:(h + 1) * LANES]
        s = lax.dot_general(k, qs, _NT, preferred_element_type=F32)
        return s + bias_ref[h, qi - c] if qi - c <= 1 else s

    def values(h, c):
        return vt_ref[0, h * LANES:(h + 1) * LANES, c * t:(c + 1) * t]

    def finish(h, qi, o):
        ot = o[:, :t] - lam * o[:, t:]
        y = _rms(ot.T, g) * (1.0 - lambda_init)
        o_ref[qi * t:(qi + 1) * t, h * LANES:(h + 1) * LANES] = y.astype(BF16)

    return _attend(heads, q_ref.shape[0] // t, q_rows, score, finish, values, DIFF_LOOKAHEAD)


def _mla_steps(heads, q_ref, k_ref, vt_ref, o_ref):
    t = ATT_TILE
    w = MLA_QK_PAD
    kc = lax.broadcasted_iota(jnp.int32, (t, t), 0) // CHUNK
    qc = lax.broadcasted_iota(jnp.int32, (t, t), 1) // CHUNK
    allowed = kc <= qc

    def q_rows(h, qi):
        return q_ref[qi * t:(qi + 1) * t, h * w:(h + 1) * w]

    def score(h, qi, q, c):
        k = k_ref[c * t:(c + 1) * t, h * w:(h + 1) * w]
        s = lax.dot_general(k, q, _NT, preferred_element_type=F32)
        return jnp.where(allowed, s, NEG_BIG) if c == qi else s

    def values(h, c):
        return vt_ref[0, h * LANES:(h + 1) * LANES, c * t:(c + 1) * t]

    def finish(h, qi, o):
        o_ref[qi * t:(qi + 1) * t, h * LANES:(h + 1) * LANES] = o.T.astype(BF16)

    return _attend(heads, q_ref.shape[0] // t, q_rows, score, finish, values, MLA_LOOKAHEAD)


def _attn_kernel(dq_ref, dk_ref, dvt_ref, bias_ref, lam_ref, g_ref, qm_ref, km_ref, vmt_ref,
                 oa_ref, ob_ref, *, lambda_init):
    heads = range(ATT_HEADS_PER_STEP)
    streams = [_mla_steps(heads, qm_ref, km_ref, vmt_ref, ob_ref),
               _diff_steps(heads, dq_ref, dk_ref, dvt_ref, bias_ref, lam_ref, g_ref, oa_ref,
                           lambda_init)]
    while streams:
        for s in list(streams):
            if next(s, "done") == "done":
                streams.remove(s)


def _attn_call(dq, dk, dvt, bias, wp, l, qm, km, vmt, seq, lambda_init):
    ntok = dq.shape[0]
    n = ATT_HEADS_PER_STEP

    def rows(width):
        return pl.BlockSpec((seq, n * width), lambda b, h: (b, h))

    vt_spec = pl.BlockSpec((1, n * LANES, seq), lambda b, h: (b, h, 0))
    return pl.pallas_call(
        functools.partial(_attn_kernel, lambda_init=lambda_init),
        grid=(ntok // seq, HEADS // n),
        in_specs=[rows(LANES), rows(LANES), vt_spec,
                  pl.BlockSpec((n,) + bias.shape[1:], lambda b, h: (h, 0, 0, 0)),
                  _layer_spec(wp["lam"], l), _layer_spec(wp["subln"], l),
                  rows(MLA_QK_PAD), rows(MLA_QK_PAD), vt_spec],
        out_specs=[rows(LANES), rows(LANES)],
        out_shape=[jax.ShapeDtypeStruct((ntok, DIFF_WIDTH), BF16),
                   jax.ShapeDtypeStruct((ntok, MLA_WIDTH), BF16)],
        compiler_params=pltpu.CompilerParams(dimension_semantics=("arbitrary", "arbitrary"),
                                             vmem_limit_bytes=VMEM_LIMIT),
        name="attn",
    )(dq, dk, dvt, bias, wp["lam"], wp["subln"], qm, km, vmt)


def _post_kernel(x_ref, ma_ref, mb_ref, wo_ref, g1_ref, b1_ref, wi_ref, wout_ref, g2_ref, b2_ref,
                 o_ref):
    n_blocks = x_ref.shape[0] // POST_ROW_BLOCK
    n_chunks = D_FF // FF_CHUNK

    def attn_out(r):
        rs = slice(r * POST_ROW_BLOCK, (r + 1) * POST_ROW_BLOCK)
        y = (jnp.dot(ma_ref[rs, :], wo_ref[0, :DIFF_WIDTH, :], preferred_element_type=F32)
             + jnp.dot(mb_ref[rs, :], wo_ref[0, DIFF_WIDTH:, :], preferred_element_type=F32))
        return _layer_norm(ALPHA * x_ref[rs, :] + y, g1_ref[0], b1_ref[0])

    def up(x1b, c):
        return jnp.dot(x1b, wi_ref[0, :, c * FF_CHUNK:(c + 1) * FF_CHUNK],
                       preferred_element_type=F32)

    x1 = attn_out(0)
    for r in range(n_blocks):
        x1_next = attn_out(r + 1) if r + 1 < n_blocks else None
        x1b = x1.astype(BF16)
        u = up(x1b, 0)
        y2 = None
        for c in range(n_chunks):
            u_next = up(x1b, c + 1) if c + 1 < n_chunks else None
            act = jnp.square(jnp.maximum(u, 0.0)).astype(BF16)
            d = jnp.dot(act, wout_ref[0, c * FF_CHUNK:(c + 1) * FF_CHUNK, :],
                        preferred_element_type=F32)
            y2 = d if y2 is None else y2 + d
            u = u_next
        o_ref[r * POST_ROW_BLOCK:(r + 1) * POST_ROW_BLOCK, :] = _layer_norm(
            ALPHA * x1 + y2, g2_ref[0], b2_ref[0])
        x1 = x1_next


def _post_call(x2, mix_a, mix_b, wp, l):
    ntok = x2.shape[0]
    rows = POST_ROWS

    def const(name):
        return _layer_spec(wp[name], l, pipeline_mode=pl.Buffered(1))

    def tok(width):
        return pl.BlockSpec((rows, width), lambda i: (i, 0))

    return pl.pallas_call(
        _post_kernel,
        grid=(ntok // rows,),
        in_specs=[tok(D_MODEL), tok(DIFF_WIDTH), tok(MLA_WIDTH), const("wo"), const("g1"),
                  const("b1"), const("wi"), const("wout"), const("g2"), const("b2")],
        out_specs=tok(D_MODEL),
        out_shape=jax.ShapeDtypeStruct((ntok, D_MODEL), F32),
        compiler_params=pltpu.CompilerParams(dimension_semantics=("arbitrary",),
                                             vmem_limit_bytes=VMEM_LIMIT),
        name="post",
    )(x2, mix_a, mix_b, wp["wo"], wp["g1"], wp["b1"], wp["wi"], wp["wout"], wp["g2"], wp["b2"])


def _t5_bucket(rel):
    nb = N_BUCKETS // 2
    ret = (rel > 0).astype(jnp.int32) * nb
    n = jnp.abs(rel)
    max_exact = nb // 2
    nf = jnp.maximum(n, 1).astype(F32)
    large = max_exact + (jnp.log(nf / max_exact) / math.log(MAX_DISTANCE / max_exact)
                         * (nb - max_exact)).astype(jnp.int32)
    large = jnp.minimum(large, nb - 1)
    return ret + jnp.where(n < max_exact, n, large)


def _bias_tiles(rel_bias):
    t = ATT_TILE
    p = 2 * t
    table = rel_bias.astype(F32).T
    far = table[:, N_BUCKETS // 2 - 1]
    w = jnp.arange(p, dtype=jnp.int32)[None, :]
    d = jnp.arange(2, dtype=jnp.int32)[:, None]
    rel = (t - 1) - w - d * t
    vec = (table[:, _t5_bucket(rel)] - far[:, None, None]) * LOG2E
    rows = jnp.tile(vec, (1, 1, t))[..., :t * (p - 1)].reshape(HEADS, 2, t, p - 1)
    bias = rows[..., t - 1:2 * t - 1]
    kk = jnp.arange(t, dtype=jnp.int32)[:, None]
    qq = jnp.arange(t, dtype=jnp.int32)[None, :]
    allowed = (kk // CHUNK <= qq // CHUNK)[None, None] | (d > 0)[None, :, :, None]
    bias = jnp.where(allowed, bias, NEG_BIG)
    return jnp.concatenate([bias, bias], axis=-1)


def _rope_tables(seq):
    pos = jnp.arange(seq, dtype=F32)
    inv = ROPE_THETA ** (-jnp.arange(0, MLA_ROPE_DIM, 2, dtype=F32) / MLA_ROPE_DIM)
    ang = pos[:, None] * inv[None, :]
    cos, sin = jnp.cos(ang), jnp.sin(ang)
    z = jnp.zeros((seq, LANES - MLA_ROPE_DIM), F32)
    return (jnp.concatenate([cos, cos, z], axis=-1), jnp.concatenate([-sin, sin, z], axis=-1))


def _swap_halves(w):
    half = w.shape[-1] // 2
    return jnp.concatenate([w[..., half:], w[..., :half]], axis=-1)


def _pack_params(w_in, lambda_q1, lambda_k1, lambda_q2, lambda_k2, subln_g, q_norm_g, w_uq,
                 kv_norm_g, w_ukv, w_o, ln1_g, ln1_b, w_mlp_in, w_mlp_out, ln2_g, ln2_b):
    depth = w_in.shape[0]
    o_k, o_v, o_cq = DIFF_WIDTH, 2 * DIFF_WIDTH, 3 * DIFF_WIDTH
    o_ckv = o_cq + MLA_Q_RANK
    o_kr = o_ckv + MLA_KV_RANK
    kr = w_in[..., o_kr:]
    w1 = jnp.concatenate([w_in[..., :o_k] * (DIFF_HEAD_DIM ** -0.5 * LOG2E), w_in[..., o_k:o_v],
                          w_in[..., o_cq:o_ckv], w_in[..., o_ckv:o_kr], kr, _swap_halves(kr)],
                         axis=-1)
    mla_scale = (MLA_NOPE_DIM + MLA_ROPE_DIM) ** -0.5 * LOG2E
    uq = (w_uq * mla_scale).reshape(depth, MLA_Q_RANK, HEADS, MLA_NOPE_DIM + MLA_ROPE_DIM)
    zq = jnp.zeros((depth, MLA_Q_RANK, HEADS, MLA_QK_PAD - MLA_NOPE_DIM - MLA_ROPE_DIM), F32)
    wuq = jnp.concatenate([uq, zq], axis=-1).reshape(depth, MLA_Q_RANK, HEADS * MLA_QK_PAD)
    wuqs = jnp.concatenate([_swap_halves(uq[..., MLA_NOPE_DIM:]), zq], axis=-1)
    wuqs = wuqs.reshape(depth, MLA_Q_RANK, HEADS * LANES)
    ukv = w_ukv.reshape(depth, MLA_KV_RANK, HEADS, MLA_NOPE_DIM + MLA_V_DIM)
    wukvk = ukv[..., :MLA_NOPE_DIM].reshape(depth, MLA_KV_RANK, HEADS * MLA_NOPE_DIM)
    wukvv = ukv[..., MLA_NOPE_DIM:].reshape(depth, MLA_KV_RANK, HEADS * MLA_V_DIM)
    row = lambda v: v.reshape(depth, 1, -1).astype(F32)
    return dict(
        w1=w1.astype(BF16), wvt=jnp.swapaxes(w_in[..., o_v:o_cq], 1, 2).astype(BF16),
        wuq=wuq.astype(BF16), wuqs=wuqs.astype(BF16), wukvk=wukvk.astype(BF16),
        wukvvt=jnp.swapaxes(wukvv, 1, 2).astype(BF16), gq=row(q_norm_g), gkv=row(kv_norm_g),
        subln=row(subln_g),
        lam=jnp.stack([lambda_q1, lambda_k1, lambda_q2, lambda_k2], axis=1).astype(F32),
        wo=w_o.astype(BF16), g1=row(ln1_g), b1=row(ln1_b), wi=w_mlp_in.astype(BF16),
        wout=w_mlp_out.astype(BF16), g2=row(ln2_g), b2=row(ln2_b))


def kernel(x, w_in, lambda_q1, lambda_k1, lambda_q2, lambda_k2, subln_g, q_norm_g, w_uq, kv_norm_g,
           w_ukv, rel_bias, w_o, ln1_g, ln1_b, w_mlp_in, w_mlp_out, ln2_g, ln2_b):
    batch, seq, _ = x.shape
    assert seq % PROJ_ROWS == 0 and seq % ATT_TILE == 0 and (batch * seq) % POST_ROWS == 0
    wp = _pack_params(w_in, lambda_q1, lambda_k1, lambda_q2, lambda_k2, subln_g, q_norm_g, w_uq,
                      kv_norm_g, w_ukv, w_o, ln1_g, ln1_b, w_mlp_in, w_mlp_out, ln2_g, ln2_b)
    wp["cos"], wp["sin"] = _rope_tables(seq)
    bias = _bias_tiles(rel_bias)
    x2 = x.reshape(batch * seq, D_MODEL)
    for l in range(DEPTH):
        lambda_init = 0.8 - 0.6 * math.exp(-0.3 * l)
        dq, dk, dvt, qm, km, vmt = _proj_call(x2, wp, l, seq)
        mix_a, mix_b = _attn_call(dq, dk, dvt, bias, wp, l, qm, km, vmt, seq, lambda_init)
        x2 = _post_call(x2, mix_a, mix_b, wp, l)
    return x2.reshape(batch, seq, D_MODEL)
```

```python
import functools
import math

import jax
import jax.numpy as jnp
from jax import lax
from jax.experimental import pallas as pl
from jax.experimental.pallas import tpu as pltpu

F32 = jnp.float32
BF16 = jnp.bfloat16

D_MODEL = 1024
DEPTH = 2
CHUNK = 64
HEADS = 4
DIFF_HEAD_DIM = 64
DIFF_V_DIM = 2 * DIFF_HEAD_DIM
DIFF_WIDTH = HEADS * DIFF_V_DIM
MLA_NOPE_DIM = 128
MLA_ROPE_DIM = 64
MLA_V_DIM = 128
MLA_Q_RANK = 256
MLA_KV_RANK = 128
MLA_WIDTH = HEADS * MLA_V_DIM
MLA_QK_PAD = 256
D_FF = 4 * D_MODEL
N_BUCKETS = 32
MAX_DISTANCE = 128
ROPE_THETA = 10000.0
ALPHA = (2 * DEPTH) ** 0.25
LN_EPS = 1e-5
RMS_EPS = 1e-6
NEG_BIG = -1e30
LOG2E = math.log2(math.e)

LANES = 128
BF16_SUBLANES = 16
ATT_TILE = 256
ATT_HEADS_PER_STEP = 2
PROJ_ROWS = 1024
POST_ROWS = 1024
POST_ROW_BLOCK = 256
FF_CHUNK = 2048
DIFF_LOOKAHEAD = 2
MLA_LOOKAHEAD = 4
VMEM_LIMIT = 56 * 1024 * 1024
assert ATT_TILE >= MAX_DISTANCE and ATT_TILE % CHUNK == 0

C_DQ = 0
C_DK = C_DQ + DIFF_WIDTH
C_CQ = C_DK + DIFF_WIDTH
C_CKV = C_CQ + MLA_Q_RANK
C_KR = C_CKV + MLA_KV_RANK
W1_COLS = C_KR + LANES

_NT = (((1,), (1,)), ((), ()))


def _rms(x, g):
    return x * lax.rsqrt(jnp.mean(x * x, axis=-1, keepdims=True) + RMS_EPS) * g


def _layer_norm(x, g, b):
    mu = jnp.mean(x, axis=-1, keepdims=True)
    xc = x - mu
    var = jnp.mean(xc * xc, axis=-1, keepdims=True)
    return xc * lax.rsqrt(var + LN_EPS) * g + b


def _layer_spec(arr, l, **kwargs):
    index = (l,) + (0,) * (arr.ndim - 1)
    return pl.BlockSpec((1,) + arr.shape[1:], lambda *_: index, **kwargs)


def _proj_kernel(x_ref, w1_ref, wvt_ref, wuq_ref, wuqs_ref, wukvk_ref, wukvvt_ref,
                 gq_ref, gkv_ref, cos_ref, sin_ref,
                 dq_ref, dk_ref, dvt_ref, qm_ref, km_ref, vmt_ref):
    xb = x_ref[...].astype(BF16)
    h = jnp.dot(xb, w1_ref[0], preferred_element_type=F32)
    dq_ref[...] = h[:, C_DQ:C_DQ + DIFF_WIDTH].astype(BF16)
    dk_ref[...] = h[:, C_DK:C_DK + DIFF_WIDTH].astype(BF16)
    dvt_ref[0] = lax.dot_general(wvt_ref[0], xb, _NT, preferred_element_type=F32).astype(BF16)

    cos = cos_ref[...]
    sin = sin_ref[...]
    c_q = _rms(h[:, C_CQ:C_CQ + MLA_Q_RANK], gq_ref[0]).astype(BF16)
    qf = jnp.dot(c_q, wuq_ref[0], preferred_element_type=F32)
    qsw = jnp.dot(c_q, wuqs_ref[0], preferred_element_type=F32)
    c_kv = _rms(h[:, C_CKV:C_CKV + MLA_KV_RANK], gkv_ref[0]).astype(BF16)
    kn = jnp.dot(c_kv, wukvk_ref[0], preferred_element_type=F32)
    vmt_ref[0] = lax.dot_general(wukvvt_ref[0], c_kv, _NT,
                                 preferred_element_type=F32).astype(BF16)
    kr = h[:, C_KR:C_KR + LANES]
    k_rope = (kr * cos + pltpu.roll(kr, LANES // 2, 1) * sin).astype(BF16)

    for hh in range(HEADS):
        b0 = hh * MLA_QK_PAD
        qm_ref[:, b0:b0 + LANES] = qf[:, b0:b0 + LANES].astype(BF16)
        qm_ref[:, b0 + LANES:b0 + 2 * LANES] = (
            qf[:, b0 + LANES:b0 + 2 * LANES] * cos + qsw[:, hh * LANES:(hh + 1) * LANES] * sin
        ).astype(BF16)
        km_ref[:, b0:b0 + LANES] = kn[:, hh * LANES:(hh + 1) * LANES].astype(BF16)
        km_ref[:, b0 + LANES:b0 + 2 * LANES] = k_rope


def _proj_call(x2, wp, l, seq):
    ntok = x2.shape[0]
    batch = ntok // seq
    rows = PROJ_ROWS
    steps_per_seq = seq // rows

    def const(name):
        return _layer_spec(wp[name], l)

    def tok(width):
        return pl.BlockSpec((rows, width), lambda i: (i, 0))

    vt_spec = pl.BlockSpec((1, HEADS * LANES, rows),
                           lambda i: (i // steps_per_seq, 0, i % steps_per_seq))
    pos_spec = pl.BlockSpec((rows, LANES), lambda i: (i % steps_per_seq, 0))
    vt_shape = jax.ShapeDtypeStruct((batch, HEADS * LANES, seq), BF16)
    return pl.pallas_call(
        _proj_kernel,
        grid=(ntok // rows,),
        in_specs=[tok(D_MODEL), const("w1"), const("wvt"), const("wuq"), const("wuqs"),
                  const("wukvk"), const("wukvvt"), const("gq"), const("gkv"), pos_spec, pos_spec],
        out_specs=[tok(DIFF_WIDTH), tok(DIFF_WIDTH), vt_spec,
                   tok(HEADS * MLA_QK_PAD), tok(HEADS * MLA_QK_PAD), vt_spec],
        out_shape=[jax.ShapeDtypeStruct((ntok, DIFF_WIDTH), BF16),
                   jax.ShapeDtypeStruct((ntok, DIFF_WIDTH), BF16),
                   vt_shape,
                   jax.ShapeDtypeStruct((ntok, HEADS * MLA_QK_PAD), BF16),
                   jax.ShapeDtypeStruct((ntok, HEADS * MLA_QK_PAD), BF16),
                   vt_shape],
        compiler_params=pltpu.CompilerParams(dimension_semantics=("arbitrary",),
                                             vmem_limit_bytes=VMEM_LIMIT),
        name="proj",
    )(x2, wp["w1"], wp["wvt"], wp["wuq"], wp["wuqs"], wp["wukvk"], wp["wukvvt"],
      wp["gq"], wp["gkv"], wp["cos"], wp["sin"])


def _col_max(a, b):
    m = jnp.max(b, axis=0, keepdims=True)
    return m if a is None else jnp.maximum(a, m)


def _attend(heads, n_tiles, q_fn, score_fn, finish_fn, vt_fn, lookahead):
    ones = jnp.ones((BF16_SUBLANES, ATT_TILE), BF16)
    items = [(h, qi, c) for h in heads for qi in range(n_tiles) for c in range(qi + 1)]
    qs, scores = {}, {}

    def issue(idx):
        h, qi, c = items[idx]
        if c == 0:
            qs[h, qi] = q_fn(h, qi)
        scores[idx] = score_fn(h, qi, qs[h, qi], c)

    for idx in range(min(lookahead, len(items))):
        issue(idx)
    m, acc = None, None
    for idx, (h, qi, c) in enumerate(items):
        if idx + lookahead < len(items):
            issue(idx + lookahead)
        s = scores.pop(idx)
        m_new = _col_max(m, s)
        e = jnp.exp2(s - m_new).astype(BF16)
        vt = jnp.concatenate([vt_fn(h, c), ones], axis=0)
        pv = jnp.dot(vt, e, preferred_element_type=F32)
        acc = pv if acc is None else acc * jnp.exp2(m - m_new) + pv
        m = m_new
        if c == qi:
            finish_fn(h, qi, acc[:LANES] / acc[LANES:LANES + 1])
            m, acc = None, None
        yield


def _diff_steps(heads, q_ref, k_ref, vt_ref, bias_ref, lam_ref, g_ref, o_ref, lambda_init):
    t = ATT_TILE
    lane = lax.broadcasted_iota(jnp.int32, (t, LANES), 1)
    lp = lam_ref[0]
    lam = (jnp.exp(jnp.sum(lp[0:1] * lp[1:2], axis=-1, keepdims=True))
           - jnp.exp(jnp.sum(lp[2:3] * lp[3:4], axis=-1, keepdims=True)) + lambda_init)
    g = g_ref[0]

    def q_rows(h, qi):
        q = q_ref[qi * t:(qi + 1) * t, h * LANES:(h + 1) * LANES]
        zero = jnp.zeros_like(q)
        return jnp.concatenate([jnp.where(lane < DIFF_HEAD_DIM, q, zero),
                                jnp.where(lane >= DIFF_HEAD_DIM, q, zero)], axis=0)

    def score(h, qi, qs, c):
        k = k_ref[c * t:(c + 1) * t, h * LANES:(h + 1) * LANES]
        s = lax.dot_general(k, qs, _NT, preferred_element_type=F32)
        return s + bias_ref[h, qi - c] if qi - c <= 1 else s

    def values(h, c):
        return vt_ref[0, h * LANES:(h + 1) * LANES, c * t:(c + 1) * t]

    def finish(h, qi, o):
        ot = o[:, :t] - lam * o[:, t:]
        y = _rms(ot.T, g) * (1.0 - lambda_init)
        o_ref[qi * t:(qi + 1) * t, h * LANES:(h + 1) * LANES] = y.astype(BF16)

    return _attend(heads, q_ref.shape[0] // t, q_rows, score, finish, values, DIFF_LOOKAHEAD)


def _mla_steps(heads, q_ref, k_ref, vt_ref, o_ref):
    t = ATT_TILE
    w = MLA_QK_PAD
    kc = lax.broadcasted_iota(jnp.int32, (t, t), 0) // CHUNK
    qc = lax.broadcasted_iota(jnp.int32, (t, t), 1) // CHUNK
    allowed = kc <= qc

    def q_rows(h, qi):
        return q_ref[qi * t:(qi + 1) * t, h * w:(h + 1) * w]

    def score(h, qi, q, c):
        k = k_ref[c * t:(c + 1) * t, h * w:(h + 1) * w]
        s = lax.dot_general(k, q, _NT, preferred_element_type=F32)
        return jnp.where(allowed, s, NEG_BIG) if c == qi else s

    def values(h, c):
        return vt_ref[0, h * LANES:(h + 1) * LANES, c * t:(c + 1) * t]

    def finish(h, qi, o):
        o_ref[qi * t:(qi + 1) * t, h * LANES:(h + 1) * LANES] = o.T.astype(BF16)

    return _attend(heads, q_ref.shape[0] // t, q_rows, score, finish, values, MLA_LOOKAHEAD)


def _attn_kernel(dq_ref, dk_ref, dvt_ref, bias_ref, lam_ref, g_ref, qm_ref, km_ref, vmt_ref,
                 oa_ref, ob_ref, *, lambda_init):
    heads = range(ATT_HEADS_PER_STEP)
    streams = [_diff_steps(heads, dq_ref, dk_ref, dvt_ref, bias_ref, lam_ref, g_ref, oa_ref,
                           lambda_init),
               _mla_steps(heads, qm_ref, km_ref, vmt_ref, ob_ref)]
    while streams:
        for s in list(streams):
            if next(s, "done") == "done":
                streams.remove(s)


def _attn_call(dq, dk, dvt, bias, wp, l, qm, km, vmt, seq, lambda_init):
    ntok = dq.shape[0]
    n = ATT_HEADS_PER_STEP

    def rows(width):
        return pl.BlockSpec((seq, n * width), lambda b, h: (b, h))

    vt_spec = pl.BlockSpec((1, n * LANES, seq), lambda b, h: (b, h, 0))
    return pl.pallas_call(
        functools.partial(_attn_kernel, lambda_init=lambda_init),
        grid=(ntok // seq, HEADS // n),
        in_specs=[rows(LANES), rows(LANES), vt_spec,
                  pl.BlockSpec((n,) + bias.shape[1:], lambda b, h: (h, 0, 0, 0)),
                  _layer_spec(wp["lam"], l), _layer_spec(wp["subln"], l),
                  rows(MLA_QK_PAD), rows(MLA_QK_PAD), vt_spec],
        out_specs=[rows(LANES), rows(LANES)],
        out_shape=[jax.ShapeDtypeStruct((ntok, DIFF_WIDTH), BF16),
                   jax.ShapeDtypeStruct((ntok, MLA_WIDTH), BF16)],
        compiler_params=pltpu.CompilerParams(dimension_semantics=("arbitrary", "arbitrary"),
                                             vmem_limit_bytes=VMEM_LIMIT),
        name="attn",
    )(dq, dk, dvt, bias, wp["lam"], wp["subln"], qm, km, vmt)


def _post_kernel(x_ref, ma_ref, mb_ref, wo_ref, g1_ref, b1_ref, wi_ref, wout_ref, g2_ref, b2_ref,
                 o_ref):
    n_blocks = x_ref.shape[0] // POST_ROW_BLOCK
    n_chunks = D_FF // FF_CHUNK

    def attn_out(r):
        rs = slice(r * POST_ROW_BLOCK, (r + 1) * POST_ROW_BLOCK)
        y = (jnp.dot(ma_ref[rs, :], wo_ref[0, :DIFF_WIDTH, :], preferred_element_type=F32)
             + jnp.dot(mb_ref[rs, :], wo_ref[0, DIFF_WIDTH:, :], preferred_element_type=F32))
        return _layer_norm(ALPHA * x_ref[rs, :] + y, g1_ref[0], b1_ref[0])

    def up(x1b, c):
        return jnp.dot(x1b, wi_ref[0, :, c * FF_CHUNK:(c + 1) * FF_CHUNK],
                       preferred_element_type=F32)

    x1 = attn_out(0)
    for r in range(n_blocks):
        x1_next = attn_out(r + 1) if r + 1 < n_blocks else None
        x1b = x1.astype(BF16)
        u = up(x1b, 0)
        y2 = None
        for c in range(n_chunks):
            u_next = up(x1b, c + 1) if c + 1 < n_chunks else None
            act = jnp.square(jnp.maximum(u, 0.0)).astype(BF16)
            d = jnp.dot(act, wout_ref[0, c * FF_CHUNK:(c + 1) * FF_CHUNK, :],
                        preferred_element_type=F32)
            y2 = d if y2 is None else y2 + d
            u = u_next
        o_ref[r * POST_ROW_BLOCK:(r + 1) * POST_ROW_BLOCK, :] = _layer_norm(
            ALPHA * x1 + y2, g2_ref[0], b2_ref[0])
        x1 = x1_next


def _post_call(x2, mix_a, mix_b, wp, l):
    ntok = x2.shape[0]
    rows = POST_ROWS

    def const(name):
        return _layer_spec(wp[name], l, pipeline_mode=pl.Buffered(1))

    def tok(width):
        return pl.BlockSpec((rows, width), lambda i: (i, 0))

    return pl.pallas_call(
        _post_kernel,
        grid=(ntok // rows,),
        in_specs=[tok(D_MODEL), tok(DIFF_WIDTH), tok(MLA_WIDTH), const("wo"), const("g1"),
                  const("b1"), const("wi"), const("wout"), const("g2"), const("b2")],
        out_specs=tok(D_MODEL),
        out_shape=jax.ShapeDtypeStruct((ntok, D_MODEL), F32),
        compiler_params=pltpu.CompilerParams(dimension_semantics=("arbitrary",),
                                             vmem_limit_bytes=VMEM_LIMIT),
        name="post",
    )(x2, mix_a, mix_b, wp["wo"], wp["g1"], wp["b1"], wp["wi"], wp["wout"], wp["g2"], wp["b2"])


def _t5_bucket(rel):
    nb = N_BUCKETS // 2
    ret = (rel > 0).astype(jnp.int32) * nb
    n = jnp.abs(rel)
    max_exact = nb // 2
    nf = jnp.maximum(n, 1).astype(F32)
    large = max_exact + (jnp.log(nf / max_exact) / math.log(MAX_DISTANCE / max_exact)
                         * (nb - max_exact)).astype(jnp.int32)
    large = jnp.minimum(large, nb - 1)
    return ret + jnp.where(n < max_exact, n, large)


def _bias_tiles(rel_bias):
    t = ATT_TILE
    p = 2 * t
    table = rel_bias.astype(F32).T
    far = table[:, N_BUCKETS // 2 - 1]
    w = jnp.arange(p, dtype=jnp.int32)[None, :]
    d = jnp.arange(2, dtype=jnp.int32)[:, None]
    rel = (t - 1) - w - d * t
    vec = (table[:, _t5_bucket(rel)] - far[:, None, None]) * LOG2E
    rows = jnp.tile(vec, (1, 1, t))[..., :t * (p - 1)].reshape(HEADS, 2, t, p - 1)
    bias = rows[..., t - 1:2 * t - 1]
    kk = jnp.arange(t, dtype=jnp.int32)[:, None]
    qq = jnp.arange(t, dtype=jnp.int32)[None, :]
    allowed = (kk // CHUNK <= qq // CHUNK)[None, None] | (d > 0)[None, :, :, None]
    bias = jnp.where(allowed, bias, NEG_BIG)
    return jnp.concatenate([bias, bias], axis=-1)


def _rope_tables(seq):
    pos = jnp.arange(seq, dtype=F32)
    inv = ROPE_THETA ** (-jnp.arange(0, MLA_ROPE_DIM, 2, dtype=F32) / MLA_ROPE_DIM)
    ang = pos[:, None] * inv[None, :]
    cos, sin = jnp.cos(ang), jnp.sin(ang)
    z = jnp.zeros((seq, LANES - MLA_ROPE_DIM), F32)
    return (jnp.concatenate([cos, cos, z], axis=-1), jnp.concatenate([-sin, sin, z], axis=-1))


def _swap_halves(w):
    half = w.shape[-1] // 2
    return jnp.concatenate([w[..., half:], w[..., :half]], axis=-1)


def _pack_params(w_in, lambda_q1, lambda_k1, lambda_q2, lambda_k2, subln_g, q_norm_g, w_uq,
                 kv_norm_g, w_ukv, w_o, ln1_g, ln1_b, w_mlp_in, w_mlp_out, ln2_g, ln2_b):
    depth = w_in.shape[0]
    o_k, o_v, o_cq = DIFF_WIDTH, 2 * DIFF_WIDTH, 3 * DIFF_WIDTH
    o_ckv = o_cq + MLA_Q_RANK
    o_kr = o_ckv + MLA_KV_RANK
    kr = w_in[..., o_kr:]
    w1 = jnp.concatenate([w_in[..., :o_k] * (DIFF_HEAD_DIM ** -0.5 * LOG2E), w_in[..., o_k:o_v],
                          w_in[..., o_cq:o_ckv], w_in[..., o_ckv:o_kr], kr, _swap_halves(kr)],
                         axis=-1)
    mla_scale = (MLA_NOPE_DIM + MLA_ROPE_DIM) ** -0.5 * LOG2E
    uq = (w_uq * mla_scale).reshape(depth, MLA_Q_RANK, HEADS, MLA_NOPE_DIM + MLA_ROPE_DIM)
    zq = jnp.zeros((depth, MLA_Q_RANK, HEADS, MLA_QK_PAD - MLA_NOPE_DIM - MLA_ROPE_DIM), F32)
    wuq = jnp.concatenate([uq, zq], axis=-1).reshape(depth, MLA_Q_RANK, HEADS * MLA_QK_PAD)
    wuqs = jnp.concatenate([_swap_halves(uq[..., MLA_NOPE_DIM:]), zq], axis=-1)
    wuqs = wuqs.reshape(depth, MLA_Q_RANK, HEADS * LANES)
    ukv = w_ukv.reshape(depth, MLA_KV_RANK, HEADS, MLA_NOPE_DIM + MLA_V_DIM)
    wukvk = ukv[..., :MLA_NOPE_DIM].reshape(depth, MLA_KV_RANK, HEADS * MLA_NOPE_DIM)
    wukvv = ukv[..., MLA_NOPE_DIM:].reshape(depth, MLA_KV_RANK, HEADS * MLA_V_DIM)
    row = lambda v: v.reshape(depth, 1, -1).astype(F32)
    return dict(
        w1=w1.astype(BF16), wvt=jnp.swapaxes(w_in[..., o_v:o_cq], 1, 2).astype(BF16),
        wuq=wuq.astype(BF16), wuqs=wuqs.astype(BF16), wukvk=wukvk.astype(BF16),
        wukvvt=jnp.swapaxes(wukvv, 1, 2).astype(BF16), gq=row(q_norm_g), gkv=row(kv_norm_g),
        subln=row(subln_g),
        lam=jnp.stack([lambda_q1, lambda_k1, lambda_q2, lambda_k2], axis=1).astype(F32),
        wo=w_o.astype(BF16), g1=row(ln1_g), b1=row(ln1_b), wi=w_mlp_in.astype(BF16),
        wout=w_mlp_out.astype(BF16), g2=row(ln2_g), b2=row(ln2_b))


def kernel(x, w_in, lambda_q1, lambda_k1, lambda_q2, lambda_k2, subln_g, q_norm_g, w_uq, kv_norm_g,
           w_ukv, rel_bias, w_o, ln1_g, ln1_b, w_mlp_in, w_mlp_out, ln2_g, ln2_b):
    batch, seq, _ = x.shape
    assert seq % PROJ_ROWS == 0 and seq % ATT_TILE == 0 and (batch * seq) % POST_ROWS == 0
    wp = _pack_params(w_in, lambda_q1, lambda_k1, lambda_q2, lambda_k2, subln_g, q_norm_g, w_uq,
                      kv_norm_g, w_ukv, w_o, ln1_g, ln1_b, w_mlp_in, w_mlp_out, ln2_g, ln2_b)
    wp["cos"], wp["sin"] = _rope_tables(seq)
    bias = _bias_tiles(rel_bias)
    x2 = x.reshape(batch * seq, D_MODEL)
    for l in range(DEPTH):
        lambda_init = 0.8 - 0.6 * math.exp(-0.3 * l)
        dq, dk, dvt, qm, km, vmt = _proj_call(x2, wp, l, seq)
        mix_a, mix_b = _attn_call(dq, dk, dvt, bias, wp, l, qm, km, vmt, seq, lambda_init)
        x2 = _post_call(x2, mix_a, mix_b, wp, l)
    return x2.reshape(batch, seq, D_MODEL)
```

```python
import functools
import math

import jax
import jax.numpy as jnp
from jax import lax
from jax.experimental import pallas as pl
from jax.experimental.pallas import tpu as pltpu

F32 = jnp.float32
BF16 = jnp.bfloat16

D_MODEL = 1024
DEPTH = 2
CHUNK = 64
HEADS = 4
DIFF_HEAD_DIM = 64
DIFF_V_DIM = 2 * DIFF_HEAD_DIM
DIFF_WIDTH = HEADS * DIFF_V_DIM
MLA_NOPE_DIM = 128
MLA_ROPE_DIM = 64
MLA_V_DIM = 128
MLA_Q_RANK = 256
MLA_KV_RANK = 128
MLA_WIDTH = HEADS * MLA_V_DIM
MLA_QK_PAD = 256
D_FF = 4 * D_MODEL
N_BUCKETS = 32
MAX_DISTANCE = 128
ROPE_THETA = 10000.0
ALPHA = (2 * DEPTH) ** 0.25
LN_EPS = 1e-5
RMS_EPS = 1e-6
NEG_BIG = -1e30
LOG2E = math.log2(math.e)

LANES = 128
BF16_SUBLANES = 16
ATT_TILE = 256
ATT_HEADS_PER_STEP = 4
PROJ_ROWS = 1024
POST_ROWS = 1024
POST_ROW_BLOCK = 256
FF_CHUNK = 1024
DIFF_LOOKAHEAD = 2
MLA_LOOKAHEAD = 4
VMEM_LIMIT = 56 * 1024 * 1024
assert ATT_TILE >= MAX_DISTANCE and ATT_TILE % CHUNK == 0

C_DQ = 0
C_DK = C_DQ + DIFF_WIDTH
C_CQ = C_DK + DIFF_WIDTH
C_CKV = C_CQ + MLA_Q_RANK
C_KR = C_CKV + MLA_KV_RANK
W1_COLS = C_KR + LANES

_NT = (((1,), (1,)), ((), ()))


def _rms(x, g):
    return x * lax.rsqrt(jnp.mean(x * x, axis=-1, keepdims=True) + RMS_EPS) * g


def _layer_norm(x, g, b):
    mu = jnp.mean(x, axis=-1, keepdims=True)
    xc = x - mu
    var = jnp.mean(xc * xc, axis=-1, keepdims=True)
    return xc * lax.rsqrt(var + LN_EPS) * g + b


def _layer_spec(arr, l, **kwargs):
    index = (l,) + (0,) * (arr.ndim - 1)
    return pl.BlockSpec((1,) + arr.shape[1:], lambda *_: index, **kwargs)


def _proj_kernel(x_ref, w1_ref, wvt_ref, wuq_ref, wuqs_ref, wukvk_ref, wukvvt_ref,
                 gq_ref, gkv_ref, cos_ref, sin_ref,
                 dq_ref, dk_ref, dvt_ref, qm_ref, km_ref, vmt_ref):
    xb = x_ref[...].astype(BF16)
    h = jnp.dot(xb, w1_ref[0], preferred_element_type=F32)
    dq_ref[...] = h[:, C_DQ:C_DQ + DIFF_WIDTH].astype(BF16)
    dk_ref[...] = h[:, C_DK:C_DK + DIFF_WIDTH].astype(BF16)
    dvt_ref[0] = lax.dot_general(wvt_ref[0], xb, _NT, preferred_element_type=F32).astype(BF16)

    cos = cos_ref[...]
    sin = sin_ref[...]
    c_q = _rms(h[:, C_CQ:C_CQ + MLA_Q_RANK], gq_ref[0]).astype(BF16)
    qf = jnp.dot(c_q, wuq_ref[0], preferred_element_type=F32)
    qsw = jnp.dot(c_q, wuqs_ref[0], preferred_element_type=F32)
    c_kv = _rms(h[:, C_CKV:C_CKV + MLA_KV_RANK], gkv_ref[0]).astype(BF16)
    kn = jnp.dot(c_kv, wukvk_ref[0], preferred_element_type=F32)
    vmt_ref[0] = lax.dot_general(wukvvt_ref[0], c_kv, _NT,
                                 preferred_element_type=F32).astype(BF16)
    kr = h[:, C_KR:C_KR + LANES]
    k_rope = (kr * cos + pltpu.roll(kr, LANES // 2, 1) * sin).astype(BF16)

    for hh in range(HEADS):
        b0 = hh * MLA_QK_PAD
        qm_ref[:, b0:b0 + LANES] = qf[:, b0:b0 + LANES].astype(BF16)
        qm_ref[:, b0 + LANES:b0 + 2 * LANES] = (
            qf[:, b0 + LANES:b0 + 2 * LANES] * cos + qsw[:, hh * LANES:(hh + 1) * LANES] * sin
        ).astype(BF16)
        km_ref[:, b0:b0 + LANES] = kn[:, hh * LANES:(hh + 1) * LANES].astype(BF16)
        km_ref[:, b0 + LANES:b0 + 2 * LANES] = k_rope


def _proj_call(x2, wp, l, seq):
    ntok = x2.shape[0]
    batch = ntok // seq
    rows = PROJ_ROWS
    steps_per_seq = seq // rows

    def const(name):
        return _layer_spec(wp[name], l)

    def tok(width):
        return pl.BlockSpec((rows, width), lambda i: (i, 0))

    vt_spec = pl.BlockSpec((1, HEADS * LANES, rows),
                           lambda i: (i // steps_per_seq, 0, i % steps_per_seq))
    pos_spec = pl.BlockSpec((rows, LANES), lambda i: (i % steps_per_seq, 0))
    vt_shape = jax.ShapeDtypeStruct((batch, HEADS * LANES, seq), BF16)
    return pl.pallas_call(
        _proj_kernel,
        grid=(ntok // rows,),
        in_specs=[tok(D_MODEL), const("w1"), const("wvt"), const("wuq"), const("wuqs"),
                  const("wukvk"), const("wukvvt"), const("gq"), const("gkv"), pos_spec, pos_spec],
        out_specs=[tok(DIFF_WIDTH), tok(DIFF_WIDTH), vt_spec,
                   tok(HEADS * MLA_QK_PAD), tok(HEADS * MLA_QK_PAD), vt_spec],
        out_shape=[jax.ShapeDtypeStruct((ntok, DIFF_WIDTH), BF16),
                   jax.ShapeDtypeStruct((ntok, DIFF_WIDTH), BF16),
                   vt_shape,
                   jax.ShapeDtypeStruct((ntok, HEADS * MLA_QK_PAD), BF16),
                   jax.ShapeDtypeStruct((ntok, HEADS * MLA_QK_PAD), BF16),
                   vt_shape],
        compiler_params=pltpu.CompilerParams(dimension_semantics=("arbitrary",),
                                             vmem_limit_bytes=VMEM_LIMIT),
        name="proj",
    )(x2, wp["w1"], wp["wvt"], wp["wuq"], wp["wuqs"], wp["wukvk"], wp["wukvvt"],
      wp["gq"], wp["gkv"], wp["cos"], wp["sin"])


def _col_max(a, b):
    m = jnp.max(b, axis=0, keepdims=True)
    return m if a is None else jnp.maximum(a, m)


def _attend(heads, n_tiles, q_fn, score_fn, finish_fn, vt_fn, lookahead):
    ones = jnp.ones((BF16_SUBLANES, ATT_TILE), BF16)
    items = [(h, qi, c) for h in heads for qi in range(n_tiles) for c in range(qi + 1)]
    qs, scores = {}, {}

    def issue(idx):
        h, qi, c = items[idx]
        if c == 0:
            qs[h, qi] = q_fn(h, qi)
        scores[idx] = score_fn(h, qi, qs[h, qi], c)

    for idx in range(min(lookahead, len(items))):
        issue(idx)
    m, acc = None, None
    for idx, (h, qi, c) in enumerate(items):
        if idx + lookahead < len(items):
            issue(idx + lookahead)
        s = scores.pop(idx)
        m_new = _col_max(m, s)
        e = jnp.exp2(s - m_new).astype(BF16)
        vt = jnp.concatenate([vt_fn(h, c), ones], axis=0)
        pv = jnp.dot(vt, e, preferred_element_type=F32)
        acc = pv if acc is None else acc * jnp.exp2(m - m_new) + pv
        m = m_new
        if c == qi:
            finish_fn(h, qi, acc[:LANES] / acc[LANES:LANES + 1])
            m, acc = None, None
        yield


def _diff_steps(heads, q_ref, k_ref, vt_ref, bias_ref, lam_ref, g_ref, o_ref, lambda_init):
    t = ATT_TILE
    lane = lax.broadcasted_iota(jnp.int32, (t, LANES), 1)
    lp = lam_ref[0]
    lam = (jnp.exp(jnp.sum(lp[0:1] * lp[1:2], axis=-1, keepdims=True))
           - jnp.exp(jnp.sum(lp[2:3] * lp[3:4], axis=-1, keepdims=True)) + lambda_init)
    g = g_ref[0]

    def q_rows(h, qi):
        q = q_ref[qi * t:(qi + 1) * t, h * LANES:(h + 1) * LANES]
        zero = jnp.zeros_like(q)
        return jnp.concatenate([jnp.where(lane < DIFF_HEAD_DIM, q, zero),
                                jnp.where(lane >= DIFF_HEAD_DIM, q, zero)], axis=0)

    def score(h, qi, qs, c):
        k = k_ref[c * t:(c + 1) * t, h * LANES:(h + 1) * LANES]
        s = lax.dot_general(k, qs, _NT, preferred_element_type=F32)
        return s + bias_ref[h, qi - c] if qi - c <= 1 else s

    def values(h, c):
        return vt_ref[0, h * LANES:(h + 1) * LANES, c * t:(c + 1) * t]

    def finish(h, qi, o):
        ot = o[:, :t] - lam * o[:, t:]
        y = _rms(ot.T, g) * (1.0 - lambda_init)
        o_ref[qi * t:(qi + 1) * t, h * LANES:(h + 1) * LANES] = y.astype(BF16)

    return _attend(heads, q_ref.shape[0] // t, q_rows, score, finish, values, DIFF_LOOKAHEAD)


def _mla_steps(heads, q_ref, k_ref, vt_ref, o_ref):
    t = ATT_TILE
    w = MLA_QK_PAD
    kc = lax.broadcasted_iota(jnp.int32, (t, t), 0) // CHUNK
    qc = lax.broadcasted_iota(jnp.int32, (t, t), 1) // CHUNK
    allowed = kc <= qc

    def q_rows(h, qi):
        return q_ref[qi * t:(qi + 1) * t, h * w:(h + 1) * w]

    def score(h, qi, q, c):
        k = k_ref[c * t:(c + 1) * t, h * w:(h + 1) * w]
        s = lax.dot_general(k, q, _NT, preferred_element_type=F32)
        return jnp.where(allowed, s, NEG_BIG) if c == qi else s

    def values(h, c):
        return vt_ref[0, h * LANES:(h + 1) * LANES, c * t:(c + 1) * t]

    def finish(h, qi, o):
        o_ref[qi * t:(qi + 1) * t, h * LANES:(h + 1) * LANES] = o.T.astype(BF16)

    return _attend(heads, q_ref.shape[0] // t, q_rows, score, finish, values, MLA_LOOKAHEAD)


def _attn_kernel(dq_ref, dk_ref, dvt_ref, bias_ref, lam_ref, g_ref, qm_ref, km_ref, vmt_ref,
                 oa_ref, ob_ref, *, lambda_init):
    heads = range(ATT_HEADS_PER_STEP)
    streams = [_diff_steps(heads, dq_ref, dk_ref, dvt_ref, bias_ref, lam_ref, g_ref, oa_ref,
                           lambda_init),
               _mla_steps(heads, qm_ref, km_ref, vmt_ref, ob_ref)]
    while streams:
        for s in list(streams):
            if next(s, "done") == "done":
                streams.remove(s)


def _attn_call(dq, dk, dvt, bias, wp, l, qm, km, vmt, seq, lambda_init):
    ntok = dq.shape[0]
    n = ATT_HEADS_PER_STEP

    def rows(width):
        return pl.BlockSpec((seq, n * width), lambda b, h: (b, h))

    vt_spec = pl.BlockSpec((1, n * LANES, seq), lambda b, h: (b, h, 0))
    return pl.pallas_call(
        functools.partial(_attn_kernel, lambda_init=lambda_init),
        grid=(ntok // seq, HEADS // n),
        in_specs=[rows(LANES), rows(LANES), vt_spec,
                  pl.BlockSpec((n,) + bias.shape[1:], lambda b, h: (h, 0, 0, 0)),
                  _layer_spec(wp["lam"], l), _layer_spec(wp["subln"], l),
                  rows(MLA_QK_PAD), rows(MLA_QK_PAD), vt_spec],
        out_specs=[rows(LANES), rows(LANES)],
        out_shape=[jax.ShapeDtypeStruct((ntok, DIFF_WIDTH), BF16),
                   jax.ShapeDtypeStruct((ntok, MLA_WIDTH), BF16)],
        compiler_params=pltpu.CompilerParams(dimension_semantics=("arbitrary", "arbitrary"),
                                             vmem_limit_bytes=VMEM_LIMIT),
        name="attn",
    )(dq, dk, dvt, bias, wp["lam"], wp["subln"], qm, km, vmt)


def _post_kernel(x_ref, ma_ref, mb_ref, wo_ref, g1_ref, b1_ref, wi_ref, wout_ref, g2_ref, b2_ref,
                 o_ref):
    n_blocks = x_ref.shape[0] // POST_ROW_BLOCK
    n_chunks = D_FF // FF_CHUNK

    def attn_out(r):
        rs = slice(r * POST_ROW_BLOCK, (r + 1) * POST_ROW_BLOCK)
        y = (jnp.dot(ma_ref[rs, :], wo_ref[0, :DIFF_WIDTH, :], preferred_element_type=F32)
             + jnp.dot(mb_ref[rs, :], wo_ref[0, DIFF_WIDTH:, :], preferred_element_type=F32))
        return _layer_norm(ALPHA * x_ref[rs, :] + y, g1_ref[0], b1_ref[0])

    def up(x1b, c):
        return jnp.dot(x1b, wi_ref[0, :, c * FF_CHUNK:(c + 1) * FF_CHUNK],
                       preferred_element_type=F32)

    x1 = attn_out(0)
    for r in range(n_blocks):
        x1_next = attn_out(r + 1) if r + 1 < n_blocks else None
        x1b = x1.astype(BF16)
        u = up(x1b, 0)
        y2 = None
        for c in range(n_chunks):
            u_next = up(x1b, c + 1) if c + 1 < n_chunks else None
            act = jnp.square(jnp.maximum(u, 0.0)).astype(BF16)
            d = jnp.dot(act, wout_ref[0, c * FF_CHUNK:(c + 1) * FF_CHUNK, :],
                        preferred_element_type=F32)
            y2 = d if y2 is None else y2 + d
            u = u_next
        o_ref[r * POST_ROW_BLOCK:(r + 1) * POST_ROW_BLOCK, :] = _layer_norm(
            ALPHA * x1 + y2, g2_ref[0], b2_ref[0])
        x1 = x1_next


def _post_call(x2, mix_a, mix_b, wp, l):
    ntok = x2.shape[0]
    rows = POST_ROWS

    def const(name):
        return _layer_spec(wp[name], l, pipeline_mode=pl.Buffered(1))

    def tok(width):
        return pl.BlockSpec((rows, width), lambda i: (i, 0))

    return pl.pallas_call(
        _post_kernel,
        grid=(ntok // rows,),
        in_specs=[tok(D_MODEL), tok(DIFF_WIDTH), tok(MLA_WIDTH), const("wo"), const("g1"),
                  const("b1"), const("wi"), const("wout"), const("g2"), const("b2")],
        out_specs=tok(D_MODEL),
        out_shape=jax.ShapeDtypeStruct((ntok, D_MODEL), F32),
        compiler_params=pltpu.CompilerParams(dimension_semantics=("arbitrary",),
                                             vmem_limit_bytes=VMEM_LIMIT),
        name="post",
    )(x2, mix_a, mix_b, wp["wo"], wp["g1"], wp["b1"], wp["wi"], wp["wout"], wp["g2"], wp["b2"])


def _t5_bucket(rel):
    nb = N_BUCKETS // 2
    ret = (rel > 0).astype(jnp.int32) * nb
    n = jnp.abs(rel)
    max_exact = nb // 2
    nf = jnp.maximum(n, 1).astype(F32)
    large = max_exact + (jnp.log(nf / max_exact) / math.log(MAX_DISTANCE / max_exact)
                         * (nb - max_exact)).astype(jnp.int32)
    large = jnp.minimum(large, nb - 1)
    return ret + jnp.where(n < max_exact, n, large)


def _bias_tiles(rel_bias):
    t = ATT_TILE
    p = 2 * t
    table = rel_bias.astype(F32).T
    far = table[:, N_BUCKETS // 2 - 1]
    w = jnp.arange(p, dtype=jnp.int32)[None, :]
    d = jnp.arange(2, dtype=jnp.int32)[:, None]
    rel = (t - 1) - w - d * t
    vec = (table[:, _t5_bucket(rel)] - far[:, None, None]) * LOG2E
    rows = jnp.tile(vec, (1, 1, t))[..., :t * (p - 1)].reshape(HEADS, 2, t, p - 1)
    bias = rows[..., t - 1:2 * t - 1]
    kk = jnp.arange(t, dtype=jnp.int32)[:, None]
    qq = jnp.arange(t, dtype=jnp.int32)[None, :]
    allowed = (kk // CHUNK <= qq // CHUNK)[None, None] | (d > 0)[None, :, :, None]
    bias = jnp.where(allowed, bias, NEG_BIG)
    return jnp.concatenate([bias, bias], axis=-1)


def _rope_tables(seq):
    pos = jnp.arange(seq, dtype=F32)
    inv = ROPE_THETA ** (-jnp.arange(0, MLA_ROPE_DIM, 2, dtype=F32) / MLA_ROPE_DIM)
    ang = pos[:, None] * inv[None, :]
    cos, sin = jnp.cos(ang), jnp.sin(ang)
    z = jnp.zeros((seq, LANES - MLA_ROPE_DIM), F32)
    return (jnp.concatenate([cos, cos, z], axis=-1), jnp.concatenate([-sin, sin, z], axis=-1))


def _swap_halves(w):
    half = w.shape[-1] // 2
    return jnp.concatenate([w[..., half:], w[..., :half]], axis=-1)


def _pack_params(w_in, lambda_q1, lambda_k1, lambda_q2, lambda_k2, subln_g, q_norm_g, w_uq,
                 kv_norm_g, w_ukv, w_o, ln1_g, ln1_b, w_mlp_in, w_mlp_out, ln2_g, ln2_b):
    depth = w_in.shape[0]
    o_k, o_v, o_cq = DIFF_WIDTH, 2 * DIFF_WIDTH, 3 * DIFF_WIDTH
    o_ckv = o_cq + MLA_Q_RANK
    o_kr = o_ckv + MLA_KV_RANK
    kr = w_in[..., o_kr:]
    w1 = jnp.concatenate([w_in[..., :o_k] * (DIFF_HEAD_DIM ** -0.5 * LOG2E), w_in[..., o_k:o_v],
                          w_in[..., o_cq:o_ckv], w_in[..., o_ckv:o_kr], kr, _swap_halves(kr)],
                         axis=-1)
    mla_scale = (MLA_NOPE_DIM + MLA_ROPE_DIM) ** -0.5 * LOG2E
    uq = (w_uq * mla_scale).reshape(depth, MLA_Q_RANK, HEADS, MLA_NOPE_DIM + MLA_ROPE_DIM)
    zq = jnp.zeros((depth, MLA_Q_RANK, HEADS, MLA_QK_PAD - MLA_NOPE_DIM - MLA_ROPE_DIM), F32)
    wuq = jnp.concatenate([uq, zq], axis=-1).reshape(depth, MLA_Q_RANK, HEADS * MLA_QK_PAD)
    wuqs = jnp.concatenate([_swap_halves(uq[..., MLA_NOPE_DIM:]), zq], axis=-1)
    wuqs = wuqs.reshape(depth, MLA_Q_RANK, HEADS * LANES)
    ukv = w_ukv.reshape(depth, MLA_KV_RANK, HEADS, MLA_NOPE_DIM + MLA_V_DIM)
    wukvk = ukv[..., :MLA_NOPE_DIM].reshape(depth, MLA_KV_RANK, HEADS * MLA_NOPE_DIM)
    wukvv = ukv[..., MLA_NOPE_DIM:].reshape(depth, MLA_KV_RANK, HEADS * MLA_V_DIM)
    row = lambda v: v.reshape(depth, 1, -1).astype(F32)
    return dict(
        w1=w1.astype(BF16), wvt=jnp.swapaxes(w_in[..., o_v:o_cq], 1, 2).astype(BF16),
        wuq=wuq.astype(BF16), wuqs=wuqs.astype(BF16), wukvk=wukvk.astype(BF16),
        wukvvt=jnp.swapaxes(wukvv, 1, 2).astype(BF16), gq=row(q_norm_g), gkv=row(kv_norm_g),
        subln=row(subln_g),
        lam=jnp.stack([lambda_q1, lambda_k1, lambda_q2, lambda_k2], axis=1).astype(F32),
        wo=w_o.astype(BF16), g1=row(ln1_g), b1=row(ln1_b), wi=w_mlp_in.astype(BF16),
        wout=w_mlp_out.astype(BF16), g2=row(ln2_g), b2=row(ln2_b))


def kernel(x, w_in, lambda_q1, lambda_k1, lambda_q2, lambda_k2, subln_g, q_norm_g, w_uq, kv_norm_g,
           w_ukv, rel_bias, w_o, ln1_g, ln1_b, w_mlp_in, w_mlp_out, ln2_g, ln2_b):
    batch, seq, _ = x.shape
    assert seq % PROJ_ROWS == 0 and seq % ATT_TILE == 0 and (batch * seq) % POST_ROWS == 0
    wp = _pack_params(w_in, lambda_q1, lambda_k1, lambda_q2, lambda_k2, subln_g, q_norm_g, w_uq,
                      kv_norm_g, w_ukv, w_o, ln1_g, ln1_b, w_mlp_in, w_mlp_out, ln2_g, ln2_b)
    wp["cos"], wp["sin"] = _rope_tables(seq)
    bias = _bias_tiles(rel_bias)
    x2 = x.reshape(batch * seq, D_MODEL)
    for l in range(DEPTH):
        lambda_init = 0.8 - 0.6 * math.exp(-0.3 * l)
        dq, dk, dvt, qm, km, vmt = _proj_call(x2, wp, l, seq)
        mix_a, mix_b = _attn_call(dq, dk, dvt, bias, wp, l, qm, km, vmt, seq, lambda_init)
        x2 = _post_call(x2, mix_a, mix_b, wp, l)
    return x2.reshape(batch, seq, D_MODEL)
```

```python
import functools
import math

import jax
import jax.numpy as jnp
from jax import lax
from jax.experimental import pallas as pl
from jax.experimental.pallas import tpu as pltpu

F32 = jnp.float32
BF16 = jnp.bfloat16

D_MODEL = 1024
DEPTH = 2
CHUNK = 64
HEADS = 4
DIFF_HEAD_DIM = 64
DIFF_V_DIM = 2 * DIFF_HEAD_DIM
DIFF_WIDTH = HEADS * DIFF_V_DIM
MLA_NOPE_DIM = 128
MLA_ROPE_DIM = 64
MLA_V_DIM = 128
MLA_Q_RANK = 256
MLA_KV_RANK = 128
MLA_WIDTH = HEADS * MLA_V_DIM
MLA_QK_PAD = 256
D_FF = 4 * D_MODEL
N_BUCKETS = 32
MAX_DISTANCE = 128
ROPE_THETA = 10000.0
ALPHA = (2 * DEPTH) ** 0.25
LN_EPS = 1e-5
RMS_EPS = 1e-6
NEG_BIG = -1e30
LOG2E = math.log2(math.e)

LANES = 128
BF16_SUBLANES = 16
ATT_TILE = 256
ATT_HEADS_PER_STEP = 1
PROJ_ROWS = 1024
POST_ROWS = 1024
POST_ROW_BLOCK = 256
FF_CHUNK = 1024
DIFF_LOOKAHEAD = 2
MLA_LOOKAHEAD = 4
VMEM_LIMIT = 56 * 1024 * 1024
assert ATT_TILE >= MAX_DISTANCE and ATT_TILE % CHUNK == 0

C_DQ = 0
C_DK = C_DQ + DIFF_WIDTH
C_CQ = C_DK + DIFF_WIDTH
C_CKV = C_CQ + MLA_Q_RANK
C_KR = C_CKV + MLA_KV_RANK
W1_COLS = C_KR + LANES

_NT = (((1,), (1,)), ((), ()))


def _rms(x, g):
    return x * lax.rsqrt(jnp.mean(x * x, axis=-1, keepdims=True) + RMS_EPS) * g


def _layer_norm(x, g, b):
    mu = jnp.mean(x, axis=-1, keepdims=True)
    xc = x - mu
    var = jnp.mean(xc * xc, axis=-1, keepdims=True)
    return xc * lax.rsqrt(var + LN_EPS) * g + b


def _layer_spec(arr, l, **kwargs):
    index = (l,) + (0,) * (arr.ndim - 1)
    return pl.BlockSpec((1,) + arr.shape[1:], lambda *_: index, **kwargs)


def _proj_kernel(x_ref, w1_ref, wvt_ref, wuq_ref, wuqs_ref, wukvk_ref, wukvvt_ref,
                 gq_ref, gkv_ref, cos_ref, sin_ref,
                 dq_ref, dk_ref, dvt_ref, qm_ref, km_ref, vmt_ref):
    xb = x_ref[...].astype(BF16)
    h = jnp.dot(xb, w1_ref[0], preferred_element_type=F32)
    dq_ref[...] = h[:, C_DQ:C_DQ + DIFF_WIDTH].astype(BF16)
    dk_ref[...] = h[:, C_DK:C_DK + DIFF_WIDTH].astype(BF16)
    dvt_ref[0] = lax.dot_general(wvt_ref[0], xb, _NT, preferred_element_type=F32).astype(BF16)

    cos = cos_ref[...]
    sin = sin_ref[...]
    c_q = _rms(h[:, C_CQ:C_CQ + MLA_Q_RANK], gq_ref[0]).astype(BF16)
    qf = jnp.dot(c_q, wuq_ref[0], preferred_element_type=F32)
    qsw = jnp.dot(c_q, wuqs_ref[0], preferred_element_type=F32)
    c_kv = _rms(h[:, C_CKV:C_CKV + MLA_KV_RANK], gkv_ref[0]).astype(BF16)
    kn = jnp.dot(c_kv, wukvk_ref[0], preferred_element_type=F32)
    vmt_ref[0] = lax.dot_general(wukvvt_ref[0], c_kv, _NT,
                                 preferred_element_type=F32).astype(BF16)
    kr = h[:, C_KR:C_KR + LANES]
    k_rope = (kr * cos + pltpu.roll(kr, LANES // 2, 1) * sin).astype(BF16)

    for hh in range(HEADS):
        b0 = hh * MLA_QK_PAD
        qm_ref[:, b0:b0 + LANES] = qf[:, b0:b0 + LANES].astype(BF16)
        qm_ref[:, b0 + LANES:b0 + 2 * LANES] = (
            qf[:, b0 + LANES:b0 + 2 * LANES] * cos + qsw[:, hh * LANES:(hh + 1) * LANES] * sin
        ).astype(BF16)
        km_ref[:, b0:b0 + LANES] = kn[:, hh * LANES:(hh + 1) * LANES].astype(BF16)
        km_ref[:, b0 + LANES:b0 + 2 * LANES] = k_rope


def _proj_call(x2, wp, l, seq):
    ntok = x2.shape[0]
    batch = ntok // seq
    rows = PROJ_ROWS
    steps_per_seq = seq // rows

    def const(name):
        return _layer_spec(wp[name], l)

    def tok(width):
        return pl.BlockSpec((rows, width), lambda i: (i, 0))

    vt_spec = pl.BlockSpec((1, HEADS * LANES, rows),
                           lambda i: (i // steps_per_seq, 0, i % steps_per_seq))
    pos_spec = pl.BlockSpec((rows, LANES), lambda i: (i % steps_per_seq, 0))
    vt_shape = jax.ShapeDtypeStruct((batch, HEADS * LANES, seq), BF16)
    return pl.pallas_call(
        _proj_kernel,
        grid=(ntok // rows,),
        in_specs=[tok(D_MODEL), const("w1"), const("wvt"), const("wuq"), const("wuqs"),
                  const("wukvk"), const("wukvvt"), const("gq"), const("gkv"), pos_spec, pos_spec],
        out_specs=[tok(DIFF_WIDTH), tok(DIFF_WIDTH), vt_spec,
                   tok(HEADS * MLA_QK_PAD), tok(HEADS * MLA_QK_PAD), vt_spec],
        out_shape=[jax.ShapeDtypeStruct((ntok, DIFF_WIDTH), BF16),
                   jax.ShapeDtypeStruct((ntok, DIFF_WIDTH), BF16),
                   vt_shape,
                   jax.ShapeDtypeStruct((ntok, HEADS * MLA_QK_PAD), BF16),
                   jax.ShapeDtypeStruct((ntok, HEADS * MLA_QK_PAD), BF16),
                   vt_shape],
        compiler_params=pltpu.CompilerParams(dimension_semantics=("arbitrary",),
                                             vmem_limit_bytes=VMEM_LIMIT),
        name="proj",
    )(x2, wp["w1"], wp["wvt"], wp["wuq"], wp["wuqs"], wp["wukvk"], wp["wukvvt"],
      wp["gq"], wp["gkv"], wp["cos"], wp["sin"])


def _col_max(a, b):
    m = jnp.max(b, axis=0, keepdims=True)
    return m if a is None else jnp.maximum(a, m)


def _attend(heads, n_tiles, q_fn, score_fn, finish_fn, vt_fn, lookahead):
    ones = jnp.ones((BF16_SUBLANES, ATT_TILE), BF16)
    items = [(h, qi, c) for h in heads for qi in range(n_tiles) for c in range(qi + 1)]
    qs, scores = {}, {}

    def issue(idx):
        h, qi, c = items[idx]
        if c == 0:
            qs[h, qi] = q_fn(h, qi)
        scores[idx] = score_fn(h, qi, qs[h, qi], c)

    for idx in range(min(lookahead, len(items))):
        issue(idx)
    m, acc = None, None
    for idx, (h, qi, c) in enumerate(items):
        if idx + lookahead < len(items):
            issue(idx + lookahead)
        s = scores.pop(idx)
        m_new = _col_max(m, s)
        e = jnp.exp2(s - m_new).astype(BF16)
        vt = jnp.concatenate([vt_fn(h, c), ones], axis=0)
        pv = jnp.dot(vt, e, preferred_element_type=F32)
        acc = pv if acc is None else acc * jnp.exp2(m - m_new) + pv
        m = m_new
        if c == qi:
            finish_fn(h, qi, acc[:LANES] / acc[LANES:LANES + 1])
            m, acc = None, None
        yield


def _diff_steps(heads, q_ref, k_ref, vt_ref, bias_ref, lam_ref, g_ref, o_ref, lambda_init):
    t = ATT_TILE
    lane = lax.broadcasted_iota(jnp.int32, (t, LANES), 1)
    lp = lam_ref[0]
    lam = (jnp.exp(jnp.sum(lp[0:1] * lp[1:2], axis=-1, keepdims=True))
           - jnp.exp(jnp.sum(lp[2:3] * lp[3:4], axis=-1, keepdims=True)) + lambda_init)
    g = g_ref[0]

    def q_rows(h, qi):
        q = q_ref[qi * t:(qi + 1) * t, h * LANES:(h + 1) * LANES]
        zero = jnp.zeros_like(q)
        return jnp.concatenate([jnp.where(lane < DIFF_HEAD_DIM, q, zero),
                                jnp.where(lane >= DIFF_HEAD_DIM, q, zero)], axis=0)

    def score(h, qi, qs, c):
        k = k_ref[c * t:(c + 1) * t, h * LANES:(h + 1) * LANES]
        s = lax.dot_general(k, qs, _NT, preferred_element_type=F32)
        return s + bias_ref[h, qi - c] if qi - c <= 1 else s

    def values(h, c):
        return vt_ref[0, h * LANES:(h + 1) * LANES, c * t:(c + 1) * t]

    def finish(h, qi, o):
        ot = o[:, :t] - lam * o[:, t:]
        y = _rms(ot.T, g) * (1.0 - lambda_init)
        o_ref[qi * t:(qi + 1) * t, h * LANES:(h + 1) * LANES] = y.astype(BF16)

    return _attend(heads, q_ref.shape[0] // t, q_rows, score, finish, values, DIFF_LOOKAHEAD)


def _mla_steps(heads, q_ref, k_ref, vt_ref, o_ref):
    t = ATT_TILE
    w = MLA_QK_PAD
    kc = lax.broadcasted_iota(jnp.int32, (t, t), 0) // CHUNK
    qc = lax.broadcasted_iota(jnp.int32, (t, t), 1) // CHUNK
    allowed = kc <= qc

    def q_rows(h, qi):
        return q_ref[qi * t:(qi + 1) * t, h * w:(h + 1) * w]

    def score(h, qi, q, c):
        k = k_ref[c * t:(c + 1) * t, h * w:(h + 1) * w]
        s = lax.dot_general(k, q, _NT, preferred_element_type=F32)
        return jnp.where(allowed, s, NEG_BIG) if c == qi else s

    def values(h, c):
        return vt_ref[0, h * LANES:(h + 1) * LANES, c * t:(c + 1) * t]

    def finish(h, qi, o):
        o_ref[qi * t:(qi + 1) * t, h * LANES:(h + 1) * LANES] = o.T.astype(BF16)

    return _attend(heads, q_ref.shape[0] // t, q_rows, score, finish, values, MLA_LOOKAHEAD)


def _attn_kernel(dq_ref, dk_ref, dvt_ref, bias_ref, lam_ref, g_ref, qm_ref, km_ref, vmt_ref,
                 oa_ref, ob_ref, *, lambda_init):
    heads = range(ATT_HEADS_PER_STEP)
    streams = [_diff_steps(heads, dq_ref, dk_ref, dvt_ref, bias_ref, lam_ref, g_ref, oa_ref,
                           lambda_init),
               _mla_steps(heads, qm_ref, km_ref, vmt_ref, ob_ref)]
    while streams:
        for s in list(streams):
            if next(s, "done") == "done":
                streams.remove(s)


def _attn_call(dq, dk, dvt, bias, wp, l, qm, km, vmt, seq, lambda_init):
    ntok = dq.shape[0]
    n = ATT_HEADS_PER_STEP

    def rows(width):
        return pl.BlockSpec((seq, n * width), lambda b, h: (b, h))

    vt_spec = pl.BlockSpec((1, n * LANES, seq), lambda b, h: (b, h, 0))
    return pl.pallas_call(
        functools.partial(_attn_kernel, lambda_init=lambda_init),
        grid=(ntok // seq, HEADS // n),
        in_specs=[rows(LANES), rows(LANES), vt_spec,
                  pl.BlockSpec((n,) + bias.shape[1:], lambda b, h: (h, 0, 0, 0)),
                  _layer_spec(wp["lam"], l), _layer_spec(wp["subln"], l),
                  rows(MLA_QK_PAD), rows(MLA_QK_PAD), vt_spec],
        out_specs=[rows(LANES), rows(LANES)],
        out_shape=[jax.ShapeDtypeStruct((ntok, DIFF_WIDTH), BF16),
                   jax.ShapeDtypeStruct((ntok, MLA_WIDTH), BF16)],
        compiler_params=pltpu.CompilerParams(dimension_semantics=("arbitrary", "arbitrary"),
                                             vmem_limit_bytes=VMEM_LIMIT),
        name="attn",
    )(dq, dk, dvt, bias, wp["lam"], wp["subln"], qm, km, vmt)


def _post_kernel(x_ref, ma_ref, mb_ref, wo_ref, g1_ref, b1_ref, wi_ref, wout_ref, g2_ref, b2_ref,
                 o_ref):
    n_blocks = x_ref.shape[0] // POST_ROW_BLOCK
    n_chunks = D_FF // FF_CHUNK

    def attn_out(r):
        rs = slice(r * POST_ROW_BLOCK, (r + 1) * POST_ROW_BLOCK)
        y = (jnp.dot(ma_ref[rs, :], wo_ref[0, :DIFF_WIDTH, :], preferred_element_type=F32)
             + jnp.dot(mb_ref[rs, :], wo_ref[0, DIFF_WIDTH:, :], preferred_element_type=F32))
        return _layer_norm(ALPHA * x_ref[rs, :] + y, g1_ref[0], b1_ref[0])

    def up(x1b, c):
        return jnp.dot(x1b, wi_ref[0, :, c * FF_CHUNK:(c + 1) * FF_CHUNK],
                       preferred_element_type=F32)

    x1 = attn_out(0)
    for r in range(n_blocks):
        x1_next = attn_out(r + 1) if r + 1 < n_blocks else None
        x1b = x1.astype(BF16)
        u = up(x1b, 0)
        y2 = None
        for c in range(n_chunks):
            u_next = up(x1b, c + 1) if c + 1 < n_chunks else None
            act = jnp.square(jnp.maximum(u, 0.0)).astype(BF16)
            d = jnp.dot(act, wout_ref[0, c * FF_CHUNK:(c + 1) * FF_CHUNK, :],
                        preferred_element_type=F32)
            y2 = d if y2 is None else y2 + d
            u = u_next
        o_ref[r * POST_ROW_BLOCK:(r + 1) * POST_ROW_BLOCK, :] = _layer_norm(
            ALPHA * x1 + y2, g2_ref[0], b2_ref[0])
        x1 = x1_next


def _post_call(x2, mix_a, mix_b, wp, l):
    ntok = x2.shape[0]
    rows = POST_ROWS

    def const(name):
        return _layer_spec(wp[name], l, pipeline_mode=pl.Buffered(1))

    def tok(width):
        return pl.BlockSpec((rows, width), lambda i: (i, 0))

    return pl.pallas_call(
        _post_kernel,
        grid=(ntok // rows,),
        in_specs=[tok(D_MODEL), tok(DIFF_WIDTH), tok(MLA_WIDTH), const("wo"), const("g1"),
                  const("b1"), const("wi"), const("wout"), const("g2"), const("b2")],
        out_specs=tok(D_MODEL),
        out_shape=jax.ShapeDtypeStruct((ntok, D_MODEL), F32),
        compiler_params=pltpu.CompilerParams(dimension_semantics=("arbitrary",),
                                             vmem_limit_bytes=VMEM_LIMIT),
        name="post",
    )(x2, mix_a, mix_b, wp["wo"], wp["g1"], wp["b1"], wp["wi"], wp["wout"], wp["g2"], wp["b2"])


def _t5_bucket(rel):
    nb = N_BUCKETS // 2
    ret = (rel > 0).astype(jnp.int32) * nb
    n = jnp.abs(rel)
    max_exact = nb // 2
    nf = jnp.maximum(n, 1).astype(F32)
    large = max_exact + (jnp.log(nf / max_exact) / math.log(MAX_DISTANCE / max_exact)
                         * (nb - max_exact)).astype(jnp.int32)
    large = jnp.minimum(large, nb - 1)
    return ret + jnp.where(n < max_exact, n, large)


def _bias_tiles(rel_bias):
    t = ATT_TILE
    p = 2 * t
    table = rel_bias.astype(F32).T
    far = table[:, N_BUCKETS // 2 - 1]
    w = jnp.arange(p, dtype=jnp.int32)[None, :]
    d = jnp.arange(2, dtype=jnp.int32)[:, None]
    rel = (t - 1) - w - d * t
    vec = (table[:, _t5_bucket(rel)] - far[:, None, None]) * LOG2E
    rows = jnp.tile(vec, (1, 1, t))[..., :t * (p - 1)].reshape(HEADS, 2, t, p - 1)
    bias = rows[..., t - 1:2 * t - 1]
    kk = jnp.arange(t, dtype=jnp.int32)[:, None]
    qq = jnp.arange(t, dtype=jnp.int32)[None, :]
    allowed = (kk // CHUNK <= qq // CHUNK)[None, None] | (d > 0)[None, :, :, None]
    bias = jnp.where(allowed, bias, NEG_BIG)
    return jnp.concatenate([bias, bias], axis=-1)


def _rope_tables(seq):
    pos = jnp.arange(seq, dtype=F32)
    inv = ROPE_THETA ** (-jnp.arange(0, MLA_ROPE_DIM, 2, dtype=F32) / MLA_ROPE_DIM)
    ang = pos[:, None] * inv[None, :]
    cos, sin = jnp.cos(ang), jnp.sin(ang)
    z = jnp.zeros((seq, LANES - MLA_ROPE_DIM), F32)
    return (jnp.concatenate([cos, cos, z], axis=-1), jnp.concatenate([-sin, sin, z], axis=-1))


def _swap_halves(w):
    half = w.shape[-1] // 2
    return jnp.concatenate([w[..., half:], w[..., :half]], axis=-1)


def _pack_params(w_in, lambda_q1, lambda_k1, lambda_q2, lambda_k2, subln_g, q_norm_g, w_uq,
                 kv_norm_g, w_ukv, w_o, ln1_g, ln1_b, w_mlp_in, w_mlp_out, ln2_g, ln2_b):
    depth = w_in.shape[0]
    o_k, o_v, o_cq = DIFF_WIDTH, 2 * DIFF_WIDTH, 3 * DIFF_WIDTH
    o_ckv = o_cq + MLA_Q_RANK
    o_kr = o_ckv + MLA_KV_RANK
    kr = w_in[..., o_kr:]
    w1 = jnp.concatenate([w_in[..., :o_k] * (DIFF_HEAD_DIM ** -0.5 * LOG2E), w_in[..., o_k:o_v],
                          w_in[..., o_cq:o_ckv], w_in[..., o_ckv:o_kr], kr, _swap_halves(kr)],
                         axis=-1)
    mla_scale = (MLA_NOPE_DIM + MLA_ROPE_DIM) ** -0.5 * LOG2E
    uq = (w_uq * mla_scale).reshape(depth, MLA_Q_RANK, HEADS, MLA_NOPE_DIM + MLA_ROPE_DIM)
    zq = jnp.zeros((depth, MLA_Q_RANK, HEADS, MLA_QK_PAD - MLA_NOPE_DIM - MLA_ROPE_DIM), F32)
    wuq = jnp.concatenate([uq, zq], axis=-1).reshape(depth, MLA_Q_RANK, HEADS * MLA_QK_PAD)
    wuqs = jnp.concatenate([_swap_halves(uq[..., MLA_NOPE_DIM:]), zq], axis=-1)
    wuqs = wuqs.reshape(depth, MLA_Q_RANK, HEADS * LANES)
    ukv = w_ukv.reshape(depth, MLA_KV_RANK, HEADS, MLA_NOPE_DIM + MLA_V_DIM)
    wukvk = ukv[..., :MLA_NOPE_DIM].reshape(depth, MLA_KV_RANK, HEADS * MLA_NOPE_DIM)
    wukvv = ukv[..., MLA_NOPE_DIM:].reshape(depth, MLA_KV_RANK, HEADS * MLA_V_DIM)
    row = lambda v: v.reshape(depth, 1, -1).astype(F32)
    return dict(
        w1=w1.astype(BF16), wvt=jnp.swapaxes(w_in[..., o_v:o_cq], 1, 2).astype(BF16),
        wuq=wuq.astype(BF16), wuqs=wuqs.astype(BF16), wukvk=wukvk.astype(BF16),
        wukvvt=jnp.swapaxes(wukvv, 1, 2).astype(BF16), gq=row(q_norm_g), gkv=row(kv_norm_g),
        subln=row(subln_g),
        lam=jnp.stack([lambda_q1, lambda_k1, lambda_q2, lambda_k2], axis=1).astype(F32),
        wo=w_o.astype(BF16), g1=row(ln1_g), b1=row(ln1_b), wi=w_mlp_in.astype(BF16),
        wout=w_mlp_out.astype(BF16), g2=row(ln2_g), b2=row(ln2_b))


def kernel(x, w_in, lambda_q1, lambda_k1, lambda_q2, lambda_k2, subln_g, q_norm_g, w_uq, kv_norm_g,
           w_ukv, rel_bias, w_o, ln1_g, ln1_b, w_mlp_in, w_mlp_out, ln2_g, ln2_b):
    batch, seq, _ = x.shape
    assert seq % PROJ_ROWS == 0 and seq % ATT_TILE == 0 and (batch * seq) % POST_ROWS == 0
    wp = _pack_params(w_in, lambda_q1, lambda_k1, lambda_q2, lambda_k2, subln_g, q_norm_g, w_uq,
                      kv_norm_g, w_ukv, w_o, ln1_g, ln1_b, w_mlp_in, w_mlp_out, ln2_g, ln2_b)
    wp["cos"], wp["sin"] = _rope_tables(seq)
    bias = _bias_tiles(rel_bias)
    x2 = x.reshape(batch * seq, D_MODEL)
    for l in range(DEPTH):
        lambda_init = 0.8 - 0.6 * math.exp(-0.3 * l)
        dq, dk, dvt, qm, km, vmt = _proj_call(x2, wp, l, seq)
        mix_a, mix_b = _attn_call(dq, dk, dvt, bias, wp, l, qm, km, vmt, seq, lambda_init)
        x2 = _post_call(x2, mix_a, mix_b, wp, l)
    return x2.reshape(batch, seq, D_MODEL)
```

```python
import functools
import math

import jax
import jax.numpy as jnp
from jax import lax
from jax.experimental import pallas as pl
from jax.experimental.pallas import tpu as pltpu

F32 = jnp.float32
BF16 = jnp.bfloat16

D_MODEL = 1024
DEPTH = 2
CHUNK = 64
HEADS = 4
DIFF_HEAD_DIM = 64
DIFF_V_DIM = 2 * DIFF_HEAD_DIM
DIFF_WIDTH = HEADS * DIFF_V_DIM
MLA_NOPE_DIM = 128
MLA_ROPE_DIM = 64
MLA_V_DIM = 128
MLA_Q_RANK = 256
MLA_KV_RANK = 128
MLA_WIDTH = HEADS * MLA_V_DIM
MLA_QK_PAD = 256
D_FF = 4 * D_MODEL
N_BUCKETS = 32
MAX_DISTANCE = 128
ROPE_THETA = 10000.0
ALPHA = (2 * DEPTH) ** 0.25
LN_EPS = 1e-5
RMS_EPS = 1e-6
NEG_BIG = -1e30
LOG2E = math.log2(math.e)

LANES = 128
BF16_SUBLANES = 16
ATT_TILE = 256
ATT_HEADS_PER_STEP = 2
PROJ_ROWS = 1024
POST_ROWS = 1024
POST_ROW_BLOCK = 256
FF_CHUNK = 1024
DIFF_LOOKAHEAD = 2
MLA_LOOKAHEAD = 4
VMEM_LIMIT = 56 * 1024 * 1024
assert ATT_TILE // 2 >= MAX_DISTANCE and ATT_TILE % (2 * CHUNK) == 0

C_DQ = 0
C_DK = C_DQ + DIFF_WIDTH
C_CQ = C_DK + DIFF_WIDTH
C_CKV = C_CQ + MLA_Q_RANK
C_KR = C_CKV + MLA_KV_RANK
W1_COLS = C_KR + LANES

_NT = (((1,), (1,)), ((), ()))


def _rms(x, g):
    return x * lax.rsqrt(jnp.mean(x * x, axis=-1, keepdims=True) + RMS_EPS) * g


def _layer_norm(x, g, b):
    mu = jnp.mean(x, axis=-1, keepdims=True)
    xc = x - mu
    var = jnp.mean(xc * xc, axis=-1, keepdims=True)
    return xc * lax.rsqrt(var + LN_EPS) * g + b


def _layer_spec(arr, l, **kwargs):
    index = (l,) + (0,) * (arr.ndim - 1)
    return pl.BlockSpec((1,) + arr.shape[1:], lambda *_: index, **kwargs)


def _proj_kernel(x_ref, w1_ref, wvt_ref, wuq_ref, wuqs_ref, wukvk_ref, wukvvt_ref,
                 gq_ref, gkv_ref, cos_ref, sin_ref,
                 dq_ref, dk_ref, dvt_ref, qm_ref, km_ref, vmt_ref):
    xb = x_ref[...].astype(BF16)
    h = jnp.dot(xb, w1_ref[0], preferred_element_type=F32)
    dq_ref[...] = h[:, C_DQ:C_DQ + DIFF_WIDTH].astype(BF16)
    dk_ref[...] = h[:, C_DK:C_DK + DIFF_WIDTH].astype(BF16)
    dvt_ref[0] = lax.dot_general(wvt_ref[0], xb, _NT, preferred_element_type=F32).astype(BF16)

    cos = cos_ref[...]
    sin = sin_ref[...]
    c_q = _rms(h[:, C_CQ:C_CQ + MLA_Q_RANK], gq_ref[0]).astype(BF16)
    qf = jnp.dot(c_q, wuq_ref[0], preferred_element_type=F32)
    qsw = jnp.dot(c_q, wuqs_ref[0], preferred_element_type=F32)
    c_kv = _rms(h[:, C_CKV:C_CKV + MLA_KV_RANK], gkv_ref[0]).astype(BF16)
    kn = jnp.dot(c_kv, wukvk_ref[0], preferred_element_type=F32)
    vmt_ref[0] = lax.dot_general(wukvvt_ref[0], c_kv, _NT,
                                 preferred_element_type=F32).astype(BF16)
    kr = h[:, C_KR:C_KR + LANES]
    k_rope = (kr * cos + pltpu.roll(kr, LANES // 2, 1) * sin).astype(BF16)

    for hh in range(HEADS):
        b0 = hh * MLA_QK_PAD
        qm_ref[:, b0:b0 + LANES] = qf[:, b0:b0 + LANES].astype(BF16)
        qm_ref[:, b0 + LANES:b0 + 2 * LANES] = (
            qf[:, b0 + LANES:b0 + 2 * LANES] * cos + qsw[:, hh * LANES:(hh + 1) * LANES] * sin
        ).astype(BF16)
        km_ref[:, b0:b0 + LANES] = kn[:, hh * LANES:(hh + 1) * LANES].astype(BF16)
        km_ref[:, b0 + LANES:b0 + 2 * LANES] = k_rope


def _proj_call(x2, wp, l, seq):
    ntok = x2.shape[0]
    batch = ntok // seq
    rows = PROJ_ROWS
    steps_per_seq = seq // rows

    def const(name):
        return _layer_spec(wp[name], l)

    def tok(width):
        return pl.BlockSpec((rows, width), lambda i: (i, 0))

    vt_spec = pl.BlockSpec((1, HEADS * LANES, rows),
                           lambda i: (i // steps_per_seq, 0, i % steps_per_seq))
    pos_spec = pl.BlockSpec((rows, LANES), lambda i: (i % steps_per_seq, 0))
    vt_shape = jax.ShapeDtypeStruct((batch, HEADS * LANES, seq), BF16)
    return pl.pallas_call(
        _proj_kernel,
        grid=(ntok // rows,),
        in_specs=[tok(D_MODEL), const("w1"), const("wvt"), const("wuq"), const("wuqs"),
                  const("wukvk"), const("wukvvt"), const("gq"), const("gkv"), pos_spec, pos_spec],
        out_specs=[tok(DIFF_WIDTH), tok(DIFF_WIDTH), vt_spec,
                   tok(HEADS * MLA_QK_PAD), tok(HEADS * MLA_QK_PAD), vt_spec],
        out_shape=[jax.ShapeDtypeStruct((ntok, DIFF_WIDTH), BF16),
                   jax.ShapeDtypeStruct((ntok, DIFF_WIDTH), BF16),
                   vt_shape,
                   jax.ShapeDtypeStruct((ntok, HEADS * MLA_QK_PAD), BF16),
                   jax.ShapeDtypeStruct((ntok, HEADS * MLA_QK_PAD), BF16),
                   vt_shape],
        compiler_params=pltpu.CompilerParams(dimension_semantics=("arbitrary",),
                                             vmem_limit_bytes=VMEM_LIMIT),
        name="proj",
    )(x2, wp["w1"], wp["wvt"], wp["wuq"], wp["wuqs"], wp["wukvk"], wp["wukvvt"],
      wp["gq"], wp["gkv"], wp["cos"], wp["sin"])


def _col_max(a, b):
    m = jnp.max(b, axis=0, keepdims=True)
    return m if a is None else jnp.maximum(a, m)


def _diag_probabilities(s, m):
    half = ATT_TILE // 2
    top, bot = s[:half], s[half:]
    m_top = _col_max(m, top)
    m_parts, e_parts = [], []
    for b in range(s.shape[1] // LANES):
        cols = slice(b * LANES, (b + 1) * LANES)
        if b % 2 == 0:
            m_parts.append(m_top[:, cols])
            e_parts.append(jnp.zeros((half, LANES), BF16))
        else:
            m_b = jnp.maximum(m_top[:, cols], jnp.max(bot[:, cols], axis=0, keepdims=True))
            m_parts.append(m_b)
            e_parts.append(jnp.exp2(bot[:, cols] - m_b).astype(BF16))
    m_new = jnp.concatenate(m_parts, axis=1)
    e_top = jnp.exp2(top - m_new).astype(BF16)
    return jnp.concatenate([e_top, jnp.concatenate(e_parts, axis=1)], axis=0), m_new


def _attend(heads, n_tiles, q_fn, score_fn, finish_fn, vt_fn, lookahead):
    ones = jnp.ones((BF16_SUBLANES, ATT_TILE), BF16)
    items = [(h, qi, c) for h in heads for qi in range(n_tiles) for c in range(qi + 1)]
    qs, scores = {}, {}

    def issue(idx):
        h, qi, c = items[idx]
        if c == 0:
            qs[h, qi] = q_fn(h, qi)
        scores[idx] = score_fn(h, qi, qs[h, qi], c)

    for idx in range(min(lookahead, len(items))):
        issue(idx)
    m, acc = None, None
    for idx, (h, qi, c) in enumerate(items):
        if idx + lookahead < len(items):
            issue(idx + lookahead)
        s = scores.pop(idx)
        if c == qi:
            e, m_new = _diag_probabilities(s, m)
        else:
            m_new = _col_max(m, s)
            e = jnp.exp2(s - m_new).astype(BF16)
        vt = jnp.concatenate([vt_fn(h, c), ones], axis=0)
        pv = jnp.dot(vt, e, preferred_element_type=F32)
        acc = pv if acc is None else acc * jnp.exp2(m - m_new) + pv
        m = m_new
        if c == qi:
            finish_fn(h, qi, acc[:LANES] / acc[LANES:LANES + 1])
            m, acc = None, None
        yield


def _diff_steps(heads, q_ref, k_ref, vt_ref, bias_ref, lam_ref, g_ref, o_ref, lambda_init):
    t = ATT_TILE
    lane = lax.broadcasted_iota(jnp.int32, (t, LANES), 1)
    lp = lam_ref[0]
    lam = (jnp.exp(jnp.sum(lp[0:1] * lp[1:2], axis=-1, keepdims=True))
           - jnp.exp(jnp.sum(lp[2:3] * lp[3:4], axis=-1, keepdims=True)) + lambda_init)
    g = g_ref[0]

    def q_rows(h, qi):
        q = q_ref[qi * t:(qi + 1) * t, h * LANES:(h + 1) * LANES]
        zero = jnp.zeros_like(q)
        return jnp.concatenate([jnp.where(lane < DIFF_HEAD_DIM, q, zero),
                                jnp.where(lane >= DIFF_HEAD_DIM, q, zero)], axis=0)

    def score(h, qi, qs, c):
        k = k_ref[c * t:(c + 1) * t, h * LANES:(h + 1) * LANES]
        s = lax.dot_general(k, qs, _NT, preferred_element_type=F32)
        if qi - c == 0:
            return s + bias_ref[h, 0]
        if qi - c == 1:
            return jnp.concatenate([s[:t // 2], s[t // 2:] + bias_ref[h, 1, t // 2:, :]], axis=0)
        return s

    def values(h, c):
        return vt_ref[0, h * LANES:(h + 1) * LANES, c * t:(c + 1) * t]

    def finish(h, qi, o):
        ot = o[:, :t] - lam * o[:, t:]
        y = _rms(ot.T, g) * (1.0 - lambda_init)
        o_ref[qi * t:(qi + 1) * t, h * LANES:(h + 1) * LANES] = y.astype(BF16)

    return _attend(heads, q_ref.shape[0] // t, q_rows, score, finish, values, DIFF_LOOKAHEAD)


def _mla_steps(heads, q_ref, k_ref, vt_ref, o_ref):
    t = ATT_TILE
    w = MLA_QK_PAD
    kc = lax.broadcasted_iota(jnp.int32, (t, t), 0) // CHUNK
    qc = lax.broadcasted_iota(jnp.int32, (t, t), 1) // CHUNK
    allowed = kc <= qc

    def q_rows(h, qi):
        return q_ref[qi * t:(qi + 1) * t, h * w:(h + 1) * w]

    def score(h, qi, q, c):
        k = k_ref[c * t:(c + 1) * t, h * w:(h + 1) * w]
        s = lax.dot_general(k, q, _NT, preferred_element_type=F32)
        return jnp.where(allowed, s, NEG_BIG) if c == qi else s

    def values(h, c):
        return vt_ref[0, h * LANES:(h + 1) * LANES, c * t:(c + 1) * t]

    def finish(h, qi, o):
        o_ref[qi * t:(qi + 1) * t, h * LANES:(h + 1) * LANES] = o.T.astype(BF16)

    return _attend(heads, q_ref.shape[0] // t, q_rows, score, finish, values, MLA_LOOKAHEAD)


def _attn_kernel(dq_ref, dk_ref, dvt_ref, bias_ref, lam_ref, g_ref, qm_ref, km_ref, vmt_ref,
                 oa_ref, ob_ref, *, lambda_init):
    heads = range(ATT_HEADS_PER_STEP)
    streams = [_diff_steps(heads, dq_ref, dk_ref, dvt_ref, bias_ref, lam_ref, g_ref, oa_ref,
                           lambda_init),
               _mla_steps(heads, qm_ref, km_ref, vmt_ref, ob_ref)]
    while streams:
        for s in list(streams):
            if next(s, "done") == "done":
                streams.remove(s)


def _attn_call(dq, dk, dvt, bias, wp, l, qm, km, vmt, seq, lambda_init):
    ntok = dq.shape[0]
    n = ATT_HEADS_PER_STEP

    def rows(width):
        return pl.BlockSpec((seq, n * width), lambda b, h: (b, h))

    vt_spec = pl.BlockSpec((1, n * LANES, seq), lambda b, h: (b, h, 0))
    return pl.pallas_call(
        functools.partial(_attn_kernel, lambda_init=lambda_init),
        grid=(ntok // seq, HEADS // n),
        in_specs=[rows(LANES), rows(LANES), vt_spec,
                  pl.BlockSpec((n,) + bias.shape[1:], lambda b, h: (h, 0, 0, 0)),
                  _layer_spec(wp["lam"], l), _layer_spec(wp["subln"], l),
                  rows(MLA_QK_PAD), rows(MLA_QK_PAD), vt_spec],
        out_specs=[rows(LANES), rows(LANES)],
        out_shape=[jax.ShapeDtypeStruct((ntok, DIFF_WIDTH), BF16),
                   jax.ShapeDtypeStruct((ntok, MLA_WIDTH), BF16)],
        compiler_params=pltpu.CompilerParams(dimension_semantics=("arbitrary", "arbitrary"),
                                             vmem_limit_bytes=VMEM_LIMIT),
        name="attn",
    )(dq, dk, dvt, bias, wp["lam"], wp["subln"], qm, km, vmt)


def _post_kernel(x_ref, ma_ref, mb_ref, wo_ref, g1_ref, b1_ref, wi_ref, wout_ref, g2_ref, b2_ref,
                 o_ref):
    n_blocks = x_ref.shape[0] // POST_ROW_BLOCK
    n_chunks = D_FF // FF_CHUNK

    def attn_out(r):
        rs = slice(r * POST_ROW_BLOCK, (r + 1) * POST_ROW_BLOCK)
        y = (jnp.dot(ma_ref[rs, :], wo_ref[0, :DIFF_WIDTH, :], preferred_element_type=F32)
             + jnp.dot(mb_ref[rs, :], wo_ref[0, DIFF_WIDTH:, :], preferred_element_type=F32))
        return _layer_norm(ALPHA * x_ref[rs, :] + y, g1_ref[0], b1_ref[0])

    def up(x1b, c):
        return jnp.dot(x1b, wi_ref[0, :, c * FF_CHUNK:(c + 1) * FF_CHUNK],
                       preferred_element_type=F32)

    x1 = attn_out(0)
    for r in range(n_blocks):
        x1_next = attn_out(r + 1) if r + 1 < n_blocks else None
        x1b = x1.astype(BF16)
        u = up(x1b, 0)
        y2 = None
        for c in range(n_chunks):
            u_next = up(x1b, c + 1) if c + 1 < n_chunks else None
            act = jnp.square(jnp.maximum(u, 0.0)).astype(BF16)
            d = jnp.dot(act, wout_ref[0, c * FF_CHUNK:(c + 1) * FF_CHUNK, :],
                        preferred_element_type=F32)
            y2 = d if y2 is None else y2 + d
            u = u_next
        o_ref[r * POST_ROW_BLOCK:(r + 1) * POST_ROW_BLOCK, :] = _layer_norm(
            ALPHA * x1 + y2, g2_ref[0], b2_ref[0])
        x1 = x1_next


def _post_call(x2, mix_a, mix_b, wp, l):
    ntok = x2.shape[0]
    rows = POST_ROWS

    def const(name):
        return _layer_spec(wp[name], l, pipeline_mode=pl.Buffered(1))

    def tok(width):
        return pl.BlockSpec((rows, width), lambda i: (i, 0))

    return pl.pallas_call(
        _post_kernel,
        grid=(ntok // rows,),
        in_specs=[tok(D_MODEL), tok(DIFF_WIDTH), tok(MLA_WIDTH), const("wo"), const("g1"),
                  const("b1"), const("wi"), const("wout"), const("g2"), const("b2")],
        out_specs=tok(D_MODEL),
        out_shape=jax.ShapeDtypeStruct((ntok, D_MODEL), F32),
        compiler_params=pltpu.CompilerParams(dimension_semantics=("arbitrary",),
                                             vmem_limit_bytes=VMEM_LIMIT),
        name="post",
    )(x2, mix_a, mix_b, wp["wo"], wp["g1"], wp["b1"], wp["wi"], wp["wout"], wp["g2"], wp["b2"])


def _t5_bucket(rel):
    nb = N_BUCKETS // 2
    ret = (rel > 0).astype(jnp.int32) * nb
    n = jnp.abs(rel)
    max_exact = nb // 2
    nf = jnp.maximum(n, 1).astype(F32)
    large = max_exact + (jnp.log(nf / max_exact) / math.log(MAX_DISTANCE / max_exact)
                         * (nb - max_exact)).astype(jnp.int32)
    large = jnp.minimum(large, nb - 1)
    return ret + jnp.where(n < max_exact, n, large)


def _bias_tiles(rel_bias):
    t = ATT_TILE
    p = 2 * t
    table = rel_bias.astype(F32).T
    far = table[:, N_BUCKETS // 2 - 1]
    w = jnp.arange(p, dtype=jnp.int32)[None, :]
    d = jnp.arange(2, dtype=jnp.int32)[:, None]
    rel = (t - 1) - w - d * t
    vec = (table[:, _t5_bucket(rel)] - far[:, None, None]) * LOG2E
    rows = jnp.tile(vec, (1, 1, t))[..., :t * (p - 1)].reshape(HEADS, 2, t, p - 1)
    bias = rows[..., t - 1:2 * t - 1]
    kk = jnp.arange(t, dtype=jnp.int32)[:, None]
    qq = jnp.arange(t, dtype=jnp.int32)[None, :]
    allowed = (kk // CHUNK <= qq // CHUNK)[None, None] | (d > 0)[None, :, :, None]
    bias = jnp.where(allowed, bias, NEG_BIG)
    return jnp.concatenate([bias, bias], axis=-1)


def _rope_tables(seq):
    pos = jnp.arange(seq, dtype=F32)
    inv = ROPE_THETA ** (-jnp.arange(0, MLA_ROPE_DIM, 2, dtype=F32) / MLA_ROPE_DIM)
    ang = pos[:, None] * inv[None, :]
    cos, sin = jnp.cos(ang), jnp.sin(ang)
    z = jnp.zeros((seq, LANES - MLA_ROPE_DIM), F32)
    return (jnp.concatenate([cos, cos, z], axis=-1), jnp.concatenate([-sin, sin, z], axis=-1))


def _swap_halves(w):
    half = w.shape[-1] // 2
    return jnp.concatenate([w[..., half:], w[..., :half]], axis=-1)


def _pack_params(w_in, lambda_q1, lambda_k1, lambda_q2, lambda_k2, subln_g, q_norm_g, w_uq,
                 kv_norm_g, w_ukv, w_o, ln1_g, ln1_b, w_mlp_in, w_mlp_out, ln2_g, ln2_b):
    depth = w_in.shape[0]
    o_k, o_v, o_cq = DIFF_WIDTH, 2 * DIFF_WIDTH, 3 * DIFF_WIDTH
    o_ckv = o_cq + MLA_Q_RANK
    o_kr = o_ckv + MLA_KV_RANK
    kr = w_in[..., o_kr:]
    w1 = jnp.concatenate([w_in[..., :o_k] * (DIFF_HEAD_DIM ** -0.5 * LOG2E), w_in[..., o_k:o_v],
                          w_in[..., o_cq:o_ckv], w_in[..., o_ckv:o_kr], kr, _swap_halves(kr)],
                         axis=-1)
    mla_scale = (MLA_NOPE_DIM + MLA_ROPE_DIM) ** -0.5 * LOG2E
    uq = (w_uq * mla_scale).reshape(depth, MLA_Q_RANK, HEADS, MLA_NOPE_DIM + MLA_ROPE_DIM)
    zq = jnp.zeros((depth, MLA_Q_RANK, HEADS, MLA_QK_PAD - MLA_NOPE_DIM - MLA_ROPE_DIM), F32)
    wuq = jnp.concatenate([uq, zq], axis=-1).reshape(depth, MLA_Q_RANK, HEADS * MLA_QK_PAD)
    wuqs = jnp.concatenate([_swap_halves(uq[..., MLA_NOPE_DIM:]), zq], axis=-1)
    wuqs = wuqs.reshape(depth, MLA_Q_RANK, HEADS * LANES)
    ukv = w_ukv.reshape(depth, MLA_KV_RANK, HEADS, MLA_NOPE_DIM + MLA_V_DIM)
    wukvk = ukv[..., :MLA_NOPE_DIM].reshape(depth, MLA_KV_RANK, HEADS * MLA_NOPE_DIM)
    wukvv = ukv[..., MLA_NOPE_DIM:].reshape(depth, MLA_KV_RANK, HEADS * MLA_V_DIM)
    row = lambda v: v.reshape(depth, 1, -1).astype(F32)
    return dict(
        w1=w1.astype(BF16), wvt=jnp.swapaxes(w_in[..., o_v:o_cq], 1, 2).astype(BF16),
        wuq=wuq.astype(BF16), wuqs=wuqs.astype(BF16), wukvk=wukvk.astype(BF16),
        wukvvt=jnp.swapaxes(wukvv, 1, 2).astype(BF16), gq=row(q_norm_g), gkv=row(kv_norm_g),
        subln=row(subln_g),
        lam=jnp.stack([lambda_q1, lambda_k1, lambda_q2, lambda_k2], axis=1).astype(F32),
        wo=w_o.astype(BF16), g1=row(ln1_g), b1=row(ln1_b), wi=w_mlp_in.astype(BF16),
        wout=w_mlp_out.astype(BF16), g2=row(ln2_g), b2=row(ln2_b))


def kernel(x, w_in, lambda_q1, lambda_k1, lambda_q2, lambda_k2, subln_g, q_norm_g, w_uq, kv_norm_g,
           w_ukv, rel_bias, w_o, ln1_g, ln1_b, w_mlp_in, w_mlp_out, ln2_g, ln2_b):
    batch, seq, _ = x.shape
    assert seq % PROJ_ROWS == 0 and seq % ATT_TILE == 0 and (batch * seq) % POST_ROWS == 0
    wp = _pack_params(w_in, lambda_q1, lambda_k1, lambda_q2, lambda_k2, subln_g, q_norm_g, w_uq,
                      kv_norm_g, w_ukv, w_o, ln1_g, ln1_b, w_mlp_in, w_mlp_out, ln2_g, ln2_b)
    wp["cos"], wp["sin"] = _rope_tables(seq)
    bias = _bias_tiles(rel_bias)
    x2 = x.reshape(batch * seq, D_MODEL)
    for l in range(DEPTH):
        lambda_init = 0.8 - 0.6 * math.exp(-0.3 * l)
        dq, dk, dvt, qm, km, vmt = _proj_call(x2, wp, l, seq)
        mix_a, mix_b = _attn_call(dq, dk, dvt, bias, wp, l, qm, km, vmt, seq, lambda_init)
        x2 = _post_call(x2, mix_a, mix_b, wp, l)
    return x2.reshape(batch, seq, D_MODEL)
```

```python
import functools
import math

import jax
import jax.numpy as jnp
from jax import lax
from jax.experimental import pallas as pl
from jax.experimental.pallas import tpu as pltpu

F32 = jnp.float32
BF16 = jnp.bfloat16

D_MODEL = 1024
DEPTH = 2
CHUNK = 64
HEADS = 4
DIFF_HEAD_DIM = 64
DIFF_V_DIM = 2 * DIFF_HEAD_DIM
DIFF_WIDTH = HEADS * DIFF_V_DIM
MLA_NOPE_DIM = 128
MLA_ROPE_DIM = 64
MLA_V_DIM = 128
MLA_Q_RANK = 256
MLA_KV_RANK = 128
MLA_WIDTH = HEADS * MLA_V_DIM
MLA_QK_PAD = 256
D_FF = 4 * D_MODEL
N_BUCKETS = 32
MAX_DISTANCE = 128
ROPE_THETA = 10000.0
ALPHA = (2 * DEPTH) ** 0.25
LN_EPS = 1e-5
RMS_EPS = 1e-6
NEG_BIG = -1e30
LOG2E = math.log2(math.e)

LANES = 128
BF16_SUBLANES = 16
ATT_TILE = 256
ATT_HEADS_PER_STEP = 2
PROJ_ROWS = 1024
POST_ROWS = 1024
POST_ROW_BLOCKS = (256, 256, 256, 128, 128)
FF_CHUNK = 1024
DIFF_LOOKAHEAD = 2
MLA_LOOKAHEAD = 4
VMEM_LIMIT = 56 * 1024 * 1024
assert ATT_TILE >= MAX_DISTANCE and ATT_TILE % CHUNK == 0

C_DQ = 0
C_DK = C_DQ + DIFF_WIDTH
C_CQ = C_DK + DIFF_WIDTH
C_CKV = C_CQ + MLA_Q_RANK
C_KR = C_CKV + MLA_KV_RANK
W1_COLS = C_KR + LANES

_NT = (((1,), (1,)), ((), ()))


def _rms(x, g):
    return x * lax.rsqrt(jnp.mean(x * x, axis=-1, keepdims=True) + RMS_EPS) * g


def _layer_norm(x, g, b):
    mu = jnp.mean(x, axis=-1, keepdims=True)
    xc = x - mu
    var = jnp.mean(xc * xc, axis=-1, keepdims=True)
    return xc * lax.rsqrt(var + LN_EPS) * g + b


def _layer_spec(arr, l, **kwargs):
    index = (l,) + (0,) * (arr.ndim - 1)
    return pl.BlockSpec((1,) + arr.shape[1:], lambda *_: index, **kwargs)


def _proj_kernel(x_ref, w1_ref, wvt_ref, wuq_ref, wuqs_ref, wukvk_ref, wukvvt_ref,
                 gq_ref, gkv_ref, rope_ref,
                 dq_ref, dk_ref, dvt_ref, qm_ref, km_ref, vmt_ref):
    xb = x_ref[...].astype(BF16)
    h = jnp.dot(xb, w1_ref[0], preferred_element_type=F32)
    dq_ref[...] = h[:, C_DQ:C_DQ + DIFF_WIDTH].astype(BF16)
    dk_ref[...] = h[:, C_DK:C_DK + DIFF_WIDTH].astype(BF16)
    dvt_ref[0] = lax.dot_general(wvt_ref[0], xb, _NT, preferred_element_type=F32).astype(BF16)

    cos = rope_ref[:, :LANES]
    sin = rope_ref[:, LANES:]
    c_q = _rms(h[:, C_CQ:C_CQ + MLA_Q_RANK], gq_ref[0]).astype(BF16)
    qf = jnp.dot(c_q, wuq_ref[0], preferred_element_type=F32)
    qsw = jnp.dot(c_q, wuqs_ref[0], preferred_element_type=F32)
    c_kv = _rms(h[:, C_CKV:C_CKV + MLA_KV_RANK], gkv_ref[0]).astype(BF16)
    kn = jnp.dot(c_kv, wukvk_ref[0], preferred_element_type=F32)
    vmt_ref[0] = lax.dot_general(wukvvt_ref[0], c_kv, _NT,
                                 preferred_element_type=F32).astype(BF16)
    kr = h[:, C_KR:C_KR + LANES]
    k_rope = (kr * cos + pltpu.roll(kr, LANES // 2, 1) * sin).astype(BF16)

    for hh in range(HEADS):
        b0 = hh * MLA_QK_PAD
        qm_ref[:, b0:b0 + LANES] = qf[:, b0:b0 + LANES].astype(BF16)
        qm_ref[:, b0 + LANES:b0 + 2 * LANES] = (
            qf[:, b0 + LANES:b0 + 2 * LANES] * cos + qsw[:, hh * LANES:(hh + 1) * LANES] * sin
        ).astype(BF16)
        km_ref[:, b0:b0 + LANES] = kn[:, hh * LANES:(hh + 1) * LANES].astype(BF16)
        km_ref[:, b0 + LANES:b0 + 2 * LANES] = k_rope


def _proj_call(x2, wp, l, seq):
    ntok = x2.shape[0]
    batch = ntok // seq
    rows = PROJ_ROWS
    steps_per_seq = seq // rows

    def const(name):
        return _layer_spec(wp[name], l)

    def tok(width):
        return pl.BlockSpec((rows, width), lambda i: (i, 0))

    vt_spec = pl.BlockSpec((1, HEADS * LANES, rows),
                           lambda i: (i // steps_per_seq, 0, i % steps_per_seq))
    pos_spec = pl.BlockSpec((rows, 2 * LANES), lambda i: (i % steps_per_seq, 0))
    vt_shape = jax.ShapeDtypeStruct((batch, HEADS * LANES, seq), BF16)
    return pl.pallas_call(
        _proj_kernel,
        grid=(ntok // rows,),
        in_specs=[tok(D_MODEL), const("w1"), const("wvt"), const("wuq"), const("wuqs"),
                  const("wukvk"), const("wukvvt"), const("gq"), const("gkv"), pos_spec],
        out_specs=[tok(DIFF_WIDTH), tok(DIFF_WIDTH), vt_spec,
                   tok(HEADS * MLA_QK_PAD), tok(HEADS * MLA_QK_PAD), vt_spec],
        out_shape=[jax.ShapeDtypeStruct((ntok, DIFF_WIDTH), BF16),
                   jax.ShapeDtypeStruct((ntok, DIFF_WIDTH), BF16),
                   vt_shape,
                   jax.ShapeDtypeStruct((ntok, HEADS * MLA_QK_PAD), BF16),
                   jax.ShapeDtypeStruct((ntok, HEADS * MLA_QK_PAD), BF16),
                   vt_shape],
        compiler_params=pltpu.CompilerParams(dimension_semantics=("arbitrary",),
                                             vmem_limit_bytes=VMEM_LIMIT),
        name="proj",
    )(x2, wp["w1"], wp["wvt"], wp["wuq"], wp["wuqs"], wp["wukvk"], wp["wukvvt"],
      wp["gq"], wp["gkv"], wp["rope"])


def _col_max(a, b):
    m = jnp.max(b, axis=0, keepdims=True)
    return m if a is None else jnp.maximum(a, m)


def _attend(n_heads, n_tiles, q_fn, score_fn, finish_fn, vt_fn, lookahead):
    ones = jnp.ones((BF16_SUBLANES, ATT_TILE), BF16)
    items = [(h, qi, c) for h in range(n_heads) for qi in range(n_tiles) for c in range(qi + 1)]
    qs, scores = {}, {}

    def issue(idx):
        h, qi, c = items[idx]
        if c == 0:
            qs[h, qi] = q_fn(h, qi)
        scores[idx] = score_fn(h, qi, qs[h, qi], c)

    for idx in range(min(lookahead, len(items))):
        issue(idx)
    m, acc = None, None
    for idx, (h, qi, c) in enumerate(items):
        if idx + lookahead < len(items):
            issue(idx + lookahead)
        s = scores.pop(idx)
        m_new = _col_max(m, s)
        e = jnp.exp2(s - m_new).astype(BF16)
        vt = jnp.concatenate([vt_fn(h, c), ones], axis=0)
        pv = jnp.dot(vt, e, preferred_element_type=F32)
        acc = pv if acc is None else acc * jnp.exp2(m - m_new) + pv
        m = m_new
        if c == qi:
            finish_fn(h, qi, acc[:LANES] / acc[LANES:LANES + 1])
            m, acc = None, None
        yield


def _diff_steps(q_ref, k_ref, vt_ref, bias_ref, lam_ref, g_ref, o_ref, lambda_init):
    t = ATT_TILE
    lane = lax.broadcasted_iota(jnp.int32, (t, LANES), 1)
    lp = lam_ref[0]
    lam = (jnp.exp(jnp.sum(lp[0:1] * lp[1:2], axis=-1, keepdims=True))
           - jnp.exp(jnp.sum(lp[2:3] * lp[3:4], axis=-1, keepdims=True)) + lambda_init)
    g = g_ref[0]

    def q_rows(h, qi):
        q = q_ref[qi * t:(qi + 1) * t, h * LANES:(h + 1) * LANES]
        zero = jnp.zeros_like(q)
        return jnp.concatenate([jnp.where(lane < DIFF_HEAD_DIM, q, zero),
                                jnp.where(lane >= DIFF_HEAD_DIM, q, zero)], axis=0)

    def score(h, qi, qs, c):
        k = k_ref[c * t:(c + 1) * t, h * LANES:(h + 1) * LANES]
        s = lax.dot_general(k, qs, _NT, preferred_element_type=F32)
        return s + bias_ref[h, qi - c] if qi - c <= 1 else s

    def values(h, c):
        return vt_ref[0, h * LANES:(h + 1) * LANES, c * t:(c + 1) * t]

    def finish(h, qi, o):
        ot = o[:, :t] - lam * o[:, t:]
        y = _rms(ot.T, g) * (1.0 - lambda_init)
        o_ref[qi * t:(qi + 1) * t, h * LANES:(h + 1) * LANES] = y.astype(BF16)

    return _attend(ATT_HEADS_PER_STEP, q_ref.shape[0] // t, q_rows, score, finish, values,
                   DIFF_LOOKAHEAD)


def _mla_steps(q_ref, k_ref, vt_ref, o_ref):
    t = ATT_TILE
    w = MLA_QK_PAD
    kc = lax.broadcasted_iota(jnp.int32, (t, t), 0) // CHUNK
    qc = lax.broadcasted_iota(jnp.int32, (t, t), 1) // CHUNK
    allowed = kc <= qc

    def q_rows(h, qi):
        return q_ref[qi * t:(qi + 1) * t, h * w:(h + 1) * w]

    def score(h, qi, q, c):
        k = k_ref[c * t:(c + 1) * t, h * w:(h + 1) * w]
        s = lax.dot_general(k, q, _NT, preferred_element_type=F32)
        return jnp.where(allowed, s, NEG_BIG) if c == qi else s

    def values(h, c):
        return vt_ref[0, h * LANES:(h + 1) * LANES, c * t:(c + 1) * t]

    def finish(h, qi, o):
        o_ref[qi * t:(qi + 1) * t, h * LANES:(h + 1) * LANES] = o.T.astype(BF16)

    return _attend(ATT_HEADS_PER_STEP, q_ref.shape[0] // t, q_rows, score, finish, values,
                   MLA_LOOKAHEAD)


def _attn_kernel(dq_ref, dk_ref, dvt_ref, bias_ref, lam_ref, g_ref, qm_ref, km_ref, vmt_ref,
                 oa_ref, ob_ref, *, lambda_init):
    streams = [_diff_steps(dq_ref, dk_ref, dvt_ref, bias_ref, lam_ref, g_ref, oa_ref, lambda_init),
               _mla_steps(qm_ref, km_ref, vmt_ref, ob_ref)]
    while streams:
        for s in list(streams):
            if next(s, "done") == "done":
                streams.remove(s)


def _attn_call(dq, dk, dvt, bias, wp, l, qm, km, vmt, seq, lambda_init):
    ntok = dq.shape[0]
    n = ATT_HEADS_PER_STEP

    def rows(width):
        return pl.BlockSpec((seq, n * width), lambda b, h: (b, h))

    vt_spec = pl.BlockSpec((1, n * LANES, seq), lambda b, h: (b, h, 0))
    return pl.pallas_call(
        functools.partial(_attn_kernel, lambda_init=lambda_init),
        grid=(ntok // seq, HEADS // n),
        in_specs=[rows(LANES), rows(LANES), vt_spec,
                  pl.BlockSpec((n,) + bias.shape[1:], lambda b, h: (h, 0, 0, 0)),
                  _layer_spec(wp["lam"], l), _layer_spec(wp["subln"], l),
                  rows(MLA_QK_PAD), rows(MLA_QK_PAD), vt_spec],
        out_specs=[rows(LANES), rows(LANES)],
        out_shape=[jax.ShapeDtypeStruct((ntok, DIFF_WIDTH), BF16),
                   jax.ShapeDtypeStruct((ntok, MLA_WIDTH), BF16)],
        compiler_params=pltpu.CompilerParams(dimension_semantics=("arbitrary", "arbitrary"),
                                             vmem_limit_bytes=VMEM_LIMIT),
        name="attn",
    )(dq, dk, dvt, bias, wp["lam"], wp["subln"], qm, km, vmt)


def _post_kernel(x_ref, ma_ref, mb_ref, wo_ref, g1_ref, b1_ref, wi_ref, wout_ref, g2_ref, b2_ref,
                 o_ref):
    starts = [sum(POST_ROW_BLOCKS[:r]) for r in range(len(POST_ROW_BLOCKS) + 1)]
    n_blocks = len(POST_ROW_BLOCKS)
    n_chunks = D_FF // FF_CHUNK

    def attn_out(r):
        rs = slice(starts[r], starts[r + 1])
        y = (jnp.dot(ma_ref[rs, :], wo_ref[0, :DIFF_WIDTH, :], preferred_element_type=F32)
             + jnp.dot(mb_ref[rs, :], wo_ref[0, DIFF_WIDTH:, :], preferred_element_type=F32))
        return _layer_norm(ALPHA * x_ref[rs, :] + y, g1_ref[0], b1_ref[0])

    def up(x1b, c):
        return jnp.dot(x1b, wi_ref[0, :, c * FF_CHUNK:(c + 1) * FF_CHUNK],
                       preferred_element_type=F32)

    x1 = attn_out(0)
    for r in range(n_blocks):
        x1_next = attn_out(r + 1) if r + 1 < n_blocks else None
        x1b = x1.astype(BF16)
        u = up(x1b, 0)
        y2 = None
        for c in range(n_chunks):
            u_next = up(x1b, c + 1) if c + 1 < n_chunks else None
            act = jnp.square(jnp.maximum(u, 0.0)).astype(BF16)
            d = jnp.dot(act, wout_ref[0, c * FF_CHUNK:(c + 1) * FF_CHUNK, :],
                        preferred_element_type=F32)
            y2 = d if y2 is None else y2 + d
            u = u_next
        o_ref[starts[r]:starts[r + 1], :] = _layer_norm(
            ALPHA * x1 + y2, g2_ref[0], b2_ref[0])
        x1 = x1_next


def _post_call(x2, mix_a, mix_b, wp, l):
    ntok = x2.shape[0]
    rows = POST_ROWS

    def const(name):
        return _layer_spec(wp[name], l, pipeline_mode=pl.Buffered(1))

    def tok(width):
        return pl.BlockSpec((rows, width), lambda i: (i, 0))

    return pl.pallas_call(
        _post_kernel,
        grid=(ntok // rows,),
        in_specs=[tok(D_MODEL), tok(DIFF_WIDTH), tok(MLA_WIDTH), const("wo"), const("g1"),
                  const("b1"), const("wi"), const("wout"), const("g2"), const("b2")],
        out_specs=tok(D_MODEL),
        out_shape=jax.ShapeDtypeStruct((ntok, D_MODEL), F32),
        compiler_params=pltpu.CompilerParams(dimension_semantics=("arbitrary",),
                                             vmem_limit_bytes=VMEM_LIMIT),
        name="post",
    )(x2, mix_a, mix_b, wp["wo"], wp["g1"], wp["b1"], wp["wi"], wp["wout"], wp["g2"], wp["b2"])


def _t5_bucket(rel):
    nb = N_BUCKETS // 2
    ret = (rel > 0).astype(jnp.int32) * nb
    n = jnp.abs(rel)
    max_exact = nb // 2
    nf = jnp.maximum(n, 1).astype(F32)
    large = max_exact + (jnp.log(nf / max_exact) / math.log(MAX_DISTANCE / max_exact)
                         * (nb - max_exact)).astype(jnp.int32)
    large = jnp.minimum(large, nb - 1)
    return ret + jnp.where(n < max_exact, n, large)


def _bias_tiles(rel_bias):
    t = ATT_TILE
    p = 2 * t
    table = rel_bias.astype(F32).T
    far = table[:, N_BUCKETS // 2 - 1]
    w = jnp.arange(p, dtype=jnp.int32)[None, :]
    d = jnp.arange(2, dtype=jnp.int32)[:, None]
    rel = (t - 1) - w - d * t
    vec = (table[:, _t5_bucket(rel)] - far[:, None, None]) * LOG2E
    rows = jnp.tile(vec, (1, 1, t))[..., :t * (p - 1)].reshape(HEADS, 2, t, p - 1)
    bias = rows[..., t - 1:2 * t - 1]
    kk = jnp.arange(t, dtype=jnp.int32)[:, None]
    qq = jnp.arange(t, dtype=jnp.int32)[None, :]
    allowed = (kk // CHUNK <= qq // CHUNK)[None, None] | (d > 0)[None, :, :, None]
    bias = jnp.where(allowed, bias, NEG_BIG)
    return jnp.concatenate([bias, bias], axis=-1)


def _rope_tables(seq):
    pos = jnp.arange(seq, dtype=F32)
    inv = ROPE_THETA ** (-jnp.arange(0, MLA_ROPE_DIM, 2, dtype=F32) / MLA_ROPE_DIM)
    ang = pos[:, None] * inv[None, :]
    cos, sin = jnp.cos(ang), jnp.sin(ang)
    z = jnp.zeros((seq, LANES - MLA_ROPE_DIM), F32)
    return jnp.concatenate([cos, cos, z, -sin, sin, z], axis=-1)


def _swap_halves(w):
    half = w.shape[-1] // 2
    return jnp.concatenate([w[..., half:], w[..., :half]], axis=-1)


def _pack_params(w_in, lambda_q1, lambda_k1, lambda_q2, lambda_k2, subln_g, q_norm_g, w_uq,
                 kv_norm_g, w_ukv, w_o, ln1_g, ln1_b, w_mlp_in, w_mlp_out, ln2_g, ln2_b):
    depth = w_in.shape[0]
    o_k, o_v, o_cq = DIFF_WIDTH, 2 * DIFF_WIDTH, 3 * DIFF_WIDTH
    o_ckv = o_cq + MLA_Q_RANK
    o_kr = o_ckv + MLA_KV_RANK
    kr = w_in[..., o_kr:]
    w1 = jnp.concatenate([w_in[..., :o_k] * (DIFF_HEAD_DIM ** -0.5 * LOG2E), w_in[..., o_k:o_v],
                          w_in[..., o_cq:o_ckv], w_in[..., o_ckv:o_kr], kr, _swap_halves(kr)],
                         axis=-1)
    mla_scale = (MLA_NOPE_DIM + MLA_ROPE_DIM) ** -0.5 * LOG2E
    uq = (w_uq * mla_scale).reshape(depth, MLA_Q_RANK, HEADS, MLA_NOPE_DIM + MLA_ROPE_DIM)
    zq = jnp.zeros((depth, MLA_Q_RANK, HEADS, MLA_QK_PAD - MLA_NOPE_DIM - MLA_ROPE_DIM), F32)
    wuq = jnp.concatenate([uq, zq], axis=-1).reshape(depth, MLA_Q_RANK, HEADS * MLA_QK_PAD)
    wuqs = jnp.concatenate([_swap_halves(uq[..., MLA_NOPE_DIM:]), zq], axis=-1)
    wuqs = wuqs.reshape(depth, MLA_Q_RANK, HEADS * LANES)
    ukv = w_ukv.reshape(depth, MLA_KV_RANK, HEADS, MLA_NOPE_DIM + MLA_V_DIM)
    wukvk = ukv[..., :MLA_NOPE_DIM].reshape(depth, MLA_KV_RANK, HEADS * MLA_NOPE_DIM)
    wukvv = ukv[..., MLA_NOPE_DIM:].reshape(depth, MLA_KV_RANK, HEADS * MLA_V_DIM)
    row = lambda v: v.reshape(depth, 1, -1).astype(F32)
    return dict(
        w1=w1.astype(BF16), wvt=jnp.swapaxes(w_in[..., o_v:o_cq].astype(BF16), 1, 2),
        wuq=wuq.astype(BF16), wuqs=wuqs.astype(BF16), wukvk=wukvk.astype(BF16),
        wukvvt=jnp.swapaxes(wukvv.astype(BF16), 1, 2), gq=row(q_norm_g), gkv=row(kv_norm_g),
        subln=row(subln_g),
        lam=jnp.stack([lambda_q1, lambda_k1, lambda_q2, lambda_k2], axis=1).astype(F32),
        wo=w_o.astype(BF16), g1=row(ln1_g), b1=row(ln1_b), wi=w_mlp_in.astype(BF16),
        wout=w_mlp_out.astype(BF16), g2=row(ln2_g), b2=row(ln2_b))


def kernel(x, w_in, lambda_q1, lambda_k1, lambda_q2, lambda_k2, subln_g, q_norm_g, w_uq, kv_norm_g,
           w_ukv, rel_bias, w_o, ln1_g, ln1_b, w_mlp_in, w_mlp_out, ln2_g, ln2_b):
    batch, seq, _ = x.shape
    assert seq % PROJ_ROWS == 0 and seq % ATT_TILE == 0 and (batch * seq) % POST_ROWS == 0
    assert sum(POST_ROW_BLOCKS) == POST_ROWS
    wp = _pack_params(w_in, lambda_q1, lambda_k1, lambda_q2, lambda_k2, subln_g, q_norm_g, w_uq,
                      kv_norm_g, w_ukv, w_o, ln1_g, ln1_b, w_mlp_in, w_mlp_out, ln2_g, ln2_b)
    wp["rope"] = _rope_tables(seq)
    bias = _bias_tiles(rel_bias)
    x2 = x.reshape(batch * seq, D_MODEL)
    for l in range(DEPTH):
        lambda_init = 0.8 - 0.6 * math.exp(-0.3 * l)
        dq, dk, dvt, qm, km, vmt = _proj_call(x2, wp, l, seq)
        mix_a, mix_b = _attn_call(dq, dk, dvt, bias, wp, l, qm, km, vmt, seq, lambda_init)
        x2 = _post_call(x2, mix_a, mix_b, wp, l)
    return x2.reshape(batch, seq, D_MODEL)
```

```python
import functools
import math

import jax
import jax.numpy as jnp
from jax import lax
from jax.experimental import pallas as pl
from jax.experimental.pallas import tpu as pltpu

F32 = jnp.float32
BF16 = jnp.bfloat16

D_MODEL = 1024
DEPTH = 2
CHUNK = 64
HEADS = 4
DIFF_HEAD_DIM = 64
DIFF_V_DIM = 2 * DIFF_HEAD_DIM
DIFF_WIDTH = HEADS * DIFF_V_DIM
MLA_NOPE_DIM = 128
MLA_ROPE_DIM = 64
MLA_V_DIM = 128
MLA_Q_RANK = 256
MLA_KV_RANK = 128
MLA_WIDTH = HEADS * MLA_V_DIM
MLA_QK_PAD = 256
D_FF = 4 * D_MODEL
N_BUCKETS = 32
MAX_DISTANCE = 128
ROPE_THETA = 10000.0
ALPHA = (2 * DEPTH) ** 0.25
LN_EPS = 1e-5
RMS_EPS = 1e-6
NEG_BIG = -1e30
LOG2E = math.log2(math.e)

LANES = 128
BF16_SUBLANES = 16
ATT_TILE = 256
ATT_HEADS_PER_STEP = 2
PROJ_ROWS = 1024
POST_ROWS = 1024
POST_ROW_BLOCK = 256
FF_CHUNK = 512
DIFF_LOOKAHEAD = 2
MLA_LOOKAHEAD = 4
VMEM_LIMIT = 56 * 1024 * 1024
assert ATT_TILE >= MAX_DISTANCE and ATT_TILE % CHUNK == 0

C_DQ = 0
C_DK = C_DQ + DIFF_WIDTH
C_CQ = C_DK + DIFF_WIDTH
C_CKV = C_CQ + MLA_Q_RANK
C_KR = C_CKV + MLA_KV_RANK
W1_COLS = C_KR + LANES

_NT = (((1,), (1,)), ((), ()))


def _rms(x, g):
    return x * lax.rsqrt(jnp.mean(x * x, axis=-1, keepdims=True) + RMS_EPS) * g


def _layer_norm(x, g, b):
    mu = jnp.mean(x, axis=-1, keepdims=True)
    xc = x - mu
    var = jnp.mean(xc * xc, axis=-1, keepdims=True)
    return xc * lax.rsqrt(var + LN_EPS) * g + b


def _layer_spec(arr, l, **kwargs):
    index = (l,) + (0,) * (arr.ndim - 1)
    return pl.BlockSpec((1,) + arr.shape[1:], lambda *_: index, **kwargs)


def _proj_kernel(x_ref, w1_ref, wvt_ref, wuq_ref, wuqs_ref, wukvk_ref, wukvvt_ref,
                 gq_ref, gkv_ref, cos_ref, sin_ref,
                 dq_ref, dk_ref, dvt_ref, qm_ref, km_ref, vmt_ref):
    xb = x_ref[...].astype(BF16)
    h = jnp.dot(xb, w1_ref[0], preferred_element_type=F32)
    dq_ref[...] = h[:, C_DQ:C_DQ + DIFF_WIDTH].astype(BF16)
    dk_ref[...] = h[:, C_DK:C_DK + DIFF_WIDTH].astype(BF16)
    dvt_ref[0] = lax.dot_general(wvt_ref[0], xb, _NT, preferred_element_type=F32).astype(BF16)

    cos = cos_ref[...]
    sin = sin_ref[...]
    c_q = _rms(h[:, C_CQ:C_CQ + MLA_Q_RANK], gq_ref[0]).astype(BF16)
    qf = jnp.dot(c_q, wuq_ref[0], preferred_element_type=F32)
    qsw = jnp.dot(c_q, wuqs_ref[0], preferred_element_type=F32)
    c_kv = _rms(h[:, C_CKV:C_CKV + MLA_KV_RANK], gkv_ref[0]).astype(BF16)
    kn = jnp.dot(c_kv, wukvk_ref[0], preferred_element_type=F32)
    vmt_ref[0] = lax.dot_general(wukvvt_ref[0], c_kv, _NT,
                                 preferred_element_type=F32).astype(BF16)
    kr = h[:, C_KR:C_KR + LANES]
    k_rope = (kr * cos + pltpu.roll(kr, LANES // 2, 1) * sin).astype(BF16)

    for hh in range(HEADS):
        b0 = hh * MLA_QK_PAD
        qm_ref[:, b0:b0 + LANES] = qf[:, b0:b0 + LANES].astype(BF16)
        qm_ref[:, b0 + LANES:b0 + 2 * LANES] = (
            qf[:, b0 + LANES:b0 + 2 * LANES] * cos + qsw[:, hh * LANES:(hh + 1) * LANES] * sin
        ).astype(BF16)
        km_ref[:, b0:b0 + LANES] = kn[:, hh * LANES:(hh + 1) * LANES].astype(BF16)
        km_ref[:, b0 + LANES:b0 + 2 * LANES] = k_rope


def _proj_call(x2, wp, l, seq):
    ntok = x2.shape[0]
    batch = ntok // seq
    rows = PROJ_ROWS
    steps_per_seq = seq // rows

    def const(name):
        return _layer_spec(wp[name], l)

    def tok(width):
        return pl.BlockSpec((rows, width), lambda i: (i, 0))

    vt_spec = pl.BlockSpec((1, HEADS * LANES, rows),
                           lambda i: (i // steps_per_seq, 0, i % steps_per_seq))
    pos_spec = pl.BlockSpec((rows, LANES), lambda i: (i % steps_per_seq, 0))
    vt_shape = jax.ShapeDtypeStruct((batch, HEADS * LANES, seq), BF16)
    return pl.pallas_call(
        _proj_kernel,
        grid=(ntok // rows,),
        in_specs=[tok(D_MODEL), const("w1"), const("wvt"), const("wuq"), const("wuqs"),
                  const("wukvk"), const("wukvvt"), const("gq"), const("gkv"), pos_spec, pos_spec],
        out_specs=[tok(DIFF_WIDTH), tok(DIFF_WIDTH), vt_spec,
                   tok(HEADS * MLA_QK_PAD), tok(HEADS * MLA_QK_PAD), vt_spec],
        out_shape=[jax.ShapeDtypeStruct((ntok, DIFF_WIDTH), BF16),
                   jax.ShapeDtypeStruct((ntok, DIFF_WIDTH), BF16),
                   vt_shape,
                   jax.ShapeDtypeStruct((ntok, HEADS * MLA_QK_PAD), BF16),
                   jax.ShapeDtypeStruct((ntok, HEADS * MLA_QK_PAD), BF16),
                   vt_shape],
        compiler_params=pltpu.CompilerParams(dimension_semantics=("arbitrary",),
                                             vmem_limit_bytes=VMEM_LIMIT),
        name="proj",
    )(x2, wp["w1"], wp["wvt"], wp["wuq"], wp["wuqs"], wp["wukvk"], wp["wukvvt"],
      wp["gq"], wp["gkv"], wp["cos"], wp["sin"])


def _col_max(a, b):
    m = jnp.max(b, axis=0, keepdims=True)
    return m if a is None else jnp.maximum(a, m)


def _attend(n_heads, n_tiles, q_fn, score_fn, finish_fn, vt_fn, lookahead):
    ones = jnp.ones((BF16_SUBLANES, ATT_TILE), BF16)
    items = [(h, qi, c) for h in range(n_heads) for qi in range(n_tiles) for c in range(qi + 1)]
    qs, scores = {}, {}

    def issue(idx):
        h, qi, c = items[idx]
        if c == 0:
            qs[h, qi] = q_fn(h, qi)
        scores[idx] = score_fn(h, qi, qs[h, qi], c)

    for idx in range(min(lookahead, len(items))):
        issue(idx)
    m, acc = None, None
    for idx, (h, qi, c) in enumerate(items):
        if idx + lookahead < len(items):
            issue(idx + lookahead)
        s = scores.pop(idx)
        m_new = _col_max(m, s)
        e = jnp.exp2(s - m_new).astype(BF16)
        vt = jnp.concatenate([vt_fn(h, c), ones], axis=0)
        pv = jnp.dot(vt, e, preferred_element_type=F32)
        acc = pv if acc is None else acc * jnp.exp2(m - m_new) + pv
        m = m_new
        if c == qi:
            finish_fn(h, qi, acc[:LANES] / acc[LANES:LANES + 1])
            m, acc = None, None
        yield


def _diff_steps(q_ref, k_ref, vt_ref, bias_ref, lam_ref, g_ref, o_ref, lambda_init):
    t = ATT_TILE
    lane = lax.broadcasted_iota(jnp.int32, (t, LANES), 1)
    lp = lam_ref[0]
    lam = (jnp.exp(jnp.sum(lp[0:1] * lp[1:2], axis=-1, keepdims=True))
           - jnp.exp(jnp.sum(lp[2:3] * lp[3:4], axis=-1, keepdims=True)) + lambda_init)
    g = g_ref[0]

    def q_rows(h, qi):
        q = q_ref[qi * t:(qi + 1) * t, h * LANES:(h + 1) * LANES]
        zero = jnp.zeros_like(q)
        return jnp.concatenate([jnp.where(lane < DIFF_HEAD_DIM, q, zero),
                                jnp.where(lane >= DIFF_HEAD_DIM, q, zero)], axis=0)

    def score(h, qi, qs, c):
        k = k_ref[c * t:(c + 1) * t, h * LANES:(h + 1) * LANES]
        s = lax.dot_general(k, qs, _NT, preferred_element_type=F32)
        return s + bias_ref[h, qi - c] if qi - c <= 1 else s

    def values(h, c):
        return vt_ref[0, h * LANES:(h + 1) * LANES, c * t:(c + 1) * t]

    def finish(h, qi, o):
        ot = o[:, :t] - lam * o[:, t:]
        y = _rms(ot.T, g) * (1.0 - lambda_init)
        o_ref[qi * t:(qi + 1) * t, h * LANES:(h + 1) * LANES] = y.astype(BF16)

    return _attend(ATT_HEADS_PER_STEP, q_ref.shape[0] // t, q_rows, score, finish, values,
                   DIFF_LOOKAHEAD)


def _mla_steps(q_ref, k_ref, vt_ref, o_ref):
    t = ATT_TILE
    w = MLA_QK_PAD
    kc = lax.broadcasted_iota(jnp.int32, (t, t), 0) // CHUNK
    qc = lax.broadcasted_iota(jnp.int32, (t, t), 1) // CHUNK
    allowed = kc <= qc

    def q_rows(h, qi):
        return q_ref[qi * t:(qi + 1) * t, h * w:(h + 1) * w]

    def score(h, qi, q, c):
        k = k_ref[c * t:(c + 1) * t, h * w:(h + 1) * w]
        s = lax.dot_general(k, q, _NT, preferred_element_type=F32)
        return jnp.where(allowed, s, NEG_BIG) if c == qi else s

    def values(h, c):
        return vt_ref[0, h * LANES:(h + 1) * LANES, c * t:(c + 1) * t]

    def finish(h, qi, o):
        o_ref[qi * t:(qi + 1) * t, h * LANES:(h + 1) * LANES] = o.T.astype(BF16)

    return _attend(ATT_HEADS_PER_STEP, q_ref.shape[0] // t, q_rows, score, finish, values,
                   MLA_LOOKAHEAD)


def _attn_kernel(dq_ref, dk_ref, dvt_ref, bias_ref, lam_ref, g_ref, qm_ref, km_ref, vmt_ref,
                 oa_ref, ob_ref, *, lambda_init):
    streams = [_diff_steps(dq_ref, dk_ref, dvt_ref, bias_ref, lam_ref, g_ref, oa_ref, lambda_init),
               _mla_steps(qm_ref, km_ref, vmt_ref, ob_ref)]
    while streams:
        for s in list(streams):
            if next(s, "done") == "done":
                streams.remove(s)


def _attn_call(dq, dk, dvt, bias, wp, l, qm, km, vmt, seq, lambda_init):
    ntok = dq.shape[0]
    n = ATT_HEADS_PER_STEP

    def rows(width):
        return pl.BlockSpec((seq, n * width), lambda b, h: (b, h))

    vt_spec = pl.BlockSpec((1, n * LANES, seq), lambda b, h: (b, h, 0))
    return pl.pallas_call(
        functools.partial(_attn_kernel, lambda_init=lambda_init),
        grid=(ntok // seq, HEADS // n),
        in_specs=[rows(LANES), rows(LANES), vt_spec,
                  pl.BlockSpec((n,) + bias.shape[1:], lambda b, h: (h, 0, 0, 0)),
                  _layer_spec(wp["lam"], l), _layer_spec(wp["subln"], l),
                  rows(MLA_QK_PAD), rows(MLA_QK_PAD), vt_spec],
        out_specs=[rows(LANES), rows(LANES)],
        out_shape=[jax.ShapeDtypeStruct((ntok, DIFF_WIDTH), BF16),
                   jax.ShapeDtypeStruct((ntok, MLA_WIDTH), BF16)],
        compiler_params=pltpu.CompilerParams(dimension_semantics=("arbitrary", "arbitrary"),
                                             vmem_limit_bytes=VMEM_LIMIT),
        name="attn",
    )(dq, dk, dvt, bias, wp["lam"], wp["subln"], qm, km, vmt)


def _post_kernel(x_ref, ma_ref, mb_ref, wo_ref, g1_ref, b1_ref, wi_ref, wout_ref, g2_ref, b2_ref,
                 o_ref):
    n_blocks = x_ref.shape[0] // POST_ROW_BLOCK
    n_chunks = D_FF // FF_CHUNK

    def attn_out(r):
        rs = slice(r * POST_ROW_BLOCK, (r + 1) * POST_ROW_BLOCK)
        y = (jnp.dot(ma_ref[rs, :], wo_ref[0, :DIFF_WIDTH, :], preferred_element_type=F32)
             + jnp.dot(mb_ref[rs, :], wo_ref[0, DIFF_WIDTH:, :], preferred_element_type=F32))
        return _layer_norm(ALPHA * x_ref[rs, :] + y, g1_ref[0], b1_ref[0])

    def up(x1b, c):
        return jnp.dot(x1b, wi_ref[0, :, c * FF_CHUNK:(c + 1) * FF_CHUNK],
                       preferred_element_type=F32)

    x1 = attn_out(0)
    for r in range(n_blocks):
        x1_next = attn_out(r + 1) if r + 1 < n_blocks else None
        x1b = x1.astype(BF16)
        u = up(x1b, 0)
        y2 = None
        for c in range(n_chunks):
            u_next = up(x1b, c + 1) if c + 1 < n_chunks else None
            act = jnp.square(jnp.maximum(u, 0.0)).astype(BF16)
            d = jnp.dot(act, wout_ref[0, c * FF_CHUNK:(c + 1) * FF_CHUNK, :],
                        preferred_element_type=F32)
            y2 = d if y2 is None else y2 + d
            u = u_next
        o_ref[r * POST_ROW_BLOCK:(r + 1) * POST_ROW_BLOCK, :] = _layer_norm(
            ALPHA * x1 + y2, g2_ref[0], b2_ref[0])
        x1 = x1_next


def _post_call(x2, mix_a, mix_b, wp, l):
    ntok = x2.shape[0]
    rows = POST_ROWS

    def const(name):
        return _layer_spec(wp[name], l, pipeline_mode=pl.Buffered(1))

    def tok(width):
        return pl.BlockSpec((rows, width), lambda i: (i, 0))

    return pl.pallas_call(
        _post_kernel,
        grid=(ntok // rows,),
        in_specs=[tok(D_MODEL), tok(DIFF_WIDTH), tok(MLA_WIDTH), const("wo"), const("g1"),
                  const("b1"), const("wi"), const("wout"), const("g2"), const("b2")],
        out_specs=tok(D_MODEL),
        out_shape=jax.ShapeDtypeStruct((ntok, D_MODEL), F32),
        compiler_params=pltpu.CompilerParams(dimension_semantics=("arbitrary",),
                                             vmem_limit_bytes=VMEM_LIMIT),
        name="post",
    )(x2, mix_a, mix_b, wp["wo"], wp["g1"], wp["b1"], wp["wi"], wp["wout"], wp["g2"], wp["b2"])


def _t5_bucket(rel):
    nb = N_BUCKETS // 2
    ret = (rel > 0).astype(jnp.int32) * nb
    n = jnp.abs(rel)
    max_exact = nb // 2
    nf = jnp.maximum(n, 1).astype(F32)
    large = max_exact + (jnp.log(nf / max_exact) / math.log(MAX_DISTANCE / max_exact)
                         * (nb - max_exact)).astype(jnp.int32)
    large = jnp.minimum(large, nb - 1)
    return ret + jnp.where(n < max_exact, n, large)


def _bias_tiles(rel_bias):
    t = ATT_TILE
    p = 2 * t
    table = rel_bias.astype(F32).T
    far = table[:, N_BUCKETS // 2 - 1]
    w = jnp.arange(p, dtype=jnp.int32)[None, :]
    d = jnp.arange(2, dtype=jnp.int32)[:, None]
    rel = (t - 1) - w - d * t
    vec = (table[:, _t5_bucket(rel)] - far[:, None, None]) * LOG2E
    rows = jnp.tile(vec, (1, 1, t))[..., :t * (p - 1)].reshape(HEADS, 2, t, p - 1)
    bias = rows[..., t - 1:2 * t - 1]
    kk = jnp.arange(t, dtype=jnp.int32)[:, None]
    qq = jnp.arange(t, dtype=jnp.int32)[None, :]
    allowed = (kk // CHUNK <= qq // CHUNK)[None, None] | (d > 0)[None, :, :, None]
    bias = jnp.where(allowed, bias, NEG_BIG)
    return jnp.concatenate([bias, bias], axis=-1)


def _rope_tables(seq):
    pos = jnp.arange(seq, dtype=F32)
    inv = ROPE_THETA ** (-jnp.arange(0, MLA_ROPE_DIM, 2, dtype=F32) / MLA_ROPE_DIM)
    ang = pos[:, None] * inv[None, :]
    cos, sin = jnp.cos(ang), jnp.sin(ang)
    z = jnp.zeros((seq, LANES - MLA_ROPE_DIM), F32)
    return (jnp.concatenate([cos, cos, z], axis=-1), jnp.concatenate([-sin, sin, z], axis=-1))


def _swap_halves(w):
    half = w.shape[-1] // 2
    return jnp.concatenate([w[..., half:], w[..., :half]], axis=-1)


def _pack_params(w_in, lambda_q1, lambda_k1, lambda_q2, lambda_k2, subln_g, q_norm_g, w_uq,
                 kv_norm_g, w_ukv, w_o, ln1_g, ln1_b, w_mlp_in, w_mlp_out, ln2_g, ln2_b):
    depth = w_in.shape[0]
    o_k, o_v, o_cq = DIFF_WIDTH, 2 * DIFF_WIDTH, 3 * DIFF_WIDTH
    o_ckv = o_cq + MLA_Q_RANK
    o_kr = o_ckv + MLA_KV_RANK
    kr = w_in[..., o_kr:]
    w1 = jnp.concatenate([w_in[..., :o_k] * (DIFF_HEAD_DIM ** -0.5 * LOG2E), w_in[..., o_k:o_v],
                          w_in[..., o_cq:o_ckv], w_in[..., o_ckv:o_kr], kr, _swap_halves(kr)],
                         axis=-1)
    mla_scale = (MLA_NOPE_DIM + MLA_ROPE_DIM) ** -0.5 * LOG2E
    uq = (w_uq * mla_scale).reshape(depth, MLA_Q_RANK, HEADS, MLA_NOPE_DIM + MLA_ROPE_DIM)
    zq = jnp.zeros((depth, MLA_Q_RANK, HEADS, MLA_QK_PAD - MLA_NOPE_DIM - MLA_ROPE_DIM), F32)
    wuq = jnp.concatenate([uq, zq], axis=-1).reshape(depth, MLA_Q_RANK, HEADS * MLA_QK_PAD)
    wuqs = jnp.concatenate([_swap_halves(uq[..., MLA_NOPE_DIM:]), zq], axis=-1)
    wuqs = wuqs.reshape(depth, MLA_Q_RANK, HEADS * LANES)
    ukv = w_ukv.reshape(depth, MLA_KV_RANK, HEADS, MLA_NOPE_DIM + MLA_V_DIM)
    wukvk = ukv[..., :MLA_NOPE_DIM].reshape(depth, MLA_KV_RANK, HEADS * MLA_NOPE_DIM)
    wukvv = ukv[..., MLA_NOPE_DIM:].reshape(depth, MLA_KV_RANK, HEADS * MLA_V_DIM)
    row = lambda v: v.reshape(depth, 1, -1).astype(F32)
    return dict(
        w1=w1.astype(BF16), wvt=jnp.swapaxes(w_in[..., o_v:o_cq], 1, 2).astype(BF16),
        wuq=wuq.astype(BF16), wuqs=wuqs.astype(BF16), wukvk=wukvk.astype(BF16),
        wukvvt=jnp.swapaxes(wukvv, 1, 2).astype(BF16), gq=row(q_norm_g), gkv=row(kv_norm_g),
        subln=row(subln_g),
        lam=jnp.stack([lambda_q1, lambda_k1, lambda_q2, lambda_k2], axis=1).astype(F32),
        wo=w_o.astype(BF16), g1=row(ln1_g), b1=row(ln1_b), wi=w_mlp_in.astype(BF16),
        wout=w_mlp_out.astype(BF16), g2=row(ln2_g), b2=row(ln2_b))


def kernel(x, w_in, lambda_q1, lambda_k1, lambda_q2, lambda_k2, subln_g, q_norm_g, w_uq, kv_norm_g,
           w_ukv, rel_bias, w_o, ln1_g, ln1_b, w_mlp_in, w_mlp_out, ln2_g, ln2_b):
    batch, seq, _ = x.shape
    assert seq % PROJ_ROWS == 0 and seq % ATT_TILE == 0 and (batch * seq) % POST_ROWS == 0
    wp = _pack_params(w_in, lambda_q1, lambda_k1, lambda_q2, lambda_k2, subln_g, q_norm_g, w_uq,
                      kv_norm_g, w_ukv, w_o, ln1_g, ln1_b, w_mlp_in, w_mlp_out, ln2_g, ln2_b)
    wp["cos"], wp["sin"] = _rope_tables(seq)
    bias = _bias_tiles(rel_bias)
    x2 = x.reshape(batch * seq, D_MODEL)
    for l in range(DEPTH):
        lambda_init = 0.8 - 0.6 * math.exp(-0.3 * l)
        dq, dk, dvt, qm, km, vmt = _proj_call(x2, wp, l, seq)
        mix_a, mix_b = _attn_call(dq, dk, dvt, bias, wp, l, qm, km, vmt, seq, lambda_init)
        x2 = _post_call(x2, mix_a, mix_b, wp, l)
    return x2.reshape(batch, seq, D_MODEL)
```

```python
import functools
import math

import jax
import jax.numpy as jnp
from jax import lax
from jax.experimental import pallas as pl
from jax.experimental.pallas import tpu as pltpu

F32 = jnp.float32
BF16 = jnp.bfloat16

D_MODEL = 1024
DEPTH = 2
CHUNK = 64
HEADS = 4
DIFF_HEAD_DIM = 64
DIFF_V_DIM = 2 * DIFF_HEAD_DIM
DIFF_WIDTH = HEADS * DIFF_V_DIM
MLA_NOPE_DIM = 128
MLA_ROPE_DIM = 64
MLA_V_DIM = 128
MLA_Q_RANK = 256
MLA_KV_RANK = 128
MLA_WIDTH = HEADS * MLA_V_DIM
MLA_QK_PAD = 256
D_FF = 4 * D_MODEL
N_BUCKETS = 32
MAX_DISTANCE = 128
ROPE_THETA = 10000.0
ALPHA = (2 * DEPTH) ** 0.25
LN_EPS = 1e-5
RMS_EPS = 1e-6
NEG_BIG = -1e30
LOG2E = math.log2(math.e)

LANES = 128
BF16_SUBLANES = 16
ATT_TILE = 256
ATT_HEADS_PER_STEP = 2
PROJ_ROWS = 1024
POST_ROWS = 1024
POST_ROW_BLOCK = 256
FF_CHUNK = 1024
DIFF_LOOKAHEAD = 2
MLA_LOOKAHEAD = 4
VMEM_LIMIT = 56 * 1024 * 1024
assert ATT_TILE >= MAX_DISTANCE and ATT_TILE % CHUNK == 0

C_DQ = 0
C_DK = C_DQ + DIFF_WIDTH
C_CQ = C_DK + DIFF_WIDTH
C_CKV = C_CQ + MLA_Q_RANK
C_KR = C_CKV + MLA_KV_RANK
W1_COLS = C_KR + LANES

_NT = (((1,), (1,)), ((), ()))


def _rms(x, g):
    return x * lax.rsqrt(jnp.mean(x * x, axis=-1, keepdims=True) + RMS_EPS) * g


def _layer_norm(x, g, b):
    mu = jnp.mean(x, axis=-1, keepdims=True)
    xc = x - mu
    var = jnp.mean(xc * xc, axis=-1, keepdims=True)
    return xc * lax.rsqrt(var + LN_EPS) * g + b


def _layer_spec(arr, l, **kwargs):
    index = (l,) + (0,) * (arr.ndim - 1)
    return pl.BlockSpec((1,) + arr.shape[1:], lambda *_: index, **kwargs)


def _proj_kernel(x_ref, w1_ref, wvt_ref, wuq_ref, wuqs_ref, wukvk_ref, wukvvt_ref,
                 gq_ref, gkv_ref, cos_ref, sin_ref,
                 dq_ref, dk_ref, dvt_ref, qm_ref, km_ref, vmt_ref):
    xb = x_ref[...].astype(BF16)
    h = jnp.dot(xb, w1_ref[0], preferred_element_type=F32)
    dq_ref[...] = h[:, C_DQ:C_DQ + DIFF_WIDTH].astype(BF16)
    dk_ref[...] = h[:, C_DK:C_DK + DIFF_WIDTH].astype(BF16)
    dvt_ref[0] = lax.dot_general(wvt_ref[0], xb, _NT, preferred_element_type=F32).astype(BF16)

    cos = cos_ref[...]
    sin = sin_ref[...]
    c_q = _rms(h[:, C_CQ:C_CQ + MLA_Q_RANK], gq_ref[0]).astype(BF16)
    qf = jnp.dot(c_q, wuq_ref[0], preferred_element_type=F32)
    qsw = jnp.dot(c_q, wuqs_ref[0], preferred_element_type=F32)
    c_kv = _rms(h[:, C_CKV:C_CKV + MLA_KV_RANK], gkv_ref[0]).astype(BF16)
    kn = jnp.dot(c_kv, wukvk_ref[0], preferred_element_type=F32)
    vmt_ref[0] = lax.dot_general(wukvvt_ref[0], c_kv, _NT,
                                 preferred_element_type=F32).astype(BF16)
    kr = h[:, C_KR:C_KR + LANES]
    k_rope = (kr * cos + pltpu.roll(kr, LANES // 2, 1) * sin).astype(BF16)

    for hh in range(HEADS):
        b0 = hh * MLA_QK_PAD
        qm_ref[:, b0:b0 + LANES] = qf[:, b0:b0 + LANES].astype(BF16)
        qm_ref[:, b0 + LANES:b0 + 2 * LANES] = (
            qf[:, b0 + LANES:b0 + 2 * LANES] * cos + qsw[:, hh * LANES:(hh + 1) * LANES] * sin
        ).astype(BF16)
        km_ref[:, b0:b0 + LANES] = kn[:, hh * LANES:(hh + 1) * LANES].astype(BF16)
        km_ref[:, b0 + LANES:b0 + 2 * LANES] = k_rope


def _proj_call(x2, wp, l, seq):
    ntok = x2.shape[0]
    batch = ntok // seq
    rows = PROJ_ROWS
    steps_per_seq = seq // rows

    def const(name):
        return _layer_spec(wp[name], l)

    def tok(width):
        return pl.BlockSpec((rows, width), lambda i: (i, 0))

    vt_spec = pl.BlockSpec((1, HEADS * LANES, rows),
                           lambda i: (i // steps_per_seq, 0, i % steps_per_seq))
    pos_spec = pl.BlockSpec((rows, LANES), lambda i: (i % steps_per_seq, 0))
    vt_shape = jax.ShapeDtypeStruct((batch, HEADS * LANES, seq), BF16)
    return pl.pallas_call(
        _proj_kernel,
        grid=(ntok // rows,),
        in_specs=[tok(D_MODEL), const("w1"), const("wvt"), const("wuq"), const("wuqs"),
                  const("wukvk"), const("wukvvt"), const("gq"), const("gkv"), pos_spec, pos_spec],
        out_specs=[tok(DIFF_WIDTH), tok(DIFF_WIDTH), vt_spec,
                   tok(HEADS * MLA_QK_PAD), tok(HEADS * MLA_QK_PAD), vt_spec],
        out_shape=[jax.ShapeDtypeStruct((ntok, DIFF_WIDTH), BF16),
                   jax.ShapeDtypeStruct((ntok, DIFF_WIDTH), BF16),
                   vt_shape,
                   jax.ShapeDtypeStruct((ntok, HEADS * MLA_QK_PAD), BF16),
                   jax.ShapeDtypeStruct((ntok, HEADS * MLA_QK_PAD), BF16),
                   vt_shape],
        compiler_params=pltpu.CompilerParams(dimension_semantics=("arbitrary",),
                                             vmem_limit_bytes=VMEM_LIMIT),
        name="proj",
    )(x2, wp["w1"], wp["wvt"], wp["wuq"], wp["wuqs"], wp["wukvk"], wp["wukvvt"],
      wp["gq"], wp["gkv"], wp["cos"], wp["sin"])


def _col_max(a, b):
    m = jnp.max(b, axis=0, keepdims=True)
    return m if a is None else jnp.maximum(a, m)


def _attend(n_heads, n_tiles, q_fn, score_fn, finish_fn, vt_fn, lookahead):
    ones = jnp.ones((BF16_SUBLANES, ATT_TILE), BF16)
    order = [q for pair in zip(range(n_tiles - 1, -1, -1), range(n_tiles)) for q in pair][:n_tiles]
    items = [(h, qi, c) for h in range(n_heads) for qi in order for c in range(qi + 1)]
    qs, scores = {}, {}

    def issue(idx):
        h, qi, c = items[idx]
        if c == 0:
            qs[h, qi] = q_fn(h, qi)
        scores[idx] = score_fn(h, qi, qs[h, qi], c)

    for idx in range(min(lookahead, len(items))):
        issue(idx)
    m, acc = None, None
    for idx, (h, qi, c) in enumerate(items):
        if idx + lookahead < len(items):
            issue(idx + lookahead)
        s = scores.pop(idx)
        m_new = _col_max(m, s)
        e = jnp.exp2(s - m_new).astype(BF16)
        vt = jnp.concatenate([vt_fn(h, c), ones], axis=0)
        pv = jnp.dot(vt, e, preferred_element_type=F32)
        acc = pv if acc is None else acc * jnp.exp2(m - m_new) + pv
        m = m_new
        if c == qi:
            finish_fn(h, qi, acc[:LANES] / acc[LANES:LANES + 1])
            m, acc = None, None
        yield


def _diff_steps(q_ref, k_ref, vt_ref, bias_ref, lam_ref, g_ref, o_ref, lambda_init):
    t = ATT_TILE
    lane = lax.broadcasted_iota(jnp.int32, (t, LANES), 1)
    lp = lam_ref[0]
    lam = (jnp.exp(jnp.sum(lp[0:1] * lp[1:2], axis=-1, keepdims=True))
           - jnp.exp(jnp.sum(lp[2:3] * lp[3:4], axis=-1, keepdims=True)) + lambda_init)
    g = g_ref[0]

    def q_rows(h, qi):
        q = q_ref[qi * t:(qi + 1) * t, h * LANES:(h + 1) * LANES]
        zero = jnp.zeros_like(q)
        return jnp.concatenate([jnp.where(lane < DIFF_HEAD_DIM, q, zero),
                                jnp.where(lane >= DIFF_HEAD_DIM, q, zero)], axis=0)

    def score(h, qi, qs, c):
        k = k_ref[c * t:(c + 1) * t, h * LANES:(h + 1) * LANES]
        s = lax.dot_general(k, qs, _NT, preferred_element_type=F32)
        return s + bias_ref[h, qi - c] if qi - c <= 1 else s

    def values(h, c):
        return vt_ref[0, h * LANES:(h + 1) * LANES, c * t:(c + 1) * t]

    def finish(h, qi, o):
        ot = o[:, :t] - lam * o[:, t:]
        y = _rms(ot.T, g) * (1.0 - lambda_init)
        o_ref[qi * t:(qi + 1) * t, h * LANES:(h + 1) * LANES] = y.astype(BF16)

    return _attend(ATT_HEADS_PER_STEP, q_ref.shape[0] // t, q_rows, score, finish, values,
                   DIFF_LOOKAHEAD)


def _mla_steps(q_ref, k_ref, vt_ref, o_ref):
    t = ATT_TILE
    w = MLA_QK_PAD
    kc = lax.broadcasted_iota(jnp.int32, (t, t), 0) // CHUNK
    qc = lax.broadcasted_iota(jnp.int32, (t, t), 1) // CHUNK
    allowed = kc <= qc

    def q_rows(h, qi):
        return q_ref[qi * t:(qi + 1) * t, h * w:(h + 1) * w]

    def score(h, qi, q, c):
        k = k_ref[c * t:(c + 1) * t, h * w:(h + 1) * w]
        s = lax.dot_general(k, q, _NT, preferred_element_type=F32)
        return jnp.where(allowed, s, NEG_BIG) if c == qi else s

    def values(h, c):
        return vt_ref[0, h * LANES:(h + 1) * LANES, c * t:(c + 1) * t]

    def finish(h, qi, o):
        o_ref[qi * t:(qi + 1) * t, h * LANES:(h + 1) * LANES] = o.T.astype(BF16)

    return _attend(ATT_HEADS_PER_STEP, q_ref.shape[0] // t, q_rows, score, finish, values,
                   MLA_LOOKAHEAD)


def _attn_kernel(dq_ref, dk_ref, dvt_ref, bias_ref, lam_ref, g_ref, qm_ref, km_ref, vmt_ref,
                 oa_ref, ob_ref, *, lambda_init):
    streams = [_diff_steps(dq_ref, dk_ref, dvt_ref, bias_ref, lam_ref, g_ref, oa_ref, lambda_init),
               _mla_steps(qm_ref, km_ref, vmt_ref, ob_ref)]
    while streams:
        for s in list(streams):
            if next(s, "done") == "done":
                streams.remove(s)


def _attn_call(dq, dk, dvt, bias, wp, l, qm, km, vmt, seq, lambda_init):
    ntok = dq.shape[0]
    n = ATT_HEADS_PER_STEP

    def rows(width):
        return pl.BlockSpec((seq, n * width), lambda b, h: (b, h))

    vt_spec = pl.BlockSpec((1, n * LANES, seq), lambda b, h: (b, h, 0))
    return pl.pallas_call(
        functools.partial(_attn_kernel, lambda_init=lambda_init),
        grid=(ntok // seq, HEADS // n),
        in_specs=[rows(LANES), rows(LANES), vt_spec,
                  pl.BlockSpec((n,) + bias.shape[1:], lambda b, h: (h, 0, 0, 0)),
                  _layer_spec(wp["lam"], l), _layer_spec(wp["subln"], l),
                  rows(MLA_QK_PAD), rows(MLA_QK_PAD), vt_spec],
        out_specs=[rows(LANES), rows(LANES)],
        out_shape=[jax.ShapeDtypeStruct((ntok, DIFF_WIDTH), BF16),
                   jax.ShapeDtypeStruct((ntok, MLA_WIDTH), BF16)],
        compiler_params=pltpu.CompilerParams(dimension_semantics=("arbitrary", "arbitrary"),
                                             vmem_limit_bytes=VMEM_LIMIT),
        name="attn",
    )(dq, dk, dvt, bias, wp["lam"], wp["subln"], qm, km, vmt)


def _post_kernel(x_ref, ma_ref, mb_ref, wo_ref, g1_ref, b1_ref, wi_ref, wout_ref, g2_ref, b2_ref,
                 o_ref):
    n_blocks = x_ref.shape[0] // POST_ROW_BLOCK
    n_chunks = D_FF // FF_CHUNK

    def attn_out(r):
        rs = slice(r * POST_ROW_BLOCK, (r + 1) * POST_ROW_BLOCK)
        y = (jnp.dot(ma_ref[rs, :], wo_ref[0, :DIFF_WIDTH, :], preferred_element_type=F32)
             + jnp.dot(mb_ref[rs, :], wo_ref[0, DIFF_WIDTH:, :], preferred_element_type=F32))
        return _layer_norm(ALPHA * x_ref[rs, :] + y, g1_ref[0], b1_ref[0])

    def up(x1b, c):
        return jnp.dot(x1b, wi_ref[0, :, c * FF_CHUNK:(c + 1) * FF_CHUNK],
                       preferred_element_type=F32)

    x1 = attn_out(0)
    for r in range(n_blocks):
        x1_next = attn_out(r + 1) if r + 1 < n_blocks else None
        x1b = x1.astype(BF16)
        u = up(x1b, 0)
        y2 = None
        for c in range(n_chunks):
            u_next = up(x1b, c + 1) if c + 1 < n_chunks else None
            act = jnp.square(jnp.maximum(u, 0.0)).astype(BF16)
            d = jnp.dot(act, wout_ref[0, c * FF_CHUNK:(c + 1) * FF_CHUNK, :],
                        preferred_element_type=F32)
            y2 = d if y2 is None else y2 + d
            u = u_next
        o_ref[r * POST_ROW_BLOCK:(r + 1) * POST_ROW_BLOCK, :] = _layer_norm(
            ALPHA * x1 + y2, g2_ref[0], b2_ref[0])
        x1 = x1_next


def _post_call(x2, mix_a, mix_b, wp, l):
    ntok = x2.shape[0]
    rows = POST_ROWS

    def const(name):
        return _layer_spec(wp[name], l, pipeline_mode=pl.Buffered(1))

    def tok(width):
        return pl.BlockSpec((rows, width), lambda i: (i, 0))

    return pl.pallas_call(
        _post_kernel,
        grid=(ntok // rows,),
        in_specs=[tok(D_MODEL), tok(DIFF_WIDTH), tok(MLA_WIDTH), const("wo"), const("g1"),
                  const("b1"), const("wi"), const("wout"), const("g2"), const("b2")],
        out_specs=tok(D_MODEL),
        out_shape=jax.ShapeDtypeStruct((ntok, D_MODEL), F32),
        compiler_params=pltpu.CompilerParams(dimension_semantics=("arbitrary",),
                                             vmem_limit_bytes=VMEM_LIMIT),
        name="post",
    )(x2, mix_a, mix_b, wp["wo"], wp["g1"], wp["b1"], wp["wi"], wp["wout"], wp["g2"], wp["b2"])


def _t5_bucket(rel):
    nb = N_BUCKETS // 2
    ret = (rel > 0).astype(jnp.int32) * nb
    n = jnp.abs(rel)
    max_exact = nb // 2
    nf = jnp.maximum(n, 1).astype(F32)
    large = max_exact + (jnp.log(nf / max_exact) / math.log(MAX_DISTANCE / max_exact)
                         * (nb - max_exact)).astype(jnp.int32)
    large = jnp.minimum(large, nb - 1)
    return ret + jnp.where(n < max_exact, n, large)


def _bias_tiles(rel_bias):
    t = ATT_TILE
    p = 2 * t
    table = rel_bias.astype(F32).T
    far = table[:, N_BUCKETS // 2 - 1]
    w = jnp.arange(p, dtype=jnp.int32)[None, :]
    d = jnp.arange(2, dtype=jnp.int32)[:, None]
    rel = (t - 1) - w - d * t
    vec = (table[:, _t5_bucket(rel)] - far[:, None, None]) * LOG2E
    rows = jnp.tile(vec, (1, 1, t))[..., :t * (p - 1)].reshape(HEADS, 2, t, p - 1)
    bias = rows[..., t - 1:2 * t - 1]
    kk = jnp.arange(t, dtype=jnp.int32)[:, None]
    qq = jnp.arange(t, dtype=jnp.int32)[None, :]
    allowed = (kk // CHUNK <= qq // CHUNK)[None, None] | (d > 0)[None, :, :, None]
    bias = jnp.where(allowed, bias, NEG_BIG)
    return jnp.concatenate([bias, bias], axis=-1)


def _rope_tables(seq):
    pos = jnp.arange(seq, dtype=F32)
    inv = ROPE_THETA ** (-jnp.arange(0, MLA_ROPE_DIM, 2, dtype=F32) / MLA_ROPE_DIM)
    ang = pos[:, None] * inv[None, :]
    cos, sin = jnp.cos(ang), jnp.sin(ang)
    z = jnp.zeros((seq, LANES - MLA_ROPE_DIM), F32)
    return (jnp.concatenate([cos, cos, z], axis=-1), jnp.concatenate([-sin, sin, z], axis=-1))


def _swap_halves(w):
    half = w.shape[-1] // 2
    return jnp.concatenate([w[..., half:], w[..., :half]], axis=-1)


def _pack_params(w_in, lambda_q1, lambda_k1, lambda_q2, lambda_k2, subln_g, q_norm_g, w_uq,
                 kv_norm_g, w_ukv, w_o, ln1_g, ln1_b, w_mlp_in, w_mlp_out, ln2_g, ln2_b):
    depth = w_in.shape[0]
    o_k, o_v, o_cq = DIFF_WIDTH, 2 * DIFF_WIDTH, 3 * DIFF_WIDTH
    o_ckv = o_cq + MLA_Q_RANK
    o_kr = o_ckv + MLA_KV_RANK
    kr = w_in[..., o_kr:]
    w1 = jnp.concatenate([w_in[..., :o_k] * (DIFF_HEAD_DIM ** -0.5 * LOG2E), w_in[..., o_k:o_v],
                          w_in[..., o_cq:o_ckv], w_in[..., o_ckv:o_kr], kr, _swap_halves(kr)],
                         axis=-1)
    mla_scale = (MLA_NOPE_DIM + MLA_ROPE_DIM) ** -0.5 * LOG2E
    uq = (w_uq * mla_scale).reshape(depth, MLA_Q_RANK, HEADS, MLA_NOPE_DIM + MLA_ROPE_DIM)
    zq = jnp.zeros((depth, MLA_Q_RANK, HEADS, MLA_QK_PAD - MLA_NOPE_DIM - MLA_ROPE_DIM), F32)
    wuq = jnp.concatenate([uq, zq], axis=-1).reshape(depth, MLA_Q_RANK, HEADS * MLA_QK_PAD)
    wuqs = jnp.concatenate([_swap_halves(uq[..., MLA_NOPE_DIM:]), zq], axis=-1)
    wuqs = wuqs.reshape(depth, MLA_Q_RANK, HEADS * LANES)
    ukv = w_ukv.reshape(depth, MLA_KV_RANK, HEADS, MLA_NOPE_DIM + MLA_V_DIM)
    wukvk = ukv[..., :MLA_NOPE_DIM].reshape(depth, MLA_KV_RANK, HEADS * MLA_NOPE_DIM)
    wukvv = ukv[..., MLA_NOPE_DIM:].reshape(depth, MLA_KV_RANK, HEADS * MLA_V_DIM)
    row = lambda v: v.reshape(depth, 1, -1).astype(F32)
    return dict(
        w1=w1.astype(BF16), wvt=jnp.swapaxes(w_in[..., o_v:o_cq], 1, 2).astype(BF16),
        wuq=wuq.astype(BF16), wuqs=wuqs.astype(BF16), wukvk=wukvk.astype(BF16),
        wukvvt=jnp.swapaxes(wukvv, 1, 2).astype(BF16), gq=row(q_norm_g), gkv=row(kv_norm_g),
        subln=row(subln_g),
        lam=jnp.stack([lambda_q1, lambda_k1, lambda_q2, lambda_k2], axis=1).astype(F32),
        wo=w_o.astype(BF16), g1=row(ln1_g), b1=row(ln1_b), wi=w_mlp_in.astype(BF16),
        wout=w_mlp_out.astype(BF16), g2=row(ln2_g), b2=row(ln2_b))


def kernel(x, w_in, lambda_q1, lambda_k1, lambda_q2, lambda_k2, subln_g, q_norm_g, w_uq, kv_norm_g,
           w_ukv, rel_bias, w_o, ln1_g, ln1_b, w_mlp_in, w_mlp_out, ln2_g, ln2_b):
    batch, seq, _ = x.shape
    assert seq % PROJ_ROWS == 0 and seq % ATT_TILE == 0 and (batch * seq) % POST_ROWS == 0
    wp = _pack_params(w_in, lambda_q1, lambda_k1, lambda_q2, lambda_k2, subln_g, q_norm_g, w_uq,
                      kv_norm_g, w_ukv, w_o, ln1_g, ln1_b, w_mlp_in, w_mlp_out, ln2_g, ln2_b)
    wp["cos"], wp["sin"] = _rope_tables(seq)
    bias = _bias_tiles(rel_bias)
    x2 = x.reshape(batch * seq, D_MODEL)
    for l in range(DEPTH):
        lambda_init = 0.8 - 0.6 * math.exp(-0.3 * l)
        dq, dk, dvt, qm, km, vmt = _proj_call(x2, wp, l, seq)
        mix_a, mix_b = _attn_call(dq, dk, dvt, bias, wp, l, qm, km, vmt, seq, lambda_init)
        x2 = _post_call(x2, mix_a, mix_b, wp, l)
    return x2.reshape(batch, seq, D_MODEL)
```

```python
import functools
import math

import jax
import jax.numpy as jnp
from jax import lax
from jax.experimental import pallas as pl
from jax.experimental.pallas import tpu as pltpu

F32 = jnp.float32
BF16 = jnp.bfloat16

D_MODEL = 1024
DEPTH = 2
CHUNK = 64
HEADS = 4
DIFF_HEAD_DIM = 64
DIFF_V_DIM = 2 * DIFF_HEAD_DIM
DIFF_WIDTH = HEADS * DIFF_V_DIM
MLA_NOPE_DIM = 128
MLA_ROPE_DIM = 64
MLA_V_DIM = 128
MLA_Q_RANK = 256
MLA_KV_RANK = 128
MLA_WIDTH = HEADS * MLA_V_DIM
MLA_QK_PAD = 256
D_FF = 4 * D_MODEL
N_BUCKETS = 32
MAX_DISTANCE = 128
ROPE_THETA = 10000.0
ALPHA = (2 * DEPTH) ** 0.25
LN_EPS = 1e-5
RMS_EPS = 1e-6
NEG_BIG = -1e30
LOG2E = math.log2(math.e)

LANES = 128
BF16_SUBLANES = 16
ATT_TILE = 256
ATT_HEADS_PER_STEP = 2
PROJ_ROWS = 1024
POST_ROWS = 1024
POST_ROW_BLOCK = 256
FF_CHUNK = 1024
DIFF_LOOKAHEAD = 2
MLA_LOOKAHEAD = 4
VMEM_LIMIT = 56 * 1024 * 1024
assert ATT_TILE >= MAX_DISTANCE and ATT_TILE % CHUNK == 0

C_DQ = 0
C_DK = C_DQ + DIFF_WIDTH
C_CQ = C_DK + DIFF_WIDTH
C_CKV = C_CQ + MLA_Q_RANK
C_KR = C_CKV + MLA_KV_RANK
W1_COLS = C_KR + LANES

_NT = (((1,), (1,)), ((), ()))


def _rms(x, g):
    return x * lax.rsqrt(jnp.mean(x * x, axis=-1, keepdims=True) + RMS_EPS) * g


def _layer_norm(x, g, b):
    mu = jnp.mean(x, axis=-1, keepdims=True)
    xc = x - mu
    var = jnp.mean(xc * xc, axis=-1, keepdims=True)
    return xc * lax.rsqrt(var + LN_EPS) * g + b


def _layer_spec(arr, l, **kwargs):
    index = (l,) + (0,) * (arr.ndim - 1)
    return pl.BlockSpec((1,) + arr.shape[1:], lambda *_: index, **kwargs)


def _proj_kernel(x_ref, w1_ref, wvt_ref, wuq_ref, wuqs_ref, wukvk_ref, wukvvt_ref,
                 gq_ref, gkv_ref, cos_ref, sin_ref,
                 dq_ref, dk_ref, dvt_ref, qm_ref, km_ref, vmt_ref):
    xb = x_ref[...].astype(BF16)
    h = jnp.dot(xb, w1_ref[0], preferred_element_type=F32)
    dq_ref[...] = h[:, C_DQ:C_DQ + DIFF_WIDTH].astype(BF16)
    dk_ref[...] = h[:, C_DK:C_DK + DIFF_WIDTH].astype(BF16)
    dvt_ref[0] = lax.dot_general(wvt_ref[0], xb, _NT, preferred_element_type=F32).astype(BF16)

    cos = cos_ref[...]
    sin = sin_ref[...]
    c_q = _rms(h[:, C_CQ:C_CQ + MLA_Q_RANK], gq_ref[0]).astype(BF16)
    qf = jnp.dot(c_q, wuq_ref[0], preferred_element_type=F32)
    qsw = jnp.dot(c_q, wuqs_ref[0], preferred_element_type=F32)
    c_kv = _rms(h[:, C_CKV:C_CKV + MLA_KV_RANK], gkv_ref[0]).astype(BF16)
    kn = jnp.dot(c_kv, wukvk_ref[0], preferred_element_type=F32)
    vmt_ref[0] = lax.dot_general(wukvvt_ref[0], c_kv, _NT,
                                 preferred_element_type=F32).astype(BF16)
    kr = h[:, C_KR:C_KR + LANES]
    k_rope = (kr * cos + pltpu.roll(kr, LANES // 2, 1) * sin).astype(BF16)

    for hh in range(HEADS):
        b0 = hh * MLA_QK_PAD
        qm_ref[:, b0:b0 + LANES] = qf[:, b0:b0 + LANES].astype(BF16)
        qm_ref[:, b0 + LANES:b0 + 2 * LANES] = (
            qf[:, b0 + LANES:b0 + 2 * LANES] * cos + qsw[:, hh * LANES:(hh + 1) * LANES] * sin
        ).astype(BF16)
        km_ref[:, b0:b0 + LANES] = kn[:, hh * LANES:(hh + 1) * LANES].astype(BF16)
        km_ref[:, b0 + LANES:b0 + 2 * LANES] = k_rope


def _proj_call(x2, wp, l, seq):
    ntok = x2.shape[0]
    batch = ntok // seq
    rows = PROJ_ROWS
    steps_per_seq = seq // rows

    def const(name):
        return _layer_spec(wp[name], l)

    def tok(width):
        return pl.BlockSpec((rows, width), lambda i: (i, 0))

    vt_spec = pl.BlockSpec((1, HEADS * LANES, rows),
                           lambda i: (i // steps_per_seq, 0, i % steps_per_seq))
    pos_spec = pl.BlockSpec((rows, LANES), lambda i: (i % steps_per_seq, 0))
    vt_shape = jax.ShapeDtypeStruct((batch, HEADS * LANES, seq), BF16)
    return pl.pallas_call(
        _proj_kernel,
        grid=(ntok // rows,),
        in_specs=[tok(D_MODEL), const("w1"), const("wvt"), const("wuq"), const("wuqs"),
                  const("wukvk"), const("wukvvt"), const("gq"), const("gkv"), pos_spec, pos_spec],
        out_specs=[tok(DIFF_WIDTH), tok(DIFF_WIDTH), vt_spec,
                   tok(HEADS * MLA_QK_PAD), tok(HEADS * MLA_QK_PAD), vt_spec],
        out_shape=[jax.ShapeDtypeStruct((ntok, DIFF_WIDTH), BF16),
                   jax.ShapeDtypeStruct((ntok, DIFF_WIDTH), BF16),
                   vt_shape,
                   jax.ShapeDtypeStruct((ntok, HEADS * MLA_QK_PAD), BF16),
                   jax.ShapeDtypeStruct((ntok, HEADS * MLA_QK_PAD), BF16),
                   vt_shape],
        compiler_params=pltpu.CompilerParams(dimension_semantics=("arbitrary",),
                                             vmem_limit_bytes=VMEM_LIMIT),
        name="proj",
    )(x2, wp["w1"], wp["wvt"], wp["wuq"], wp["wuqs"], wp["wukvk"], wp["wukvvt"],
      wp["gq"], wp["gkv"], wp["cos"], wp["sin"])


def _col_max(a, b):
    m = jnp.max(b, axis=0, keepdims=True)
    return m if a is None else jnp.maximum(a, m)


def _attend(n_heads, n_tiles, q_fn, score_fn, finish_fn, vt_fn, lookahead):
    ones = jnp.ones((BF16_SUBLANES, ATT_TILE), BF16)
    items = [(h, qi, c) for h in range(n_heads) for qi in range(n_tiles) for c in range(qi + 1)]
    qs, scores = {}, {}

    def issue(idx):
        h, qi, c = items[idx]
        if c == 0:
            qs[h, qi] = q_fn(h, qi)
        scores[idx] = score_fn(h, qi, qs[h, qi], c)

    for idx in range(min(lookahead, len(items))):
        issue(idx)
    m, acc = None, None
    for idx, (h, qi, c) in enumerate(items):
        if idx + lookahead < len(items):
            issue(idx + lookahead)
        s = scores.pop(idx)
        m_new = _col_max(m, s)
        e = jnp.exp2(s - m_new).astype(BF16)
        vt = jnp.concatenate([vt_fn(h, c), ones], axis=0)
        pv = jnp.dot(vt, e, preferred_element_type=F32)
        acc = pv if acc is None else acc * jnp.exp2(m - m_new) + pv
        m = m_new
        if c == qi:
            finish_fn(h, qi, acc[:LANES] / acc[LANES:LANES + 1])
            m, acc = None, None
        yield


def _diff_steps(q_ref, k_ref, vt_ref, bias_ref, lam_ref, g_ref, o_ref, lambda_init):
    t = ATT_TILE
    lane = lax.broadcasted_iota(jnp.int32, (t, LANES), 1)
    lp = lam_ref[0]
    lam = (jnp.exp(jnp.sum(lp[0:1] * lp[1:2], axis=-1, keepdims=True))
           - jnp.exp(jnp.sum(lp[2:3] * lp[3:4], axis=-1, keepdims=True)) + lambda_init)
    g = g_ref[0]

    def q_rows(h, qi):
        q = q_ref[qi * t:(qi + 1) * t, h * LANES:(h + 1) * LANES]
        zero = jnp.zeros_like(q)
        return jnp.concatenate([jnp.where(lane < DIFF_HEAD_DIM, q, zero),
                                jnp.where(lane >= DIFF_HEAD_DIM, q, zero)], axis=0)

    def score(h, qi, qs, c):
        k = k_ref[c * t:(c + 1) * t, h * LANES:(h + 1) * LANES]
        s = lax.dot_general(k, qs, _NT, preferred_element_type=F32)
        return s + bias_ref[h, qi - c] if qi - c <= 1 else s

    def values(h, c):
        return vt_ref[0, h * LANES:(h + 1) * LANES, c * t:(c + 1) * t]

    def finish(h, qi, o):
        ot = o[:, :t] - lam * o[:, t:]
        y = _rms(ot.T, g) * (1.0 - lambda_init)
        o_ref[qi * t:(qi + 1) * t, h * LANES:(h + 1) * LANES] = y.astype(BF16)

    return _attend(ATT_HEADS_PER_STEP, q_ref.shape[0] // t, q_rows, score, finish, values,
                   DIFF_LOOKAHEAD)


def _mla_steps(q_ref, k_ref, vt_ref, o_ref):
    t = ATT_TILE
    w = MLA_QK_PAD
    kc = lax.broadcasted_iota(jnp.int32, (t, t), 0) // CHUNK
    qc = lax.broadcasted_iota(jnp.int32, (t, t), 1) // CHUNK
    allowed = kc <= qc

    def q_rows(h, qi):
        return q_ref[qi * t:(qi + 1) * t, h * w:(h + 1) * w]

    def score(h, qi, q, c):
        k = k_ref[c * t:(c + 1) * t, h * w:(h + 1) * w]
        s = lax.dot_general(k, q, _NT, preferred_element_type=F32)
        return jnp.where(allowed, s, NEG_BIG) if c == qi else s

    def values(h, c):
        return vt_ref[0, h * LANES:(h + 1) * LANES, c * t:(c + 1) * t]

    def finish(h, qi, o):
        o_ref[qi * t:(qi + 1) * t, h * LANES:(h + 1) * LANES] = o.T.astype(BF16)

    return _attend(ATT_HEADS_PER_STEP, q_ref.shape[0] // t, q_rows, score, finish, values,
                   MLA_LOOKAHEAD)


def _attn_kernel(dq_ref, dk_ref, dvt_ref, bias_ref, lam_ref, g_ref, qm_ref, km_ref, vmt_ref,
                 oa_ref, ob_ref, *, lambda_init):
    streams = [_diff_steps(dq_ref, dk_ref, dvt_ref, bias_ref, lam_ref, g_ref, oa_ref, lambda_init),
               _mla_steps(qm_ref, km_ref, vmt_ref, ob_ref)]
    while streams:
        for s in list(streams):
            if next(s, "done") == "done":
                streams.remove(s)


def _attn_call(dq, dk, dvt, bias, wp, l, qm, km, vmt, seq, lambda_init):
    ntok = dq.shape[0]
    n = ATT_HEADS_PER_STEP

    def rows(width):
        return pl.BlockSpec((seq, n * width), lambda b, h: (b, h))

    vt_spec = pl.BlockSpec((1, n * LANES, seq), lambda b, h: (b, h, 0))
    return pl.pallas_call(
        functools.partial(_attn_kernel, lambda_init=lambda_init),
        grid=(ntok // seq, HEADS // n),
        in_specs=[rows(LANES), rows(LANES), vt_spec,
                  pl.BlockSpec((n,) + bias.shape[1:], lambda b, h: (h, 0, 0, 0)),
                  _layer_spec(wp["lam"], l), _layer_spec(wp["subln"], l),
                  rows(MLA_QK_PAD), rows(MLA_QK_PAD), vt_spec],
        out_specs=[rows(LANES), rows(LANES)],
        out_shape=[jax.ShapeDtypeStruct((ntok, DIFF_WIDTH), BF16),
                   jax.ShapeDtypeStruct((ntok, MLA_WIDTH), BF16)],
        compiler_params=pltpu.CompilerParams(dimension_semantics=("arbitrary", "arbitrary"),
                                             vmem_limit_bytes=VMEM_LIMIT),
        name="attn",
    )(dq, dk, dvt, bias, wp["lam"], wp["subln"], qm, km, vmt)


def _post_kernel(x_ref, ma_ref, mb_ref, wo_ref, g1_ref, b1_ref, wi_ref, wout_ref, g2_ref, b2_ref,
                 o_ref):
    n_blocks = x_ref.shape[0] // POST_ROW_BLOCK
    n_chunks = D_FF // FF_CHUNK

    def attn_out(r):
        rs = slice(r * POST_ROW_BLOCK, (r + 1) * POST_ROW_BLOCK)
        y = (jnp.dot(ma_ref[rs, :], wo_ref[0, :DIFF_WIDTH, :], preferred_element_type=F32)
             + jnp.dot(mb_ref[rs, :], wo_ref[0, DIFF_WIDTH:, :], preferred_element_type=F32))
        return _layer_norm(ALPHA * x_ref[rs, :] + y, g1_ref[0], b1_ref[0])

    def up(x1b, c):
        return jnp.dot(x1b, wi_ref[0, :, c * FF_CHUNK:(c + 1) * FF_CHUNK],
                       preferred_element_type=F32)

    x1 = attn_out(0)
    for r in range(n_blocks):
        x1_next = attn_out(r + 1) if r + 1 < n_blocks else None
        x1b = x1.astype(BF16)
        u = up(x1b, 0)
        y2 = None
        for c in range(n_chunks):
            u_next = up(x1b, c + 1) if c + 1 < n_chunks else None
            act = jnp.square(jnp.maximum(u, 0.0)).astype(BF16)
            d = jnp.dot(act, wout_ref[0, c * FF_CHUNK:(c + 1) * FF_CHUNK, :],
                        preferred_element_type=F32)
            y2 = d if y2 is None else y2 + d
            u = u_next
        o_ref[r * POST_ROW_BLOCK:(r + 1) * POST_ROW_BLOCK, :] = _layer_norm(
            ALPHA * x1 + y2, g2_ref[0], b2_ref[0])
        x1 = x1_next


def _post_call(x2, mix_a, mix_b, wp, l):
    ntok = x2.shape[0]
    rows = POST_ROWS

    def const(name):
        return _layer_spec(wp[name], l, pipeline_mode=pl.Buffered(1))

    def tok(width):
        return pl.BlockSpec((rows, width), lambda i: (i, 0))

    return pl.pallas_call(
        _post_kernel,
        grid=(ntok // rows,),
        in_specs=[tok(D_MODEL), tok(DIFF_WIDTH), tok(MLA_WIDTH), const("wo"), const("g1"),
                  const("b1"), const("wi"), const("wout"), const("g2"), const("b2")],
        out_specs=tok(D_MODEL),
        out_shape=jax.ShapeDtypeStruct((ntok, D_MODEL), F32),
        compiler_params=pltpu.CompilerParams(dimension_semantics=("arbitrary",),
                                             vmem_limit_bytes=VMEM_LIMIT),
        name="post",
    )(x2, mix_a, mix_b, wp["wo"], wp["g1"], wp["b1"], wp["wi"], wp["wout"], wp["g2"], wp["b2"])


def _t5_bucket(rel):
    nb = N_BUCKETS // 2
    ret = (rel > 0).astype(jnp.int32) * nb
    n = jnp.abs(rel)
    max_exact = nb // 2
    nf = jnp.maximum(n, 1).astype(F32)
    large = max_exact + (jnp.log(nf / max_exact) / math.log(MAX_DISTANCE / max_exact)
                         * (nb - max_exact)).astype(jnp.int32)
    large = jnp.minimum(large, nb - 1)
    return ret + jnp.where(n < max_exact, n, large)


def _bias_tiles(rel_bias):
    t = ATT_TILE
    p = 2 * t
    table = rel_bias.astype(F32).T
    far = table[:, N_BUCKETS // 2 - 1]
    w = jnp.arange(p, dtype=jnp.int32)[None, :]
    d = jnp.arange(2, dtype=jnp.int32)[:, None]
    rel = (t - 1) - w - d * t
    vec = (table[:, _t5_bucket(rel)] - far[:, None, None]) * LOG2E
    rows = jnp.tile(vec, (1, 1, t))[..., :t * (p - 1)].reshape(HEADS, 2, t, p - 1)
    bias = rows[..., t - 1:2 * t - 1]
    kk = jnp.arange(t, dtype=jnp.int32)[:, None]
    qq = jnp.arange(t, dtype=jnp.int32)[None, :]
    allowed = (kk // CHUNK <= qq // CHUNK)[None, None] | (d > 0)[None, :, :, None]
    bias = jnp.where(allowed, bias, NEG_BIG)
    return jnp.concatenate([bias, bias], axis=-1)


def _rope_tables(seq):
    pos = jnp.arange(seq, dtype=F32)
    inv = ROPE_THETA ** (-jnp.arange(0, MLA_ROPE_DIM, 2, dtype=F32) / MLA_ROPE_DIM)
    ang = pos[:, None] * inv[None, :]
    cos, sin = jnp.cos(ang), jnp.sin(ang)
    z = jnp.zeros((seq, LANES - MLA_ROPE_DIM), F32)
    return (jnp.concatenate([cos, cos, z], axis=-1), jnp.concatenate([-sin, sin, z], axis=-1))


def _swap_halves(w):
    half = w.shape[-1] // 2
    return jnp.concatenate([w[..., half:], w[..., :half]], axis=-1)


def _pack_params(w_in, lambda_q1, lambda_k1, lambda_q2, lambda_k2, subln_g, q_norm_g, w_uq,
                 kv_norm_g, w_ukv, w_o, ln1_g, ln1_b, w_mlp_in, w_mlp_out, ln2_g, ln2_b):
    depth = w_in.shape[0]
    o_k, o_v, o_cq = DIFF_WIDTH, 2 * DIFF_WIDTH, 3 * DIFF_WIDTH
    o_ckv = o_cq + MLA_Q_RANK
    o_kr = o_ckv + MLA_KV_RANK
    kr = w_in[..., o_kr:]
    w1 = jnp.concatenate([w_in[..., :o_k] * (DIFF_HEAD_DIM ** -0.5 * LOG2E), w_in[..., o_k:o_v],
                          w_in[..., o_cq:o_ckv], w_in[..., o_ckv:o_kr], kr, _swap_halves(kr)],
                         axis=-1)
    mla_scale = (MLA_NOPE_DIM + MLA_ROPE_DIM) ** -0.5 * LOG2E
    uq = (w_uq * mla_scale).reshape(depth, MLA_Q_RANK, HEADS, MLA_NOPE_DIM + MLA_ROPE_DIM)
    zq = jnp.zeros((depth, MLA_Q_RANK, HEADS, MLA_QK_PAD - MLA_NOPE_DIM - MLA_ROPE_DIM), F32)
    wuq = jnp.concatenate([uq, zq], axis=-1).reshape(depth, MLA_Q_RANK, HEADS * MLA_QK_PAD)
    wuqs = jnp.concatenate([_swap_halves(uq[..., MLA_NOPE_DIM:]), zq], axis=-1)
    wuqs = wuqs.reshape(depth, MLA_Q_RANK, HEADS * LANES)
    ukv = w_ukv.reshape(depth, MLA_KV_RANK, HEADS, MLA_NOPE_DIM + MLA_V_DIM)
    wukvk = ukv[..., :MLA_NOPE_DIM].reshape(depth, MLA_KV_RANK, HEADS * MLA_NOPE_DIM)
    wukvv = ukv[..., MLA_NOPE_DIM:].reshape(depth, MLA_KV_RANK, HEADS * MLA_V_DIM)
    row = lambda v: v.reshape(depth, 1, -1).astype(F32)
    return dict(
        w1=w1.astype(BF16), wvt=jnp.swapaxes(w_in[..., o_v:o_cq], 1, 2).astype(BF16),
        wuq=wuq.astype(BF16), wuqs=wuqs.astype(BF16), wukvk=wukvk.astype(BF16),
        wukvvt=jnp.swapaxes(wukvv, 1, 2).astype(BF16), gq=row(q_norm_g), gkv=row(kv_norm_g),
        subln=row(subln_g),
        lam=jnp.stack([lambda_q1, lambda_k1, lambda_q2, lambda_k2], axis=1).astype(F32),
        wo=w_o.astype(BF16), g1=row(ln1_g), b1=row(ln1_b), wi=w_mlp_in.astype(BF16),
        wout=w_mlp_out.astype(BF16), g2=row(ln2_g), b2=row(ln2_b))


def kernel(x, w_in, lambda_q1, lambda_k1, lambda_q2, lambda_k2, subln_g, q_norm_g, w_uq, kv_norm_g,
           w_ukv, rel_bias, w_o, ln1_g, ln1_b, w_mlp_in, w_mlp_out, ln2_g, ln2_b):
    batch, seq, _ = x.shape
    assert seq % PROJ_ROWS == 0 and seq % ATT_TILE == 0 and (batch * seq) % POST_ROWS == 0
    wp = _pack_params(w_in, lambda_q1, lambda_k1, lambda_q2, lambda_k2, subln_g, q_norm_g, w_uq,
                      kv_norm_g, w_ukv, w_o, ln1_g, ln1_b, w_mlp_in, w_mlp_out, ln2_g, ln2_b)
    wp["cos"], wp["sin"] = _rope_tables(seq)
    bias = _bias_tiles(rel_bias)
    x2 = x.reshape(batch * seq, D_MODEL)
    for l in range(DEPTH):
        lambda_init = 0.8 - 0.6 * math.exp(-0.3 * l)
        dq, dk, dvt, qm, km, vmt = _proj_call(x2, wp, l, seq)
        mix_a, mix_b = _attn_call(dq, dk, dvt, bias, wp, l, qm, km, vmt, seq, lambda_init)
        x2 = _post_call(x2, mix_a, mix_b, wp, l)
    return x2.reshape(batch, seq, D_MODEL)
```

```python
import functools
import math

import jax
import jax.numpy as jnp
from jax import lax
from jax.experimental import pallas as pl
from jax.experimental.pallas import tpu as pltpu

F32 = jnp.float32
BF16 = jnp.bfloat16

D_MODEL = 1024
DEPTH = 2
CHUNK = 64
HEADS = 4
DIFF_HEAD_DIM = 64
DIFF_V_DIM = 2 * DIFF_HEAD_DIM
DIFF_WIDTH = HEADS * DIFF_V_DIM
MLA_NOPE_DIM = 128
MLA_ROPE_DIM = 64
MLA_V_DIM = 128
MLA_Q_RANK = 256
MLA_KV_RANK = 128
MLA_WIDTH = HEADS * MLA_V_DIM
MLA_QK_PAD = 256
D_FF = 4 * D_MODEL
N_BUCKETS = 32
MAX_DISTANCE = 128
ROPE_THETA = 10000.0
ALPHA = (2 * DEPTH) ** 0.25
LN_EPS = 1e-5
RMS_EPS = 1e-6
NEG_BIG = -1e30
LOG2E = math.log2(math.e)

LANES = 128
BF16_SUBLANES = 16
ATT_TILE = 256
ATT_HEADS_PER_STEP = 2
PROJ_ROWS = 1024
POST_ROWS = 1024
POST_ROW_BLOCK = 256
FF_CHUNK = 1024
DIFF_LOOKAHEAD = 2
MLA_LOOKAHEAD = 4
VMEM_LIMIT = 56 * 1024 * 1024
assert ATT_TILE >= MAX_DISTANCE and ATT_TILE % CHUNK == 0

C_DQ = 0
C_DK = C_DQ + DIFF_WIDTH
C_CQ = C_DK + DIFF_WIDTH
C_CKV = C_CQ + MLA_Q_RANK
C_KR = C_CKV + MLA_KV_RANK
W1_COLS = C_KR + LANES

_NT = (((1,), (1,)), ((), ()))


def _rms(x, g):
    return x * lax.rsqrt(jnp.mean(x * x, axis=-1, keepdims=True) + RMS_EPS) * g


def _layer_norm(x, g, b):
    mu = jnp.mean(x, axis=-1, keepdims=True)
    xc = x - mu
    var = jnp.mean(xc * xc, axis=-1, keepdims=True)
    return xc * lax.rsqrt(var + LN_EPS) * g + b


def _layer_spec(arr, l, **kwargs):
    index = (l,) + (0,) * (arr.ndim - 1)
    return pl.BlockSpec((1,) + arr.shape[1:], lambda *_: index, **kwargs)


def _proj_kernel(x_ref, w1_ref, wvt_ref, wuq_ref, wuqs_ref, wukvk_ref, wukvvt_ref,
                 gq_ref, gkv_ref, cos_ref, sin_ref,
                 dq_ref, dk_ref, dvt_ref, qm_ref, km_ref, vmt_ref):
    xb = x_ref[...].astype(BF16)
    h = jnp.dot(xb, w1_ref[0], preferred_element_type=F32)
    dq_ref[...] = h[:, C_DQ:C_DQ + DIFF_WIDTH].astype(BF16)
    dk_ref[...] = h[:, C_DK:C_DK + DIFF_WIDTH].astype(BF16)
    dvt_ref[0] = lax.dot_general(wvt_ref[0], xb, _NT, preferred_element_type=F32).astype(BF16)

    cos = cos_ref[...]
    sin = sin_ref[...]
    c_q = _rms(h[:, C_CQ:C_CQ + MLA_Q_RANK], gq_ref[0]).astype(BF16)
    qf = jnp.dot(c_q, wuq_ref[0], preferred_element_type=F32)
    qsw = jnp.dot(c_q, wuqs_ref[0], preferred_element_type=F32)
    c_kv = _rms(h[:, C_CKV:C_CKV + MLA_KV_RANK], gkv_ref[0]).astype(BF16)
    kn = jnp.dot(c_kv, wukvk_ref[0], preferred_element_type=F32)
    vmt_ref[0] = lax.dot_general(wukvvt_ref[0], c_kv, _NT,
                                 preferred_element_type=F32).astype(BF16)
    kr = h[:, C_KR:C_KR + LANES]
    k_rope = (kr * cos + pltpu.roll(kr, LANES // 2, 1) * sin).astype(BF16)

    for hh in range(HEADS):
        b0 = hh * MLA_QK_PAD
        qm_ref[:, b0:b0 + LANES] = qf[:, b0:b0 + LANES].astype(BF16)
        qm_ref[:, b0 + LANES:b0 + 2 * LANES] = (
            qf[:, b0 + LANES:b0 + 2 * LANES] * cos + qsw[:, hh * LANES:(hh + 1) * LANES] * sin
        ).astype(BF16)
        km_ref[:, b0:b0 + LANES] = kn[:, hh * LANES:(hh + 1) * LANES].astype(BF16)
        km_ref[:, b0 + LANES:b0 + 2 * LANES] = k_rope


def _proj_call(x2, wp, l, seq):
    ntok = x2.shape[0]
    batch = ntok // seq
    rows = PROJ_ROWS
    steps_per_seq = seq // rows

    def const(name):
        return _layer_spec(wp[name], l)

    def tok(width):
        return pl.BlockSpec((rows, width), lambda i: (i, 0))

    vt_spec = pl.BlockSpec((1, HEADS * LANES, rows),
                           lambda i: (i // steps_per_seq, 0, i % steps_per_seq))
    pos_spec = pl.BlockSpec((rows, LANES), lambda i: (i % steps_per_seq, 0))
    vt_shape = jax.ShapeDtypeStruct((batch, HEADS * LANES, seq), BF16)
    return pl.pallas_call(
        _proj_kernel,
        grid=(ntok // rows,),
        in_specs=[tok(D_MODEL), const("w1"), const("wvt"), const("wuq"), const("wuqs"),
                  const("wukvk"), const("wukvvt"), const("gq"), const("gkv"), pos_spec, pos_spec],
        out_specs=[tok(DIFF_WIDTH), tok(DIFF_WIDTH), vt_spec,
                   tok(HEADS * MLA_QK_PAD), tok(HEADS * MLA_QK_PAD), vt_spec],
        out_shape=[jax.ShapeDtypeStruct((ntok, DIFF_WIDTH), BF16),
                   jax.ShapeDtypeStruct((ntok, DIFF_WIDTH), BF16),
                   vt_shape,
                   jax.ShapeDtypeStruct((ntok, HEADS * MLA_QK_PAD), BF16),
                   jax.ShapeDtypeStruct((ntok, HEADS * MLA_QK_PAD), BF16),
                   vt_shape],
        compiler_params=pltpu.CompilerParams(dimension_semantics=("arbitrary",),
                                             vmem_limit_bytes=VMEM_LIMIT),
        name="proj",
    )(x2, wp["w1"], wp["wvt"], wp["wuq"], wp["wuqs"], wp["wukvk"], wp["wukvvt"],
      wp["gq"], wp["gkv"], wp["cos"], wp["sin"])


def _col_max(a, b):
    m = jnp.max(b, axis=0, keepdims=True)
    return m if a is None else jnp.maximum(a, m)


def _attend(n_heads, n_tiles, q_fn, score_fn, finish_fn, vt_fn, lookahead):
    ones = jnp.ones((BF16_SUBLANES, ATT_TILE), BF16)
    items = [(h, qi, c) for h in range(n_heads) for qi in reversed(range(n_tiles))
             for c in range(qi + 1)]
    qs, scores = {}, {}

    def issue(idx):
        h, qi, c = items[idx]
        if c == 0:
            qs[h, qi] = q_fn(h, qi)
        scores[idx] = score_fn(h, qi, qs[h, qi], c)

    for idx in range(min(lookahead, len(items))):
        issue(idx)
    m, acc = None, None
    for idx, (h, qi, c) in enumerate(items):
        if idx + lookahead < len(items):
            issue(idx + lookahead)
        s = scores.pop(idx)
        m_new = _col_max(m, s)
        e = jnp.exp2(s - m_new).astype(BF16)
        vt = jnp.concatenate([vt_fn(h, c), ones], axis=0)
        pv = jnp.dot(vt, e, preferred_element_type=F32)
        acc = pv if acc is None else acc * jnp.exp2(m - m_new) + pv
        m = m_new
        if c == qi:
            finish_fn(h, qi, acc[:LANES] / acc[LANES:LANES + 1])
            m, acc = None, None
        yield


def _diff_steps(q_ref, k_ref, vt_ref, bias_ref, lam_ref, g_ref, o_ref, lambda_init):
    t = ATT_TILE
    lane = lax.broadcasted_iota(jnp.int32, (t, LANES), 1)
    lp = lam_ref[0]
    lam = (jnp.exp(jnp.sum(lp[0:1] * lp[1:2], axis=-1, keepdims=True))
           - jnp.exp(jnp.sum(lp[2:3] * lp[3:4], axis=-1, keepdims=True)) + lambda_init)
    g = g_ref[0]

    def q_rows(h, qi):
        q = q_ref[qi * t:(qi + 1) * t, h * LANES:(h + 1) * LANES]
        zero = jnp.zeros_like(q)
        return jnp.concatenate([jnp.where(lane < DIFF_HEAD_DIM, q, zero),
                                jnp.where(lane >= DIFF_HEAD_DIM, q, zero)], axis=0)

    def score(h, qi, qs, c):
        k = k_ref[c * t:(c + 1) * t, h * LANES:(h + 1) * LANES]
        s = lax.dot_general(k, qs, _NT, preferred_element_type=F32)
        return s + bias_ref[h, qi - c] if qi - c <= 1 else s

    def values(h, c):
        return vt_ref[0, h * LANES:(h + 1) * LANES, c * t:(c + 1) * t]

    def finish(h, qi, o):
        ot = o[:, :t] - lam * o[:, t:]
        y = _rms(ot.T, g) * (1.0 - lambda_init)
        o_ref[qi * t:(qi + 1) * t, h * LANES:(h + 1) * LANES] = y.astype(BF16)

    return _attend(ATT_HEADS_PER_STEP, q_ref.shape[0] // t, q_rows, score, finish, values,
                   DIFF_LOOKAHEAD)


def _mla_steps(q_ref, k_ref, vt_ref, o_ref):
    t = ATT_TILE
    w = MLA_QK_PAD
    kc = lax.broadcasted_iota(jnp.int32, (t, t), 0) // CHUNK
    qc = lax.broadcasted_iota(jnp.int32, (t, t), 1) // CHUNK
    allowed = kc <= qc

    def q_rows(h, qi):
        return q_ref[qi * t:(qi + 1) * t, h * w:(h + 1) * w]

    def score(h, qi, q, c):
        k = k_ref[c * t:(c + 1) * t, h * w:(h + 1) * w]
        s = lax.dot_general(k, q, _NT, preferred_element_type=F32)
        return jnp.where(allowed, s, NEG_BIG) if c == qi else s

    def values(h, c):
        return vt_ref[0, h * LANES:(h + 1) * LANES, c * t:(c + 1) * t]

    def finish(h, qi, o):
        o_ref[qi * t:(qi + 1) * t, h * LANES:(h + 1) * LANES] = o.T.astype(BF16)

    return _attend(ATT_HEADS_PER_STEP, q_ref.shape[0] // t, q_rows, score, finish, values,
                   MLA_LOOKAHEAD)


def _attn_kernel(dq_ref, dk_ref, dvt_ref, bias_ref, lam_ref, g_ref, qm_ref, km_ref, vmt_ref,
                 oa_ref, ob_ref, *, lambda_init):
    streams = [_diff_steps(dq_ref, dk_ref, dvt_ref, bias_ref, lam_ref, g_ref, oa_ref, lambda_init),
               _mla_steps(qm_ref, km_ref, vmt_ref, ob_ref)]
    while streams:
        for s in list(streams):
            if next(s, "done") == "done":
                streams.remove(s)


def _attn_call(dq, dk, dvt, bias, wp, l, qm, km, vmt, seq, lambda_init):
    ntok = dq.shape[0]
    n = ATT_HEADS_PER_STEP

    def rows(width):
        return pl.BlockSpec((seq, n * width), lambda b, h: (b, h))

    vt_spec = pl.BlockSpec((1, n * LANES, seq), lambda b, h: (b, h, 0))
    return pl.pallas_call(
        functools.partial(_attn_kernel, lambda_init=lambda_init),
        grid=(ntok // seq, HEADS // n),
        in_specs=[rows(LANES), rows(LANES), vt_spec,
                  pl.BlockSpec((n,) + bias.shape[1:], lambda b, h: (h, 0, 0, 0)),
                  _layer_spec(wp["lam"], l), _layer_spec(wp["subln"], l),
                  rows(MLA_QK_PAD), rows(MLA_QK_PAD), vt_spec],
        out_specs=[rows(LANES), rows(LANES)],
        out_shape=[jax.ShapeDtypeStruct((ntok, DIFF_WIDTH), BF16),
                   jax.ShapeDtypeStruct((ntok, MLA_WIDTH), BF16)],
        compiler_params=pltpu.CompilerParams(dimension_semantics=("arbitrary", "arbitrary"),
                                             vmem_limit_bytes=VMEM_LIMIT),
        name="attn",
    )(dq, dk, dvt, bias, wp["lam"], wp["subln"], qm, km, vmt)


def _post_kernel(x_ref, ma_ref, mb_ref, wo_ref, g1_ref, b1_ref, wi_ref, wout_ref, g2_ref, b2_ref,
                 o_ref):
    n_blocks = x_ref.shape[0] // POST_ROW_BLOCK
    n_chunks = D_FF // FF_CHUNK

    def attn_out(r):
        rs = slice(r * POST_ROW_BLOCK, (r + 1) * POST_ROW_BLOCK)
        y = (jnp.dot(ma_ref[rs, :], wo_ref[0, :DIFF_WIDTH, :], preferred_element_type=F32)
             + jnp.dot(mb_ref[rs, :], wo_ref[0, DIFF_WIDTH:, :], preferred_element_type=F32))
        return _layer_norm(ALPHA * x_ref[rs, :] + y, g1_ref[0], b1_ref[0])

    def up(x1b, c):
        return jnp.dot(x1b, wi_ref[0, :, c * FF_CHUNK:(c + 1) * FF_CHUNK],
                       preferred_element_type=F32)

    x1 = attn_out(0)
    for r in range(n_blocks):
        x1_next = attn_out(r + 1) if r + 1 < n_blocks else None
        x1b = x1.astype(BF16)
        u = up(x1b, 0)
        y2 = None
        for c in range(n_chunks):
            u_next = up(x1b, c + 1) if c + 1 < n_chunks else None
            act = jnp.square(jnp.maximum(u, 0.0)).astype(BF16)
            d = jnp.dot(act, wout_ref[0, c * FF_CHUNK:(c + 1) * FF_CHUNK, :],
                        preferred_element_type=F32)
            y2 = d if y2 is None else y2 + d
            u = u_next
        o_ref[r * POST_ROW_BLOCK:(r + 1) * POST_ROW_BLOCK, :] = _layer_norm(
            ALPHA * x1 + y2, g2_ref[0], b2_ref[0])
        x1 = x1_next


def _post_call(x2, mix_a, mix_b, wp, l):
    ntok = x2.shape[0]
    rows = POST_ROWS

    def const(name):
        return _layer_spec(wp[name], l, pipeline_mode=pl.Buffered(1))

    def tok(width):
        return pl.BlockSpec((rows, width), lambda i: (i, 0))

    return pl.pallas_call(
        _post_kernel,
        grid=(ntok // rows,),
        in_specs=[tok(D_MODEL), tok(DIFF_WIDTH), tok(MLA_WIDTH), const("wo"), const("g1"),
                  const("b1"), const("wi"), const("wout"), const("g2"), const("b2")],
        out_specs=tok(D_MODEL),
        out_shape=jax.ShapeDtypeStruct((ntok, D_MODEL), F32),
        compiler_params=pltpu.CompilerParams(dimension_semantics=("arbitrary",),
                                             vmem_limit_bytes=VMEM_LIMIT),
        name="post",
    )(x2, mix_a, mix_b, wp["wo"], wp["g1"], wp["b1"], wp["wi"], wp["wout"], wp["g2"], wp["b2"])


def _t5_bucket(rel):
    nb = N_BUCKETS // 2
    ret = (rel > 0).astype(jnp.int32) * nb
    n = jnp.abs(rel)
    max_exact = nb // 2
    nf = jnp.maximum(n, 1).astype(F32)
    large = max_exact + (jnp.log(nf / max_exact) / math.log(MAX_DISTANCE / max_exact)
                         * (nb - max_exact)).astype(jnp.int32)
    large = jnp.minimum(large, nb - 1)
    return ret + jnp.where(n < max_exact, n, large)


def _bias_tiles(rel_bias):
    t = ATT_TILE
    p = 2 * t
    table = rel_bias.astype(F32).T
    far = table[:, N_BUCKETS // 2 - 1]
    w = jnp.arange(p, dtype=jnp.int32)[None, :]
    d = jnp.arange(2, dtype=jnp.int32)[:, None]
    rel = (t - 1) - w - d * t
    vec = (table[:, _t5_bucket(rel)] - far[:, None, None]) * LOG2E
    rows = jnp.tile(vec, (1, 1, t))[..., :t * (p - 1)].reshape(HEADS, 2, t, p - 1)
    bias = rows[..., t - 1:2 * t - 1]
    kk = jnp.arange(t, dtype=jnp.int32)[:, None]
    qq = jnp.arange(t, dtype=jnp.int32)[None, :]
    allowed = (kk // CHUNK <= qq // CHUNK)[None, None] | (d > 0)[None, :, :, None]
    bias = jnp.where(allowed, bias, NEG_BIG)
    return jnp.concatenate([bias, bias], axis=-1)


def _rope_tables(seq):
    pos = jnp.arange(seq, dtype=F32)
    inv = ROPE_THETA ** (-jnp.arange(0, MLA_ROPE_DIM, 2, dtype=F32) / MLA_ROPE_DIM)
    ang = pos[:, None] * inv[None, :]
    cos, sin = jnp.cos(ang), jnp.sin(ang)
    z = jnp.zeros((seq, LANES - MLA_ROPE_DIM), F32)
    return (jnp.concatenate([cos, cos, z], axis=-1), jnp.concatenate([-sin, sin, z], axis=-1))


def _swap_halves(w):
    half = w.shape[-1] // 2
    return jnp.concatenate([w[..., half:], w[..., :half]], axis=-1)


def _pack_params(w_in, lambda_q1, lambda_k1, lambda_q2, lambda_k2, subln_g, q_norm_g, w_uq,
                 kv_norm_g, w_ukv, w_o, ln1_g, ln1_b, w_mlp_in, w_mlp_out, ln2_g, ln2_b):
    depth = w_in.shape[0]
    o_k, o_v, o_cq = DIFF_WIDTH, 2 * DIFF_WIDTH, 3 * DIFF_WIDTH
    o_ckv = o_cq + MLA_Q_RANK
    o_kr = o_ckv + MLA_KV_RANK
    kr = w_in[..., o_kr:]
    w1 = jnp.concatenate([w_in[..., :o_k] * (DIFF_HEAD_DIM ** -0.5 * LOG2E), w_in[..., o_k:o_v],
                          w_in[..., o_cq:o_ckv], w_in[..., o_ckv:o_kr], kr, _swap_halves(kr)],
                         axis=-1)
    mla_scale = (MLA_NOPE_DIM + MLA_ROPE_DIM) ** -0.5 * LOG2E
    uq = (w_uq * mla_scale).reshape(depth, MLA_Q_RANK, HEADS, MLA_NOPE_DIM + MLA_ROPE_DIM)
    zq = jnp.zeros((depth, MLA_Q_RANK, HEADS, MLA_QK_PAD - MLA_NOPE_DIM - MLA_ROPE_DIM), F32)
    wuq = jnp.concatenate([uq, zq], axis=-1).reshape(depth, MLA_Q_RANK, HEADS * MLA_QK_PAD)
    wuqs = jnp.concatenate([_swap_halves(uq[..., MLA_NOPE_DIM:]), zq], axis=-1)
    wuqs = wuqs.reshape(depth, MLA_Q_RANK, HEADS * LANES)
    ukv = w_ukv.reshape(depth, MLA_KV_RANK, HEADS, MLA_NOPE_DIM + MLA_V_DIM)
    wukvk = ukv[..., :MLA_NOPE_DIM].reshape(depth, MLA_KV_RANK, HEADS * MLA_NOPE_DIM)
    wukvv = ukv[..., MLA_NOPE_DIM:].reshape(depth, MLA_KV_RANK, HEADS * MLA_V_DIM)
    row = lambda v: v.reshape(depth, 1, -1).astype(F32)
    return dict(
        w1=w1.astype(BF16), wvt=jnp.swapaxes(w_in[..., o_v:o_cq], 1, 2).astype(BF16),
        wuq=wuq.astype(BF16), wuqs=wuqs.astype(BF16), wukvk=wukvk.astype(BF16),
        wukvvt=jnp.swapaxes(wukvv, 1, 2).astype(BF16), gq=row(q_norm_g), gkv=row(kv_norm_g),
        subln=row(subln_g),
        lam=jnp.stack([lambda_q1, lambda_k1, lambda_q2, lambda_k2], axis=1).astype(F32),
        wo=w_o.astype(BF16), g1=row(ln1_g), b1=row(ln1_b), wi=w_mlp_in.astype(BF16),
        wout=w_mlp_out.astype(BF16), g2=row(ln2_g), b2=row(ln2_b))


def kernel(x, w_in, lambda_q1, lambda_k1, lambda_q2, lambda_k2, subln_g, q_norm_g, w_uq, kv_norm_g,
           w_ukv, rel_bias, w_o, ln1_g, ln1_b, w_mlp_in, w_mlp_out, ln2_g, ln2_b):
    batch, seq, _ = x.shape
    assert seq % PROJ_ROWS == 0 and seq % ATT_TILE == 0 and (batch * seq) % POST_ROWS == 0
    wp = _pack_params(w_in, lambda_q1, lambda_k1, lambda_q2, lambda_k2, subln_g, q_norm_g, w_uq,
                      kv_norm_g, w_ukv, w_o, ln1_g, ln1_b, w_mlp_in, w_mlp_out, ln2_g, ln2_b)
    wp["cos"], wp["sin"] = _rope_tables(seq)
    bias = _bias_tiles(rel_bias)
    x2 = x.reshape(batch * seq, D_MODEL)
    for l in range(DEPTH):
        lambda_init = 0.8 - 0.6 * math.exp(-0.3 * l)
        dq, dk, dvt, qm, km, vmt = _proj_call(x2, wp, l, seq)
        mix_a, mix_b = _attn_call(dq, dk, dvt, bias, wp, l, qm, km, vmt, seq, lambda_init)
        x2 = _post_call(x2, mix_a, mix_b, wp, l)
    return x2.reshape(batch, seq, D_MODEL)
```

```python
import functools
import math

import jax
import jax.numpy as jnp
from jax import lax
from jax.experimental import pallas as pl
from jax.experimental.pallas import tpu as pltpu

F32 = jnp.float32
BF16 = jnp.bfloat16

D_MODEL = 1024
DEPTH = 2
CHUNK = 64
HEADS = 4
DIFF_HEAD_DIM = 64
DIFF_V_DIM = 2 * DIFF_HEAD_DIM
DIFF_WIDTH = HEADS * DIFF_V_DIM
MLA_NOPE_DIM = 128
MLA_ROPE_DIM = 64
MLA_V_DIM = 128
MLA_Q_RANK = 256
MLA_KV_RANK = 128
MLA_WIDTH = HEADS * MLA_V_DIM
MLA_QK_PAD = 256
D_FF = 4 * D_MODEL
N_BUCKETS = 32
MAX_DISTANCE = 128
ROPE_THETA = 10000.0
ALPHA = (2 * DEPTH) ** 0.25
LN_EPS = 1e-5
RMS_EPS = 1e-6
NEG_BIG = -1e30
LOG2E = math.log2(math.e)

LANES = 128
BF16_SUBLANES = 16
ATT_TILE = 256
ATT_HEADS_PER_STEP = 2
PROJ_ROWS = 1024
POST_ROWS = 1024
POST_ROW_BLOCK = 256
FF_CHUNK = 1024
DIFF_LOOKAHEAD = 2
MLA_LOOKAHEAD = 2
VMEM_LIMIT = 56 * 1024 * 1024
assert ATT_TILE >= MAX_DISTANCE and ATT_TILE % CHUNK == 0

C_DQ = 0
C_DK = C_DQ + DIFF_WIDTH
C_CQ = C_DK + DIFF_WIDTH
C_CKV = C_CQ + MLA_Q_RANK
C_KR = C_CKV + MLA_KV_RANK
W1_COLS = C_KR + LANES

_NT = (((1,), (1,)), ((), ()))


def _rms(x, g):
    return x * lax.rsqrt(jnp.mean(x * x, axis=-1, keepdims=True) + RMS_EPS) * g


def _layer_norm(x, g, b):
    mu = jnp.mean(x, axis=-1, keepdims=True)
    xc = x - mu
    var = jnp.mean(xc * xc, axis=-1, keepdims=True)
    return xc * lax.rsqrt(var + LN_EPS) * g + b


def _layer_spec(arr, l, **kwargs):
    index = (l,) + (0,) * (arr.ndim - 1)
    return pl.BlockSpec((1,) + arr.shape[1:], lambda *_: index, **kwargs)


def _proj_kernel(x_ref, w1_ref, wvt_ref, wuq_ref, wuqs_ref, wukvk_ref, wukvvt_ref,
                 gq_ref, gkv_ref, cos_ref, sin_ref,
                 dq_ref, dk_ref, dvt_ref, qm_ref, km_ref, vmt_ref):
    xb = x_ref[...].astype(BF16)
    h = jnp.dot(xb, w1_ref[0], preferred_element_type=F32)
    dq_ref[...] = h[:, C_DQ:C_DQ + DIFF_WIDTH].astype(BF16)
    dk_ref[...] = h[:, C_DK:C_DK + DIFF_WIDTH].astype(BF16)
    dvt_ref[0] = lax.dot_general(wvt_ref[0], xb, _NT, preferred_element_type=F32).astype(BF16)

    cos = cos_ref[...]
    sin = sin_ref[...]
    c_q = _rms(h[:, C_CQ:C_CQ + MLA_Q_RANK], gq_ref[0]).astype(BF16)
    qf = jnp.dot(c_q, wuq_ref[0], preferred_element_type=F32)
    qsw = jnp.dot(c_q, wuqs_ref[0], preferred_element_type=F32)
    c_kv = _rms(h[:, C_CKV:C_CKV + MLA_KV_RANK], gkv_ref[0]).astype(BF16)
    kn = jnp.dot(c_kv, wukvk_ref[0], preferred_element_type=F32)
    vmt_ref[0] = lax.dot_general(wukvvt_ref[0], c_kv, _NT,
                                 preferred_element_type=F32).astype(BF16)
    kr = h[:, C_KR:C_KR + LANES]
    k_rope = (kr * cos + pltpu.roll(kr, LANES // 2, 1) * sin).astype(BF16)

    for hh in range(HEADS):
        b0 = hh * MLA_QK_PAD
        qm_ref[:, b0:b0 + LANES] = qf[:, b0:b0 + LANES].astype(BF16)
        qm_ref[:, b0 + LANES:b0 + 2 * LANES] = (
            qf[:, b0 + LANES:b0 + 2 * LANES] * cos + qsw[:, hh * LANES:(hh + 1) * LANES] * sin
        ).astype(BF16)
        km_ref[:, b0:b0 + LANES] = kn[:, hh * LANES:(hh + 1) * LANES].astype(BF16)
        km_ref[:, b0 + LANES:b0 + 2 * LANES] = k_rope


def _proj_call(x2, wp, l, seq):
    ntok = x2.shape[0]
    batch = ntok // seq
    rows = PROJ_ROWS
    steps_per_seq = seq // rows

    def const(name):
        return _layer_spec(wp[name], l)

    def tok(width):
        return pl.BlockSpec((rows, width), lambda i: (i, 0))

    vt_spec = pl.BlockSpec((1, HEADS * LANES, rows),
                           lambda i: (i // steps_per_seq, 0, i % steps_per_seq))
    pos_spec = pl.BlockSpec((rows, LANES), lambda i: (i % steps_per_seq, 0))
    vt_shape = jax.ShapeDtypeStruct((batch, HEADS * LANES, seq), BF16)
    return pl.pallas_call(
        _proj_kernel,
        grid=(ntok // rows,),
        in_specs=[tok(D_MODEL), const("w1"), const("wvt"), const("wuq"), const("wuqs"),
                  const("wukvk"), const("wukvvt"), const("gq"), const("gkv"), pos_spec, pos_spec],
        out_specs=[tok(DIFF_WIDTH), tok(DIFF_WIDTH), vt_spec,
                   tok(HEADS * MLA_QK_PAD), tok(HEADS * MLA_QK_PAD), vt_spec],
        out_shape=[jax.ShapeDtypeStruct((ntok, DIFF_WIDTH), BF16),
                   jax.ShapeDtypeStruct((ntok, DIFF_WIDTH), BF16),
                   vt_shape,
                   jax.ShapeDtypeStruct((ntok, HEADS * MLA_QK_PAD), BF16),
                   jax.ShapeDtypeStruct((ntok, HEADS * MLA_QK_PAD), BF16),
                   vt_shape],
        compiler_params=pltpu.CompilerParams(dimension_semantics=("arbitrary",),
                                             vmem_limit_bytes=VMEM_LIMIT),
        name="proj",
    )(x2, wp["w1"], wp["wvt"], wp["wuq"], wp["wuqs"], wp["wukvk"], wp["wukvvt"],
      wp["gq"], wp["gkv"], wp["cos"], wp["sin"])


def _col_max(a, b):
    m = jnp.max(b, axis=0, keepdims=True)
    return m if a is None else jnp.maximum(a, m)


def _attend(n_heads, n_blocks, n_keys_fn, q_fn, score_fn, finish_fn, vt_fn, lookahead):
    ones = jnp.ones((BF16_SUBLANES, ATT_TILE), BF16)
    items = [(h, qi, c) for h in range(n_heads) for qi in range(n_blocks)
             for c in range(n_keys_fn(qi))]
    qs, scores = {}, {}

    def issue(idx):
        h, qi, c = items[idx]
        if c == 0:
            qs[h, qi] = q_fn(h, qi)
        scores[idx] = score_fn(h, qi, qs[h, qi], c)

    for idx in range(min(lookahead, len(items))):
        issue(idx)
    m, acc = None, None
    for idx, (h, qi, c) in enumerate(items):
        if idx + lookahead < len(items):
            issue(idx + lookahead)
        s = scores.pop(idx)
        m_new = _col_max(m, s)
        e = jnp.exp2(s - m_new).astype(BF16)
        vt = jnp.concatenate([vt_fn(h, c), ones], axis=0)
        pv = jnp.dot(vt, e, preferred_element_type=F32)
        acc = pv if acc is None else acc * jnp.exp2(m - m_new) + pv
        m = m_new
        if c == n_keys_fn(qi) - 1:
            finish_fn(h, qi, acc[:LANES] / acc[LANES:LANES + 1])
            m, acc = None, None
        yield


def _diff_steps(q_ref, k_ref, vt_ref, bias_ref, lam_ref, g_ref, o_ref, lambda_init):
    t = ATT_TILE
    lane = lax.broadcasted_iota(jnp.int32, (t, LANES), 1)
    lp = lam_ref[0]
    lam = (jnp.exp(jnp.sum(lp[0:1] * lp[1:2], axis=-1, keepdims=True))
           - jnp.exp(jnp.sum(lp[2:3] * lp[3:4], axis=-1, keepdims=True)) + lambda_init)
    g = g_ref[0]

    def q_rows(h, qi):
        q = q_ref[qi * t:(qi + 1) * t, h * LANES:(h + 1) * LANES]
        zero = jnp.zeros_like(q)
        return jnp.concatenate([jnp.where(lane < DIFF_HEAD_DIM, q, zero),
                                jnp.where(lane >= DIFF_HEAD_DIM, q, zero)], axis=0)

    def score(h, qi, qs, c):
        k = k_ref[c * t:(c + 1) * t, h * LANES:(h + 1) * LANES]
        s = lax.dot_general(k, qs, _NT, preferred_element_type=F32)
        return s + bias_ref[h, qi - c] if qi - c <= 1 else s

    def values(h, c):
        return vt_ref[0, h * LANES:(h + 1) * LANES, c * t:(c + 1) * t]

    def finish(h, qi, o):
        ot = o[:, :t] - lam * o[:, t:]
        y = _rms(ot.T, g) * (1.0 - lambda_init)
        o_ref[qi * t:(qi + 1) * t, h * LANES:(h + 1) * LANES] = y.astype(BF16)

    return _attend(ATT_HEADS_PER_STEP, q_ref.shape[0] // t, lambda qi: qi + 1, q_rows, score,
                   finish, values, DIFF_LOOKAHEAD)


def _mla_steps(q_ref, k_ref, vt_ref, o_ref):
    t = ATT_TILE
    w = MLA_QK_PAD
    kc = lax.broadcasted_iota(jnp.int32, (t, 2 * t), 0) // CHUNK
    col = lax.broadcasted_iota(jnp.int32, (t, 2 * t), 1)
    second = col >= t
    allowed = kc <= jnp.where(second, col - t, col) // CHUNK
    mask_even = second | allowed
    mask_odd = second & allowed

    def q_rows(h, qi):
        return q_ref[qi * 2 * t:(qi + 1) * 2 * t, h * w:(h + 1) * w]

    def score(h, qi, q, c):
        k = k_ref[c * t:(c + 1) * t, h * w:(h + 1) * w]
        s = lax.dot_general(k, q, _NT, preferred_element_type=F32)
        if c == 2 * qi:
            return jnp.where(mask_even, s, NEG_BIG)
        if c == 2 * qi + 1:
            return jnp.where(mask_odd, s, NEG_BIG)
        return s

    def values(h, c):
        return vt_ref[0, h * LANES:(h + 1) * LANES, c * t:(c + 1) * t]

    def finish(h, qi, o):
        o_ref[qi * 2 * t:(qi + 1) * 2 * t, h * LANES:(h + 1) * LANES] = o.T.astype(BF16)

    return _attend(ATT_HEADS_PER_STEP, q_ref.shape[0] // (2 * t), lambda qi: 2 * qi + 2, q_rows,
                   score, finish, values, MLA_LOOKAHEAD)


def _attn_kernel(dq_ref, dk_ref, dvt_ref, bias_ref, lam_ref, g_ref, qm_ref, km_ref, vmt_ref,
                 oa_ref, ob_ref, *, lambda_init):
    streams = [_diff_steps(dq_ref, dk_ref, dvt_ref, bias_ref, lam_ref, g_ref, oa_ref, lambda_init),
               _mla_steps(qm_ref, km_ref, vmt_ref, ob_ref)]
    n_tiles = dq_ref.shape[0] // ATT_TILE
    totals = [ATT_HEADS_PER_STEP * n_tiles * (n_tiles + 1) // 2,
              ATT_HEADS_PER_STEP * (n_tiles // 2) * (n_tiles // 2 + 1)]
    done = [0, 0]
    while done[0] < totals[0] or done[1] < totals[1]:
        i = 0 if done[0] * totals[1] <= done[1] * totals[0] else 1
        if done[i] == totals[i]:
            i = 1 - i
        next(streams[i])
        done[i] += 1


def _attn_call(dq, dk, dvt, bias, wp, l, qm, km, vmt, seq, lambda_init):
    ntok = dq.shape[0]
    n = ATT_HEADS_PER_STEP

    def rows(width):
        return pl.BlockSpec((seq, n * width), lambda b, h: (b, h))

    vt_spec = pl.BlockSpec((1, n * LANES, seq), lambda b, h: (b, h, 0))
    return pl.pallas_call(
        functools.partial(_attn_kernel, lambda_init=lambda_init),
        grid=(ntok // seq, HEADS // n),
        in_specs=[rows(LANES), rows(LANES), vt_spec,
                  pl.BlockSpec((n,) + bias.shape[1:], lambda b, h: (h, 0, 0, 0)),
                  _layer_spec(wp["lam"], l), _layer_spec(wp["subln"], l),
                  rows(MLA_QK_PAD), rows(MLA_QK_PAD), vt_spec],
        out_specs=[rows(LANES), rows(LANES)],
        out_shape=[jax.ShapeDtypeStruct((ntok, DIFF_WIDTH), BF16),
                   jax.ShapeDtypeStruct((ntok, MLA_WIDTH), BF16)],
        compiler_params=pltpu.CompilerParams(dimension_semantics=("arbitrary", "arbitrary"),
                                             vmem_limit_bytes=VMEM_LIMIT),
        name="attn",
    )(dq, dk, dvt, bias, wp["lam"], wp["subln"], qm, km, vmt)


def _post_kernel(x_ref, ma_ref, mb_ref, wo_ref, g1_ref, b1_ref, wi_ref, wout_ref, g2_ref, b2_ref,
                 o_ref):
    n_blocks = x_ref.shape[0] // POST_ROW_BLOCK
    n_chunks = D_FF // FF_CHUNK

    def attn_out(r):
        rs = slice(r * POST_ROW_BLOCK, (r + 1) * POST_ROW_BLOCK)
        y = (jnp.dot(ma_ref[rs, :], wo_ref[0, :DIFF_WIDTH, :], preferred_element_type=F32)
             + jnp.dot(mb_ref[rs, :], wo_ref[0, DIFF_WIDTH:, :], preferred_element_type=F32))
        return _layer_norm(ALPHA * x_ref[rs, :] + y, g1_ref[0], b1_ref[0])

    def up(x1b, c):
        return jnp.dot(x1b, wi_ref[0, :, c * FF_CHUNK:(c + 1) * FF_CHUNK],
                       preferred_element_type=F32)

    x1 = attn_out(0)
    for r in range(n_blocks):
        x1_next = attn_out(r + 1) if r + 1 < n_blocks else None
        x1b = x1.astype(BF16)
        u = up(x1b, 0)
        y2 = None
        for c in range(n_chunks):
            u_next = up(x1b, c + 1) if c + 1 < n_chunks else None
            act = jnp.square(jnp.maximum(u, 0.0)).astype(BF16)
            d = jnp.dot(act, wout_ref[0, c * FF_CHUNK:(c + 1) * FF_CHUNK, :],
                        preferred_element_type=F32)
            y2 = d if y2 is None else y2 + d
            u = u_next
        o_ref[r * POST_ROW_BLOCK:(r + 1) * POST_ROW_BLOCK, :] = _layer_norm(
            ALPHA * x1 + y2, g2_ref[0], b2_ref[0])
        x1 = x1_next


def _post_call(x2, mix_a, mix_b, wp, l):
    ntok = x2.shape[0]
    rows = POST_ROWS

    def const(name):
        return _layer_spec(wp[name], l, pipeline_mode=pl.Buffered(1))

    def tok(width):
        return pl.BlockSpec((rows, width), lambda i: (i, 0))

    return pl.pallas_call(
        _post_kernel,
        grid=(ntok // rows,),
        in_specs=[tok(D_MODEL), tok(DIFF_WIDTH), tok(MLA_WIDTH), const("wo"), const("g1"),
                  const("b1"), const("wi"), const("wout"), const("g2"), const("b2")],
        out_specs=tok(D_MODEL),
        out_shape=jax.ShapeDtypeStruct((ntok, D_MODEL), F32),
        compiler_params=pltpu.CompilerParams(dimension_semantics=("arbitrary",),
                                             vmem_limit_bytes=VMEM_LIMIT),
        name="post",
    )(x2, mix_a, mix_b, wp["wo"], wp["g1"], wp["b1"], wp["wi"], wp["wout"], wp["g2"], wp["b2"])


def _t5_bucket(rel):
    nb = N_BUCKETS // 2
    ret = (rel > 0).astype(jnp.int32) * nb
    n = jnp.abs(rel)
    max_exact = nb // 2
    nf = jnp.maximum(n, 1).astype(F32)
    large = max_exact + (jnp.log(nf / max_exact) / math.log(MAX_DISTANCE / max_exact)
                         * (nb - max_exact)).astype(jnp.int32)
    large = jnp.minimum(large, nb - 1)
    return ret + jnp.where(n < max_exact, n, large)


def _bias_tiles(rel_bias):
    t = ATT_TILE
    p = 2 * t
    table = rel_bias.astype(F32).T
    far = table[:, N_BUCKETS // 2 - 1]
    w = jnp.arange(p, dtype=jnp.int32)[None, :]
    d = jnp.arange(2, dtype=jnp.int32)[:, None]
    rel = (t - 1) - w - d * t
    vec = (table[:, _t5_bucket(rel)] - far[:, None, None]) * LOG2E
    rows = jnp.tile(vec, (1, 1, t))[..., :t * (p - 1)].reshape(HEADS, 2, t, p - 1)
    bias = rows[..., t - 1:2 * t - 1]
    kk = jnp.arange(t, dtype=jnp.int32)[:, None]
    qq = jnp.arange(t, dtype=jnp.int32)[None, :]
    allowed = (kk // CHUNK <= qq // CHUNK)[None, None] | (d > 0)[None, :, :, None]
    bias = jnp.where(allowed, bias, NEG_BIG)
    return jnp.concatenate([bias, bias], axis=-1)


def _rope_tables(seq):
    pos = jnp.arange(seq, dtype=F32)
    inv = ROPE_THETA ** (-jnp.arange(0, MLA_ROPE_DIM, 2, dtype=F32) / MLA_ROPE_DIM)
    ang = pos[:, None] * inv[None, :]
    cos, sin = jnp.cos(ang), jnp.sin(ang)
    z = jnp.zeros((seq, LANES - MLA_ROPE_DIM), F32)
    return (jnp.concatenate([cos, cos, z], axis=-1), jnp.concatenate([-sin, sin, z], axis=-1))


def _swap_halves(w):
    half = w.shape[-1] // 2
    return jnp.concatenate([w[..., half:], w[..., :half]], axis=-1)


def _pack_params(w_in, lambda_q1, lambda_k1, lambda_q2, lambda_k2, subln_g, q_norm_g, w_uq,
                 kv_norm_g, w_ukv, w_o, ln1_g, ln1_b, w_mlp_in, w_mlp_out, ln2_g, ln2_b):
    depth = w_in.shape[0]
    o_k, o_v, o_cq = DIFF_WIDTH, 2 * DIFF_WIDTH, 3 * DIFF_WIDTH
    o_ckv = o_cq + MLA_Q_RANK
    o_kr = o_ckv + MLA_KV_RANK
    kr = w_in[..., o_kr:]
    w1 = jnp.concatenate([w_in[..., :o_k] * (DIFF_HEAD_DIM ** -0.5 * LOG2E), w_in[..., o_k:o_v],
                          w_in[..., o_cq:o_ckv], w_in[..., o_ckv:o_kr], kr, _swap_halves(kr)],
                         axis=-1)
    mla_scale = (MLA_NOPE_DIM + MLA_ROPE_DIM) ** -0.5 * LOG2E
    uq = (w_uq * mla_scale).reshape(depth, MLA_Q_RANK, HEADS, MLA_NOPE_DIM + MLA_ROPE_DIM)
    zq = jnp.zeros((depth, MLA_Q_RANK, HEADS, MLA_QK_PAD - MLA_NOPE_DIM - MLA_ROPE_DIM), F32)
    wuq = jnp.concatenate([uq, zq], axis=-1).reshape(depth, MLA_Q_RANK, HEADS * MLA_QK_PAD)
    wuqs = jnp.concatenate([_swap_halves(uq[..., MLA_NOPE_DIM:]), zq], axis=-1)
    wuqs = wuqs.reshape(depth, MLA_Q_RANK, HEADS * LANES)
    ukv = w_ukv.reshape(depth, MLA_KV_RANK, HEADS, MLA_NOPE_DIM + MLA_V_DIM)
    wukvk = ukv[..., :MLA_NOPE_DIM].reshape(depth, MLA_KV_RANK, HEADS * MLA_NOPE_DIM)
    wukvv = ukv[..., MLA_NOPE_DIM:].reshape(depth, MLA_KV_RANK, HEADS * MLA_V_DIM)
    row = lambda v: v.reshape(depth, 1, -1).astype(F32)
    return dict(
        w1=w1.astype(BF16), wvt=jnp.swapaxes(w_in[..., o_v:o_cq], 1, 2).astype(BF16),
        wuq=wuq.astype(BF16), wuqs=wuqs.astype(BF16), wukvk=wukvk.astype(BF16),
        wukvvt=jnp.swapaxes(wukvv, 1, 2).astype(BF16), gq=row(q_norm_g), gkv=row(kv_norm_g),
        subln=row(subln_g),
        lam=jnp.stack([lambda_q1, lambda_k1, lambda_q2, lambda_k2], axis=1).astype(F32),
        wo=w_o.astype(BF16), g1=row(ln1_g), b1=row(ln1_b), wi=w_mlp_in.astype(BF16),
        wout=w_mlp_out.astype(BF16), g2=row(ln2_g), b2=row(ln2_b))


def kernel(x, w_in, lambda_q1, lambda_k1, lambda_q2, lambda_k2, subln_g, q_norm_g, w_uq, kv_norm_g,
           w_ukv, rel_bias, w_o, ln1_g, ln1_b, w_mlp_in, w_mlp_out, ln2_g, ln2_b):
    batch, seq, _ = x.shape
    assert seq % PROJ_ROWS == 0 and seq % ATT_TILE == 0 and (batch * seq) % POST_ROWS == 0
    wp = _pack_params(w_in, lambda_q1, lambda_k1, lambda_q2, lambda_k2, subln_g, q_norm_g, w_uq,
                      kv_norm_g, w_ukv, w_o, ln1_g, ln1_b, w_mlp_in, w_mlp_out, ln2_g, ln2_b)
    wp["cos"], wp["sin"] = _rope_tables(seq)
    bias = _bias_tiles(rel_bias)
    x2 = x.reshape(batch * seq, D_MODEL)
    for l in range(DEPTH):
        lambda_init = 0.8 - 0.6 * math.exp(-0.3 * l)
        dq, dk, dvt, qm, km, vmt = _proj_call(x2, wp, l, seq)
        mix_a, mix_b = _attn_call(dq, dk, dvt, bias, wp, l, qm, km, vmt, seq, lambda_init)
        x2 = _post_call(x2, mix_a, mix_b, wp, l)
    return x2.reshape(batch, seq, D_MODEL)
```

```python
import functools
import math

import jax
import jax.numpy as jnp
from jax import lax
from jax.experimental import pallas as pl
from jax.experimental.pallas import tpu as pltpu

F32 = jnp.float32
BF16 = jnp.bfloat16

D_MODEL = 1024
DEPTH = 2
CHUNK = 64
HEADS = 4
DIFF_HEAD_DIM = 64
DIFF_V_DIM = 2 * DIFF_HEAD_DIM
DIFF_WIDTH = HEADS * DIFF_V_DIM
MLA_NOPE_DIM = 128
MLA_ROPE_DIM = 64
MLA_V_DIM = 128
MLA_Q_RANK = 256
MLA_KV_RANK = 128
MLA_WIDTH = HEADS * MLA_V_DIM
MLA_QK_PAD = 256
D_FF = 4 * D_MODEL
N_BUCKETS = 32
MAX_DISTANCE = 128
ROPE_THETA = 10000.0
ALPHA = (2 * DEPTH) ** 0.25
LN_EPS = 1e-5
RMS_EPS = 1e-6
NEG_BIG = -1e30
LOG2E = math.log2(math.e)

LANES = 128
BF16_SUBLANES = 16
ATT_TILE = 256
ATT_HEADS_PER_STEP = 2
PROJ_ROWS = 1024
POST_ROWS = 1024
POST_ROW_BLOCK = 256
FF_CHUNK = 1024
DIFF_LOOKAHEAD = 4
MLA_LOOKAHEAD = 4
VMEM_LIMIT = 56 * 1024 * 1024
assert ATT_TILE >= MAX_DISTANCE and ATT_TILE % CHUNK == 0

C_DQ = 0
C_DK = C_DQ + DIFF_WIDTH
C_CQ = C_DK + DIFF_WIDTH
C_CKV = C_CQ + MLA_Q_RANK
C_KR = C_CKV + MLA_KV_RANK
W1_COLS = C_KR + LANES

_NT = (((1,), (1,)), ((), ()))


def _rms(x, g):
    return x * lax.rsqrt(jnp.mean(x * x, axis=-1, keepdims=True) + RMS_EPS) * g


def _layer_norm(x, g, b):
    mu = jnp.mean(x, axis=-1, keepdims=True)
    xc = x - mu
    var = jnp.mean(xc * xc, axis=-1, keepdims=True)
    return xc * lax.rsqrt(var + LN_EPS) * g + b


def _layer_spec(arr, l, **kwargs):
    index = (l,) + (0,) * (arr.ndim - 1)
    return pl.BlockSpec((1,) + arr.shape[1:], lambda *_: index, **kwargs)


def _proj_kernel(x_ref, w1_ref, wvt_ref, wuq_ref, wuqs_ref, wukvk_ref, wukvvt_ref,
                 gq_ref, gkv_ref, cos_ref, sin_ref,
                 dq_ref, dk_ref, dvt_ref, qm_ref, km_ref, vmt_ref):
    xb = x_ref[...].astype(BF16)
    h = jnp.dot(xb, w1_ref[0], preferred_element_type=F32)
    dq_ref[...] = h[:, C_DQ:C_DQ + DIFF_WIDTH].astype(BF16)
    dk_ref[...] = h[:, C_DK:C_DK + DIFF_WIDTH].astype(BF16)
    dvt_ref[0] = lax.dot_general(wvt_ref[0], xb, _NT, preferred_element_type=F32).astype(BF16)

    cos = cos_ref[...]
    sin = sin_ref[...]
    c_q = _rms(h[:, C_CQ:C_CQ + MLA_Q_RANK], gq_ref[0]).astype(BF16)
    qf = jnp.dot(c_q, wuq_ref[0], preferred_element_type=F32)
    qsw = jnp.dot(c_q, wuqs_ref[0], preferred_element_type=F32)
    c_kv = _rms(h[:, C_CKV:C_CKV + MLA_KV_RANK], gkv_ref[0]).astype(BF16)
    kn = jnp.dot(c_kv, wukvk_ref[0], preferred_element_type=F32)
    vmt_ref[0] = lax.dot_general(wukvvt_ref[0], c_kv, _NT,
                                 preferred_element_type=F32).astype(BF16)
    kr = h[:, C_KR:C_KR + LANES]
    k_rope = (kr * cos + pltpu.roll(kr, LANES // 2, 1) * sin).astype(BF16)

    for hh in range(HEADS):
        b0 = hh * MLA_QK_PAD
        qm_ref[:, b0:b0 + LANES] = qf[:, b0:b0 + LANES].astype(BF16)
        qm_ref[:, b0 + LANES:b0 + 2 * LANES] = (
            qf[:, b0 + LANES:b0 + 2 * LANES] * cos + qsw[:, hh * LANES:(hh + 1) * LANES] * sin
        ).astype(BF16)
        km_ref[:, b0:b0 + LANES] = kn[:, hh * LANES:(hh + 1) * LANES].astype(BF16)
        km_ref[:, b0 + LANES:b0 + 2 * LANES] = k_rope


def _proj_call(x2, wp, l, seq):
    ntok = x2.shape[0]
    batch = ntok // seq
    rows = PROJ_ROWS
    steps_per_seq = seq // rows

    def const(name):
        return _layer_spec(wp[name], l)

    def tok(width):
        return pl.BlockSpec((rows, width), lambda i: (i, 0))

    vt_spec = pl.BlockSpec((1, HEADS * LANES, rows),
                           lambda i: (i // steps_per_seq, 0, i % steps_per_seq))
    pos_spec = pl.BlockSpec((rows, LANES), lambda i: (i % steps_per_seq, 0))
    vt_shape = jax.ShapeDtypeStruct((batch, HEADS * LANES, seq), BF16)
    return pl.pallas_call(
        _proj_kernel,
        grid=(ntok // rows,),
        in_specs=[tok(D_MODEL), const("w1"), const("wvt"), const("wuq"), const("wuqs"),
                  const("wukvk"), const("wukvvt"), const("gq"), const("gkv"), pos_spec, pos_spec],
        out_specs=[tok(DIFF_WIDTH), tok(DIFF_WIDTH), vt_spec,
                   tok(HEADS * MLA_QK_PAD), tok(HEADS * MLA_QK_PAD), vt_spec],
        out_shape=[jax.ShapeDtypeStruct((ntok, DIFF_WIDTH), BF16),
                   jax.ShapeDtypeStruct((ntok, DIFF_WIDTH), BF16),
                   vt_shape,
                   jax.ShapeDtypeStruct((ntok, HEADS * MLA_QK_PAD), BF16),
                   jax.ShapeDtypeStruct((ntok, HEADS * MLA_QK_PAD), BF16),
                   vt_shape],
        compiler_params=pltpu.CompilerParams(dimension_semantics=("arbitrary",),
                                             vmem_limit_bytes=VMEM_LIMIT),
        name="proj",
    )(x2, wp["w1"], wp["wvt"], wp["wuq"], wp["wuqs"], wp["wukvk"], wp["wukvvt"],
      wp["gq"], wp["gkv"], wp["cos"], wp["sin"])


def _col_max(a, b):
    m = jnp.max(b, axis=0, keepdims=True)
    return m if a is None else jnp.maximum(a, m)


def _attend(n_heads, n_blocks, n_keys_fn, q_fn, score_fn, finish_fn, vt_fn, lookahead):
    ones = jnp.ones((BF16_SUBLANES, ATT_TILE), BF16)
    items = [(h, qi, c) for h in range(n_heads) for qi in range(n_blocks)
             for c in range(n_keys_fn(qi))]
    qs, scores = {}, {}

    def issue(idx):
        h, qi, c = items[idx]
        if c == 0:
            qs[h, qi] = q_fn(h, qi)
        scores[idx] = score_fn(h, qi, qs[h, qi], c)

    for idx in range(min(lookahead, len(items))):
        issue(idx)
    m, acc = None, None
    for idx, (h, qi, c) in enumerate(items):
        if idx + lookahead < len(items):
            issue(idx + lookahead)
        s = scores.pop(idx)
        m_new = _col_max(m, s)
        e = jnp.exp2(s - m_new).astype(BF16)
        vt = jnp.concatenate([vt_fn(h, c), ones], axis=0)
        pv = jnp.dot(vt, e, preferred_element_type=F32)
        acc = pv if acc is None else acc * jnp.exp2(m - m_new) + pv
        m = m_new
        if c == n_keys_fn(qi) - 1:
            finish_fn(h, qi, acc[:LANES] / acc[LANES:LANES + 1])
            m, acc = None, None
        yield


def _diff_steps(q_ref, k_ref, vt_ref, bias_ref, lam_ref, g_ref, o_ref, lambda_init):
    t = ATT_TILE
    lane = lax.broadcasted_iota(jnp.int32, (t, LANES), 1)
    lp = lam_ref[0]
    lam = (jnp.exp(jnp.sum(lp[0:1] * lp[1:2], axis=-1, keepdims=True))
           - jnp.exp(jnp.sum(lp[2:3] * lp[3:4], axis=-1, keepdims=True)) + lambda_init)
    g = g_ref[0]

    def q_rows(h, b):
        qi, m = divmod(b, 2)
        q = q_ref[qi * t:(qi + 1) * t, h * LANES:(h + 1) * LANES]
        keep = lane < DIFF_HEAD_DIM if m == 0 else lane >= DIFF_HEAD_DIM
        return jnp.where(keep, q, jnp.zeros_like(q))

    def score(h, b, q, c):
        qi = b // 2
        k = k_ref[c * t:(c + 1) * t, h * LANES:(h + 1) * LANES]
        s = lax.dot_general(k, q, _NT, preferred_element_type=F32)
        return s + bias_ref[h, qi - c] if qi - c <= 1 else s

    def values(h, c):
        return vt_ref[0, h * LANES:(h + 1) * LANES, c * t:(c + 1) * t]

    first_map = {}

    def finish(h, b, o):
        qi, m = divmod(b, 2)
        if m == 0:
            first_map[h] = o
            return
        ot = first_map.pop(h) - lam * o
        y = _rms(ot.T, g) * (1.0 - lambda_init)
        o_ref[qi * t:(qi + 1) * t, h * LANES:(h + 1) * LANES] = y.astype(BF16)

    return _attend(ATT_HEADS_PER_STEP, 2 * (q_ref.shape[0] // t), lambda b: b // 2 + 1, q_rows,
                   score, finish, values, DIFF_LOOKAHEAD)


def _mla_steps(q_ref, k_ref, vt_ref, o_ref):
    t = ATT_TILE
    w = MLA_QK_PAD
    kc = lax.broadcasted_iota(jnp.int32, (t, t), 0) // CHUNK
    qc = lax.broadcasted_iota(jnp.int32, (t, t), 1) // CHUNK
    allowed = kc <= qc

    def q_rows(h, qi):
        return q_ref[qi * t:(qi + 1) * t, h * w:(h + 1) * w]

    def score(h, qi, q, c):
        k = k_ref[c * t:(c + 1) * t, h * w:(h + 1) * w]
        s = lax.dot_general(k, q, _NT, preferred_element_type=F32)
        return jnp.where(allowed, s, NEG_BIG) if c == qi else s

    def values(h, c):
        return vt_ref[0, h * LANES:(h + 1) * LANES, c * t:(c + 1) * t]

    def finish(h, qi, o):
        o_ref[qi * t:(qi + 1) * t, h * LANES:(h + 1) * LANES] = o.T.astype(BF16)

    return _attend(ATT_HEADS_PER_STEP, q_ref.shape[0] // t, lambda qi: qi + 1, q_rows, score,
                   finish, values, MLA_LOOKAHEAD)


def _attn_kernel(dq_ref, dk_ref, dvt_ref, bias_ref, lam_ref, g_ref, qm_ref, km_ref, vmt_ref,
                 oa_ref, ob_ref, *, lambda_init):
    streams = [_diff_steps(dq_ref, dk_ref, dvt_ref, bias_ref, lam_ref, g_ref, oa_ref, lambda_init),
               _mla_steps(qm_ref, km_ref, vmt_ref, ob_ref)]
    n_tiles = dq_ref.shape[0] // ATT_TILE
    per_head = n_tiles * (n_tiles + 1) // 2
    totals = [2 * ATT_HEADS_PER_STEP * per_head, ATT_HEADS_PER_STEP * per_head]
    done = [0, 0]
    while done[0] < totals[0] or done[1] < totals[1]:
        i = 0 if done[0] * totals[1] <= done[1] * totals[0] else 1
        next(streams[i])
        done[i] += 1


def _attn_call(dq, dk, dvt, bias, wp, l, qm, km, vmt, seq, lambda_init):
    ntok = dq.shape[0]
    n = ATT_HEADS_PER_STEP

    def rows(width):
        return pl.BlockSpec((seq, n * width), lambda b, h: (b, h))

    vt_spec = pl.BlockSpec((1, n * LANES, seq), lambda b, h: (b, h, 0))
    return pl.pallas_call(
        functools.partial(_attn_kernel, lambda_init=lambda_init),
        grid=(ntok // seq, HEADS // n),
        in_specs=[rows(LANES), rows(LANES), vt_spec,
                  pl.BlockSpec((n,) + bias.shape[1:], lambda b, h: (h, 0, 0, 0)),
                  _layer_spec(wp["lam"], l), _layer_spec(wp["subln"], l),
                  rows(MLA_QK_PAD), rows(MLA_QK_PAD), vt_spec],
        out_specs=[rows(LANES), rows(LANES)],
        out_shape=[jax.ShapeDtypeStruct((ntok, DIFF_WIDTH), BF16),
                   jax.ShapeDtypeStruct((ntok, MLA_WIDTH), BF16)],
        compiler_params=pltpu.CompilerParams(dimension_semantics=("arbitrary", "arbitrary"),
                                             vmem_limit_bytes=VMEM_LIMIT),
        name="attn",
    )(dq, dk, dvt, bias, wp["lam"], wp["subln"], qm, km, vmt)


def _post_kernel(x_ref, ma_ref, mb_ref, wo_ref, g1_ref, b1_ref, wi_ref, wout_ref, g2_ref, b2_ref,
                 o_ref):
    n_blocks = x_ref.shape[0] // POST_ROW_BLOCK
    n_chunks = D_FF // FF_CHUNK

    def attn_out(r):
        rs = slice(r * POST_ROW_BLOCK, (r + 1) * POST_ROW_BLOCK)
        y = (jnp.dot(ma_ref[rs, :], wo_ref[0, :DIFF_WIDTH, :], preferred_element_type=F32)
             + jnp.dot(mb_ref[rs, :], wo_ref[0, DIFF_WIDTH:, :], preferred_element_type=F32))
        return _layer_norm(ALPHA * x_ref[rs, :] + y, g1_ref[0], b1_ref[0])

    def up(x1b, c):
        return jnp.dot(x1b, wi_ref[0, :, c * FF_CHUNK:(c + 1) * FF_CHUNK],
                       preferred_element_type=F32)

    x1 = attn_out(0)
    for r in range(n_blocks):
        x1_next = attn_out(r + 1) if r + 1 < n_blocks else None
        x1b = x1.astype(BF16)
        u = up(x1b, 0)
        y2 = None
        for c in range(n_chunks):
            u_next = up(x1b, c + 1) if c + 1 < n_chunks else None
            act = jnp.square(jnp.maximum(u, 0.0)).astype(BF16)
            d = jnp.dot(act, wout_ref[0, c * FF_CHUNK:(c + 1) * FF_CHUNK, :],
                        preferred_element_type=F32)
            y2 = d if y2 is None else y2 + d
            u = u_next
        o_ref[r * POST_ROW_BLOCK:(r + 1) * POST_ROW_BLOCK, :] = _layer_norm(
            ALPHA * x1 + y2, g2_ref[0], b2_ref[0])
        x1 = x1_next


def _post_call(x2, mix_a, mix_b, wp, l):
    ntok = x2.shape[0]
    rows = POST_ROWS

    def const(name):
        return _layer_spec(wp[name], l, pipeline_mode=pl.Buffered(1))

    def tok(width):
        return pl.BlockSpec((rows, width), lambda i: (i, 0))

    return pl.pallas_call(
        _post_kernel,
        grid=(ntok // rows,),
        in_specs=[tok(D_MODEL), tok(DIFF_WIDTH), tok(MLA_WIDTH), const("wo"), const("g1"),
                  const("b1"), const("wi"), const("wout"), const("g2"), const("b2")],
        out_specs=tok(D_MODEL),
        out_shape=jax.ShapeDtypeStruct((ntok, D_MODEL), F32),
        compiler_params=pltpu.CompilerParams(dimension_semantics=("arbitrary",),
                                             vmem_limit_bytes=VMEM_LIMIT),
        name="post",
    )(x2, mix_a, mix_b, wp["wo"], wp["g1"], wp["b1"], wp["wi"], wp["wout"], wp["g2"], wp["b2"])


def _t5_bucket(rel):
    nb = N_BUCKETS // 2
    ret = (rel > 0).astype(jnp.int32) * nb
    n = jnp.abs(rel)
    max_exact = nb // 2
    nf = jnp.maximum(n, 1).astype(F32)
    large = max_exact + (jnp.log(nf / max_exact) / math.log(MAX_DISTANCE / max_exact)
                         * (nb - max_exact)).astype(jnp.int32)
    large = jnp.minimum(large, nb - 1)
    return ret + jnp.where(n < max_exact, n, large)


def _bias_tiles(rel_bias):
    t = ATT_TILE
    p = 2 * t
    table = rel_bias.astype(F32).T
    far = table[:, N_BUCKETS // 2 - 1]
    w = jnp.arange(p, dtype=jnp.int32)[None, :]
    d = jnp.arange(2, dtype=jnp.int32)[:, None]
    rel = (t - 1) - w - d * t
    vec = (table[:, _t5_bucket(rel)] - far[:, None, None]) * LOG2E
    rows = jnp.tile(vec, (1, 1, t))[..., :t * (p - 1)].reshape(HEADS, 2, t, p - 1)
    bias = rows[..., t - 1:2 * t - 1]
    kk = jnp.arange(t, dtype=jnp.int32)[:, None]
    qq = jnp.arange(t, dtype=jnp.int32)[None, :]
    allowed = (kk // CHUNK <= qq // CHUNK)[None, None] | (d > 0)[None, :, :, None]
    bias = jnp.where(allowed, bias, NEG_BIG)
    return bias


def _rope_tables(seq):
    pos = jnp.arange(seq, dtype=F32)
    inv = ROPE_THETA ** (-jnp.arange(0, MLA_ROPE_DIM, 2, dtype=F32) / MLA_ROPE_DIM)
    ang = pos[:, None] * inv[None, :]
    cos, sin = jnp.cos(ang), jnp.sin(ang)
    z = jnp.zeros((seq, LANES - MLA_ROPE_DIM), F32)
    return (jnp.concatenate([cos, cos, z], axis=-1), jnp.concatenate([-sin, sin, z], axis=-1))


def _swap_halves(w):
    half = w.shape[-1] // 2
    return jnp.concatenate([w[..., half:], w[..., :half]], axis=-1)


def _pack_params(w_in, lambda_q1, lambda_k1, lambda_q2, lambda_k2, subln_g, q_norm_g, w_uq,
                 kv_norm_g, w_ukv, w_o, ln1_g, ln1_b, w_mlp_in, w_mlp_out, ln2_g, ln2_b):
    depth = w_in.shape[0]
    o_k, o_v, o_cq = DIFF_WIDTH, 2 * DIFF_WIDTH, 3 * DIFF_WIDTH
    o_ckv = o_cq + MLA_Q_RANK
    o_kr = o_ckv + MLA_KV_RANK
    kr = w_in[..., o_kr:]
    w1 = jnp.concatenate([w_in[..., :o_k] * (DIFF_HEAD_DIM ** -0.5 * LOG2E), w_in[..., o_k:o_v],
                          w_in[..., o_cq:o_ckv], w_in[..., o_ckv:o_kr], kr, _swap_halves(kr)],
                         axis=-1)
    mla_scale = (MLA_NOPE_DIM + MLA_ROPE_DIM) ** -0.5 * LOG2E
    uq = (w_uq * mla_scale).reshape(depth, MLA_Q_RANK, HEADS, MLA_NOPE_DIM + MLA_ROPE_DIM)
    zq = jnp.zeros((depth, MLA_Q_RANK, HEADS, MLA_QK_PAD - MLA_NOPE_DIM - MLA_ROPE_DIM), F32)
    wuq = jnp.concatenate([uq, zq], axis=-1).reshape(depth, MLA_Q_RANK, HEADS * MLA_QK_PAD)
    wuqs = jnp.concatenate([_swap_halves(uq[..., MLA_NOPE_DIM:]), zq], axis=-1)
    wuqs = wuqs.reshape(depth, MLA_Q_RANK, HEADS * LANES)
    ukv = w_ukv.reshape(depth, MLA_KV_RANK, HEADS, MLA_NOPE_DIM + MLA_V_DIM)
    wukvk = ukv[..., :MLA_NOPE_DIM].reshape(depth, MLA_KV_RANK, HEADS * MLA_NOPE_DIM)
    wukvv = ukv[..., MLA_NOPE_DIM:].reshape(depth, MLA_KV_RANK, HEADS * MLA_V_DIM)
    row = lambda v: v.reshape(depth, 1, -1).astype(F32)
    return dict(
        w1=w1.astype(BF16), wvt=jnp.swapaxes(w_in[..., o_v:o_cq], 1, 2).astype(BF16),
        wuq=wuq.astype(BF16), wuqs=wuqs.astype(BF16), wukvk=wukvk.astype(BF16),
        wukvvt=jnp.swapaxes(wukvv, 1, 2).astype(BF16), gq=row(q_norm_g), gkv=row(kv_norm_g),
        subln=row(subln_g),
        lam=jnp.stack([lambda_q1, lambda_k1, lambda_q2, lambda_k2], axis=1).astype(F32),
        wo=w_o.astype(BF16), g1=row(ln1_g), b1=row(ln1_b), wi=w_mlp_in.astype(BF16),
        wout=w_mlp_out.astype(BF16), g2=row(ln2_g), b2=row(ln2_b))


def kernel(x, w_in, lambda_q1, lambda_k1, lambda_q2, lambda_k2, subln_g, q_norm_g, w_uq, kv_norm_g,
           w_ukv, rel_bias, w_o, ln1_g, ln1_b, w_mlp_in, w_mlp_out, ln2_g, ln2_b):
    batch, seq, _ = x.shape
    assert seq % PROJ_ROWS == 0 and seq % ATT_TILE == 0 and (batch * seq) % POST_ROWS == 0
    wp = _pack_params(w_in, lambda_q1, lambda_k1, lambda_q2, lambda_k2, subln_g, q_norm_g, w_uq,
                      kv_norm_g, w_ukv, w_o, ln1_g, ln1_b, w_mlp_in, w_mlp_out, ln2_g, ln2_b)
    wp["cos"], wp["sin"] = _rope_tables(seq)
    bias = _bias_tiles(rel_bias)
    x2 = x.reshape(batch * seq, D_MODEL)
    for l in range(DEPTH):
        lambda_init = 0.8 - 0.6 * math.exp(-0.3 * l)
        dq, dk, dvt, qm, km, vmt = _proj_call(x2, wp, l, seq)
        mix_a, mix_b = _attn_call(dq, dk, dvt, bias, wp, l, qm, km, vmt, seq, lambda_init)
        x2 = _post_call(x2, mix_a, mix_b, wp, l)
    return x2.reshape(batch, seq, D_MODEL)
```

```python
import functools
import math

import jax
import jax.numpy as jnp
from jax import lax
from jax.experimental import pallas as pl
from jax.experimental.pallas import tpu as pltpu

F32 = jnp.float32
BF16 = jnp.bfloat16

D_MODEL = 1024
DEPTH = 2
CHUNK = 64
HEADS = 4
DIFF_HEAD_DIM = 64
DIFF_V_DIM = 2 * DIFF_HEAD_DIM
DIFF_WIDTH = HEADS * DIFF_V_DIM
MLA_NOPE_DIM = 128
MLA_ROPE_DIM = 64
MLA_V_DIM = 128
MLA_Q_RANK = 256
MLA_KV_RANK = 128
MLA_WIDTH = HEADS * MLA_V_DIM
MLA_QK_PAD = 256
D_FF = 4 * D_MODEL
N_BUCKETS = 32
MAX_DISTANCE = 128
ROPE_THETA = 10000.0
ALPHA = (2 * DEPTH) ** 0.25
LN_EPS = 1e-5
RMS_EPS = 1e-6
NEG_BIG = -1e30
LOG2E = math.log2(math.e)

LANES = 128
BF16_SUBLANES = 16
ATT_TILE = 256
ATT_HEADS_PER_STEP = 2
PROJ_ROWS = 1024
POST_ROWS = 1024
POST_ROW_BLOCK = 256
FF_CHUNK = 1024
DIFF_LOOKAHEAD = 6
MLA_LOOKAHEAD = 4
VMEM_LIMIT = 56 * 1024 * 1024
assert ATT_TILE >= MAX_DISTANCE and ATT_TILE % CHUNK == 0

C_DQ = 0
C_DK = C_DQ + DIFF_WIDTH
C_CQ = C_DK + DIFF_WIDTH
C_CKV = C_CQ + MLA_Q_RANK
C_KR = C_CKV + MLA_KV_RANK
W1_COLS = C_KR + LANES

_NT = (((1,), (1,)), ((), ()))


def _rms(x, g):
    return x * lax.rsqrt(jnp.mean(x * x, axis=-1, keepdims=True) + RMS_EPS) * g


def _layer_norm(x, g, b):
    mu = jnp.mean(x, axis=-1, keepdims=True)
    xc = x - mu
    var = jnp.mean(xc * xc, axis=-1, keepdims=True)
    return xc * lax.rsqrt(var + LN_EPS) * g + b


def _layer_spec(arr, l, **kwargs):
    index = (l,) + (0,) * (arr.ndim - 1)
    return pl.BlockSpec((1,) + arr.shape[1:], lambda *_: index, **kwargs)


def _proj_kernel(x_ref, w1_ref, wvt_ref, wuq_ref, wuqs_ref, wukvk_ref, wukvvt_ref,
                 gq_ref, gkv_ref, cos_ref, sin_ref,
                 dq_ref, dk_ref, dvt_ref, qm_ref, km_ref, vmt_ref):
    xb = x_ref[...].astype(BF16)
    h = jnp.dot(xb, w1_ref[0], preferred_element_type=F32)
    dq_ref[...] = h[:, C_DQ:C_DQ + DIFF_WIDTH].astype(BF16)
    dk_ref[...] = h[:, C_DK:C_DK + DIFF_WIDTH].astype(BF16)
    dvt_ref[0] = lax.dot_general(wvt_ref[0], xb, _NT, preferred_element_type=F32).astype(BF16)

    cos = cos_ref[...]
    sin = sin_ref[...]
    c_q = _rms(h[:, C_CQ:C_CQ + MLA_Q_RANK], gq_ref[0]).astype(BF16)
    qf = jnp.dot(c_q, wuq_ref[0], preferred_element_type=F32)
    qsw = jnp.dot(c_q, wuqs_ref[0], preferred_element_type=F32)
    c_kv = _rms(h[:, C_CKV:C_CKV + MLA_KV_RANK], gkv_ref[0]).astype(BF16)
    kn = jnp.dot(c_kv, wukvk_ref[0], preferred_element_type=F32)
    vmt_ref[0] = lax.dot_general(wukvvt_ref[0], c_kv, _NT,
                                 preferred_element_type=F32).astype(BF16)
    kr = h[:, C_KR:C_KR + LANES]
    k_rope = (kr * cos + pltpu.roll(kr, LANES // 2, 1) * sin).astype(BF16)

    for hh in range(HEADS):
        b0 = hh * MLA_QK_PAD
        qm_ref[:, b0:b0 + LANES] = qf[:, b0:b0 + LANES].astype(BF16)
        qm_ref[:, b0 + LANES:b0 + 2 * LANES] = (
            qf[:, b0 + LANES:b0 + 2 * LANES] * cos + qsw[:, hh * LANES:(hh + 1) * LANES] * sin
        ).astype(BF16)
        km_ref[:, b0:b0 + LANES] = kn[:, hh * LANES:(hh + 1) * LANES].astype(BF16)
        km_ref[:, b0 + LANES:b0 + 2 * LANES] = k_rope


def _proj_call(x2, wp, l, seq):
    ntok = x2.shape[0]
    batch = ntok // seq
    rows = PROJ_ROWS
    steps_per_seq = seq // rows

    def const(name):
        return _layer_spec(wp[name], l)

    def tok(width):
        return pl.BlockSpec((rows, width), lambda i: (i, 0))

    vt_spec = pl.BlockSpec((1, HEADS * LANES, rows),
                           lambda i: (i // steps_per_seq, 0, i % steps_per_seq))
    pos_spec = pl.BlockSpec((rows, LANES), lambda i: (i % steps_per_seq, 0))
    vt_shape = jax.ShapeDtypeStruct((batch, HEADS * LANES, seq), BF16)
    return pl.pallas_call(
        _proj_kernel,
        grid=(ntok // rows,),
        in_specs=[tok(D_MODEL), const("w1"), const("wvt"), const("wuq"), const("wuqs"),
                  const("wukvk"), const("wukvvt"), const("gq"), const("gkv"), pos_spec, pos_spec],
        out_specs=[tok(DIFF_WIDTH), tok(DIFF_WIDTH), vt_spec,
                   tok(HEADS * MLA_QK_PAD), tok(HEADS * MLA_QK_PAD), vt_spec],
        out_shape=[jax.ShapeDtypeStruct((ntok, DIFF_WIDTH), BF16),
                   jax.ShapeDtypeStruct((ntok, DIFF_WIDTH), BF16),
                   vt_shape,
                   jax.ShapeDtypeStruct((ntok, HEADS * MLA_QK_PAD), BF16),
                   jax.ShapeDtypeStruct((ntok, HEADS * MLA_QK_PAD), BF16),
                   vt_shape],
        compiler_params=pltpu.CompilerParams(dimension_semantics=("arbitrary",),
                                             vmem_limit_bytes=VMEM_LIMIT),
        name="proj",
    )(x2, wp["w1"], wp["wvt"], wp["wuq"], wp["wuqs"], wp["wukvk"], wp["wukvvt"],
      wp["gq"], wp["gkv"], wp["cos"], wp["sin"])


def _col_max(a, b):
    m = jnp.max(b, axis=0, keepdims=True)
    return m if a is None else jnp.maximum(a, m)


def _attend(n_heads, n_blocks, n_keys_fn, q_fn, score_fn, finish_fn, vt_fn, lookahead):
    ones = jnp.ones((BF16_SUBLANES, ATT_TILE), BF16)
    items = [(h, qi, c) for h in range(n_heads) for qi in range(n_blocks)
             for c in range(n_keys_fn(qi))]
    qs, scores = {}, {}

    def issue(idx):
        h, qi, c = items[idx]
        if c == 0:
            qs[h, qi] = q_fn(h, qi)
        scores[idx] = score_fn(h, qi, qs[h, qi], c)

    for idx in range(min(lookahead, len(items))):
        issue(idx)
    m, acc = None, None
    for idx, (h, qi, c) in enumerate(items):
        if idx + lookahead < len(items):
            issue(idx + lookahead)
        s = scores.pop(idx)
        m_new = _col_max(m, s)
        e = jnp.exp2(s - m_new).astype(BF16)
        vt = jnp.concatenate([vt_fn(h, c), ones], axis=0)
        pv = jnp.dot(vt, e, preferred_element_type=F32)
        acc = pv if acc is None else acc * jnp.exp2(m - m_new) + pv
        m = m_new
        if c == n_keys_fn(qi) - 1:
            finish_fn(h, qi, acc[:LANES] / acc[LANES:LANES + 1])
            m, acc = None, None
        yield


def _diff_steps(q_ref, k_ref, vt_ref, bias_ref, lam_ref, g_ref, o_ref, lambda_init):
    t = ATT_TILE
    lane = lax.broadcasted_iota(jnp.int32, (t, LANES), 1)
    lp = lam_ref[0]
    lam = (jnp.exp(jnp.sum(lp[0:1] * lp[1:2], axis=-1, keepdims=True))
           - jnp.exp(jnp.sum(lp[2:3] * lp[3:4], axis=-1, keepdims=True)) + lambda_init)
    g = g_ref[0]

    def q_rows(h, b):
        qi, m = divmod(b, 2)
        q = q_ref[qi * t:(qi + 1) * t, h * LANES:(h + 1) * LANES]
        keep = lane < DIFF_HEAD_DIM if m == 0 else lane >= DIFF_HEAD_DIM
        return jnp.where(keep, q, jnp.zeros_like(q))

    def score(h, b, q, c):
        qi = b // 2
        k = k_ref[c * t:(c + 1) * t, h * LANES:(h + 1) * LANES]
        s = lax.dot_general(k, q, _NT, preferred_element_type=F32)
        return s + bias_ref[h, qi - c] if qi - c <= 1 else s

    def values(h, c):
        return vt_ref[0, h * LANES:(h + 1) * LANES, c * t:(c + 1) * t]

    first_map = {}

    def finish(h, b, o):
        qi, m = divmod(b, 2)
        if m == 0:
            first_map[h] = o
            return
        ot = first_map.pop(h) - lam * o
        y = _rms(ot.T, g) * (1.0 - lambda_init)
        o_ref[qi * t:(qi + 1) * t, h * LANES:(h + 1) * LANES] = y.astype(BF16)

    return _attend(ATT_HEADS_PER_STEP, 2 * (q_ref.shape[0] // t), lambda b: b // 2 + 1, q_rows,
                   score, finish, values, DIFF_LOOKAHEAD)


def _mla_steps(q_ref, k_ref, vt_ref, o_ref):
    t = ATT_TILE
    w = MLA_QK_PAD
    kc = lax.broadcasted_iota(jnp.int32, (t, t), 0) // CHUNK
    qc = lax.broadcasted_iota(jnp.int32, (t, t), 1) // CHUNK
    allowed = kc <= qc

    def q_rows(h, qi):
        return q_ref[qi * t:(qi + 1) * t, h * w:(h + 1) * w]

    def score(h, qi, q, c):
        k = k_ref[c * t:(c + 1) * t, h * w:(h + 1) * w]
        s = lax.dot_general(k, q, _NT, preferred_element_type=F32)
        return jnp.where(allowed, s, NEG_BIG) if c == qi else s

    def values(h, c):
        return vt_ref[0, h * LANES:(h + 1) * LANES, c * t:(c + 1) * t]

    def finish(h, qi, o):
        o_ref[qi * t:(qi + 1) * t, h * LANES:(h + 1) * LANES] = o.T.astype(BF16)

    return _attend(ATT_HEADS_PER_STEP, q_ref.shape[0] // t, lambda qi: qi + 1, q_rows, score,
                   finish, values, MLA_LOOKAHEAD)


def _attn_kernel(dq_ref, dk_ref, dvt_ref, bias_ref, lam_ref, g_ref, qm_ref, km_ref, vmt_ref,
                 oa_ref, ob_ref, *, lambda_init):
    streams = [_diff_steps(dq_ref, dk_ref, dvt_ref, bias_ref, lam_ref, g_ref, oa_ref, lambda_init),
               _mla_steps(qm_ref, km_ref, vmt_ref, ob_ref)]
    n_tiles = dq_ref.shape[0] // ATT_TILE
    per_head = n_tiles * (n_tiles + 1) // 2
    totals = [2 * ATT_HEADS_PER_STEP * per_head, ATT_HEADS_PER_STEP * per_head]
    done = [0, 0]
    while done[0] < totals[0] or done[1] < totals[1]:
        i = 0 if done[0] * totals[1] <= done[1] * totals[0] else 1
        next(streams[i])
        done[i] += 1


def _attn_call(dq, dk, dvt, bias, wp, l, qm, km, vmt, seq, lambda_init):
    ntok = dq.shape[0]
    n = ATT_HEADS_PER_STEP

    def rows(width):
        return pl.BlockSpec((seq, n * width), lambda b, h: (b, h))

    vt_spec = pl.BlockSpec((1, n * LANES, seq), lambda b, h: (b, h, 0))
    return pl.pallas_call(
        functools.partial(_attn_kernel, lambda_init=lambda_init),
        grid=(ntok // seq, HEADS // n),
        in_specs=[rows(LANES), rows(LANES), vt_spec,
                  pl.BlockSpec((n,) + bias.shape[1:], lambda b, h: (h, 0, 0, 0)),
                  _layer_spec(wp["lam"], l), _layer_spec(wp["subln"], l),
                  rows(MLA_QK_PAD), rows(MLA_QK_PAD), vt_spec],
        out_specs=[rows(LANES), rows(LANES)],
        out_shape=[jax.ShapeDtypeStruct((ntok, DIFF_WIDTH), BF16),
                   jax.ShapeDtypeStruct((ntok, MLA_WIDTH), BF16)],
        compiler_params=pltpu.CompilerParams(dimension_semantics=("arbitrary", "arbitrary"),
                                             vmem_limit_bytes=VMEM_LIMIT),
        name="attn",
    )(dq, dk, dvt, bias, wp["lam"], wp["subln"], qm, km, vmt)


def _post_kernel(x_ref, ma_ref, mb_ref, wo_ref, g1_ref, b1_ref, wi_ref, wout_ref, g2_ref, b2_ref,
                 o_ref):
    n_blocks = x_ref.shape[0] // POST_ROW_BLOCK
    n_chunks = D_FF // FF_CHUNK

    def attn_out(r):
        rs = slice(r * POST_ROW_BLOCK, (r + 1) * POST_ROW_BLOCK)
        y = (jnp.dot(ma_ref[rs, :], wo_ref[0, :DIFF_WIDTH, :], preferred_element_type=F32)
             + jnp.dot(mb_ref[rs, :], wo_ref[0, DIFF_WIDTH:, :], preferred_element_type=F32))
        return _layer_norm(ALPHA * x_ref[rs, :] + y, g1_ref[0], b1_ref[0])

    def up(x1b, c):
        return jnp.dot(x1b, wi_ref[0, :, c * FF_CHUNK:(c + 1) * FF_CHUNK],
                       preferred_element_type=F32)

    x1 = attn_out(0)
    for r in range(n_blocks):
        x1_next = attn_out(r + 1) if r + 1 < n_blocks else None
        x1b = x1.astype(BF16)
        u = up(x1b, 0)
        y2 = None
        for c in range(n_chunks):
            u_next = up(x1b, c + 1) if c + 1 < n_chunks else None
            act = jnp.square(jnp.maximum(u, 0.0)).astype(BF16)
            d = jnp.dot(act, wout_ref[0, c * FF_CHUNK:(c + 1) * FF_CHUNK, :],
                        preferred_element_type=F32)
            y2 = d if y2 is None else y2 + d
            u = u_next
        o_ref[r * POST_ROW_BLOCK:(r + 1) * POST_ROW_BLOCK, :] = _layer_norm(
            ALPHA * x1 + y2, g2_ref[0], b2_ref[0])
        x1 = x1_next


def _post_call(x2, mix_a, mix_b, wp, l):
    ntok = x2.shape[0]
    rows = POST_ROWS

    def const(name):
        return _layer_spec(wp[name], l, pipeline_mode=pl.Buffered(1))

    def tok(width):
        return pl.BlockSpec((rows, width), lambda i: (i, 0))

    return pl.pallas_call(
        _post_kernel,
        grid=(ntok // rows,),
        in_specs=[tok(D_MODEL), tok(DIFF_WIDTH), tok(MLA_WIDTH), const("wo"), const("g1"),
                  const("b1"), const("wi"), const("wout"), const("g2"), const("b2")],
        out_specs=tok(D_MODEL),
        out_shape=jax.ShapeDtypeStruct((ntok, D_MODEL), F32),
        compiler_params=pltpu.CompilerParams(dimension_semantics=("arbitrary",),
                                             vmem_limit_bytes=VMEM_LIMIT),
        name="post",
    )(x2, mix_a, mix_b, wp["wo"], wp["g1"], wp["b1"], wp["wi"], wp["wout"], wp["g2"], wp["b2"])


def _t5_bucket(rel):
    nb = N_BUCKETS // 2
    ret = (rel > 0).astype(jnp.int32) * nb
    n = jnp.abs(rel)
    max_exact = nb // 2
    nf = jnp.maximum(n, 1).astype(F32)
    large = max_exact + (jnp.log(nf / max_exact) / math.log(MAX_DISTANCE / max_exact)
                         * (nb - max_exact)).astype(jnp.int32)
    large = jnp.minimum(large, nb - 1)
    return ret + jnp.where(n < max_exact, n, large)


def _bias_tiles(rel_bias):
    t = ATT_TILE
    p = 2 * t
    table = rel_bias.astype(F32).T
    far = table[:, N_BUCKETS // 2 - 1]
    w = jnp.arange(p, dtype=jnp.int32)[None, :]
    d = jnp.arange(2, dtype=jnp.int32)[:, None]
    rel = (t - 1) - w - d * t
    vec = (table[:, _t5_bucket(rel)] - far[:, None, None]) * LOG2E
    rows = jnp.tile(vec, (1, 1, t))[..., :t * (p - 1)].reshape(HEADS, 2, t, p - 1)
    bias = rows[..., t - 1:2 * t - 1]
    kk = jnp.arange(t, dtype=jnp.int32)[:, None]
    qq = jnp.arange(t, dtype=jnp.int32)[None, :]
    allowed = (kk // CHUNK <= qq // CHUNK)[None, None] | (d > 0)[None, :, :, None]
    bias = jnp.where(allowed, bias, NEG_BIG)
    return bias


def _rope_tables(seq):
    pos = jnp.arange(seq, dtype=F32)
    inv = ROPE_THETA ** (-jnp.arange(0, MLA_ROPE_DIM, 2, dtype=F32) / MLA_ROPE_DIM)
    ang = pos[:, None] * inv[None, :]
    cos, sin = jnp.cos(ang), jnp.sin(ang)
    z = jnp.zeros((seq, LANES - MLA_ROPE_DIM), F32)
    return (jnp.concatenate([cos, cos, z], axis=-1), jnp.concatenate([-sin, sin, z], axis=-1))


def _swap_halves(w):
    half = w.shape[-1] // 2
    return jnp.concatenate([w[..., half:], w[..., :half]], axis=-1)


def _pack_params(w_in, lambda_q1, lambda_k1, lambda_q2, lambda_k2, subln_g, q_norm_g, w_uq,
                 kv_norm_g, w_ukv, w_o, ln1_g, ln1_b, w_mlp_in, w_mlp_out, ln2_g, ln2_b):
    depth = w_in.shape[0]
    o_k, o_v, o_cq = DIFF_WIDTH, 2 * DIFF_WIDTH, 3 * DIFF_WIDTH
    o_ckv = o_cq + MLA_Q_RANK
    o_kr = o_ckv + MLA_KV_RANK
    kr = w_in[..., o_kr:]
    w1 = jnp.concatenate([w_in[..., :o_k] * (DIFF_HEAD_DIM ** -0.5 * LOG2E), w_in[..., o_k:o_v],
                          w_in[..., o_cq:o_ckv], w_in[..., o_ckv:o_kr], kr, _swap_halves(kr)],
                         axis=-1)
    mla_scale = (MLA_NOPE_DIM + MLA_ROPE_DIM) ** -0.5 * LOG2E
    uq = (w_uq * mla_scale).reshape(depth, MLA_Q_RANK, HEADS, MLA_NOPE_DIM + MLA_ROPE_DIM)
    zq = jnp.zeros((depth, MLA_Q_RANK, HEADS, MLA_QK_PAD - MLA_NOPE_DIM - MLA_ROPE_DIM), F32)
    wuq = jnp.concatenate([uq, zq], axis=-1).reshape(depth, MLA_Q_RANK, HEADS * MLA_QK_PAD)
    wuqs = jnp.concatenate([_swap_halves(uq[..., MLA_NOPE_DIM:]), zq], axis=-1)
    wuqs = wuqs.reshape(depth, MLA_Q_RANK, HEADS * LANES)
    ukv = w_ukv.reshape(depth, MLA_KV_RANK, HEADS, MLA_NOPE_DIM + MLA_V_DIM)
    wukvk = ukv[..., :MLA_NOPE_DIM].reshape(depth, MLA_KV_RANK, HEADS * MLA_NOPE_DIM)
    wukvv = ukv[..., MLA_NOPE_DIM:].reshape(depth, MLA_KV_RANK, HEADS * MLA_V_DIM)
    row = lambda v: v.reshape(depth, 1, -1).astype(F32)
    return dict(
        w1=w1.astype(BF16), wvt=jnp.swapaxes(w_in[..., o_v:o_cq], 1, 2).astype(BF16),
        wuq=wuq.astype(BF16), wuqs=wuqs.astype(BF16), wukvk=wukvk.astype(BF16),
        wukvvt=jnp.swapaxes(wukvv, 1, 2).astype(BF16), gq=row(q_norm_g), gkv=row(kv_norm_g),
        subln=row(subln_g),
        lam=jnp.stack([lambda_q1, lambda_k1, lambda_q2, lambda_k2], axis=1).astype(F32),
        wo=w_o.astype(BF16), g1=row(ln1_g), b1=row(ln1_b), wi=w_mlp_in.astype(BF16),
        wout=w_mlp_out.astype(BF16), g2=row(ln2_g), b2=row(ln2_b))


def kernel(x, w_in, lambda_q1, lambda_k1, lambda_q2, lambda_k2, subln_g, q_norm_g, w_uq, kv_norm_g,
           w_ukv, rel_bias, w_o, ln1_g, ln1_b, w_mlp_in, w_mlp_out, ln2_g, ln2_b):
    batch, seq, _ = x.shape
    assert seq % PROJ_ROWS == 0 and seq % ATT_TILE == 0 and (batch * seq) % POST_ROWS == 0
    wp = _pack_params(w_in, lambda_q1, lambda_k1, lambda_q2, lambda_k2, subln_g, q_norm_g, w_uq,
                      kv_norm_g, w_ukv, w_o, ln1_g, ln1_b, w_mlp_in, w_mlp_out, ln2_g, ln2_b)
    wp["cos"], wp["sin"] = _rope_tables(seq)
    bias = _bias_tiles(rel_bias)
    x2 = x.reshape(batch * seq, D_MODEL)
    for l in range(DEPTH):
        lambda_init = 0.8 - 0.6 * math.exp(-0.3 * l)
        dq, dk, dvt, qm, km, vmt = _proj_call(x2, wp, l, seq)
        mix_a, mix_b = _attn_call(dq, dk, dvt, bias, wp, l, qm, km, vmt, seq, lambda_init)
        x2 = _post_call(x2, mix_a, mix_b, wp, l)
    return x2.reshape(batch, seq, D_MODEL)
```

```python
import functools
import math

import jax
import jax.numpy as jnp
from jax import lax
from jax.experimental import pallas as pl
from jax.experimental.pallas import tpu as pltpu

F32 = jnp.float32
BF16 = jnp.bfloat16

D_MODEL = 1024
DEPTH = 2
CHUNK = 64
HEADS = 4
DIFF_HEAD_DIM = 64
DIFF_V_DIM = 2 * DIFF_HEAD_DIM
DIFF_WIDTH = HEADS * DIFF_V_DIM
MLA_NOPE_DIM = 128
MLA_ROPE_DIM = 64
MLA_V_DIM = 128
MLA_Q_RANK = 256
MLA_KV_RANK = 128
MLA_WIDTH = HEADS * MLA_V_DIM
MLA_QK_PAD = 256
D_FF = 4 * D_MODEL
N_BUCKETS = 32
MAX_DISTANCE = 128
ROPE_THETA = 10000.0
ALPHA = (2 * DEPTH) ** 0.25
LN_EPS = 1e-5
RMS_EPS = 1e-6
NEG_BIG = -1e30
LOG2E = math.log2(math.e)

LANES = 128
BF16_SUBLANES = 16
ATT_TILE = 256
ATT_HEADS_PER_STEP = 2
PROJ_ROWS = 1024
POST_ROWS = 1024
POST_ROW_BLOCK = 256
FF_CHUNK = 1024
DIFF_LOOKAHEAD = 3
MLA_LOOKAHEAD = 3
VMEM_LIMIT = 56 * 1024 * 1024
assert ATT_TILE >= MAX_DISTANCE and ATT_TILE % CHUNK == 0

C_DQ = 0
C_DK = C_DQ + DIFF_WIDTH
C_CQ = C_DK + DIFF_WIDTH
C_CKV = C_CQ + MLA_Q_RANK
C_KR = C_CKV + MLA_KV_RANK
W1_COLS = C_KR + LANES

_NT = (((1,), (1,)), ((), ()))


def _rms(x, g):
    return x * lax.rsqrt(jnp.mean(x * x, axis=-1, keepdims=True) + RMS_EPS) * g


def _layer_norm(x, g, b):
    mu = jnp.mean(x, axis=-1, keepdims=True)
    xc = x - mu
    var = jnp.mean(xc * xc, axis=-1, keepdims=True)
    return xc * lax.rsqrt(var + LN_EPS) * g + b


def _layer_spec(arr, l, **kwargs):
    index = (l,) + (0,) * (arr.ndim - 1)
    return pl.BlockSpec((1,) + arr.shape[1:], lambda *_: index, **kwargs)


def _proj_kernel(x_ref, w1_ref, wvt_ref, wuq_ref, wuqs_ref, wukvk_ref, wukvvt_ref,
                 gq_ref, gkv_ref, cos_ref, sin_ref,
                 dq_ref, dk_ref, dvt_ref, qm_ref, km_ref, vmt_ref):
    xb = x_ref[...].astype(BF16)
    h = jnp.dot(xb, w1_ref[0], preferred_element_type=F32)
    dq_ref[...] = h[:, C_DQ:C_DQ + DIFF_WIDTH].astype(BF16)
    dk_ref[...] = h[:, C_DK:C_DK + DIFF_WIDTH].astype(BF16)
    dvt_ref[0] = lax.dot_general(wvt_ref[0], xb, _NT, preferred_element_type=F32).astype(BF16)

    cos = cos_ref[...]
    sin = sin_ref[...]
    c_q = _rms(h[:, C_CQ:C_CQ + MLA_Q_RANK], gq_ref[0]).astype(BF16)
    qf = jnp.dot(c_q, wuq_ref[0], preferred_element_type=F32)
    qsw = jnp.dot(c_q, wuqs_ref[0], preferred_element_type=F32)
    c_kv = _rms(h[:, C_CKV:C_CKV + MLA_KV_RANK], gkv_ref[0]).astype(BF16)
    kn = jnp.dot(c_kv, wukvk_ref[0], preferred_element_type=F32)
    vmt_ref[0] = lax.dot_general(wukvvt_ref[0], c_kv, _NT,
                                 preferred_element_type=F32).astype(BF16)
    kr = h[:, C_KR:C_KR + LANES]
    k_rope = (kr * cos + pltpu.roll(kr, LANES // 2, 1) * sin).astype(BF16)

    for hh in range(HEADS):
        b0 = hh * MLA_QK_PAD
        qm_ref[:, b0:b0 + LANES] = qf[:, b0:b0 + LANES].astype(BF16)
        qm_ref[:, b0 + LANES:b0 + 2 * LANES] = (
            qf[:, b0 + LANES:b0 + 2 * LANES] * cos + qsw[:, hh * LANES:(hh + 1) * LANES] * sin
        ).astype(BF16)
        km_ref[:, b0:b0 + LANES] = kn[:, hh * LANES:(hh + 1) * LANES].astype(BF16)
        km_ref[:, b0 + LANES:b0 + 2 * LANES] = k_rope


def _proj_call(x2, wp, l, seq):
    ntok = x2.shape[0]
    batch = ntok // seq
    rows = PROJ_ROWS
    steps_per_seq = seq // rows

    def const(name):
        return _layer_spec(wp[name], l)

    def tok(width):
        return pl.BlockSpec((rows, width), lambda i: (i, 0))

    vt_spec = pl.BlockSpec((1, HEADS * LANES, rows),
                           lambda i: (i // steps_per_seq, 0, i % steps_per_seq))
    pos_spec = pl.BlockSpec((rows, LANES), lambda i: (i % steps_per_seq, 0))
    vt_shape = jax.ShapeDtypeStruct((batch, HEADS * LANES, seq), BF16)
    return pl.pallas_call(
        _proj_kernel,
        grid=(ntok // rows,),
        in_specs=[tok(D_MODEL), const("w1"), const("wvt"), const("wuq"), const("wuqs"),
                  const("wukvk"), const("wukvvt"), const("gq"), const("gkv"), pos_spec, pos_spec],
        out_specs=[tok(DIFF_WIDTH), tok(DIFF_WIDTH), vt_spec,
                   tok(HEADS * MLA_QK_PAD), tok(HEADS * MLA_QK_PAD), vt_spec],
        out_shape=[jax.ShapeDtypeStruct((ntok, DIFF_WIDTH), BF16),
                   jax.ShapeDtypeStruct((ntok, DIFF_WIDTH), BF16),
                   vt_shape,
                   jax.ShapeDtypeStruct((ntok, HEADS * MLA_QK_PAD), BF16),
                   jax.ShapeDtypeStruct((ntok, HEADS * MLA_QK_PAD), BF16),
                   vt_shape],
        compiler_params=pltpu.CompilerParams(dimension_semantics=("arbitrary",),
                                             vmem_limit_bytes=VMEM_LIMIT),
        name="proj",
    )(x2, wp["w1"], wp["wvt"], wp["wuq"], wp["wuqs"], wp["wukvk"], wp["wukvvt"],
      wp["gq"], wp["gkv"], wp["cos"], wp["sin"])


def _col_max(a, b):
    m = jnp.max(b, axis=0, keepdims=True)
    return m if a is None else jnp.maximum(a, m)


def _attend(n_heads, n_blocks, n_keys_fn, q_fn, score_fn, finish_fn, vt_fn, lookahead):
    ones = jnp.ones((BF16_SUBLANES, ATT_TILE), BF16)
    items = [(h, qi, c) for h in range(n_heads) for qi in range(n_blocks)
             for c in range(n_keys_fn(qi))]
    qs, scores = {}, {}

    def issue(idx):
        h, qi, c = items[idx]
        if c == 0:
            qs[h, qi] = q_fn(h, qi)
        scores[idx] = score_fn(h, qi, qs[h, qi], c)

    for idx in range(min(lookahead, len(items))):
        issue(idx)
    m, acc = None, None
    for idx, (h, qi, c) in enumerate(items):
        if idx + lookahead < len(items):
            issue(idx + lookahead)
        s = scores.pop(idx)
        m_new = _col_max(m, s)
        e = jnp.exp2(s - m_new).astype(BF16)
        vt = jnp.concatenate([vt_fn(h, c), ones], axis=0)
        pv = jnp.dot(vt, e, preferred_element_type=F32)
        acc = pv if acc is None else acc * jnp.exp2(m - m_new) + pv
        m = m_new
        if c == n_keys_fn(qi) - 1:
            finish_fn(h, qi, acc[:LANES] / acc[LANES:LANES + 1])
            m, acc = None, None
        yield


def _diff_steps(q_ref, k_ref, vt_ref, bias_ref, lam_ref, g_ref, o_ref, lambda_init):
    t = ATT_TILE
    lane = lax.broadcasted_iota(jnp.int32, (t, LANES), 1)
    lp = lam_ref[0]
    lam = (jnp.exp(jnp.sum(lp[0:1] * lp[1:2], axis=-1, keepdims=True))
           - jnp.exp(jnp.sum(lp[2:3] * lp[3:4], axis=-1, keepdims=True)) + lambda_init)
    g = g_ref[0]

    def q_rows(h, b):
        qi, m = divmod(b, 2)
        q = q_ref[qi * t:(qi + 1) * t, h * LANES:(h + 1) * LANES]
        keep = lane < DIFF_HEAD_DIM if m == 0 else lane >= DIFF_HEAD_DIM
        return jnp.where(keep, q, jnp.zeros_like(q))

    def score(h, b, q, c):
        qi = b // 2
        k = k_ref[c * t:(c + 1) * t, h * LANES:(h + 1) * LANES]
        s = lax.dot_general(k, q, _NT, preferred_element_type=F32)
        return s + bias_ref[h, qi - c] if qi - c <= 1 else s

    def values(h, c):
        return vt_ref[0, h * LANES:(h + 1) * LANES, c * t:(c + 1) * t]

    first_map = {}

    def finish(h, b, o):
        qi, m = divmod(b, 2)
        if m == 0:
            first_map[h] = o
            return
        ot = first_map.pop(h) - lam * o
        y = _rms(ot.T, g) * (1.0 - lambda_init)
        o_ref[qi * t:(qi + 1) * t, h * LANES:(h + 1) * LANES] = y.astype(BF16)

    return _attend(ATT_HEADS_PER_STEP, 2 * (q_ref.shape[0] // t), lambda b: b // 2 + 1, q_rows,
                   score, finish, values, DIFF_LOOKAHEAD)


def _mla_steps(q_ref, k_ref, vt_ref, o_ref):
    t = ATT_TILE
    w = MLA_QK_PAD
    kc = lax.broadcasted_iota(jnp.int32, (t, t), 0) // CHUNK
    qc = lax.broadcasted_iota(jnp.int32, (t, t), 1) // CHUNK
    allowed = kc <= qc

    def q_rows(h, qi):
        return q_ref[qi * t:(qi + 1) * t, h * w:(h + 1) * w]

    def score(h, qi, q, c):
        k = k_ref[c * t:(c + 1) * t, h * w:(h + 1) * w]
        s = lax.dot_general(k, q, _NT, preferred_element_type=F32)
        return jnp.where(allowed, s, NEG_BIG) if c == qi else s

    def values(h, c):
        return vt_ref[0, h * LANES:(h + 1) * LANES, c * t:(c + 1) * t]

    def finish(h, qi, o):
        o_ref[qi * t:(qi + 1) * t, h * LANES:(h + 1) * LANES] = o.T.astype(BF16)

    return _attend(ATT_HEADS_PER_STEP, q_ref.shape[0] // t, lambda qi: qi + 1, q_rows, score,
                   finish, values, MLA_LOOKAHEAD)


def _attn_kernel(dq_ref, dk_ref, dvt_ref, bias_ref, lam_ref, g_ref, qm_ref, km_ref, vmt_ref,
                 oa_ref, ob_ref, *, lambda_init):
    streams = [_diff_steps(dq_ref, dk_ref, dvt_ref, bias_ref, lam_ref, g_ref, oa_ref, lambda_init),
               _mla_steps(qm_ref, km_ref, vmt_ref, ob_ref)]
    n_tiles = dq_ref.shape[0] // ATT_TILE
    per_head = n_tiles * (n_tiles + 1) // 2
    totals = [2 * ATT_HEADS_PER_STEP * per_head, ATT_HEADS_PER_STEP * per_head]
    done = [0, 0]
    while done[0] < totals[0] or done[1] < totals[1]:
        i = 0 if done[0] * totals[1] <= done[1] * totals[0] else 1
        next(streams[i])
        done[i] += 1


def _attn_call(dq, dk, dvt, bias, wp, l, qm, km, vmt, seq, lambda_init):
    ntok = dq.shape[0]
    n = ATT_HEADS_PER_STEP

    def rows(width):
        return pl.BlockSpec((seq, n * width), lambda b, h: (b, h))

    vt_spec = pl.BlockSpec((1, n * LANES, seq), lambda b, h: (b, h, 0))
    return pl.pallas_call(
        functools.partial(_attn_kernel, lambda_init=lambda_init),
        grid=(ntok // seq, HEADS // n),
        in_specs=[rows(LANES), rows(LANES), vt_spec,
                  pl.BlockSpec((n,) + bias.shape[1:], lambda b, h: (h, 0, 0, 0)),
                  _layer_spec(wp["lam"], l), _layer_spec(wp["subln"], l),
                  rows(MLA_QK_PAD), rows(MLA_QK_PAD), vt_spec],
        out_specs=[rows(LANES), rows(LANES)],
        out_shape=[jax.ShapeDtypeStruct((ntok, DIFF_WIDTH), BF16),
                   jax.ShapeDtypeStruct((ntok, MLA_WIDTH), BF16)],
        compiler_params=pltpu.CompilerParams(dimension_semantics=("arbitrary", "arbitrary"),
                                             vmem_limit_bytes=VMEM_LIMIT),
        name="attn",
    )(dq, dk, dvt, bias, wp["lam"], wp["subln"], qm, km, vmt)


def _post_kernel(x_ref, ma_ref, mb_ref, wo_ref, g1_ref, b1_ref, wi_ref, wout_ref, g2_ref, b2_ref,
                 o_ref):
    n_blocks = x_ref.shape[0] // POST_ROW_BLOCK
    n_chunks = D_FF // FF_CHUNK

    def attn_out(r):
        rs = slice(r * POST_ROW_BLOCK, (r + 1) * POST_ROW_BLOCK)
        y = (jnp.dot(ma_ref[rs, :], wo_ref[0, :DIFF_WIDTH, :], preferred_element_type=F32)
             + jnp.dot(mb_ref[rs, :], wo_ref[0, DIFF_WIDTH:, :], preferred_element_type=F32))
        return _layer_norm(ALPHA * x_ref[rs, :] + y, g1_ref[0], b1_ref[0])

    def up(x1b, c):
        return jnp.dot(x1b, wi_ref[0, :, c * FF_CHUNK:(c + 1) * FF_CHUNK],
                       preferred_element_type=F32)

    x1 = attn_out(0)
    for r in range(n_blocks):
        x1_next = attn_out(r + 1) if r + 1 < n_blocks else None
        x1b = x1.astype(BF16)
        u = up(x1b, 0)
        y2 = None
        for c in range(n_chunks):
            u_next = up(x1b, c + 1) if c + 1 < n_chunks else None
            act = jnp.square(jnp.maximum(u, 0.0)).astype(BF16)
            d = jnp.dot(act, wout_ref[0, c * FF_CHUNK:(c + 1) * FF_CHUNK, :],
                        preferred_element_type=F32)
            y2 = d if y2 is None else y2 + d
            u = u_next
        o_ref[r * POST_ROW_BLOCK:(r + 1) * POST_ROW_BLOCK, :] = _layer_norm(
            ALPHA * x1 + y2, g2_ref[0], b2_ref[0])
        x1 = x1_next


def _post_call(x2, mix_a, mix_b, wp, l):
    ntok = x2.shape[0]
    rows = POST_ROWS

    def const(name):
        return _layer_spec(wp[name], l, pipeline_mode=pl.Buffered(1))

    def tok(width):
        return pl.BlockSpec((rows, width), lambda i: (i, 0))

    return pl.pallas_call(
        _post_kernel,
        grid=(ntok // rows,),
        in_specs=[tok(D_MODEL), tok(DIFF_WIDTH), tok(MLA_WIDTH), const("wo"), const("g1"),
                  const("b1"), const("wi"), const("wout"), const("g2"), const("b2")],
        out_specs=tok(D_MODEL),
        out_shape=jax.ShapeDtypeStruct((ntok, D_MODEL), F32),
        compiler_params=pltpu.CompilerParams(dimension_semantics=("arbitrary",),
                                             vmem_limit_bytes=VMEM_LIMIT),
        name="post",
    )(x2, mix_a, mix_b, wp["wo"], wp["g1"], wp["b1"], wp["wi"], wp["wout"], wp["g2"], wp["b2"])


def _t5_bucket(rel):
    nb = N_BUCKETS // 2
    ret = (rel > 0).astype(jnp.int32) * nb
    n = jnp.abs(rel)
    max_exact = nb // 2
    nf = jnp.maximum(n, 1).astype(F32)
    large = max_exact + (jnp.log(nf / max_exact) / math.log(MAX_DISTANCE / max_exact)
                         * (nb - max_exact)).astype(jnp.int32)
    large = jnp.minimum(large, nb - 1)
    return ret + jnp.where(n < max_exact, n, large)


def _bias_tiles(rel_bias):
    t = ATT_TILE
    p = 2 * t
    table = rel_bias.astype(F32).T
    far = table[:, N_BUCKETS // 2 - 1]
    w = jnp.arange(p, dtype=jnp.int32)[None, :]
    d = jnp.arange(2, dtype=jnp.int32)[:, None]
    rel = (t - 1) - w - d * t
    vec = (table[:, _t5_bucket(rel)] - far[:, None, None]) * LOG2E
    rows = jnp.tile(vec, (1, 1, t))[..., :t * (p - 1)].reshape(HEADS, 2, t, p - 1)
    bias = rows[..., t - 1:2 * t - 1]
    kk = jnp.arange(t, dtype=jnp.int32)[:, None]
    qq = jnp.arange(t, dtype=jnp.int32)[None, :]
    allowed = (kk // CHUNK <= qq // CHUNK)[None, None] | (d > 0)[None, :, :, None]
    bias = jnp.where(allowed, bias, NEG_BIG)
    return bias


def _rope_tables(seq):
    pos = jnp.arange(seq, dtype=F32)
    inv = ROPE_THETA ** (-jnp.arange(0, MLA_ROPE_DIM, 2, dtype=F32) / MLA_ROPE_DIM)
    ang = pos[:, None] * inv[None, :]
    cos, sin = jnp.cos(ang), jnp.sin(ang)
    z = jnp.zeros((seq, LANES - MLA_ROPE_DIM), F32)
    return (jnp.concatenate([cos, cos, z], axis=-1), jnp.concatenate([-sin, sin, z], axis=-1))


def _swap_halves(w):
    half = w.shape[-1] // 2
    return jnp.concatenate([w[..., half:], w[..., :half]], axis=-1)


def _pack_params(w_in, lambda_q1, lambda_k1, lambda_q2, lambda_k2, subln_g, q_norm_g, w_uq,
                 kv_norm_g, w_ukv, w_o, ln1_g, ln1_b, w_mlp_in, w_mlp_out, ln2_g, ln2_b):
    depth = w_in.shape[0]
    o_k, o_v, o_cq = DIFF_WIDTH, 2 * DIFF_WIDTH, 3 * DIFF_WIDTH
    o_ckv = o_cq + MLA_Q_RANK
    o_kr = o_ckv + MLA_KV_RANK
    kr = w_in[..., o_kr:]
    w1 = jnp.concatenate([w_in[..., :o_k] * (DIFF_HEAD_DIM ** -0.5 * LOG2E), w_in[..., o_k:o_v],
                          w_in[..., o_cq:o_ckv], w_in[..., o_ckv:o_kr], kr, _swap_halves(kr)],
                         axis=-1)
    mla_scale = (MLA_NOPE_DIM + MLA_ROPE_DIM) ** -0.5 * LOG2E
    uq = (w_uq * mla_scale).reshape(depth, MLA_Q_RANK, HEADS, MLA_NOPE_DIM + MLA_ROPE_DIM)
    zq = jnp.zeros((depth, MLA_Q_RANK, HEADS, MLA_QK_PAD - MLA_NOPE_DIM - MLA_ROPE_DIM), F32)
    wuq = jnp.concatenate([uq, zq], axis=-1).reshape(depth, MLA_Q_RANK, HEADS * MLA_QK_PAD)
    wuqs = jnp.concatenate([_swap_halves(uq[..., MLA_NOPE_DIM:]), zq], axis=-1)
    wuqs = wuqs.reshape(depth, MLA_Q_RANK, HEADS * LANES)
    ukv = w_ukv.reshape(depth, MLA_KV_RANK, HEADS, MLA_NOPE_DIM + MLA_V_DIM)
    wukvk = ukv[..., :MLA_NOPE_DIM].reshape(depth, MLA_KV_RANK, HEADS * MLA_NOPE_DIM)
    wukvv = ukv[..., MLA_NOPE_DIM:].reshape(depth, MLA_KV_RANK, HEADS * MLA_V_DIM)
    row = lambda v: v.reshape(depth, 1, -1).astype(F32)
    return dict(
        w1=w1.astype(BF16), wvt=jnp.swapaxes(w_in[..., o_v:o_cq], 1, 2).astype(BF16),
        wuq=wuq.astype(BF16), wuqs=wuqs.astype(BF16), wukvk=wukvk.astype(BF16),
        wukvvt=jnp.swapaxes(wukvv, 1, 2).astype(BF16), gq=row(q_norm_g), gkv=row(kv_norm_g),
        subln=row(subln_g),
        lam=jnp.stack([lambda_q1, lambda_k1, lambda_q2, lambda_k2], axis=1).astype(F32),
        wo=w_o.astype(BF16), g1=row(ln1_g), b1=row(ln1_b), wi=w_mlp_in.astype(BF16),
        wout=w_mlp_out.astype(BF16), g2=row(ln2_g), b2=row(ln2_b))


def kernel(x, w_in, lambda_q1, lambda_k1, lambda_q2, lambda_k2, subln_g, q_norm_g, w_uq, kv_norm_g,
           w_ukv, rel_bias, w_o, ln1_g, ln1_b, w_mlp_in, w_mlp_out, ln2_g, ln2_b):
    batch, seq, _ = x.shape
    assert seq % PROJ_ROWS == 0 and seq % ATT_TILE == 0 and (batch * seq) % POST_ROWS == 0
    wp = _pack_params(w_in, lambda_q1, lambda_k1, lambda_q2, lambda_k2, subln_g, q_norm_g, w_uq,
                      kv_norm_g, w_ukv, w_o, ln1_g, ln1_b, w_mlp_in, w_mlp_out, ln2_g, ln2_b)
    wp["cos"], wp["sin"] = _rope_tables(seq)
    bias = _bias_tiles(rel_bias)
    x2 = x.reshape(batch * seq, D_MODEL)
    for l in range(DEPTH):
        lambda_init = 0.8 - 0.6 * math.exp(-0.3 * l)
        dq, dk, dvt, qm, km, vmt = _proj_call(x2, wp, l, seq)
        mix_a, mix_b = _attn_call(dq, dk, dvt, bias, wp, l, qm, km, vmt, seq, lambda_init)
        x2 = _post_call(x2, mix_a, mix_b, wp, l)
    return x2.reshape(batch, seq, D_MODEL)
```

```python
import functools
import math

import jax
import jax.numpy as jnp
from jax import lax
from jax.experimental import pallas as pl
from jax.experimental.pallas import tpu as pltpu

F32 = jnp.float32
BF16 = jnp.bfloat16

D_MODEL = 1024
DEPTH = 2
CHUNK = 64
HEADS = 4
DIFF_HEAD_DIM = 64
DIFF_V_DIM = 2 * DIFF_HEAD_DIM
DIFF_WIDTH = HEADS * DIFF_V_DIM
MLA_NOPE_DIM = 128
MLA_ROPE_DIM = 64
MLA_V_DIM = 128
MLA_Q_RANK = 256
MLA_KV_RANK = 128
MLA_WIDTH = HEADS * MLA_V_DIM
MLA_QK_PAD = 256
D_FF = 4 * D_MODEL
N_BUCKETS = 32
MAX_DISTANCE = 128
ROPE_THETA = 10000.0
ALPHA = (2 * DEPTH) ** 0.25
LN_EPS = 1e-5
RMS_EPS = 1e-6
NEG_BIG = -1e30
LOG2E = math.log2(math.e)

LANES = 128
BF16_SUBLANES = 16
ATT_TILE = 256
ATT_HEADS_PER_STEP = 2
PROJ_ROWS = 1024
POST_ROWS = 1024
POST_ROW_BLOCK = 256
FF_CHUNK = 1024
DIFF_LOOKAHEAD = 3
MLA_LOOKAHEAD = 3
VMEM_LIMIT = 56 * 1024 * 1024
assert ATT_TILE >= MAX_DISTANCE and ATT_TILE % CHUNK == 0

C_DQ = 0
C_DK = C_DQ + DIFF_WIDTH
C_CQ = C_DK + DIFF_WIDTH
C_CKV = C_CQ + MLA_Q_RANK
C_KR = C_CKV + MLA_KV_RANK
W1_COLS = C_KR + LANES

_NT = (((1,), (1,)), ((), ()))


def _rms(x, g):
    return x * lax.rsqrt(jnp.mean(x * x, axis=-1, keepdims=True) + RMS_EPS) * g


def _layer_norm(x, g, b):
    mu = jnp.mean(x, axis=-1, keepdims=True)
    xc = x - mu
    var = jnp.mean(xc * xc, axis=-1, keepdims=True)
    return xc * lax.rsqrt(var + LN_EPS) * g + b


def _layer_spec(arr, l, **kwargs):
    index = (l,) + (0,) * (arr.ndim - 1)
    return pl.BlockSpec((1,) + arr.shape[1:], lambda *_: index, **kwargs)


def _proj_kernel(x_ref, w1_ref, wvt_ref, wuq_ref, wuqs_ref, wukvk_ref, wukvvt_ref,
                 gq_ref, gkv_ref, cos_ref, sin_ref,
                 dq_ref, dk_ref, dvt_ref, qm_ref, km_ref, vmt_ref):
    xb = x_ref[...].astype(BF16)
    h = jnp.dot(xb, w1_ref[0], preferred_element_type=F32)
    dq_ref[...] = h[:, C_DQ:C_DQ + DIFF_WIDTH].astype(BF16)
    dk_ref[...] = h[:, C_DK:C_DK + DIFF_WIDTH].astype(BF16)
    dvt_ref[0] = lax.dot_general(wvt_ref[0], xb, _NT, preferred_element_type=F32).astype(BF16)

    cos = cos_ref[...]
    sin = sin_ref[...]
    c_q = _rms(h[:, C_CQ:C_CQ + MLA_Q_RANK], gq_ref[0]).astype(BF16)
    qf = jnp.dot(c_q, wuq_ref[0], preferred_element_type=F32)
    qsw = jnp.dot(c_q, wuqs_ref[0], preferred_element_type=F32)
    c_kv = _rms(h[:, C_CKV:C_CKV + MLA_KV_RANK], gkv_ref[0]).astype(BF16)
    kn = jnp.dot(c_kv, wukvk_ref[0], preferred_element_type=F32)
    vmt_ref[0] = lax.dot_general(wukvvt_ref[0], c_kv, _NT,
                                 preferred_element_type=F32).astype(BF16)
    kr = h[:, C_KR:C_KR + LANES]
    k_rope = (kr * cos + pltpu.roll(kr, LANES // 2, 1) * sin).astype(BF16)

    for hh in range(HEADS):
        b0 = hh * MLA_QK_PAD
        qm_ref[:, b0:b0 + LANES] = qf[:, b0:b0 + LANES].astype(BF16)
        qm_ref[:, b0 + LANES:b0 + 2 * LANES] = (
            qf[:, b0 + LANES:b0 + 2 * LANES] * cos + qsw[:, hh * LANES:(hh + 1) * LANES] * sin
        ).astype(BF16)
        km_ref[:, b0:b0 + LANES] = kn[:, hh * LANES:(hh + 1) * LANES].astype(BF16)
        km_ref[:, b0 + LANES:b0 + 2 * LANES] = k_rope


def _proj_call(x2, wp, l, seq):
    ntok = x2.shape[0]
    batch = ntok // seq
    rows = PROJ_ROWS
    steps_per_seq = seq // rows

    def const(name):
        return _layer_spec(wp[name], l)

    def tok(width):
        return pl.BlockSpec((rows, width), lambda i: (i, 0))

    vt_spec = pl.BlockSpec((1, HEADS * LANES, rows),
                           lambda i: (i // steps_per_seq, 0, i % steps_per_seq))
    pos_spec = pl.BlockSpec((rows, LANES), lambda i: (i % steps_per_seq, 0))
    vt_shape = jax.ShapeDtypeStruct((batch, HEADS * LANES, seq), BF16)
    return pl.pallas_call(
        _proj_kernel,
        grid=(ntok // rows,),
        in_specs=[tok(D_MODEL), const("w1"), const("wvt"), const("wuq"), const("wuqs"),
                  const("wukvk"), const("wukvvt"), const("gq"), const("gkv"), pos_spec, pos_spec],
        out_specs=[tok(DIFF_WIDTH), tok(DIFF_WIDTH), vt_spec,
                   tok(HEADS * MLA_QK_PAD), tok(HEADS * MLA_QK_PAD), vt_spec],
        out_shape=[jax.ShapeDtypeStruct((ntok, DIFF_WIDTH), BF16),
                   jax.ShapeDtypeStruct((ntok, DIFF_WIDTH), BF16),
                   vt_shape,
                   jax.ShapeDtypeStruct((ntok, HEADS * MLA_QK_PAD), BF16),
                   jax.ShapeDtypeStruct((ntok, HEADS * MLA_QK_PAD), BF16),
                   vt_shape],
        compiler_params=pltpu.CompilerParams(dimension_semantics=("arbitrary",),
                                             vmem_limit_bytes=VMEM_LIMIT),
        name="proj",
    )(x2, wp["w1"], wp["wvt"], wp["wuq"], wp["wuqs"], wp["wukvk"], wp["wukvvt"],
      wp["gq"], wp["gkv"], wp["cos"], wp["sin"])


def _col_max(a, b):
    m = jnp.max(b, axis=0, keepdims=True)
    return m if a is None else jnp.maximum(a, m)


def _attend(n_heads, n_blocks, n_keys_fn, q_fn, score_fn, finish_fn, vt_fn, lookahead):
    ones = jnp.ones((BF16_SUBLANES, ATT_TILE), BF16)
    items = [(h, qi, c) for h in range(n_heads) for qi in range(n_blocks)
             for c in range(n_keys_fn(qi))]
    qs, scores = {}, {}

    def issue(idx):
        h, qi, c = items[idx]
        if c == 0:
            qs[h, qi] = q_fn(h, qi)
        scores[idx] = score_fn(h, qi, qs[h, qi], c)

    for idx in range(min(lookahead, len(items))):
        issue(idx)
    m, acc = None, None
    for idx, (h, qi, c) in enumerate(items):
        if idx + lookahead < len(items):
            issue(idx + lookahead)
        s = scores.pop(idx)
        m_new = _col_max(m, s)
        e = jnp.exp2(s - m_new).astype(BF16)
        vt = jnp.concatenate([vt_fn(h, c), ones], axis=0)
        pv = jnp.dot(vt, e, preferred_element_type=F32)
        acc = pv if acc is None else acc * jnp.exp2(m - m_new) + pv
        m = m_new
        if c == n_keys_fn(qi) - 1:
            finish_fn(h, qi, acc[:LANES] / acc[LANES:LANES + 1])
            m, acc = None, None
        yield


def _diff_steps(m, first_map, q_ref, k_ref, vt_ref, bias_ref, lam_ref, g_ref, o_ref, lambda_init):
    t = ATT_TILE
    lane = lax.broadcasted_iota(jnp.int32, (t, LANES), 1)
    lp = lam_ref[0]
    lam = (jnp.exp(jnp.sum(lp[0:1] * lp[1:2], axis=-1, keepdims=True))
           - jnp.exp(jnp.sum(lp[2:3] * lp[3:4], axis=-1, keepdims=True)) + lambda_init)
    g = g_ref[0]

    def q_rows(h, qi):
        q = q_ref[qi * t:(qi + 1) * t, h * LANES:(h + 1) * LANES]
        keep = lane < DIFF_HEAD_DIM if m == 0 else lane >= DIFF_HEAD_DIM
        return jnp.where(keep, q, jnp.zeros_like(q))

    def score(h, qi, q, c):
        k = k_ref[c * t:(c + 1) * t, h * LANES:(h + 1) * LANES]
        s = lax.dot_general(k, q, _NT, preferred_element_type=F32)
        return s + bias_ref[h, qi - c] if qi - c <= 1 else s

    def values(h, c):
        return vt_ref[0, h * LANES:(h + 1) * LANES, c * t:(c + 1) * t]

    def finish(h, qi, o):
        if m == 0:
            first_map[h, qi] = o
            return
        ot = first_map.pop((h, qi)) - lam * o
        y = _rms(ot.T, g) * (1.0 - lambda_init)
        o_ref[qi * t:(qi + 1) * t, h * LANES:(h + 1) * LANES] = y.astype(BF16)

    return _attend(ATT_HEADS_PER_STEP, q_ref.shape[0] // t, lambda qi: qi + 1, q_rows, score,
                   finish, values, DIFF_LOOKAHEAD)


def _mla_steps(q_ref, k_ref, vt_ref, o_ref):
    t = ATT_TILE
    w = MLA_QK_PAD
    kc = lax.broadcasted_iota(jnp.int32, (t, t), 0) // CHUNK
    qc = lax.broadcasted_iota(jnp.int32, (t, t), 1) // CHUNK
    allowed = kc <= qc

    def q_rows(h, qi):
        return q_ref[qi * t:(qi + 1) * t, h * w:(h + 1) * w]

    def score(h, qi, q, c):
        k = k_ref[c * t:(c + 1) * t, h * w:(h + 1) * w]
        s = lax.dot_general(k, q, _NT, preferred_element_type=F32)
        return jnp.where(allowed, s, NEG_BIG) if c == qi else s

    def values(h, c):
        return vt_ref[0, h * LANES:(h + 1) * LANES, c * t:(c + 1) * t]

    def finish(h, qi, o):
        o_ref[qi * t:(qi + 1) * t, h * LANES:(h + 1) * LANES] = o.T.astype(BF16)

    return _attend(ATT_HEADS_PER_STEP, q_ref.shape[0] // t, lambda qi: qi + 1, q_rows, score,
                   finish, values, MLA_LOOKAHEAD)


def _attn_kernel(dq_ref, dk_ref, dvt_ref, bias_ref, lam_ref, g_ref, qm_ref, km_ref, vmt_ref,
                 oa_ref, ob_ref, *, lambda_init):
    first_map = {}
    diff_refs = (dq_ref, dk_ref, dvt_ref, bias_ref, lam_ref, g_ref, oa_ref, lambda_init)
    streams = [_diff_steps(0, first_map, *diff_refs), _diff_steps(1, first_map, *diff_refs),
               _mla_steps(qm_ref, km_ref, vmt_ref, ob_ref)]
    while streams:
        for s in list(streams):
            if next(s, "done") == "done":
                streams.remove(s)


def _attn_call(dq, dk, dvt, bias, wp, l, qm, km, vmt, seq, lambda_init):
    ntok = dq.shape[0]
    n = ATT_HEADS_PER_STEP

    def rows(width):
        return pl.BlockSpec((seq, n * width), lambda b, h: (b, h))

    vt_spec = pl.BlockSpec((1, n * LANES, seq), lambda b, h: (b, h, 0))
    return pl.pallas_call(
        functools.partial(_attn_kernel, lambda_init=lambda_init),
        grid=(ntok // seq, HEADS // n),
        in_specs=[rows(LANES), rows(LANES), vt_spec,
                  pl.BlockSpec((n,) + bias.shape[1:], lambda b, h: (h, 0, 0, 0)),
                  _layer_spec(wp["lam"], l), _layer_spec(wp["subln"], l),
                  rows(MLA_QK_PAD), rows(MLA_QK_PAD), vt_spec],
        out_specs=[rows(LANES), rows(LANES)],
        out_shape=[jax.ShapeDtypeStruct((ntok, DIFF_WIDTH), BF16),
                   jax.ShapeDtypeStruct((ntok, MLA_WIDTH), BF16)],
        compiler_params=pltpu.CompilerParams(dimension_semantics=("arbitrary", "arbitrary"),
                                             vmem_limit_bytes=VMEM_LIMIT),
        name="attn",
    )(dq, dk, dvt, bias, wp["lam"], wp["subln"], qm, km, vmt)


def _post_kernel(x_ref, ma_ref, mb_ref, wo_ref, g1_ref, b1_ref, wi_ref, wout_ref, g2_ref, b2_ref,
                 o_ref):
    n_blocks = x_ref.shape[0] // POST_ROW_BLOCK
    n_chunks = D_FF // FF_CHUNK

    def attn_out(r):
        rs = slice(r * POST_ROW_BLOCK, (r + 1) * POST_ROW_BLOCK)
        y = (jnp.dot(ma_ref[rs, :], wo_ref[0, :DIFF_WIDTH, :], preferred_element_type=F32)
             + jnp.dot(mb_ref[rs, :], wo_ref[0, DIFF_WIDTH:, :], preferred_element_type=F32))
        return _layer_norm(ALPHA * x_ref[rs, :] + y, g1_ref[0], b1_ref[0])

    def up(x1b, c):
        return jnp.dot(x1b, wi_ref[0, :, c * FF_CHUNK:(c + 1) * FF_CHUNK],
                       preferred_element_type=F32)

    x1 = attn_out(0)
    for r in range(n_blocks):
        x1_next = attn_out(r + 1) if r + 1 < n_blocks else None
        x1b = x1.astype(BF16)
        u = up(x1b, 0)
        y2 = None
        for c in range(n_chunks):
            u_next = up(x1b, c + 1) if c + 1 < n_chunks else None
            act = jnp.square(jnp.maximum(u, 0.0)).astype(BF16)
            d = jnp.dot(act, wout_ref[0, c * FF_CHUNK:(c + 1) * FF_CHUNK, :],
                        preferred_element_type=F32)
            y2 = d if y2 is None else y2 + d
            u = u_next
        o_ref[r * POST_ROW_BLOCK:(r + 1) * POST_ROW_BLOCK, :] = _layer_norm(
            ALPHA * x1 + y2, g2_ref[0], b2_ref[0])
        x1 = x1_next


def _post_call(x2, mix_a, mix_b, wp, l):
    ntok = x2.shape[0]
    rows = POST_ROWS

    def const(name):
        return _layer_spec(wp[name], l, pipeline_mode=pl.Buffered(1))

    def tok(width):
        return pl.BlockSpec((rows, width), lambda i: (i, 0))

    return pl.pallas_call(
        _post_kernel,
        grid=(ntok // rows,),
        in_specs=[tok(D_MODEL), tok(DIFF_WIDTH), tok(MLA_WIDTH), const("wo"), const("g1"),
                  const("b1"), const("wi"), const("wout"), const("g2"), const("b2")],
        out_specs=tok(D_MODEL),
        out_shape=jax.ShapeDtypeStruct((ntok, D_MODEL), F32),
        compiler_params=pltpu.CompilerParams(dimension_semantics=("arbitrary",),
                                             vmem_limit_bytes=VMEM_LIMIT),
        name="post",
    )(x2, mix_a, mix_b, wp["wo"], wp["g1"], wp["b1"], wp["wi"], wp["wout"], wp["g2"], wp["b2"])


def _t5_bucket(rel):
    nb = N_BUCKETS // 2
    ret = (rel > 0).astype(jnp.int32) * nb
    n = jnp.abs(rel)
    max_exact = nb // 2
    nf = jnp.maximum(n, 1).astype(F32)
    large = max_exact + (jnp.log(nf / max_exact) / math.log(MAX_DISTANCE / max_exact)
                         * (nb - max_exact)).astype(jnp.int32)
    large = jnp.minimum(large, nb - 1)
    return ret + jnp.where(n < max_exact, n, large)


def _bias_tiles(rel_bias):
    t = ATT_TILE
    p = 2 * t
    table = rel_bias.astype(F32).T
    far = table[:, N_BUCKETS // 2 - 1]
    w = jnp.arange(p, dtype=jnp.int32)[None, :]
    d = jnp.arange(2, dtype=jnp.int32)[:, None]
    rel = (t - 1) - w - d * t
    vec = (table[:, _t5_bucket(rel)] - far[:, None, None]) * LOG2E
    rows = jnp.tile(vec, (1, 1, t))[..., :t * (p - 1)].reshape(HEADS, 2, t, p - 1)
    bias = rows[..., t - 1:2 * t - 1]
    kk = jnp.arange(t, dtype=jnp.int32)[:, None]
    qq = jnp.arange(t, dtype=jnp.int32)[None, :]
    allowed = (kk // CHUNK <= qq // CHUNK)[None, None] | (d > 0)[None, :, :, None]
    bias = jnp.where(allowed, bias, NEG_BIG)
    return bias


def _rope_tables(seq):
    pos = jnp.arange(seq, dtype=F32)
    inv = ROPE_THETA ** (-jnp.arange(0, MLA_ROPE_DIM, 2, dtype=F32) / MLA_ROPE_DIM)
    ang = pos[:, None] * inv[None, :]
    cos, sin = jnp.cos(ang), jnp.sin(ang)
    z = jnp.zeros((seq, LANES - MLA_ROPE_DIM), F32)
    return (jnp.concatenate([cos, cos, z], axis=-1), jnp.concatenate([-sin, sin, z], axis=-1))


def _swap_halves(w):
    half = w.shape[-1] // 2
    return jnp.concatenate([w[..., half:], w[..., :half]], axis=-1)


def _pack_params(w_in, lambda_q1, lambda_k1, lambda_q2, lambda_k2, subln_g, q_norm_g, w_uq,
                 kv_norm_g, w_ukv, w_o, ln1_g, ln1_b, w_mlp_in, w_mlp_out, ln2_g, ln2_b):
    depth = w_in.shape[0]
    o_k, o_v, o_cq = DIFF_WIDTH, 2 * DIFF_WIDTH, 3 * DIFF_WIDTH
    o_ckv = o_cq + MLA_Q_RANK
    o_kr = o_ckv + MLA_KV_RANK
    kr = w_in[..., o_kr:]
    w1 = jnp.concatenate([w_in[..., :o_k] * (DIFF_HEAD_DIM ** -0.5 * LOG2E), w_in[..., o_k:o_v],
                          w_in[..., o_cq:o_ckv], w_in[..., o_ckv:o_kr], kr, _swap_halves(kr)],
                         axis=-1)
    mla_scale = (MLA_NOPE_DIM + MLA_ROPE_DIM) ** -0.5 * LOG2E
    uq = (w_uq * mla_scale).reshape(depth, MLA_Q_RANK, HEADS, MLA_NOPE_DIM + MLA_ROPE_DIM)
    zq = jnp.zeros((depth, MLA_Q_RANK, HEADS, MLA_QK_PAD - MLA_NOPE_DIM - MLA_ROPE_DIM), F32)
    wuq = jnp.concatenate([uq, zq], axis=-1).reshape(depth, MLA_Q_RANK, HEADS * MLA_QK_PAD)
    wuqs = jnp.concatenate([_swap_halves(uq[..., MLA_NOPE_DIM:]), zq], axis=-1)
    wuqs = wuqs.reshape(depth, MLA_Q_RANK, HEADS * LANES)
    ukv = w_ukv.reshape(depth, MLA_KV_RANK, HEADS, MLA_NOPE_DIM + MLA_V_DIM)
    wukvk = ukv[..., :MLA_NOPE_DIM].reshape(depth, MLA_KV_RANK, HEADS * MLA_NOPE_DIM)
    wukvv = ukv[..., MLA_NOPE_DIM:].reshape(depth, MLA_KV_RANK, HEADS * MLA_V_DIM)
    row = lambda v: v.reshape(depth, 1, -1).astype(F32)
    return dict(
        w1=w1.astype(BF16), wvt=jnp.swapaxes(w_in[..., o_v:o_cq], 1, 2).astype(BF16),
        wuq=wuq.astype(BF16), wuqs=wuqs.astype(BF16), wukvk=wukvk.astype(BF16),
        wukvvt=jnp.swapaxes(wukvv, 1, 2).astype(BF16), gq=row(q_norm_g), gkv=row(kv_norm_g),
        subln=row(subln_g),
        lam=jnp.stack([lambda_q1, lambda_k1, lambda_q2, lambda_k2], axis=1).astype(F32),
        wo=w_o.astype(BF16), g1=row(ln1_g), b1=row(ln1_b), wi=w_mlp_in.astype(BF16),
        wout=w_mlp_out.astype(BF16), g2=row(ln2_g), b2=row(ln2_b))


def kernel(x, w_in, lambda_q1, lambda_k1, lambda_q2, lambda_k2, subln_g, q_norm_g, w_uq, kv_norm_g,
           w_ukv, rel_bias, w_o, ln1_g, ln1_b, w_mlp_in, w_mlp_out, ln2_g, ln2_b):
    batch, seq, _ = x.shape
    assert seq % PROJ_ROWS == 0 and seq % ATT_TILE == 0 and (batch * seq) % POST_ROWS == 0
    wp = _pack_params(w_in, lambda_q1, lambda_k1, lambda_q2, lambda_k2, subln_g, q_norm_g, w_uq,
                      kv_norm_g, w_ukv, w_o, ln1_g, ln1_b, w_mlp_in, w_mlp_out, ln2_g, ln2_b)
    wp["cos"], wp["sin"] = _rope_tables(seq)
    bias = _bias_tiles(rel_bias)
    x2 = x.reshape(batch * seq, D_MODEL)
    for l in range(DEPTH):
        lambda_init = 0.8 - 0.6 * math.exp(-0.3 * l)
        dq, dk, dvt, qm, km, vmt = _proj_call(x2, wp, l, seq)
        mix_a, mix_b = _attn_call(dq, dk, dvt, bias, wp, l, qm, km, vmt, seq, lambda_init)
        x2 = _post_call(x2, mix_a, mix_b, wp, l)
    return x2.reshape(batch, seq, D_MODEL)
```

```python
import functools
import math

import jax
import jax.numpy as jnp
from jax import lax
from jax.experimental import pallas as pl
from jax.experimental.pallas import tpu as pltpu

F32 = jnp.float32
BF16 = jnp.bfloat16

D_MODEL = 1024
DEPTH = 2
CHUNK = 64
HEADS = 4
DIFF_HEAD_DIM = 64
DIFF_V_DIM = 2 * DIFF_HEAD_DIM
DIFF_WIDTH = HEADS * DIFF_V_DIM
MLA_NOPE_DIM = 128
MLA_ROPE_DIM = 64
MLA_V_DIM = 128
MLA_Q_RANK = 256
MLA_KV_RANK = 128
MLA_WIDTH = HEADS * MLA_V_DIM
MLA_QK_PAD = 256
D_FF = 4 * D_MODEL
N_BUCKETS = 32
MAX_DISTANCE = 128
ROPE_THETA = 10000.0
ALPHA = (2 * DEPTH) ** 0.25
LN_EPS = 1e-5
RMS_EPS = 1e-6
NEG_BIG = -1e30
LOG2E = math.log2(math.e)

LANES = 128
BF16_SUBLANES = 16
ATT_TILE = 256
ATT_HEADS_PER_STEP = 2
PROJ_ROWS = 1024
POST_ROWS = 1024
POST_ROW_BLOCK = 256
FF_CHUNK = 1024
DIFF_LOOKAHEAD = 3
MLA_LOOKAHEAD = 3
VMEM_LIMIT = 56 * 1024 * 1024
assert ATT_TILE >= MAX_DISTANCE and ATT_TILE % CHUNK == 0

C_DQ = 0
C_DK = C_DQ + DIFF_WIDTH
C_CQ = C_DK + DIFF_WIDTH
C_CKV = C_CQ + MLA_Q_RANK
C_KR = C_CKV + MLA_KV_RANK
W1_COLS = C_KR + LANES

_NT = (((1,), (1,)), ((), ()))


def _rms(x, g):
    return x * lax.rsqrt(jnp.mean(x * x, axis=-1, keepdims=True) + RMS_EPS) * g


def _layer_norm(x, g, b):
    mu = jnp.mean(x, axis=-1, keepdims=True)
    xc = x - mu
    var = jnp.mean(xc * xc, axis=-1, keepdims=True)
    return xc * lax.rsqrt(var + LN_EPS) * g + b


def _layer_spec(arr, l, **kwargs):
    index = (l,) + (0,) * (arr.ndim - 1)
    return pl.BlockSpec((1,) + arr.shape[1:], lambda *_: index, **kwargs)


def _proj_kernel(x_ref, w1_ref, wvt_ref, wuq_ref, wuqs_ref, wukvk_ref, wukvvt_ref,
                 gq_ref, gkv_ref, cos_ref, sin_ref,
                 dq_ref, dk_ref, dvt_ref, qm_ref, km_ref, vmt_ref):
    xb = x_ref[...].astype(BF16)
    h = jnp.dot(xb, w1_ref[0], preferred_element_type=F32)
    dq_ref[...] = h[:, C_DQ:C_DQ + DIFF_WIDTH].astype(BF16)
    dk_ref[...] = h[:, C_DK:C_DK + DIFF_WIDTH].astype(BF16)
    dvt_ref[0] = lax.dot_general(wvt_ref[0], xb, _NT, preferred_element_type=F32).astype(BF16)

    cos = cos_ref[...]
    sin = sin_ref[...]
    c_q = _rms(h[:, C_CQ:C_CQ + MLA_Q_RANK], gq_ref[0]).astype(BF16)
    qf = jnp.dot(c_q, wuq_ref[0], preferred_element_type=F32)
    qsw = jnp.dot(c_q, wuqs_ref[0], preferred_element_type=F32)
    c_kv = _rms(h[:, C_CKV:C_CKV + MLA_KV_RANK], gkv_ref[0]).astype(BF16)
    kn = jnp.dot(c_kv, wukvk_ref[0], preferred_element_type=F32)
    vmt_ref[0] = lax.dot_general(wukvvt_ref[0], c_kv, _NT,
                                 preferred_element_type=F32).astype(BF16)
    kr = h[:, C_KR:C_KR + LANES]
    k_rope = (kr * cos + pltpu.roll(kr, LANES // 2, 1) * sin).astype(BF16)

    for hh in range(HEADS):
        b0 = hh * MLA_QK_PAD
        qm_ref[:, b0:b0 + LANES] = qf[:, b0:b0 + LANES].astype(BF16)
        qm_ref[:, b0 + LANES:b0 + 2 * LANES] = (
            qf[:, b0 + LANES:b0 + 2 * LANES] * cos + qsw[:, hh * LANES:(hh + 1) * LANES] * sin
        ).astype(BF16)
        km_ref[:, b0:b0 + LANES] = kn[:, hh * LANES:(hh + 1) * LANES].astype(BF16)
        km_ref[:, b0 + LANES:b0 + 2 * LANES] = k_rope


def _proj_call(x2, wp, l, seq):
    ntok = x2.shape[0]
    batch = ntok // seq
    rows = PROJ_ROWS
    steps_per_seq = seq // rows

    def const(name):
        return _layer_spec(wp[name], l)

    def tok(width):
        return pl.BlockSpec((rows, width), lambda i: (i, 0))

    vt_spec = pl.BlockSpec((1, HEADS * LANES, rows),
                           lambda i: (i // steps_per_seq, 0, i % steps_per_seq))
    pos_spec = pl.BlockSpec((rows, LANES), lambda i: (i % steps_per_seq, 0))
    vt_shape = jax.ShapeDtypeStruct((batch, HEADS * LANES, seq), BF16)
    return pl.pallas_call(
        _proj_kernel,
        grid=(ntok // rows,),
        in_specs=[tok(D_MODEL), const("w1"), const("wvt"), const("wuq"), const("wuqs"),
                  const("wukvk"), const("wukvvt"), const("gq"), const("gkv"), pos_spec, pos_spec],
        out_specs=[tok(DIFF_WIDTH), tok(DIFF_WIDTH), vt_spec,
                   tok(HEADS * MLA_QK_PAD), tok(HEADS * MLA_QK_PAD), vt_spec],
        out_shape=[jax.ShapeDtypeStruct((ntok, DIFF_WIDTH), BF16),
                   jax.ShapeDtypeStruct((ntok, DIFF_WIDTH), BF16),
                   vt_shape,
                   jax.ShapeDtypeStruct((ntok, HEADS * MLA_QK_PAD), BF16),
                   jax.ShapeDtypeStruct((ntok, HEADS * MLA_QK_PAD), BF16),
                   vt_shape],
        compiler_params=pltpu.CompilerParams(dimension_semantics=("arbitrary",),
                                             vmem_limit_bytes=VMEM_LIMIT),
        name="proj",
    )(x2, wp["w1"], wp["wvt"], wp["wuq"], wp["wuqs"], wp["wukvk"], wp["wukvvt"],
      wp["gq"], wp["gkv"], wp["cos"], wp["sin"])


def _col_max(a, b):
    m = jnp.max(b, axis=0, keepdims=True)
    return m if a is None else jnp.maximum(a, m)


def _attend(n_heads, n_blocks, n_keys_fn, q_fn, score_fn, finish_fn, vt_fn, lookahead):
    ones = jnp.ones((BF16_SUBLANES, ATT_TILE), BF16)
    items = [(h, qi, c) for h in range(n_heads) for qi in range(n_blocks)
             for c in range(n_keys_fn(qi))]
    qs, scores = {}, {}

    def issue(idx):
        h, qi, c = items[idx]
        if c == 0:
            qs[h, qi] = q_fn(h, qi)
        scores[idx] = score_fn(h, qi, qs[h, qi], c)

    for idx in range(min(lookahead, len(items))):
        issue(idx)
    m, acc = None, None
    for idx, (h, qi, c) in enumerate(items):
        if idx + lookahead < len(items):
            issue(idx + lookahead)
        s = scores.pop(idx)
        m_new = _col_max(m, s)
        e = jnp.exp2(s - m_new).astype(BF16)
        vt = jnp.concatenate([vt_fn(h, c), ones], axis=0)
        pv = jnp.dot(vt, e, preferred_element_type=F32)
        acc = pv if acc is None else acc * jnp.exp2(m - m_new) + pv
        m = m_new
        if c == n_keys_fn(qi) - 1:
            finish_fn(h, qi, acc[:LANES] / acc[LANES:LANES + 1])
            m, acc = None, None
        yield


def _diff_steps(q_ref, k_ref, vt_ref, bias_ref, lam_ref, g_ref, o_ref, lambda_init):
    t = ATT_TILE
    lane = lax.broadcasted_iota(jnp.int32, (t, LANES), 1)
    lp = lam_ref[0]
    lam = (jnp.exp(jnp.sum(lp[0:1] * lp[1:2], axis=-1, keepdims=True))
           - jnp.exp(jnp.sum(lp[2:3] * lp[3:4], axis=-1, keepdims=True)) + lambda_init)
    g = g_ref[0]

    def q_rows(h, b):
        qi, m = divmod(b, 2)
        q = q_ref[qi * t:(qi + 1) * t, h * LANES:(h + 1) * LANES]
        keep = lane < DIFF_HEAD_DIM if m == 0 else lane >= DIFF_HEAD_DIM
        return jnp.where(keep, q, jnp.zeros_like(q))

    def score(h, b, q, c):
        qi = b // 2
        k = k_ref[c * t:(c + 1) * t, h * LANES:(h + 1) * LANES]
        s = lax.dot_general(k, q, _NT, preferred_element_type=F32)
        return s + bias_ref[h, qi - c] if qi - c <= 1 else s

    def values(h, c):
        return vt_ref[0, h * LANES:(h + 1) * LANES, c * t:(c + 1) * t]

    first_map = {}

    def finish(h, b, o):
        qi, m = divmod(b, 2)
        if m == 0:
            first_map[h] = o
            return
        ot = first_map.pop(h) - lam * o
        y = _rms(ot.T, g) * (1.0 - lambda_init)
        o_ref[qi * t:(qi + 1) * t, h * LANES:(h + 1) * LANES] = y.astype(BF16)

    return _attend(ATT_HEADS_PER_STEP, 2 * (q_ref.shape[0] // t), lambda b: b // 2 + 1, q_rows,
                   score, finish, values, DIFF_LOOKAHEAD)


def _mla_steps(q_ref, k_ref, vt_ref, o_ref):
    t = ATT_TILE
    w = MLA_QK_PAD
    kc = lax.broadcasted_iota(jnp.int32, (t, t), 0) // CHUNK
    qc = lax.broadcasted_iota(jnp.int32, (t, t), 1) // CHUNK
    allowed = kc <= qc

    def q_rows(h, qi):
        return q_ref[qi * t:(qi + 1) * t, h * w:(h + 1) * w]

    def score(h, qi, q, c):
        k = k_ref[c * t:(c + 1) * t, h * w:(h + 1) * w]
        s = lax.dot_general(k, q, _NT, preferred_element_type=F32)
        return jnp.where(allowed, s, NEG_BIG) if c == qi else s

    def values(h, c):
        return vt_ref[0, h * LANES:(h + 1) * LANES, c * t:(c + 1) * t]

    def finish(h, qi, o):
        o_ref[qi * t:(qi + 1) * t, h * LANES:(h + 1) * LANES] = o.T.astype(BF16)

    return _attend(ATT_HEADS_PER_STEP, q_ref.shape[0] // t, lambda qi: qi + 1, q_rows, score,
                   finish, values, MLA_LOOKAHEAD)


def _attn_kernel(dq_ref, dk_ref, dvt_ref, bias_ref, lam_ref, g_ref, qm_ref, km_ref, vmt_ref,
                 oa_ref, ob_ref, *, lambda_init):
    streams = [_diff_steps(dq_ref, dk_ref, dvt_ref, bias_ref, lam_ref, g_ref, oa_ref, lambda_init),
               _mla_steps(qm_ref, km_ref, vmt_ref, ob_ref)]
    n_tiles = dq_ref.shape[0] // ATT_TILE
    per_head = n_tiles * (n_tiles + 1) // 2
    totals = [2 * ATT_HEADS_PER_STEP * per_head, ATT_HEADS_PER_STEP * per_head]
    done = [0, 0]
    while done[0] < totals[0] or done[1] < totals[1]:
        i = 0 if done[0] * totals[1] <= done[1] * totals[0] else 1
        next(streams[i])
        done[i] += 1


def _attn_call(dq, dk, dvt, bias, wp, l, qm, km, vmt, seq, lambda_init):
    ntok = dq.shape[0]
    n = ATT_HEADS_PER_STEP

    def rows(width):
        return pl.BlockSpec((seq, n * width), lambda b, h: (b, h))

    vt_spec = pl.BlockSpec((1, n * LANES, seq), lambda b, h: (b, h, 0))
    return pl.pallas_call(
        functools.partial(_attn_kernel, lambda_init=lambda_init),
        grid=(ntok // seq, HEADS // n),
        in_specs=[rows(LANES), rows(LANES), vt_spec,
                  pl.BlockSpec((n,) + bias.shape[1:], lambda b, h: (h, 0, 0, 0)),
                  _layer_spec(wp["lam"], l), _layer_spec(wp["subln"], l),
                  rows(MLA_QK_PAD), rows(MLA_QK_PAD), vt_spec],
        out_specs=[rows(LANES), rows(LANES)],
        out_shape=[jax.ShapeDtypeStruct((ntok, DIFF_WIDTH), BF16),
                   jax.ShapeDtypeStruct((ntok, MLA_WIDTH), BF16)],
        compiler_params=pltpu.CompilerParams(dimension_semantics=("arbitrary", "arbitrary"),
                                             vmem_limit_bytes=VMEM_LIMIT),
        name="attn",
    )(dq, dk, dvt, bias, wp["lam"], wp["subln"], qm, km, vmt)


def _post_kernel(x_ref, ma_ref, mb_ref, wo_ref, g1_ref, b1_ref, wi_ref, wout_ref, g2_ref, b2_ref,
                 o_ref):
    n_blocks = x_ref.shape[0] // POST_ROW_BLOCK
    n_chunks = D_FF // FF_CHUNK

    def attn_out(r):
        rs = slice(r * POST_ROW_BLOCK, (r + 1) * POST_ROW_BLOCK)
        y = (jnp.dot(ma_ref[rs, :], wo_ref[0, :DIFF_WIDTH, :], preferred_element_type=F32)
             + jnp.dot(mb_ref[rs, :], wo_ref[0, DIFF_WIDTH:, :], preferred_element_type=F32))
        return _layer_norm(ALPHA * x_ref[rs, :] + y, g1_ref[0], b1_ref[0])

    def up(x1b, c):
        return jnp.dot(x1b, wi_ref[0, :, c * FF_CHUNK:(c + 1) * FF_CHUNK],
                       preferred_element_type=F32)

    def mlp(r, x1):
        x1b = x1.astype(BF16)
        u = up(x1b, 0)
        yield
        y2 = None
        for c in range(n_chunks):
            u_next = up(x1b, c + 1) if c + 1 < n_chunks else None
            act = jnp.square(jnp.maximum(u, 0.0)).astype(BF16)
            d = jnp.dot(act, wout_ref[0, c * FF_CHUNK:(c + 1) * FF_CHUNK, :],
                        preferred_element_type=F32)
            y2 = d if y2 is None else y2 + d
            u = u_next
            yield
        o_ref[r * POST_ROW_BLOCK:(r + 1) * POST_ROW_BLOCK, :] = _layer_norm(
            ALPHA * x1 + y2, g2_ref[0], b2_ref[0])

    x1s = [attn_out(0), attn_out(1)]
    for r in range(0, n_blocks, 2):
        nxt = [attn_out(r + 2), attn_out(r + 3)] if r + 2 < n_blocks else None
        chains = [mlp(r, x1s[0]), mlp(r + 1, x1s[1])]
        while chains:
            for ch in list(chains):
                if next(ch, "done") == "done":
                    chains.remove(ch)
        x1s = nxt


def _post_call(x2, mix_a, mix_b, wp, l):
    ntok = x2.shape[0]
    rows = POST_ROWS

    def const(name):
        return _layer_spec(wp[name], l, pipeline_mode=pl.Buffered(1))

    def tok(width):
        return pl.BlockSpec((rows, width), lambda i: (i, 0))

    return pl.pallas_call(
        _post_kernel,
        grid=(ntok // rows,),
        in_specs=[tok(D_MODEL), tok(DIFF_WIDTH), tok(MLA_WIDTH), const("wo"), const("g1"),
                  const("b1"), const("wi"), const("wout"), const("g2"), const("b2")],
        out_specs=tok(D_MODEL),
        out_shape=jax.ShapeDtypeStruct((ntok, D_MODEL), F32),
        compiler_params=pltpu.CompilerParams(dimension_semantics=("arbitrary",),
                                             vmem_limit_bytes=VMEM_LIMIT),
        name="post",
    )(x2, mix_a, mix_b, wp["wo"], wp["g1"], wp["b1"], wp["wi"], wp["wout"], wp["g2"], wp["b2"])


def _t5_bucket(rel):
    nb = N_BUCKETS // 2
    ret = (rel > 0).astype(jnp.int32) * nb
    n = jnp.abs(rel)
    max_exact = nb // 2
    nf = jnp.maximum(n, 1).astype(F32)
    large = max_exact + (jnp.log(nf / max_exact) / math.log(MAX_DISTANCE / max_exact)
                         * (nb - max_exact)).astype(jnp.int32)
    large = jnp.minimum(large, nb - 1)
    return ret + jnp.where(n < max_exact, n, large)


def _bias_tiles(rel_bias):
    t = ATT_TILE
    p = 2 * t
    table = rel_bias.astype(F32).T
    far = table[:, N_BUCKETS // 2 - 1]
    w = jnp.arange(p, dtype=jnp.int32)[None, :]
    d = jnp.arange(2, dtype=jnp.int32)[:, None]
    rel = (t - 1) - w - d * t
    vec = (table[:, _t5_bucket(rel)] - far[:, None, None]) * LOG2E
    rows = jnp.tile(vec, (1, 1, t))[..., :t * (p - 1)].reshape(HEADS, 2, t, p - 1)
    bias = rows[..., t - 1:2 * t - 1]
    kk = jnp.arange(t, dtype=jnp.int32)[:, None]
    qq = jnp.arange(t, dtype=jnp.int32)[None, :]
    allowed = (kk // CHUNK <= qq // CHUNK)[None, None] | (d > 0)[None, :, :, None]
    bias = jnp.where(allowed, bias, NEG_BIG)
    return bias


def _rope_tables(seq):
    pos = jnp.arange(seq, dtype=F32)
    inv = ROPE_THETA ** (-jnp.arange(0, MLA_ROPE_DIM, 2, dtype=F32) / MLA_ROPE_DIM)
    ang = pos[:, None] * inv[None, :]
    cos, sin = jnp.cos(ang), jnp.sin(ang)
    z = jnp.zeros((seq, LANES - MLA_ROPE_DIM), F32)
    return (jnp.concatenate([cos, cos, z], axis=-1), jnp.concatenate([-sin, sin, z], axis=-1))


def _swap_halves(w):
    half = w.shape[-1] // 2
    return jnp.concatenate([w[..., half:], w[..., :half]], axis=-1)


def _pack_params(w_in, lambda_q1, lambda_k1, lambda_q2, lambda_k2, subln_g, q_norm_g, w_uq,
                 kv_norm_g, w_ukv, w_o, ln1_g, ln1_b, w_mlp_in, w_mlp_out, ln2_g, ln2_b):
    depth = w_in.shape[0]
    o_k, o_v, o_cq = DIFF_WIDTH, 2 * DIFF_WIDTH, 3 * DIFF_WIDTH
    o_ckv = o_cq + MLA_Q_RANK
    o_kr = o_ckv + MLA_KV_RANK
    kr = w_in[..., o_kr:]
    w1 = jnp.concatenate([w_in[..., :o_k] * (DIFF_HEAD_DIM ** -0.5 * LOG2E), w_in[..., o_k:o_v],
                          w_in[..., o_cq:o_ckv], w_in[..., o_ckv:o_kr], kr, _swap_halves(kr)],
                         axis=-1)
    mla_scale = (MLA_NOPE_DIM + MLA_ROPE_DIM) ** -0.5 * LOG2E
    uq = (w_uq * mla_scale).reshape(depth, MLA_Q_RANK, HEADS, MLA_NOPE_DIM + MLA_ROPE_DIM)
    zq = jnp.zeros((depth, MLA_Q_RANK, HEADS, MLA_QK_PAD - MLA_NOPE_DIM - MLA_ROPE_DIM), F32)
    wuq = jnp.concatenate([uq, zq], axis=-1).reshape(depth, MLA_Q_RANK, HEADS * MLA_QK_PAD)
    wuqs = jnp.concatenate([_swap_halves(uq[..., MLA_NOPE_DIM:]), zq], axis=-1)
    wuqs = wuqs.reshape(depth, MLA_Q_RANK, HEADS * LANES)
    ukv = w_ukv.reshape(depth, MLA_KV_RANK, HEADS, MLA_NOPE_DIM + MLA_V_DIM)
    wukvk = ukv[..., :MLA_NOPE_DIM].reshape(depth, MLA_KV_RANK, HEADS * MLA_NOPE_DIM)
    wukvv = ukv[..., MLA_NOPE_DIM:].reshape(depth, MLA_KV_RANK, HEADS * MLA_V_DIM)
    row = lambda v: v.reshape(depth, 1, -1).astype(F32)
    return dict(
        w1=w1.astype(BF16), wvt=jnp.swapaxes(w_in[..., o_v:o_cq], 1, 2).astype(BF16),
        wuq=wuq.astype(BF16), wuqs=wuqs.astype(BF16), wukvk=wukvk.astype(BF16),
        wukvvt=jnp.swapaxes(wukvv, 1, 2).astype(BF16), gq=row(q_norm_g), gkv=row(kv_norm_g),
        subln=row(subln_g),
        lam=jnp.stack([lambda_q1, lambda_k1, lambda_q2, lambda_k2], axis=1).astype(F32),
        wo=w_o.astype(BF16), g1=row(ln1_g), b1=row(ln1_b), wi=w_mlp_in.astype(BF16),
        wout=w_mlp_out.astype(BF16), g2=row(ln2_g), b2=row(ln2_b))


def kernel(x, w_in, lambda_q1, lambda_k1, lambda_q2, lambda_k2, subln_g, q_norm_g, w_uq, kv_norm_g,
           w_ukv, rel_bias, w_o, ln1_g, ln1_b, w_mlp_in, w_mlp_out, ln2_g, ln2_b):
    batch, seq, _ = x.shape
    assert seq % PROJ_ROWS == 0 and seq % ATT_TILE == 0 and (batch * seq) % POST_ROWS == 0
    wp = _pack_params(w_in, lambda_q1, lambda_k1, lambda_q2, lambda_k2, subln_g, q_norm_g, w_uq,
                      kv_norm_g, w_ukv, w_o, ln1_g, ln1_b, w_mlp_in, w_mlp_out, ln2_g, ln2_b)
    wp["cos"], wp["sin"] = _rope_tables(seq)
    bias = _bias_tiles(rel_bias)
    x2 = x.reshape(batch * seq, D_MODEL)
    for l in range(DEPTH):
        lambda_init = 0.8 - 0.6 * math.exp(-0.3 * l)
        dq, dk, dvt, qm, km, vmt = _proj_call(x2, wp, l, seq)
        mix_a, mix_b = _attn_call(dq, dk, dvt, bias, wp, l, qm, km, vmt, seq, lambda_init)
        x2 = _post_call(x2, mix_a, mix_b, wp, l)
    return x2.reshape(batch, seq, D_MODEL)
```

```python
import functools
import math

import jax
import jax.numpy as jnp
from jax import lax
from jax.experimental import pallas as pl
from jax.experimental.pallas import tpu as pltpu

F32 = jnp.float32
BF16 = jnp.bfloat16

D_MODEL = 1024
DEPTH = 2
CHUNK = 64
HEADS = 4
DIFF_HEAD_DIM = 64
DIFF_V_DIM = 2 * DIFF_HEAD_DIM
DIFF_WIDTH = HEADS * DIFF_V_DIM
MLA_NOPE_DIM = 128
MLA_ROPE_DIM = 64
MLA_V_DIM = 128
MLA_Q_RANK = 256
MLA_KV_RANK = 128
MLA_WIDTH = HEADS * MLA_V_DIM
MLA_QK_PAD = 256
D_FF = 4 * D_MODEL
N_BUCKETS = 32
MAX_DISTANCE = 128
ROPE_THETA = 10000.0
ALPHA = (2 * DEPTH) ** 0.25
LN_EPS = 1e-5
RMS_EPS = 1e-6
NEG_BIG = -1e30
LOG2E = math.log2(math.e)

LANES = 128
BF16_SUBLANES = 16
ATT_TILE = 256
ATT_HEADS_PER_STEP = 2
PROJ_ROWS = 1024
POST_ROWS = 1024
POST_ROW_BLOCK = 256
FF_CHUNK = 1024
DIFF_LOOKAHEAD = 3
MLA_LOOKAHEAD = 3
VMEM_LIMIT = 56 * 1024 * 1024
assert ATT_TILE >= MAX_DISTANCE and ATT_TILE % CHUNK == 0

C_DQ = 0
C_DK = C_DQ + DIFF_WIDTH
C_CQ = C_DK + DIFF_WIDTH
C_CKV = C_CQ + MLA_Q_RANK
C_KR = C_CKV + MLA_KV_RANK
W1_COLS = C_KR + LANES

_NT = (((1,), (1,)), ((), ()))


def _rms(x, g):
    return x * lax.rsqrt(jnp.mean(x * x, axis=-1, keepdims=True) + RMS_EPS) * g


def _layer_norm(x, g, b):
    mu = jnp.mean(x, axis=-1, keepdims=True)
    xc = x - mu
    var = jnp.mean(xc * xc, axis=-1, keepdims=True)
    return xc * lax.rsqrt(var + LN_EPS) * g + b


def _layer_spec(arr, l, **kwargs):
    index = (l,) + (0,) * (arr.ndim - 1)
    return pl.BlockSpec((1,) + arr.shape[1:], lambda *_: index, **kwargs)


def _proj_kernel(x_ref, w1_ref, wvt_ref, wuq_ref, wuqs_ref, wukvk_ref, wukvvt_ref,
                 gq_ref, gkv_ref, cos_ref, sin_ref,
                 dq_ref, dk_ref, dvt_ref, qm_ref, km_ref, vmt_ref):
    xb = x_ref[...].astype(BF16)
    h = jnp.dot(xb, w1_ref[0], preferred_element_type=F32)
    dq_ref[...] = h[:, C_DQ:C_DQ + DIFF_WIDTH].astype(BF16)
    dk_ref[...] = h[:, C_DK:C_DK + DIFF_WIDTH].astype(BF16)
    dvt_ref[0] = lax.dot_general(wvt_ref[0], xb, _NT, preferred_element_type=F32).astype(BF16)

    cos = cos_ref[...]
    sin = sin_ref[...]
    c_q = _rms(h[:, C_CQ:C_CQ + MLA_Q_RANK], gq_ref[0]).astype(BF16)
    qf = jnp.dot(c_q, wuq_ref[0], preferred_element_type=F32)
    qsw = jnp.dot(c_q, wuqs_ref[0], preferred_element_type=F32)
    c_kv = _rms(h[:, C_CKV:C_CKV + MLA_KV_RANK], gkv_ref[0]).astype(BF16)
    kn = jnp.dot(c_kv, wukvk_ref[0], preferred_element_type=F32)
    vmt_ref[0] = lax.dot_general(wukvvt_ref[0], c_kv, _NT,
                                 preferred_element_type=F32).astype(BF16)
    kr = h[:, C_KR:C_KR + LANES]
    k_rope = (kr * cos + pltpu.roll(kr, LANES // 2, 1) * sin).astype(BF16)

    for hh in range(HEADS):
        b0 = hh * MLA_QK_PAD
        qm_ref[:, b0:b0 + LANES] = qf[:, b0:b0 + LANES].astype(BF16)
        qm_ref[:, b0 + LANES:b0 + 2 * LANES] = (
            qf[:, b0 + LANES:b0 + 2 * LANES] * cos + qsw[:, hh * LANES:(hh + 1) * LANES] * sin
        ).astype(BF16)
        km_ref[:, b0:b0 + LANES] = kn[:, hh * LANES:(hh + 1) * LANES].astype(BF16)
        km_ref[:, b0 + LANES:b0 + 2 * LANES] = k_rope


def _proj_call(x2, wp, l, seq):
    ntok = x2.shape[0]
    batch = ntok // seq
    rows = PROJ_ROWS
    steps_per_seq = seq // rows

    def const(name):
        return _layer_spec(wp[name], l)

    def tok(width):
        return pl.BlockSpec((rows, width), lambda i: (i, 0))

    vt_spec = pl.BlockSpec((1, HEADS * LANES, rows),
                           lambda i: (i // steps_per_seq, 0, i % steps_per_seq))
    pos_spec = pl.BlockSpec((rows, LANES), lambda i: (i % steps_per_seq, 0))
    vt_shape = jax.ShapeDtypeStruct((batch, HEADS * LANES, seq), BF16)
    return pl.pallas_call(
        _proj_kernel,
        grid=(ntok // rows,),
        in_specs=[tok(D_MODEL), const("w1"), const("wvt"), const("wuq"), const("wuqs"),
                  const("wukvk"), const("wukvvt"), const("gq"), const("gkv"), pos_spec, pos_spec],
        out_specs=[tok(DIFF_WIDTH), tok(DIFF_WIDTH), vt_spec,
                   tok(HEADS * MLA_QK_PAD), tok(HEADS * MLA_QK_PAD), vt_spec],
        out_shape=[jax.ShapeDtypeStruct((ntok, DIFF_WIDTH), BF16),
                   jax.ShapeDtypeStruct((ntok, DIFF_WIDTH), BF16),
                   vt_shape,
                   jax.ShapeDtypeStruct((ntok, HEADS * MLA_QK_PAD), BF16),
                   jax.ShapeDtypeStruct((ntok, HEADS * MLA_QK_PAD), BF16),
                   vt_shape],
        compiler_params=pltpu.CompilerParams(dimension_semantics=("arbitrary",),
                                             vmem_limit_bytes=VMEM_LIMIT),
        name="proj",
    )(x2, wp["w1"], wp["wvt"], wp["wuq"], wp["wuqs"], wp["wukvk"], wp["wukvvt"],
      wp["gq"], wp["gkv"], wp["cos"], wp["sin"])


def _col_max(a, b):
    m = jnp.max(b, axis=0, keepdims=True)
    return m if a is None else jnp.maximum(a, m)


def _attend(n_heads, n_blocks, n_keys_fn, q_fn, score_fn, finish_fn, vt_fn, lookahead):
    ones = jnp.ones((BF16_SUBLANES, ATT_TILE), BF16)
    items = [(h, qi, c) for h in range(n_heads) for qi in range(n_blocks)
             for c in range(n_keys_fn(qi))]
    qs, scores = {}, {}

    def issue(idx):
        h, qi, c = items[idx]
        if c == 0:
            qs[h, qi] = q_fn(h, qi)
        scores[idx] = score_fn(h, qi, qs[h, qi], c)

    for idx in range(min(lookahead, len(items))):
        issue(idx)
    m, acc = None, None
    for idx, (h, qi, c) in enumerate(items):
        if idx + lookahead < len(items):
            issue(idx + lookahead)
        s = scores.pop(idx)
        m_new = _col_max(m, s)
        e = jnp.exp2(s - m_new).astype(BF16)
        vt = jnp.concatenate([vt_fn(h, c), ones], axis=0)
        pv = jnp.dot(vt, e, preferred_element_type=F32)
        acc = pv if acc is None else acc * jnp.exp2(m - m_new) + pv
        m = m_new
        if c == n_keys_fn(qi) - 1:
            finish_fn(h, qi, acc[:LANES] / acc[LANES:LANES + 1])
            m, acc = None, None
        yield


def _diff_steps(q_ref, k_ref, vt_ref, bias_ref, lam_ref, g_ref, o_ref, lambda_init):
    t = ATT_TILE
    lane = lax.broadcasted_iota(jnp.int32, (t, LANES), 1)
    lp = lam_ref[0]
    lam = (jnp.exp(jnp.sum(lp[0:1] * lp[1:2], axis=-1, keepdims=True))
           - jnp.exp(jnp.sum(lp[2:3] * lp[3:4], axis=-1, keepdims=True)) + lambda_init)
    g = g_ref[0]

    def q_rows(h, b):
        qi, m = divmod(b, 2)
        q = q_ref[qi * t:(qi + 1) * t, h * LANES:(h + 1) * LANES]
        keep = lane < DIFF_HEAD_DIM if m == 0 else lane >= DIFF_HEAD_DIM
        return jnp.where(keep, q, jnp.zeros_like(q))

    def score(h, b, q, c):
        qi = b // 2
        k = k_ref[c * t:(c + 1) * t, h * LANES:(h + 1) * LANES]
        s = lax.dot_general(k, q, _NT, preferred_element_type=F32)
        return s + bias_ref[h, qi - c] if qi - c <= 1 else s

    def values(h, c):
        return vt_ref[0, h * LANES:(h + 1) * LANES, c * t:(c + 1) * t]

    first_map = {}

    def finish(h, b, o):
        qi, m = divmod(b, 2)
        if m == 0:
            first_map[h] = o
            return
        ot = first_map.pop(h) - lam * o
        y = _rms(ot.T, g) * (1.0 - lambda_init)
        o_ref[qi * t:(qi + 1) * t, h * LANES:(h + 1) * LANES] = y.astype(BF16)

    return _attend(ATT_HEADS_PER_STEP, 2 * (q_ref.shape[0] // t), lambda b: b // 2 + 1, q_rows,
                   score, finish, values, DIFF_LOOKAHEAD)


def _mla_steps(q_ref, k_ref, vt_ref, o_ref):
    t = ATT_TILE
    w = MLA_QK_PAD
    kc = lax.broadcasted_iota(jnp.int32, (t, t), 0) // CHUNK
    qc = lax.broadcasted_iota(jnp.int32, (t, t), 1) // CHUNK
    allowed = kc <= qc

    def q_rows(h, qi):
        return q_ref[qi * t:(qi + 1) * t, h * w:(h + 1) * w]

    def score(h, qi, q, c):
        k = k_ref[c * t:(c + 1) * t, h * w:(h + 1) * w]
        s = lax.dot_general(k, q, _NT, preferred_element_type=F32)
        return jnp.where(allowed, s, NEG_BIG) if c == qi else s

    def values(h, c):
        return vt_ref[0, h * LANES:(h + 1) * LANES, c * t:(c + 1) * t]

    def finish(h, qi, o):
        o_ref[qi * t:(qi + 1) * t, h * LANES:(h + 1) * LANES] = o.T.astype(BF16)

    return _attend(ATT_HEADS_PER_STEP, q_ref.shape[0] // t, lambda qi: qi + 1, q_rows, score,
                   finish, values, MLA_LOOKAHEAD)


def _attn_kernel(dq_ref, dk_ref, dvt_ref, bias_ref, lam_ref, g_ref, qm_ref, km_ref, vmt_ref,
                 oa_ref, ob_ref, *, lambda_init):
    streams = [_diff_steps(dq_ref, dk_ref, dvt_ref, bias_ref, lam_ref, g_ref, oa_ref, lambda_init),
               _mla_steps(qm_ref, km_ref, vmt_ref, ob_ref)]
    n_tiles = dq_ref.shape[0] // ATT_TILE
    per_head = n_tiles * (n_tiles + 1) // 2
    totals = [2 * ATT_HEADS_PER_STEP * per_head, ATT_HEADS_PER_STEP * per_head]
    done = [0, 0]
    while done[0] < totals[0] or done[1] < totals[1]:
        i = 0 if done[0] * totals[1] <= done[1] * totals[0] else 1
        next(streams[i])
        done[i] += 1


def _attn_call(dq, dk, dvt, bias, wp, l, qm, km, vmt, seq, lambda_init):
    ntok = dq.shape[0]
    n = ATT_HEADS_PER_STEP

    def rows(width):
        return pl.BlockSpec((seq, n * width), lambda b, h: (b, h))

    vt_spec = pl.BlockSpec((1, n * LANES, seq), lambda b, h: (b, h, 0))
    return pl.pallas_call(
        functools.partial(_attn_kernel, lambda_init=lambda_init),
        grid=(ntok // seq, HEADS // n),
        in_specs=[rows(LANES), rows(LANES), vt_spec,
                  pl.BlockSpec((n,) + bias.shape[1:], lambda b, h: (h, 0, 0, 0)),
                  _layer_spec(wp["lam"], l), _layer_spec(wp["subln"], l),
                  rows(MLA_QK_PAD), rows(MLA_QK_PAD), vt_spec],
        out_specs=[rows(LANES), rows(LANES)],
        out_shape=[jax.ShapeDtypeStruct((ntok, DIFF_WIDTH), BF16),
                   jax.ShapeDtypeStruct((ntok, MLA_WIDTH), BF16)],
        compiler_params=pltpu.CompilerParams(dimension_semantics=("arbitrary", "arbitrary"),
                                             vmem_limit_bytes=VMEM_LIMIT),
        name="attn",
    )(dq, dk, dvt, bias, wp["lam"], wp["subln"], qm, km, vmt)


def _post_kernel(x_ref, ma_ref, mb_ref, wo_ref, g1_ref, b1_ref, wi_ref, wout_ref, g2_ref, b2_ref,
                 o_ref):
    n_blocks = x_ref.shape[0] // POST_ROW_BLOCK
    n_chunks = D_FF // FF_CHUNK

    def attn_out(r):
        rs = slice(r * POST_ROW_BLOCK, (r + 1) * POST_ROW_BLOCK)
        y = (jnp.dot(ma_ref[rs, :], wo_ref[0, :DIFF_WIDTH, :], preferred_element_type=F32)
             + jnp.dot(mb_ref[rs, :], wo_ref[0, DIFF_WIDTH:, :], preferred_element_type=F32))
        return _layer_norm(ALPHA * x_ref[rs, :] + y, g1_ref[0], b1_ref[0])

    def up(x1b, c):
        return jnp.dot(x1b, wi_ref[0, :, c * FF_CHUNK:(c + 1) * FF_CHUNK],
                       preferred_element_type=F32)

    def mlp(r, x1):
        x1b = x1.astype(BF16)
        u = up(x1b, 0)
        yield
        y2 = None
        for c in range(n_chunks):
            u_next = up(x1b, c + 1) if c + 1 < n_chunks else None
            act = jnp.square(jnp.maximum(u, 0.0)).astype(BF16)
            d = jnp.dot(act, wout_ref[0, c * FF_CHUNK:(c + 1) * FF_CHUNK, :],
                        preferred_element_type=F32)
            y2 = d if y2 is None else y2 + d
            u = u_next
            yield
        o_ref[r * POST_ROW_BLOCK:(r + 1) * POST_ROW_BLOCK, :] = _layer_norm(
            ALPHA * x1 + y2, g2_ref[0], b2_ref[0])

    chains = [mlp(r, attn_out(r)) for r in range(n_blocks)]
    while chains:
        for ch in list(chains):
            if next(ch, "done") == "done":
                chains.remove(ch)


def _post_call(x2, mix_a, mix_b, wp, l):
    ntok = x2.shape[0]
    rows = POST_ROWS

    def const(name):
        return _layer_spec(wp[name], l, pipeline_mode=pl.Buffered(1))

    def tok(width):
        return pl.BlockSpec((rows, width), lambda i: (i, 0))

    return pl.pallas_call(
        _post_kernel,
        grid=(ntok // rows,),
        in_specs=[tok(D_MODEL), tok(DIFF_WIDTH), tok(MLA_WIDTH), const("wo"), const("g1"),
                  const("b1"), const("wi"), const("wout"), const("g2"), const("b2")],
        out_specs=tok(D_MODEL),
        out_shape=jax.ShapeDtypeStruct((ntok, D_MODEL), F32),
        compiler_params=pltpu.CompilerParams(dimension_semantics=("arbitrary",),
                                             vmem_limit_bytes=VMEM_LIMIT),
        name="post",
    )(x2, mix_a, mix_b, wp["wo"], wp["g1"], wp["b1"], wp["wi"], wp["wout"], wp["g2"], wp["b2"])


def _t5_bucket(rel):
    nb = N_BUCKETS // 2
    ret = (rel > 0).astype(jnp.int32) * nb
    n = jnp.abs(rel)
    max_exact = nb // 2
    nf = jnp.maximum(n, 1).astype(F32)
    large = max_exact + (jnp.log(nf / max_exact) / math.log(MAX_DISTANCE / max_exact)
                         * (nb - max_exact)).astype(jnp.int32)
    large = jnp.minimum(large, nb - 1)
    return ret + jnp.where(n < max_exact, n, large)


def _bias_tiles(rel_bias):
    t = ATT_TILE
    p = 2 * t
    table = rel_bias.astype(F32).T
    far = table[:, N_BUCKETS // 2 - 1]
    w = jnp.arange(p, dtype=jnp.int32)[None, :]
    d = jnp.arange(2, dtype=jnp.int32)[:, None]
    rel = (t - 1) - w - d * t
    vec = (table[:, _t5_bucket(rel)] - far[:, None, None]) * LOG2E
    rows = jnp.tile(vec, (1, 1, t))[..., :t * (p - 1)].reshape(HEADS, 2, t, p - 1)
    bias = rows[..., t - 1:2 * t - 1]
    kk = jnp.arange(t, dtype=jnp.int32)[:, None]
    qq = jnp.arange(t, dtype=jnp.int32)[None, :]
    allowed = (kk // CHUNK <= qq // CHUNK)[None, None] | (d > 0)[None, :, :, None]
    bias = jnp.where(allowed, bias, NEG_BIG)
    return bias


def _rope_tables(seq):
    pos = jnp.arange(seq, dtype=F32)
    inv = ROPE_THETA ** (-jnp.arange(0, MLA_ROPE_DIM, 2, dtype=F32) / MLA_ROPE_DIM)
    ang = pos[:, None] * inv[None, :]
    cos, sin = jnp.cos(ang), jnp.sin(ang)
    z = jnp.zeros((seq, LANES - MLA_ROPE_DIM), F32)
    return (jnp.concatenate([cos, cos, z], axis=-1), jnp.concatenate([-sin, sin, z], axis=-1))


def _swap_halves(w):
    half = w.shape[-1] // 2
    return jnp.concatenate([w[..., half:], w[..., :half]], axis=-1)


def _pack_params(w_in, lambda_q1, lambda_k1, lambda_q2, lambda_k2, subln_g, q_norm_g, w_uq,
                 kv_norm_g, w_ukv, w_o, ln1_g, ln1_b, w_mlp_in, w_mlp_out, ln2_g, ln2_b):
    depth = w_in.shape[0]
    o_k, o_v, o_cq = DIFF_WIDTH, 2 * DIFF_WIDTH, 3 * DIFF_WIDTH
    o_ckv = o_cq + MLA_Q_RANK
    o_kr = o_ckv + MLA_KV_RANK
    kr = w_in[..., o_kr:]
    w1 = jnp.concatenate([w_in[..., :o_k] * (DIFF_HEAD_DIM ** -0.5 * LOG2E), w_in[..., o_k:o_v],
                          w_in[..., o_cq:o_ckv], w_in[..., o_ckv:o_kr], kr, _swap_halves(kr)],
                         axis=-1)
    mla_scale = (MLA_NOPE_DIM + MLA_ROPE_DIM) ** -0.5 * LOG2E
    uq = (w_uq * mla_scale).reshape(depth, MLA_Q_RANK, HEADS, MLA_NOPE_DIM + MLA_ROPE_DIM)
    zq = jnp.zeros((depth, MLA_Q_RANK, HEADS, MLA_QK_PAD - MLA_NOPE_DIM - MLA_ROPE_DIM), F32)
    wuq = jnp.concatenate([uq, zq], axis=-1).reshape(depth, MLA_Q_RANK, HEADS * MLA_QK_PAD)
    wuqs = jnp.concatenate([_swap_halves(uq[..., MLA_NOPE_DIM:]), zq], axis=-1)
    wuqs = wuqs.reshape(depth, MLA_Q_RANK, HEADS * LANES)
    ukv = w_ukv.reshape(depth, MLA_KV_RANK, HEADS, MLA_NOPE_DIM + MLA_V_DIM)
    wukvk = ukv[..., :MLA_NOPE_DIM].reshape(depth, MLA_KV_RANK, HEADS * MLA_NOPE_DIM)
    wukvv = ukv[..., MLA_NOPE_DIM:].reshape(depth, MLA_KV_RANK, HEADS * MLA_V_DIM)
    row = lambda v: v.reshape(depth, 1, -1).astype(F32)
    return dict(
        w1=w1.astype(BF16), wvt=jnp.swapaxes(w_in[..., o_v:o_cq], 1, 2).astype(BF16),
        wuq=wuq.astype(BF16), wuqs=wuqs.astype(BF16), wukvk=wukvk.astype(BF16),
        wukvvt=jnp.swapaxes(wukvv, 1, 2).astype(BF16), gq=row(q_norm_g), gkv=row(kv_norm_g),
        subln=row(subln_g),
        lam=jnp.stack([lambda_q1, lambda_k1, lambda_q2, lambda_k2], axis=1).astype(F32),
        wo=w_o.astype(BF16), g1=row(ln1_g), b1=row(ln1_b), wi=w_mlp_in.astype(BF16),
        wout=w_mlp_out.astype(BF16), g2=row(ln2_g), b2=row(ln2_b))


def kernel(x, w_in, lambda_q1, lambda_k1, lambda_q2, lambda_k2, subln_g, q_norm_g, w_uq, kv_norm_g,
           w_ukv, rel_bias, w_o, ln1_g, ln1_b, w_mlp_in, w_mlp_out, ln2_g, ln2_b):
    batch, seq, _ = x.shape
    assert seq % PROJ_ROWS == 0 and seq % ATT_TILE == 0 and (batch * seq) % POST_ROWS == 0
    wp = _pack_params(w_in, lambda_q1, lambda_k1, lambda_q2, lambda_k2, subln_g, q_norm_g, w_uq,
                      kv_norm_g, w_ukv, w_o, ln1_g, ln1_b, w_mlp_in, w_mlp_out, ln2_g, ln2_b)
    wp["cos"], wp["sin"] = _rope_tables(seq)
    bias = _bias_tiles(rel_bias)
    x2 = x.reshape(batch * seq, D_MODEL)
    for l in range(DEPTH):
        lambda_init = 0.8 - 0.6 * math.exp(-0.3 * l)
        dq, dk, dvt, qm, km, vmt = _proj_call(x2, wp, l, seq)
        mix_a, mix_b = _attn_call(dq, dk, dvt, bias, wp, l, qm, km, vmt, seq, lambda_init)
        x2 = _post_call(x2, mix_a, mix_b, wp, l)
    return x2.reshape(batch, seq, D_MODEL)
```
